```python
import math
import jax
import jax.numpy as jnp
from jax import lax
import numpy as np


D_MODEL = 1024
BATCH = 8
SEQ = 4096
DEPTH = 2
DEC_BATCH = 2
DEC_SEQ = 16384
PAST_LEN = 128

N_META = 16
GRID_W = 64
HEAD_DIM = 64
D_MIX = 2 * D_MODEL

SSD_HEADS = 16
SSD_HEAD_DIM = 64
SSD_INNER = SSD_HEADS * SSD_HEAD_DIM
SSD_GROUPS = 2
SSD_STATE = 128
SSD_XBC = SSD_INNER + 2 * SSD_GROUPS * SSD_STATE
SSD_CONV = 5
SSD_CHUNK = 128

WIN_Q_HEADS = 8
WIN_KV_HEADS = 2
WIN_RADIUS = 128
WIN_BLOCK = 128
ROPE_THETA = 500000.0
ROPE_DIM = HEAD_DIM // 4

NA_HEADS = 8
NA_KR = 8
NA_KC = 16
NA_QC = 16
NA_KCB = 2 * NA_KC

D_FF = 2816
FFN_CONV = 3
EPS = 1e-6

D_IN_PROJ = SSD_INNER + SSD_XBC + 2 * SSD_HEADS + (WIN_Q_HEADS + 2 * WIN_KV_HEADS) * HEAD_DIM + 3 * NA_HEADS * HEAD_DIM

kernel_name = 'hybrid_bidir_ssd_window_natten_encoder'


def _rmsnorm(x, w):
    xf = x.astype(jnp.float32)
    y = xf * lax.rsqrt(jnp.mean(xf * xf, axis=-1, keepdims=True) + EPS)
    return (y * w.astype(jnp.float32)).astype(x.dtype)


def _depthwise_conv(x, w, b):
    k_w = w.shape[0]
    pad = k_w // 2
    n = x.shape[1]
    xp = jnp.pad(x, ((0, 0), (pad, pad), (0, 0)))
    y = xp[:, 0:n] * w[0] + b
    for j in range(1, k_w):
        y = y + xp[:, j:j + n] * w[j]
    return y


def _partial_rope(x, pos):
    half = ROPE_DIM // 2
    inv = jnp.power(ROPE_THETA, -jnp.arange(half, dtype=jnp.float32) / half)
    ang = pos.astype(jnp.float32)[:, None] * inv[None, :]
    cos = jnp.cos(ang)[None, :, None, :]
    sin = jnp.sin(ang)[None, :, None, :]
    xr = x[..., :ROPE_DIM].astype(jnp.float32)
    x1, x2 = xr[..., :half], xr[..., half:]
    rot = jnp.concatenate([x1 * cos - x2 * sin, x2 * cos + x1 * sin], axis=-1).astype(x.dtype)
    return jnp.concatenate([rot, x[..., ROPE_DIM:]], axis=-1)


def _split_in_proj(u):
    sizes = [SSD_INNER, SSD_XBC, 2 * SSD_HEADS, WIN_Q_HEADS * HEAD_DIM, WIN_KV_HEADS * HEAD_DIM,
             WIN_KV_HEADS * HEAD_DIM, NA_HEADS * HEAD_DIM, NA_HEADS * HEAD_DIM, NA_HEADS * HEAD_DIM]
    idx = np.cumsum(sizes)[:-1].tolist()
    return jnp.split(u, idx, axis=-1)


def _ssd_scan(xh, dt, a_neg, bm, cm):
    bsz, t_len, n_heads, p_dim = xh.shape
    n_grp, n_state = bm.shape[-2], bm.shape[-1]
    rep = n_heads // n_grp
    q = SSD_CHUNK
    nc = t_len // q
    f32 = jnp.float32
    x = (xh.astype(f32) * dt[..., None]).reshape(bsz, nc, q, n_grp, rep, p_dim)
    a = (dt * a_neg).reshape(bsz, nc, q, n_grp, rep)
    b = bm.astype(f32).reshape(bsz, nc, q, n_grp, n_state)
    c = cm.astype(f32).reshape(bsz, nc, q, n_grp, n_state)
    acs = jnp.cumsum(a, axis=2)
    tri = jnp.tril(jnp.ones((q, q), dtype=bool))[:, :, None, None]
    decay = jnp.exp(jnp.where(tri, acs[:, :, :, None] - acs[:, :, None, :], -jnp.inf))
    cb = jnp.einsum('bclgn,bcsgn->bclsg', c, b)
    y_diag = jnp.einsum('bclsgr,bcsgrp->bclgrp', cb[..., None] * decay, x)
    decay_states = jnp.exp(acs[:, :, -1:] - acs)
    states = jnp.einsum('bclgn,bclgrp->bcgrpn', b, x * decay_states[..., None])
    chunk_decay = jnp.exp(acs[:, :, -1])

    def step(h, inp):
        s_c, d_c = inp
        return d_c[..., None, None] * h + s_c, h

    h0 = jnp.zeros((bsz, n_grp, rep, p_dim, n_state), f32)
    _, h_in = lax.scan(step, h0, (jnp.moveaxis(states, 1, 0), jnp.moveaxis(chunk_decay, 1, 0)))
    h_in = jnp.moveaxis(h_in, 0, 1)
    y_off = jnp.einsum('bclgn,bcgrpn->bclgrp', c, h_in) * jnp.exp(acs)[..., None]
    return (y_diag + y_off).reshape(bsz, t_len, n_heads, p_dim)


def _ssd_mixer(z, xbc, dt_raw, conv_w, conv_b, dt_bias, a_log, d_skip, norm_w):
    bsz, seq_len, _ = xbc.shape
    f32 = jnp.float32
    xbc = jax.nn.silu(_depthwise_conv(xbc, conv_w, conv_b))
    xs = xbc[..., :SSD_INNER].reshape(bsz, seq_len, SSD_HEADS, SSD_HEAD_DIM)
    bm = xbc[..., SSD_INNER:SSD_INNER + SSD_GROUPS * SSD_STATE].reshape(bsz, seq_len, SSD_GROUPS, SSD_STATE)
    cm = xbc[..., SSD_INNER + SSD_GROUPS * SSD_STATE:].reshape(bsz, seq_len, SSD_GROUPS, SSD_STATE)
    dt_raw = dt_raw.astype(f32)
    dt_f = jax.nn.softplus(dt_raw[..., :SSD_HEADS] + dt_bias[0].astype(f32))
    dt_b = jax.nn.softplus(dt_raw[..., SSD_HEADS:] + dt_bias[1].astype(f32))
    a_f = -jnp.exp(a_log[0].astype(f32))
    a_b = -jnp.exp(a_log[1].astype(f32))
    pad = SSD_CHUNK - N_META

    def lay(t):
        zeros = jnp.zeros((bsz, pad) + t.shape[2:], t.dtype)
        return jnp.concatenate([t[:, :N_META], zeros, t[:, N_META:]], axis=1)

    xh, bp, cp = lay(xs), lay(bm), lay(cm)
    dtf, dtb = lay(dt_f), lay(dt_b)
    y_f = _ssd_scan(xh, dtf, a_f, bp, cp)
    flip = lambda t: jnp.flip(t, axis=1)
    y_b = flip(_ssd_scan(flip(xh), flip(dtb), a_b, flip(bp), flip(cp)))
    y = y_f + y_b
    y = jnp.concatenate([y[:, :N_META], y[:, SSD_CHUNK:]], axis=1)
    y = y + d_skip.astype(f32)[:, None] * xs.astype(f32)
    y = y.reshape(bsz, seq_len, SSD_INNER) * jax.nn.silu(z.astype(f32))
    return _rmsnorm(y, norm_w).astype(z.dtype)


def _window_attention(q, k, v, sink):
    bsz, seq_len, n_q, hd = q.shape
    n_kv = k.shape[2]
    rep = n_q // n_kv
    w = WIN_BLOCK
    nb = -(-seq_len // w)
    lp = nb * w
    f32 = jnp.float32
    scale = hd ** -0.5
    qb = jnp.pad(q, ((0, 0), (0, lp - seq_len), (0, 0), (0, 0))).reshape(bsz, nb, w, n_kv, rep, hd)

    def kwin(t):
        tp = jnp.pad(t, ((0, 0), (w, lp - seq_len + w), (0, 0), (0, 0))).reshape(bsz, nb + 2, w, n_kv, hd)
        return jnp.concatenate([tp[:, :-2], tp[:, 1:-1], tp[:, 2:]], axis=2)

    kw, vw = kwin(k), kwin(v)
    km, vm = k[:, :N_META], v[:, :N_META]
    s_w = jnp.einsum('bnqgrd,bnkgd->bngrqk', qb, kw).astype(f32) * scale
    s_m = jnp.einsum('bnqgrd,bmgd->bngrqm', qb, km).astype(f32) * scale
    qi = jnp.arange(nb)[:, None, None] * w + jnp.arange(w)[None, :, None]
    kj = jnp.arange(nb)[:, None, None] * w - w + jnp.arange(3 * w)[None, None, :]
    ok = (jnp.abs(qi - kj) <= WIN_RADIUS) & (kj >= N_META) & (kj < seq_len)
    s_w = jnp.where(ok[None, :, None, None], s_w, -jnp.inf)
    sink_l = jnp.broadcast_to(sink.astype(f32).reshape(n_kv, rep)[None, None, :, :, None, None], s_w.shape[:-1] + (1,))
    p = jax.nn.softmax(jnp.concatenate([s_w, s_m, sink_l], axis=-1), axis=-1)
    p_w = p[..., :3 * w].astype(v.dtype)
    p_m = p[..., 3 * w:3 * w + N_META].astype(v.dtype)
    o = jnp.einsum('bngrqk,bnkgd->bnqgrd', p_w, vw) + jnp.einsum('bngrqm,bmgd->bnqgrd', p_m, vm)
    return o.reshape(bsz, lp, n_q * hd)[:, :seq_len]


def _na_attention(q, k, v, rpb, meta_bias):
    bsz, seq_len, n_heads, hd = q.shape
    n_tok = seq_len - N_META
    rows = n_tok // GRID_W
    kr = min(NA_KR, rows)
    ncb = GRID_W // NA_QC
    scale = hd ** -0.5
    f32 = jnp.float32
    qm, km, vm = q[:, :N_META], k[:, :N_META], v[:, :N_META]
    qg = q[:, N_META:].reshape(bsz, rows, GRID_W, n_heads, hd)
    kg = k[:, N_META:].reshape(bsz, rows, GRID_W, n_heads, hd)
    vg = v[:, N_META:].reshape(bsz, rows, GRID_W, n_heads, hd)
    qcol = np.arange(GRID_W).reshape(ncb, NA_QC)
    qcs = np.clip(qcol - NA_KC // 2, 0, GRID_W - NA_KC)
    kcol = np.clip(np.arange(ncb) * NA_QC - NA_KC // 2, 0, GRID_W - NA_KCB)[:, None] + np.arange(NA_KCB)[None, :]
    col_ok = jnp.asarray((kcol[:, None, :] >= qcs[:, :, None]) & (kcol[:, None, :] < qcs[:, :, None] + NA_KC))
    dc = jnp.asarray(np.clip(kcol[:, None, :] - qcol[:, :, None] + NA_KC - 1, 0, 2 * NA_KC - 2))
    col_idx = jnp.asarray(kcol)
    rpb32 = rpb.astype(f32)
    mb = meta_bias.astype(f32)

    def one_row(r):
        rs = jnp.clip(r - kr // 2, 0, rows - kr)
        k_rows = lax.dynamic_slice_in_dim(kg, rs, kr, axis=1)
        v_rows = lax.dynamic_slice_in_dim(vg, rs, kr, axis=1)
        kb = k_rows[:, :, col_idx]
        vb = v_rows[:, :, col_idx]
        qr = lax.dynamic_index_in_dim(qg, r, axis=1, keepdims=False).reshape(bsz, ncb, NA_QC, n_heads, hd)
        dr = rs + jnp.arange(kr) - r + NA_KR - 1
        bias = rpb32[:, dr[None, None, :, None], dc[:, :, None, :]]
        s_w = jnp.einsum('bcqhd,bicjhd->bhcqij', qr, kb).astype(f32) * scale + bias[None]
        s_w = jnp.where(col_ok[:, :, None, :], s_w, -jnp.inf).reshape(bsz, n_heads, ncb, NA_QC, kr * NA_KCB)
        s_m = jnp.einsum('bcqhd,bmhd->bhcqm', qr, km).astype(f32) * scale + mb[None, :, None, None, :]
        p = jax.nn.softmax(jnp.concatenate([s_w, s_m], axis=-1), axis=-1)
        p_w = p[..., :kr * NA_KCB].reshape(bsz, n_heads, ncb, NA_QC, kr, NA_KCB).astype(v.dtype)
        p_m = p[..., kr * NA_KCB:].astype(v.dtype)
        o = jnp.einsum('bhcqij,bicjhd->bcqhd', p_w, vb) + jnp.einsum('bhcqm,bmhd->bcqhd', p_m, vm)
        return o.reshape(bsz, GRID_W, n_heads * hd)

    o_grid = lax.map(one_row, jnp.arange(rows))
    o_grid = jnp.moveaxis(o_grid, 0, 1).reshape(bsz, n_tok, n_heads * hd)
    bias0 = rpb32[:, NA_KR - 1:NA_KR - 1 + kr, NA_KC - 1:NA_KC - 1 + NA_KC]
    k0, v0 = kg[:, :kr, :NA_KC], vg[:, :kr, :NA_KC]
    s0 = jnp.einsum('bmhd,bijhd->bhmij', qm, k0).astype(f32) * scale + bias0[None, :, None]
    smm = jnp.einsum('bmhd,bnhd->bhmn', qm, km).astype(f32) * scale + mb[None, :, None, :]
    p0 = jax.nn.softmax(jnp.concatenate([s0.reshape(bsz, n_heads, N_META, kr * NA_KC), smm], axis=-1), axis=-1)
    p0w = p0[..., :kr * NA_KC].reshape(bsz, n_heads, N_META, kr, NA_KC).astype(v.dtype)
    p0m = p0[..., kr * NA_KC:].astype(v.dtype)
    o_meta = jnp.einsum('bhmij,bijhd->bmhd', p0w, v0) + jnp.einsum('bhmn,bnhd->bmhd', p0m, vm)
    o_meta = o_meta.reshape(bsz, N_META, n_heads * hd)
    return jnp.concatenate([o_meta, o_grid], axis=1)


def _encode(x, meta_tokens, norm_mix_pre, norm_mix_post, w_in, ssd_conv_w, ssd_conv_b, ssd_dt_bias, ssd_a_log,
            ssd_d, ssd_norm_w, win_sink, na_rpb, na_meta_bias, w_out, norm_ffn_pre, norm_ffn_post, ffn_w_up,
            ffn_conv_w, ffn_conv_b, ffn_w_down):
    bsz, n_tok, _ = x.shape
    seq_len = n_tok + N_META
    meta = jnp.broadcast_to(meta_tokens.astype(x.dtype)[None], (bsz, N_META, D_MODEL))
    h = jnp.concatenate([meta, x], axis=1)
    pos = jnp.arange(seq_len)
    for i in range(DEPTH):
        a = _rmsnorm(h, norm_mix_pre[i])
        u = a @ w_in[i]
        z, xbc, dt_raw, wq, wk, wv, nq, nk, nv = _split_in_proj(u)
        y_ssd = _ssd_mixer(z, xbc, dt_raw, ssd_conv_w[i], ssd_conv_b[i], ssd_dt_bias[i], ssd_a_log[i],
                           ssd_d[i], ssd_norm_w[i])
        q = _partial_rope(wq.reshape(bsz, seq_len, WIN_Q_HEADS, HEAD_DIM), pos)
        k = _partial_rope(wk.reshape(bsz, seq_len, WIN_KV_HEADS, HEAD_DIM), pos)
        v = wv.reshape(bsz, seq_len, WIN_KV_HEADS, HEAD_DIM)
        y_win = _window_attention(q, k, v, win_sink[i])
        y_na = _na_attention(nq.reshape(bsz, seq_len, NA_HEADS, HEAD_DIM), nk.reshape(bsz, seq_len, NA_HEADS, HEAD_DIM),
                             nv.reshape(bsz, seq_len, NA_HEADS, HEAD_DIM), na_rpb[i], na_meta_bias[i])
        mix = jnp.concatenate([y_ssd, y_win, y_na], axis=-1) @ w_out[i]
        h = h + _rmsnorm(mix, norm_mix_post[i])
        f = _rmsnorm(h, norm_ffn_pre[i])
        g = _depthwise_conv(f @ ffn_w_up[i], ffn_conv_w[i], ffn_conv_b[i])
        gate, up = jnp.split(g, 2, axis=-1)
        f = (jax.nn.gelu(gate, approximate=True) * up) @ ffn_w_down[i]
        h = h + _rmsnorm(f, norm_ffn_post[i])
    return h[:, N_META:]


def setup_inputs(seed: int = 0) -> dict:
    key = jax.random.key(seed)
    ks = jax.random.split(key, 24)
    f32 = jnp.float32
    nrm = lambda k, shape, s: jax.random.normal(k, shape, f32) * s
    gain = lambda k, shape: 1.0 + 0.02 * jax.random.normal(k, shape, f32)
    dt0 = jnp.exp(jax.random.uniform(ks[8], (DEPTH, 2, SSD_HEADS), f32, math.log(1e-3), math.log(1e-1)))
    return {
        'x_prompt': nrm(ks[0], (BATCH, SEQ, D_MODEL), 1.0),
        'x_sample': nrm(ks[1], (DEC_BATCH, DEC_SEQ, D_MODEL), 1.0),
        'meta_tokens': nrm(ks[2], (N_META, D_MODEL), 1.0),
        'norm_mix_pre': gain(ks[3], (DEPTH, D_MODEL)),
        'norm_mix_post': gain(ks[4], (DEPTH, D_MODEL)),
        'w_in': nrm(ks[5], (DEPTH, D_MODEL, D_IN_PROJ), D_MODEL ** -0.5),
        'ssd_conv_w': nrm(ks[6], (DEPTH, SSD_CONV, SSD_XBC), SSD_CONV ** -0.5),
        'ssd_conv_b': nrm(ks[7], (DEPTH, SSD_XBC), 0.01),
        'ssd_dt_bias': dt0 + jnp.log(-jnp.expm1(-dt0)),
        'ssd_a_log': jnp.log(jax.random.uniform(ks[9], (DEPTH, 2, SSD_HEADS), f32, 1.0, 16.0)),
        'ssd_d': gain(ks[10], (DEPTH, SSD_HEADS)),
        'ssd_norm_w': gain(ks[11], (DEPTH, SSD_INNER)),
        'win_sink': nrm(ks[12], (DEPTH, WIN_Q_HEADS), 0.5),
        'na_rpb': nrm(ks[13], (DEPTH, NA_HEADS, 2 * NA_KR - 1, 2 * NA_KC - 1), 0.02),
        'na_meta_bias': nrm(ks[14], (DEPTH, NA_HEADS, N_META), 0.02),
        'w_out': nrm(ks[15], (DEPTH, D_MIX, D_MODEL), D_MIX ** -0.5),
        'norm_ffn_pre': gain(ks[16], (DEPTH, D_MODEL)),
        'norm_ffn_post': gain(ks[17], (DEPTH, D_MODEL)),
        'ffn_w_up': nrm(ks[18], (DEPTH, D_MODEL, 2 * D_FF), D_MODEL ** -0.5),
        'ffn_conv_w': nrm(ks[19], (DEPTH, FFN_CONV, 2 * D_FF), FFN_CONV ** -0.5),
        'ffn_conv_b': nrm(ks[20], (DEPTH, 2 * D_FF), 0.01),
        'ffn_w_down': nrm(ks[21], (DEPTH, D_FF, D_MODEL), D_FF ** -0.5),
    }


def reference(x_prompt, x_sample, meta_tokens, norm_mix_pre, norm_mix_post, w_in, ssd_conv_w, ssd_conv_b,
              ssd_dt_bias, ssd_a_log, ssd_d, ssd_norm_w, win_sink, na_rpb, na_meta_bias, w_out, norm_ffn_pre,
              norm_ffn_post, ffn_w_up, ffn_conv_w, ffn_conv_b, ffn_w_down):
    y_prompt = _encode(x_prompt, meta_tokens, norm_mix_pre, norm_mix_post, w_in, ssd_conv_w, ssd_conv_b, ssd_dt_bias,
                       ssd_a_log, ssd_d, ssd_norm_w, win_sink, na_rpb, na_meta_bias, w_out, norm_ffn_pre,
                       norm_ffn_post, ffn_w_up, ffn_conv_w, ffn_conv_b, ffn_w_down)
    y_sample = _encode(x_sample, meta_tokens, norm_mix_pre, norm_mix_post, w_in, ssd_conv_w, ssd_conv_b, ssd_dt_bias,
                       ssd_a_log, ssd_d, ssd_norm_w, win_sink, na_rpb, na_meta_bias, w_out, norm_ffn_pre,
                       norm_ffn_post, ffn_w_up, ffn_conv_w, ffn_conv_b, ffn_w_down)
    return (y_prompt, y_sample)
```

```python
import functools

import jax
import jax.numpy as jnp
import numpy as np
from jax import lax
from jax.experimental import pallas as pl
from jax.experimental.pallas import tpu as pltpu

F32 = jnp.float32
BF16 = jnp.bfloat16

D_MODEL = 1024
N_META = 16
GRID_W = 64
HEAD_DIM = 64

SSD_HEADS = 16
SSD_INNER = SSD_HEADS * HEAD_DIM
SSD_GROUPS = 2
SSD_STATE = 128
SSD_XBC = SSD_INNER + 2 * SSD_GROUPS * SSD_STATE
SSD_CONV = 5

WIN_Q_HEADS = 8
WIN_KV_HEADS = 2
WIN_RADIUS = 128
ROPE_THETA = 500000.0
ROPE_DIM = HEAD_DIM // 4

NA_HEADS = 8
NA_KR = 8
NA_KC = 16

D_FF = 2816
EPS = 1e-6

BLK = 128
LEAD = BLK
META0 = LEAD - N_META
HALO = 8
TM = 384
FF_CHUNK = 256
VMEM_LIMIT = 56 * 1024 * 1024


def _cparams(sem):
    return pltpu.CompilerParams(dimension_semantics=sem, vmem_limit_bytes=VMEM_LIMIT)


def _rms(x, w):
    return x * lax.rsqrt(jnp.mean(x * x, axis=-1, keepdims=True) + EPS) * w


def _dot(a, b):
    return jnp.dot(a, b, preferred_element_type=F32)


def _dot_nt(a, b):
    return lax.dot_general(a, b, (((1,), (1,)), ((), ())), preferred_element_type=F32)


def _rope128(x, c, s1, s2):
    return x * c + pltpu.roll(x, 128 - ROPE_DIM // 2, 1) * s1 + pltpu.roll(x, ROPE_DIM // 2, 1) * s2


def _inproj_kernel(h_ref, nw_ref, wz_ref, wx_ref, wdt_ref, wwq_ref, wwkv_ref, wnq_ref, wnk_ref,
                   wnv_ref, rc_ref, rs1_ref, rs2_ref,
                   z_ref, xbc_ref, dt_ref, wq_ref, wkv_ref, nq_ref, nk_ref, nv_ref):
    a = _rms(h_ref[0], nw_ref[...]).astype(BF16)
    z_ref[0] = _dot(a, wz_ref[...]).astype(BF16)
    xbc_ref[0] = _dot(a, wx_ref[...]).astype(BF16)
    dt_ref[0] = _dot(a, wdt_ref[...])
    c, s1, s2 = rc_ref[...], rs1_ref[...], rs2_ref[...]
    scale = HEAD_DIM ** -0.5
    q = _dot(a, wwq_ref[...])
    q = jnp.concatenate([_rope128(q[:, j * 128:(j + 1) * 128], c, s1, s2) for j in range(4)], axis=1)
    wq_ref[0] = (q * scale).astype(BF16)
    kv = _dot(a, wwkv_ref[...])
    kv = jnp.concatenate([_rope128(kv[:, :128], c, s1, s2), kv[:, 128:]], axis=1)
    wkv_ref[0] = kv.astype(BF16)
    nq_ref[0] = (_dot(a, wnq_ref[...]) * scale).astype(BF16)
    nk_ref[0] = _dot(a, wnk_ref[...]).astype(BF16)
    nv_ref[0] = _dot(a, wnv_ref[...]).astype(BF16)


def _inproj(h, nw, ws, rope):
    bsz, lp, _ = h.shape
    tile = lambda n: pl.BlockSpec((1, TM, n), lambda b, i: (b, i, 0))
    full = lambda arr: pl.BlockSpec(arr.shape, lambda b, i: (0,) * arr.ndim)
    rtab = pl.BlockSpec((TM, 128), lambda b, i: (i, 0))
    widths = [SSD_INNER, SSD_XBC, 128, 512, 256, 512, 512, 512]
    dtypes = [BF16, BF16, F32, BF16, BF16, BF16, BF16, BF16]
    return pl.pallas_call(
        _inproj_kernel,
        grid=(bsz, lp // TM),
        in_specs=[tile(D_MODEL), full(nw)] + [full(w) for w in ws] + [rtab, rtab, rtab],
        out_specs=[tile(n) for n in widths],
        out_shape=[jax.ShapeDtypeStruct((bsz, lp, n), d) for n, d in zip(widths, dtypes)],
        compiler_params=_cparams(("parallel", "parallel")),
        name="inproj",
    )(h, nw, *ws, *rope)


def _split3(x):
    hi = x.astype(BF16)
    r1 = x - hi.astype(F32)
    mid = r1.astype(BF16)
    lo = (r1 - mid.astype(F32)).astype(BF16)
    return hi, mid, lo


def _ssd_kernel(*refs, reverse, nc, final):
    if final:
        (xc_ref, xp_ref, xn_ref, dt_ref, cw_ref, cb_ref, dtb_ref, alog_ref,
         z_ref, yo_ref, dsk_ref, nw_ref, o_ref, state_ref, xbuf_ref, ybuf_ref) = refs
    else:
        (xc_ref, xp_ref, xn_ref, dt_ref, cw_ref, cb_ref, dtb_ref, alog_ref,
         o_ref, state_ref, xbuf_ref) = refs
    step = pl.program_id(1)
    chunk = (nc - 1 - step) if reverse else step

    @pl.when(step == 0)
    def _():
        state_ref[...] = jnp.zeros_like(state_ref)

    row = lax.broadcasted_iota(jnp.int32, (BLK, 1), 0)
    live = jnp.logical_or(chunk > 0, row >= META0)
    xbuf_ref[0:HALO, :] = jnp.where(chunk > 0, xp_ref[0].astype(F32), 0.0)
    xbuf_ref[HALO:HALO + BLK, :] = jnp.where(live, xc_ref[0].astype(F32), 0.0)
    xbuf_ref[HALO + BLK:, :] = jnp.where(chunk < nc - 1, xn_ref[0].astype(F32), 0.0)
    pad = SSD_CONV // 2
    acc = xbuf_ref[pl.ds(HALO - pad, BLK), :] * cw_ref[0:1, :] + cb_ref[...]
    for j in range(1, SSD_CONV):
        acc = acc + xbuf_ref[pl.ds(HALO - pad + j, BLK), :] * cw_ref[j:j + 1, :]
    xcv = acc * (1.0 / (1.0 + jnp.exp(-acc)))

    dtr = dt_ref[0] + dtb_ref[...]
    dt = jnp.maximum(dtr, 0.0) + jnp.log1p(jnp.exp(-jnp.abs(dtr)))
    dt = jnp.where(live, dt, 0.0)
    a = dt * (-jnp.exp(alog_ref[...]))

    li = lax.broadcasted_iota(jnp.int32, (BLK, BLK), 0)
    si = lax.broadcasted_iota(jnp.int32, (BLK, BLK), 1)
    causal = (si >= li) if reverse else (si <= li)
    tri = causal.astype(BF16)
    acs = sum(_dot(tri, part) for part in _split3(a))
    acs_t = acs.T
    edge = acs[0:1, :] if reverse else acs[BLK - 1:BLK, :]
    e_in = jnp.exp(acs)
    dt_out = dt * jnp.exp(edge - acs)
    e_chunk = jnp.exp(edge)

    hoff = SSD_HEADS if reverse else 0
    lane = lax.broadcasted_iota(jnp.int32, (1, 2 * HEAD_DIM), 1)
    left = lane < HEAD_DIM
    rep = SSD_HEADS // SSD_GROUPS
    for g in range(SSD_GROUPS):
        bm = xcv[:, SSD_INNER + g * SSD_STATE:SSD_INNER + (g + 1) * SSD_STATE]
        cm = xcv[:, SSD_INNER + (SSD_GROUPS + g) * SSD_STATE:SSD_INNER + (SSD_GROUPS + g + 1) * SSD_STATE]
        cb = _dot_nt(cm.astype(BF16), bm.astype(BF16))
        bm_t = bm.T.astype(BF16)
        h_all = state_ref[g]
        h_bf = h_all.astype(BF16)
        x_out = []
        decay_lanes = []
        for j in range(rep // 2):
            h0 = hoff + g * rep + 2 * j
            col = (g * rep + 2 * j) * HEAD_DIM
            xs = xcv[:, col:col + 2 * HEAD_DIM]
            x_in = (xs * jnp.where(left, dt[:, h0:h0 + 1], dt[:, h0 + 1:h0 + 2])).astype(BF16)
            x_out.append((xs * jnp.where(left, dt_out[:, h0:h0 + 1], dt_out[:, h0 + 1:h0 + 2])).astype(BF16))
            decay_lanes.append(jnp.where(left, e_chunk[:, h0:h0 + 1], e_chunk[:, h0 + 1:h0 + 2]))
            hp = h_bf[:, 2 * j * HEAD_DIM:(2 * j + 2) * HEAD_DIM]
            ys = []
            for hh in (h0, h0 + 1):
                diff = acs[:, hh:hh + 1] - acs_t[hh:hh + 1, :]
                lmat = (cb * jnp.exp(jnp.where(causal, diff, -jnp.inf))).astype(BF16)
                cin = (cm * e_in[:, hh:hh + 1]).astype(BF16)
                ys.append(_dot(lmat, x_in) + _dot(cin, hp))
            y = jnp.where(left, ys[0], ys[1])
            if final:
                ybuf_ref[:, col:col + 2 * HEAD_DIM] = y
            else:
                o_ref[0, :, col:col + 2 * HEAD_DIM] = y
        s_new = _dot(bm_t, jnp.concatenate(x_out, axis=1))
        state_ref[g] = h_all * jnp.concatenate(decay_lanes, axis=1) + s_new

    if final:
        xs_all = xcv[:, :SSD_INNER]
        y = ybuf_ref[...] + yo_ref[0] + dsk_ref[...] * xs_all
        z = z_ref[0].astype(F32)
        y = y * (z * (1.0 / (1.0 + jnp.exp(-z))))
        o_ref[0] = _rms(y, nw_ref[...]).astype(BF16)


def _ssd_pass(xbc, dt, consts, final_args, *, reverse):
    bsz, lp, _ = xbc.shape
    nc = lp // BLK
    hb = BLK // HALO
    order = (lambda c: nc - 1 - c) if reverse else (lambda c: c)
    final = final_args is not None
    cur = lambda n: pl.BlockSpec((1, BLK, n), lambda b, c: (b, order(c), 0))
    prev = pl.BlockSpec((1, HALO, SSD_XBC), lambda b, c: (b, jnp.maximum(order(c) * hb - 1, 0), 0))
    nxt = pl.BlockSpec((1, HALO, SSD_XBC), lambda b, c: (b, jnp.minimum((order(c) + 1) * hb, nc * hb - 1), 0))
    full = lambda arr: pl.BlockSpec(arr.shape, lambda b, c: (0,) * arr.ndim)
    in_specs = [cur(SSD_XBC), prev, nxt, cur(128)] + [full(x) for x in consts]
    args = [xbc, xbc, xbc, dt] + list(consts)
    scratch = [pltpu.VMEM((SSD_GROUPS, SSD_STATE, SSD_INNER // SSD_GROUPS), F32),
               pltpu.VMEM((BLK + 2 * HALO, SSD_XBC), F32)]
    if final:
        z, y_other, dsk, nw = final_args
        in_specs += [cur(SSD_INNER), cur(SSD_INNER), full(dsk), full(nw)]
        args += [z, y_other, dsk, nw]
        scratch.append(pltpu.VMEM((BLK, SSD_INNER), F32))
    return pl.pallas_call(
        functools.partial(_ssd_kernel, reverse=reverse, nc=nc, final=final),
        grid=(bsz, nc),
        in_specs=in_specs,
        out_specs=cur(SSD_INNER),
        out_shape=jax.ShapeDtypeStruct((bsz, lp, SSD_INNER), BF16 if final else F32),
        scratch_shapes=scratch,
        compiler_params=_cparams(("parallel", "arbitrary")),
        name="ssd_bwd" if reverse else "ssd_fwd",
    )(*args)


def _softmax_pv(scores, values, extra_logit=None):
    mx = functools.reduce(jnp.maximum, [jnp.max(s, axis=-1, keepdims=True) for s in scores])
    if extra_logit is not None:
        mx = jnp.maximum(mx, extra_logit)
    ps = [jnp.exp(s - mx) for s in scores]
    den = functools.reduce(jnp.add, [jnp.sum(p, axis=-1, keepdims=True) for p in ps])
    if extra_logit is not None:
        den = den + jnp.exp(extra_logit - mx)
    out = functools.reduce(jnp.add, [_dot(p.astype(BF16), v) for p, v in zip(ps, values)])
    return out / den


def _win_kernel(q_ref, kvp_ref, kvc_ref, kvn_ref, kvm_ref, sink_ref, o_ref, *, nb):
    n = pl.program_id(1)
    qi = lax.broadcasted_iota(jnp.int32, (BLK, BLK), 0)
    ki = lax.broadcasted_iota(jnp.int32, (BLK, BLK), 1)
    ok_prev = jnp.logical_and(ki >= qi, n >= 2)
    ok_cur = n >= 1
    ok_next = jnp.logical_and(ki <= qi, n + 1 <= nb - 1)
    rep = WIN_Q_HEADS // WIN_KV_HEADS
    vo = WIN_KV_HEADS * HEAD_DIM
    outs = []
    for g in range(WIN_KV_HEADS):
        ks = slice(g * HEAD_DIM, (g + 1) * HEAD_DIM)
        vs = slice(vo + g * HEAD_DIM, vo + (g + 1) * HEAD_DIM)
        kp, kc, kn, km = kvp_ref[0, :, ks], kvc_ref[0, :, ks], kvn_ref[0, :, ks], kvm_ref[0, META0:, ks]
        vals = [kvp_ref[0, :, vs], kvc_ref[0, :, vs], kvn_ref[0, :, vs], kvm_ref[0, META0:, vs]]
        for r in range(rep):
            h = g * rep + r
            qh = q_ref[0, :, h * HEAD_DIM:(h + 1) * HEAD_DIM]
            scores = [jnp.where(ok_prev, _dot_nt(qh, kp), -jnp.inf),
                      jnp.where(ok_cur, _dot_nt(qh, kc), -jnp.inf),
                      jnp.where(ok_next, _dot_nt(qh, kn), -jnp.inf),
                      _dot_nt(qh, km)]
            outs.append(_softmax_pv(scores, vals, extra_logit=sink_ref[h]))
    o_ref[0] = jnp.concatenate(outs, axis=1).astype(BF16)


def _win_attention(q, kv, sink):
    bsz, lp, _ = q.shape
    nb = lp // BLK
    kvw = 2 * WIN_KV_HEADS * HEAD_DIM
    kvspec = lambda f: pl.BlockSpec((1, BLK, kvw), lambda b, n: (b, f(n), 0))
    return pl.pallas_call(
        functools.partial(_win_kernel, nb=nb),
        grid=(bsz, nb),
        in_specs=[pl.BlockSpec((1, BLK, WIN_Q_HEADS * HEAD_DIM), lambda b, n: (b, n, 0)),
                  kvspec(lambda n: jnp.maximum(n - 1, 0)), kvspec(lambda n: n),
                  kvspec(lambda n: jnp.minimum(n + 1, nb - 1)), kvspec(lambda n: 0),
                  pl.BlockSpec(memory_space=pltpu.SMEM)],
        out_specs=pl.BlockSpec((1, BLK, WIN_Q_HEADS * HEAD_DIM), lambda b, n: (b, n, 0)),
        out_shape=jax.ShapeDtypeStruct((bsz, lp, WIN_Q_HEADS * HEAD_DIM), BF16),
        compiler_params=_cparams(("parallel", "parallel")),
        name="win_attn",
    )(q, kv, kv, kv, kv, sink)


NA_ROWS_PER_BLK = BLK // GRID_W
NA_KEYS = NA_KR * GRID_W
NA_BIAS_ROWS = 2 * NA_KR - 2


def _na_bias_table(rpb):
    c = np.arange(GRID_W)[:, None]
    kc = np.arange(GRID_W)[None, :]
    cs = np.clip(c - NA_KC // 2, 0, GRID_W - NA_KC)
    ok = (kc >= cs) & (kc < cs + NA_KC)
    dc = np.clip(kc - c + NA_KC - 1, 0, 2 * NA_KC - 2)
    t = jnp.where(jnp.asarray(ok)[None, None], rpb.astype(F32)[:, :, dc], -jnp.inf)
    return jnp.concatenate([t[:, :-1], t[:, 1:]], axis=-1)


def _na_kernel(q_ref, k_ref, v_ref, bias_ref, mb_ref, o_ref, *, rows):
    n = pl.program_id(1)

    def attend(qh, h, start, delta, qrows):
        kw = k_ref[0, pl.ds(start, NA_KEYS), h * HEAD_DIM:(h + 1) * HEAD_DIM]
        vw = v_ref[0, pl.ds(start, NA_KEYS), h * HEAD_DIM:(h + 1) * HEAD_DIM]
        km = k_ref[0, META0:LEAD, h * HEAD_DIM:(h + 1) * HEAD_DIM]
        vm = v_ref[0, META0:LEAD, h * HEAD_DIM:(h + 1) * HEAD_DIM]
        s = _dot_nt(qh, kw)
        pieces = []
        for jj in range(NA_KR // 2):
            b = bias_ref[h, delta + (NA_KR - 1) + 2 * jj]
            if qrows == 1:
                b = b[0:1, :]
            pieces.append(s[:, jj * 128:(jj + 1) * 128] + b)
        sm = _dot_nt(qh, km) + mb_ref[h:h + 1, :]
        return _softmax_pv(pieces + [sm], [vw[jj * 128:(jj + 1) * 128] for jj in range(NA_KR // 2)] + [vm])

    @pl.when(n == 0)
    def _():
        outs = [attend(q_ref[0, META0:, h * HEAD_DIM:(h + 1) * HEAD_DIM], h, LEAD, 0, 1)
                for h in range(NA_HEADS)]
        o_ref[0, :META0, :] = jnp.zeros((META0, NA_HEADS * HEAD_DIM), BF16)
        o_ref[0, META0:, :] = jnp.concatenate(outs, axis=1).astype(BF16)

    @pl.when(n > 0)
    def _():
        for j in range(NA_ROWS_PER_BLK):
            r = (n - 1) * NA_ROWS_PER_BLK + j
            rs = jnp.clip(r - NA_KR // 2, 0, rows - NA_KR)
            start = pl.multiple_of(LEAD + rs * GRID_W, GRID_W)
            outs = [attend(q_ref[0, j * GRID_W:(j + 1) * GRID_W, h * HEAD_DIM:(h + 1) * HEAD_DIM],
                           h, start, rs - r, GRID_W) for h in range(NA_HEADS)]
            o_ref[0, j * GRID_W:(j + 1) * GRID_W, :] = jnp.concatenate(outs, axis=1).astype(BF16)


def _na_attention(q, k, v, bias, mb):
    bsz, lp, width = q.shape
    nb = lp // BLK
    rows = (lp - LEAD) // GRID_W
    blk = pl.BlockSpec((1, BLK, width), lambda b, n: (b, n, 0))
    seq = pl.BlockSpec((1, lp, width), lambda b, n: (b, 0, 0), pipeline_mode=pl.Buffered(1))
    full = lambda arr: pl.BlockSpec(arr.shape, lambda b, n: (0,) * arr.ndim)
    return pl.pallas_call(
        functools.partial(_na_kernel, rows=rows),
        grid=(bsz, nb),
        in_specs=[blk, seq, seq, full(bias), full(mb)],
        out_specs=blk,
        out_shape=jax.ShapeDtypeStruct((bsz, lp, width), BF16),
        compiler_params=_cparams(("parallel", "arbitrary")),
        name="na_attn",
    )(q, k, v, bias, mb)


def _live_rows(tile_rows, tile_index):
    row = tile_index * tile_rows + lax.broadcasted_iota(jnp.int32, (tile_rows, 1), 0)
    return row >= META0


def _outproj_kernel(h_ref, ys_ref, yw_ref, yn_ref, w1_ref, w2_ref, w3_ref, nw_ref, o_ref):
    mix = _dot(ys_ref[0], w1_ref[...]) + _dot(yw_ref[0], w2_ref[...]) + _dot(yn_ref[0], w3_ref[...])
    out = h_ref[0] + _rms(mix, nw_ref[...])
    o_ref[0] = jnp.where(_live_rows(TM, pl.program_id(1)), out, 0.0)


def _outproj(h, y_ssd, y_win, y_na, w1, w2, w3, nw):
    bsz, lp, _ = h.shape
    tile = lambda n: pl.BlockSpec((1, TM, n), lambda b, i: (b, i, 0))
    full = lambda arr: pl.BlockSpec(arr.shape, lambda b, i: (0,) * arr.ndim)
    return pl.pallas_call(
        _outproj_kernel,
        grid=(bsz, lp // TM),
        in_specs=[tile(D_MODEL), tile(y_ssd.shape[-1]), tile(y_win.shape[-1]), tile(y_na.shape[-1]),
                  full(w1), full(w2), full(w3), full(nw)],
        out_specs=tile(D_MODEL),
        out_shape=jax.ShapeDtypeStruct(h.shape, F32),
        compiler_params=_cparams(("parallel", "parallel")),
        name="outproj",
    )(h, y_ssd, y_win, y_na, w1, w2, w3, nw)


def _ffn_kernel(hc_ref, hp_ref, hn_ref, nw1_ref, wup_ref, cw_ref, cb_ref, wdn_ref, nw2_ref, o_ref,
                gbuf_ref, ubuf_ref, acc_ref, *, nt):
    i = pl.program_id(1)
    nw1 = nw1_ref[...]
    f = jnp.concatenate([_rms(hp_ref[0], nw1), _rms(hc_ref[0], nw1), _rms(hn_ref[0], nw1)], axis=0).astype(BF16)
    tail = jnp.where(i < nt - 1, 1.0, 0.0)
    nch = D_FF // FF_CHUNK
    acc_ref[...] = jnp.zeros_like(acc_ref)

    def conv(buf_ref, g, c):
        buf_ref[...] = g
        buf_ref[HALO + TM:, :] = g[HALO + TM:, :] * tail
        out = buf_ref[pl.ds(HALO - 1, TM), :] * cw_ref[c, 0:1, :] + cb_ref[c]
        out = out + buf_ref[pl.ds(HALO, TM), :] * cw_ref[c, 1:2, :]
        return out + buf_ref[pl.ds(HALO + 1, TM), :] * cw_ref[c, 2:3, :]

    def body(c, carry):
        gate = conv(gbuf_ref, _dot(f, wup_ref[c]), c)
        up = conv(ubuf_ref, _dot(f, wup_ref[nch + c]), nch + c)
        act = (jax.nn.gelu(gate, approximate=True) * up).astype(BF16)
        acc_ref[...] += _dot(act, wdn_ref[c])
        return carry

    lax.fori_loop(0, nch, body, 0)
    out = hc_ref[0] + _rms(acc_ref[...], nw2_ref[...])
    o_ref[0] = jnp.where(_live_rows(TM, i), out, 0.0)


def _ffn(h, nw1, wup, cw, cb, wdn, nw2):
    bsz, lp, _ = h.shape
    nt = lp // TM
    hb = TM // HALO
    full = lambda arr: pl.BlockSpec(arr.shape, lambda b, i: (0,) * arr.ndim)
    tile = pl.BlockSpec((1, TM, D_MODEL), lambda b, i: (b, i, 0))
    prev = pl.BlockSpec((1, HALO, D_MODEL), lambda b, i: (b, jnp.maximum(i * hb - 1, 0), 0))
    nxt = pl.BlockSpec((1, HALO, D_MODEL), lambda b, i: (b, jnp.minimum((i + 1) * hb, nt * hb - 1), 0))
    return pl.pallas_call(
        functools.partial(_ffn_kernel, nt=nt),
        grid=(bsz, nt),
        in_specs=[tile, prev, nxt, full(nw1), full(wup), full(cw), full(cb), full(wdn), full(nw2)],
        out_specs=tile,
        out_shape=jax.ShapeDtypeStruct(h.shape, F32),
        scratch_shapes=[pltpu.VMEM((TM + 2 * HALO, FF_CHUNK), F32),
                        pltpu.VMEM((TM + 2 * HALO, FF_CHUNK), F32),
                        pltpu.VMEM((TM, D_MODEL), F32)],
        compiler_params=_cparams(("parallel", "parallel")),
        name="ffn",
    )(h, h, h, nw1, wup, cw, cb, wdn, nw2)


def _pad_lanes(x, n):
    return jnp.pad(x, [(0, 0)] * (x.ndim - 1) + [(0, n - x.shape[-1])])


def _rope_tables(lp):
    half = ROPE_DIM // 2
    pos = jnp.maximum(jnp.arange(lp) - META0, 0).astype(F32)
    inv = jnp.power(ROPE_THETA, -jnp.arange(half, dtype=F32) / half)
    ang = pos[:, None] * inv[None, :]
    cos, sin = jnp.cos(ang), jnp.sin(ang)
    zeros, ones = jnp.zeros_like(cos), jnp.ones((lp, HEAD_DIM - ROPE_DIM), F32)
    rest = jnp.zeros((lp, HEAD_DIM - ROPE_DIM), F32)
    c = jnp.concatenate([cos, cos, ones], axis=1)
    s1 = jnp.concatenate([-sin, zeros, rest], axis=1)
    s2 = jnp.concatenate([zeros, sin, rest], axis=1)
    return tuple(jnp.tile(t, (1, 128 // HEAD_DIM)) for t in (c, s1, s2))


def _layer_params(i, p):
    row = lambda v: v.reshape(1, -1).astype(F32)
    sizes = [SSD_INNER, SSD_XBC, 2 * SSD_HEADS, WIN_Q_HEADS * HEAD_DIM, 2 * WIN_KV_HEADS * HEAD_DIM,
             NA_HEADS * HEAD_DIM, NA_HEADS * HEAD_DIM, NA_HEADS * HEAD_DIM]
    w_in = p['w_in'][i].astype(BF16)
    ws = jnp.split(w_in, np.cumsum(sizes)[:-1].tolist(), axis=1)
    ws[2] = _pad_lanes(ws[2], 128)
    w_out = p['w_out'][i].astype(BF16)
    nch = D_FF // FF_CHUNK
    chunked = lambda m: m.reshape(m.shape[0], 2 * nch, FF_CHUNK).swapaxes(0, 1)
    return dict(
        norm_mix_pre=row(p['norm_mix_pre'][i]),
        w_in=ws,
        ssd_consts=(p['ssd_conv_w'][i].astype(F32), row(p['ssd_conv_b'][i]),
                    _pad_lanes(row(p['ssd_dt_bias'][i]), 128), _pad_lanes(row(p['ssd_a_log'][i]), 128)),
        ssd_d=row(jnp.repeat(p['ssd_d'][i], HEAD_DIM)),
        ssd_norm_w=row(p['ssd_norm_w'][i]),
        win_sink=p['win_sink'][i].astype(F32),
        na_bias=_na_bias_table(p['na_rpb'][i]),
        na_meta_bias=p['na_meta_bias'][i].astype(F32),
        w_out=(w_out[:SSD_INNER], w_out[SSD_INNER:SSD_INNER + WIN_Q_HEADS * HEAD_DIM],
               w_out[SSD_INNER + WIN_Q_HEADS * HEAD_DIM:]),
        norm_mix_post=row(p['norm_mix_post'][i]),
        norm_ffn_pre=row(p['norm_ffn_pre'][i]),
        ffn_w_up=chunked(p['ffn_w_up'][i].astype(BF16)),
        ffn_conv_w=chunked(p['ffn_conv_w'][i].astype(F32)),
        ffn_conv_b=chunked(p['ffn_conv_b'][i].astype(F32).reshape(1, -1)),
        ffn_w_down=p['ffn_w_down'][i].astype(BF16).reshape(nch, FF_CHUNK, D_MODEL),
        norm_ffn_post=row(p['norm_ffn_post'][i]),
    )


def _encode(x, meta_tokens, layers):
    bsz, n_tok, _ = x.shape
    assert n_tok % TM == 0 or (n_tok + LEAD) % TM == 0
    assert (n_tok + LEAD) % TM == 0 and n_tok % BLK == 0 and n_tok // GRID_W >= NA_KR
    lp = LEAD + n_tok
    meta = jnp.broadcast_to(meta_tokens.astype(F32)[None], (bsz, N_META, D_MODEL))
    h = jnp.concatenate([jnp.zeros((bsz, META0, D_MODEL), F32), meta, x.astype(F32)], axis=1)
    rope = _rope_tables(lp)
    for lw in layers:
        z, xbc, dt, wq, wkv, nq, nk, nv = _inproj(h, lw['norm_mix_pre'], lw['w_in'], rope)
        y_fwd = _ssd_pass(xbc, dt, lw['ssd_consts'], None, reverse=False)
        y_ssd = _ssd_pass(xbc, dt, lw['ssd_consts'], (z, y_fwd, lw['ssd_d'], lw['ssd_norm_w']), reverse=True)
        y_win = _win_attention(wq, wkv, lw['win_sink'])
        y_na = _na_attention(nq, nk, nv, lw['na_bias'], lw['na_meta_bias'])
        h = _outproj(h, y_ssd, y_win, y_na, *lw['w_out'], lw['norm_mix_post'])
        h = _ffn(h, lw['norm_ffn_pre'], lw['ffn_w_up'], lw['ffn_conv_w'], lw['ffn_conv_b'],
                 lw['ffn_w_down'], lw['norm_ffn_post'])
    return h[:, LEAD:]


def kernel(x_prompt, x_sample, meta_tokens, norm_mix_pre, norm_mix_post, w_in, ssd_conv_w, ssd_conv_b,
           ssd_dt_bias, ssd_a_log, ssd_d, ssd_norm_w, win_sink, na_rpb, na_meta_bias, w_out, norm_ffn_pre,
           norm_ffn_post, ffn_w_up, ffn_conv_w, ffn_conv_b, ffn_w_down):
    p = dict(norm_mix_pre=norm_mix_pre, norm_mix_post=norm_mix_post, w_in=w_in, ssd_conv_w=ssd_conv_w,
             ssd_conv_b=ssd_conv_b, ssd_dt_bias=ssd_dt_bias, ssd_a_log=ssd_a_log, ssd_d=ssd_d,
             ssd_norm_w=ssd_norm_w, win_sink=win_sink, na_rpb=na_rpb, na_meta_bias=na_meta_bias, w_out=w_out,
             norm_ffn_pre=norm_ffn_pre, norm_ffn_post=norm_ffn_post, ffn_w_up=ffn_w_up, ffn_conv_w=ffn_conv_w,
             ffn_conv_b=ffn_conv_b, ffn_w_down=ffn_w_down)
    layers = [_layer_params(i, p) for i in range(w_in.shape[0])]
    return (_encode(x_prompt, meta_tokens, layers), _encode(x_sample, meta_tokens, layers))
```

```python
import functools

import jax
import jax.numpy as jnp
import numpy as np
from jax import lax
from jax.experimental import pallas as pl
from jax.experimental.pallas import tpu as pltpu

F32 = jnp.float32
BF16 = jnp.bfloat16

D_MODEL = 1024
N_META = 16
GRID_W = 64
HEAD_DIM = 64

SSD_HEADS = 16
SSD_INNER = SSD_HEADS * HEAD_DIM
SSD_GROUPS = 2
SSD_STATE = 128
SSD_XBC = SSD_INNER + 2 * SSD_GROUPS * SSD_STATE
SSD_CONV = 5

WIN_Q_HEADS = 8
WIN_KV_HEADS = 2
WIN_RADIUS = 128
ROPE_THETA = 500000.0
ROPE_DIM = HEAD_DIM // 4

NA_HEADS = 8
NA_KR = 8
NA_KC = 16

D_FF = 2816
EPS = 1e-6
GELU_K = float(np.sqrt(2.0 / np.pi))
GELU_C = 0.044715

BLK = 128
LEAD = BLK
META0 = LEAD - N_META
HALO = 8
TM = 384
FF_CHUNK = 256
VMEM_LIMIT = 56 * 1024 * 1024


def _cparams(sem):
    return pltpu.CompilerParams(dimension_semantics=sem, vmem_limit_bytes=VMEM_LIMIT)


def _rms(x, w):
    return x * lax.rsqrt(jnp.mean(x * x, axis=-1, keepdims=True) + EPS) * w


def _dot(a, b):
    return jnp.dot(a, b, preferred_element_type=F32)


def _dot_nt(a, b):
    return lax.dot_general(a, b, (((1,), (1,)), ((), ())), preferred_element_type=F32)


def _rope128(x, c, s1, s2):
    return x * c + pltpu.roll(x, 128 - ROPE_DIM // 2, 1) * s1 + pltpu.roll(x, ROPE_DIM // 2, 1) * s2


def _inproj_kernel(h_ref, hp_ref, hn_ref, nw_ref, wz_ref, wx_ref, wdt_ref, wwq_ref, wwk_ref, wwv_ref,
                   wnq_ref, wnk_ref, wnv_ref, cw_ref, cb_ref, rc_ref, rs1_ref, rs2_ref, rct_ref, rst_ref,
                   z_ref, xbc_ref, dt_ref, wq_ref, wk_ref, wv_ref, nq_ref, nk_ref, nv_ref, xbuf_ref, *, nt):
    nw = nw_ref[...]
    a = _rms(h_ref[0], nw).astype(BF16)
    tail = jnp.where(pl.program_id(1) < nt - 1, 1.0, 0.0)
    wx = wx_ref[...]
    xbuf_ref[0:HALO, :] = _dot(_rms(hp_ref[0], nw).astype(BF16), wx)
    xbuf_ref[HALO:HALO + TM, :] = _dot(a, wx)
    xbuf_ref[HALO + TM:, :] = _dot(_rms(hn_ref[0], nw).astype(BF16), wx) * tail
    pad = SSD_CONV // 2
    acc = xbuf_ref[pl.ds(HALO - pad, TM), :] * cw_ref[0:1, :] + cb_ref[...]
    for j in range(1, SSD_CONV):
        acc = acc + xbuf_ref[pl.ds(HALO - pad + j, TM), :] * cw_ref[j:j + 1, :]
    xbc_ref[0] = (acc * (1.0 / (1.0 + jnp.exp(-acc)))).astype(BF16)
    z_ref[0] = _dot(a, wz_ref[...]).astype(BF16)
    dt_ref[0] = _dot(a, wdt_ref[...])

    def put_blocks(ref, xt):
        for j in range(TM // BLK):
            ref[0, j] = xt[:, j * BLK:(j + 1) * BLK].astype(BF16)

    scale = HEAD_DIM ** -0.5
    qt = _dot_nt(wwq_ref[...], a)
    cos_t, sin_t = rct_ref[...], rst_ref[...]
    half = ROPE_DIM // 2
    parts = []
    for hh in range(WIN_Q_HEADS):
        x1 = qt[hh * HEAD_DIM:hh * HEAD_DIM + half]
        x2 = qt[hh * HEAD_DIM + half:hh * HEAD_DIM + ROPE_DIM]
        parts += [x1 * cos_t - x2 * sin_t, x2 * cos_t + x1 * sin_t, qt[hh * HEAD_DIM + ROPE_DIM:(hh + 1) * HEAD_DIM]]
    put_blocks(wq_ref, jnp.concatenate(parts, axis=0) * scale)
    wk_ref[0] = _rope128(_dot(a, wwk_ref[...]), rc_ref[...], rs1_ref[...], rs2_ref[...]).astype(BF16)
    put_blocks(wv_ref, _dot_nt(wwv_ref[...], a))
    put_blocks(nq_ref, _dot_nt(wnq_ref[...], a) * scale)
    nk_ref[0] = _dot(a, wnk_ref[...]).astype(BF16)
    put_blocks(nv_ref, _dot_nt(wnv_ref[...], a))


def _inproj(h, nw, ws, conv, rope):
    bsz, lp, _ = h.shape
    nt = lp // TM
    nb = lp // BLK
    hb = TM // HALO
    tile = lambda n: pl.BlockSpec((1, TM, n), lambda b, i: (b, i, 0))
    tblk = lambda n: pl.BlockSpec((1, TM // BLK, n, BLK), lambda b, i: (b, i, 0, 0))
    prev = pl.BlockSpec((1, HALO, D_MODEL), lambda b, i: (b, jnp.maximum(i * hb - 1, 0), 0))
    nxt = pl.BlockSpec((1, HALO, D_MODEL), lambda b, i: (b, jnp.minimum((i + 1) * hb, nt * hb - 1), 0))
    full = lambda arr: pl.BlockSpec(arr.shape, lambda b, i: (0,) * arr.ndim)
    rtab = pl.BlockSpec((TM, 128), lambda b, i: (i, 0))
    rtab_t = pl.BlockSpec((ROPE_DIM // 2, TM), lambda b, i: (0, i))
    tok = lambda n, d: (tile(n), jax.ShapeDtypeStruct((bsz, lp, n), d))
    blk = lambda n: (tblk(n), jax.ShapeDtypeStruct((bsz, nb, n, BLK), BF16))
    kvw = WIN_KV_HEADS * HEAD_DIM
    outs = [tok(SSD_INNER, BF16), tok(SSD_XBC, BF16), tok(128, F32), blk(WIN_Q_HEADS * HEAD_DIM), tok(kvw, BF16),
            blk(kvw), blk(NA_HEADS * HEAD_DIM), tok(NA_HEADS * HEAD_DIM, BF16), blk(NA_HEADS * HEAD_DIM)]
    return pl.pallas_call(
        functools.partial(_inproj_kernel, nt=nt),
        grid=(bsz, nt),
        in_specs=[tile(D_MODEL), prev, nxt, full(nw)] + [full(w) for w in ws] + [full(c) for c in conv]
        + [rtab, rtab, rtab, rtab_t, rtab_t],
        out_specs=[o[0] for o in outs],
        out_shape=[o[1] for o in outs],
        scratch_shapes=[pltpu.VMEM((TM + 2 * HALO, SSD_XBC), F32)],
        compiler_params=_cparams(("parallel", "parallel")),
        name="inproj",
    )(h, h, h, nw, *ws, *conv, *rope)


def _split3(x):
    hi = x.astype(BF16)
    r1 = x - hi.astype(F32)
    mid = r1.astype(BF16)
    lo = (r1 - mid.astype(F32)).astype(BF16)
    return hi, mid, lo


def _ssd_kernel(*refs, reverse, nc, final):
    if final:
        (xc_ref, dt_ref, dtb_ref, alog_ref,
         z_ref, yo_ref, dsk_ref, nw_ref, o_ref, state_ref, ybuf_ref) = refs
    else:
        xc_ref, dt_ref, dtb_ref, alog_ref, o_ref, state_ref = refs
    step = pl.program_id(1)
    chunk = (nc - 1 - step) if reverse else step

    @pl.when(step == 0)
    def _():
        state_ref[...] = jnp.zeros_like(state_ref)

    row = lax.broadcasted_iota(jnp.int32, (BLK, 1), 0)
    live = jnp.logical_or(chunk > 0, row >= META0)
    xcv = xc_ref[0].astype(F32)

    dtr = dt_ref[0] + dtb_ref[...]
    dt = jnp.maximum(dtr, 0.0) + jnp.log1p(jnp.exp(-jnp.abs(dtr)))
    dt = jnp.where(live, dt, 0.0)
    a = dt * (-jnp.exp(alog_ref[...]))

    li = lax.broadcasted_iota(jnp.int32, (BLK, BLK), 0)
    si = lax.broadcasted_iota(jnp.int32, (BLK, BLK), 1)
    causal = (si >= li) if reverse else (si <= li)
    tri = causal.astype(BF16)
    acs = sum(_dot(tri, part) for part in _split3(a))
    acs_t = acs.T
    edge = acs[0:1, :] if reverse else acs[BLK - 1:BLK, :]
    e_in = jnp.exp(acs)
    dt_out = dt * jnp.exp(edge - acs)
    e_chunk = jnp.exp(edge)

    hoff = SSD_HEADS if reverse else 0
    lane = lax.broadcasted_iota(jnp.int32, (1, 2 * HEAD_DIM), 1)
    left = lane < HEAD_DIM
    rep = SSD_HEADS // SSD_GROUPS
    for g in range(SSD_GROUPS):
        bm = xcv[:, SSD_INNER + g * SSD_STATE:SSD_INNER + (g + 1) * SSD_STATE]
        cm = xcv[:, SSD_INNER + (SSD_GROUPS + g) * SSD_STATE:SSD_INNER + (SSD_GROUPS + g + 1) * SSD_STATE]
        cb = _dot_nt(cm.astype(BF16), bm.astype(BF16))
        bm_t = bm.T.astype(BF16)
        h_all = state_ref[g]
        h_bf = h_all.astype(BF16)
        x_out = []
        decay_lanes = []
        for j in range(rep // 2):
            h0 = hoff + g * rep + 2 * j
            col = (g * rep + 2 * j) * HEAD_DIM
            xs = xcv[:, col:col + 2 * HEAD_DIM]
            x_in = (xs * jnp.where(left, dt[:, h0:h0 + 1], dt[:, h0 + 1:h0 + 2])).astype(BF16)
            x_out.append((xs * jnp.where(left, dt_out[:, h0:h0 + 1], dt_out[:, h0 + 1:h0 + 2])).astype(BF16))
            decay_lanes.append(jnp.where(left, e_chunk[:, h0:h0 + 1], e_chunk[:, h0 + 1:h0 + 2]))
            hp = h_bf[:, 2 * j * HEAD_DIM:(2 * j + 2) * HEAD_DIM]
            ys = []
            for hh in (h0, h0 + 1):
                diff = acs[:, hh:hh + 1] - acs_t[hh:hh + 1, :]
                lmat = (cb * jnp.exp(jnp.where(causal, diff, -jnp.inf))).astype(BF16)
                cin = (cm * e_in[:, hh:hh + 1]).astype(BF16)
                ys.append(_dot(lmat, x_in) + _dot(cin, hp))
            y = jnp.where(left, ys[0], ys[1])
            if final:
                ybuf_ref[:, col:col + 2 * HEAD_DIM] = y
            else:
                o_ref[0, :, col:col + 2 * HEAD_DIM] = y
        s_new = _dot(bm_t, jnp.concatenate(x_out, axis=1))
        state_ref[g] = h_all * jnp.concatenate(decay_lanes, axis=1) + s_new

    if final:
        xs_all = xcv[:, :SSD_INNER]
        y = ybuf_ref[...] + yo_ref[0] + dsk_ref[...] * xs_all
        z = z_ref[0].astype(F32)
        y = y * (z * (1.0 / (1.0 + jnp.exp(-z))))
        o_ref[0] = _rms(y, nw_ref[...]).astype(BF16)


def _ssd_pass(xbc, dt, consts, final_args, *, reverse):
    bsz, lp, _ = xbc.shape
    nc = lp // BLK
    order = (lambda c: nc - 1 - c) if reverse else (lambda c: c)
    final = final_args is not None
    cur = lambda n: pl.BlockSpec((1, BLK, n), lambda b, c: (b, order(c), 0))
    full = lambda arr: pl.BlockSpec(arr.shape, lambda b, c: (0,) * arr.ndim)
    in_specs = [cur(SSD_XBC), cur(128)] + [full(x) for x in consts]
    args = [xbc, dt] + list(consts)
    scratch = [pltpu.VMEM((SSD_GROUPS, SSD_STATE, SSD_INNER // SSD_GROUPS), F32)]
    if final:
        z, y_other, dsk, nw = final_args
        in_specs += [cur(SSD_INNER), cur(SSD_INNER), full(dsk), full(nw)]
        args += [z, y_other, dsk, nw]
        scratch.append(pltpu.VMEM((BLK, SSD_INNER), F32))
    return pl.pallas_call(
        functools.partial(_ssd_kernel, reverse=reverse, nc=nc, final=final),
        grid=(bsz, nc),
        in_specs=in_specs,
        out_specs=cur(SSD_INNER),
        out_shape=jax.ShapeDtypeStruct((bsz, lp, SSD_INNER), BF16 if final else F32),
        scratch_shapes=scratch,
        compiler_params=_cparams(("parallel", "arbitrary")),
        name="ssd_bwd" if reverse else "ssd_fwd",
    )(*args)


def _softmax_pv_t(scores, values, pad_last, extra_logit=None):
    mx = functools.reduce(jnp.maximum, [jnp.max(s, axis=0, keepdims=True) for s in scores])
    if extra_logit is not None:
        mx = jnp.maximum(mx, extra_logit)
    ps = [jnp.exp(s - mx) for s in scores]
    den = functools.reduce(jnp.add, [jnp.sum(p, axis=0, keepdims=True) for p in ps])
    if extra_logit is not None:
        den = den + jnp.exp(extra_logit - mx)
    pb = [p.astype(BF16) for p in ps]
    pb[-1] = jnp.concatenate([pad_last, pb[-1]], axis=0)
    out = functools.reduce(jnp.add, [_dot(v, p) for v, p in zip(values, pb)])
    return out, den


def _head_rows(qt_ref, h, slot, nslots):
    qh = qt_ref[0, 0, h * HEAD_DIM:(h + 1) * HEAD_DIM, :]
    zero = jnp.zeros_like(qh)
    return jnp.concatenate([qh if s == slot else zero for s in range(nslots)], axis=0)


def _win_kernel(qt_ref, kp_ref, kc_ref, kn_ref, km_ref, vp_ref, vc_ref, vn_ref, vm_ref, sink_ref, o_ref, *, nb):
    n = pl.program_id(1)
    ki = lax.broadcasted_iota(jnp.int32, (BLK, BLK), 0)
    qi = lax.broadcasted_iota(jnp.int32, (BLK, BLK), 1)
    ok_prev = jnp.logical_and(ki >= qi, n >= 2)
    ok_cur = n >= 1
    ok_next = jnp.logical_and(ki <= qi, n + 1 <= nb - 1)
    rep = WIN_Q_HEADS // WIN_KV_HEADS
    kp, kc, kn, km = kp_ref[0], kc_ref[0], kn_ref[0], km_ref[0, META0:, :]
    vals = [vp_ref[0, 0], vc_ref[0, 0], vn_ref[0, 0], vm_ref[0, 0]]
    pad_last = jnp.zeros((META0, BLK), BF16)
    outs = []
    for h in range(WIN_Q_HEADS):
        g = h // rep
        qz = _head_rows(qt_ref, h, g, WIN_KV_HEADS)
        scores = [jnp.where(ok_prev, _dot(kp, qz), -jnp.inf),
                  jnp.where(ok_cur, _dot(kc, qz), -jnp.inf),
                  jnp.where(ok_next, _dot(kn, qz), -jnp.inf),
                  _dot(km, qz)]
        o, den = _softmax_pv_t(scores, vals, pad_last, extra_logit=sink_ref[h])
        outs.append(o[g * HEAD_DIM:(g + 1) * HEAD_DIM] / den)
    o_ref[0] = jnp.concatenate(outs, axis=0).T.astype(BF16)


def _win_attention(qt, k, vt, sink):
    bsz, nb = qt.shape[:2]
    kvw = WIN_KV_HEADS * HEAD_DIM
    assert kvw == BLK
    kspec = lambda f: pl.BlockSpec((1, BLK, kvw), lambda b, n: (b, f(n), 0))
    vspec = lambda f: pl.BlockSpec((1, 1, kvw, BLK), lambda b, n: (b, f(n), 0, 0))
    fs = [lambda n: jnp.maximum(n - 1, 0), lambda n: n, lambda n: jnp.minimum(n + 1, nb - 1), lambda n: 0]
    return pl.pallas_call(
        functools.partial(_win_kernel, nb=nb),
        grid=(bsz, nb),
        in_specs=[pl.BlockSpec((1, 1, WIN_Q_HEADS * HEAD_DIM, BLK), lambda b, n: (b, n, 0, 0))]
        + [kspec(f) for f in fs] + [vspec(f) for f in fs] + [pl.BlockSpec(memory_space=pltpu.SMEM)],
        out_specs=pl.BlockSpec((1, BLK, WIN_Q_HEADS * HEAD_DIM), lambda b, n: (b, n, 0)),
        out_shape=jax.ShapeDtypeStruct((bsz, nb * BLK, WIN_Q_HEADS * HEAD_DIM), BF16),
        compiler_params=_cparams(("parallel", "parallel")),
        name="win_attn",
    )(qt, k, k, k, k, vt, vt, vt, vt, sink)


NA_ROWS_PER_BLK = BLK // GRID_W
NA_WIN_BLKS = NA_KR // NA_ROWS_PER_BLK + 1
NA_WIN_ROWS = NA_WIN_BLKS * NA_ROWS_PER_BLK
NA_E_MIN = -1


def _na_bias_table(rpb):
    c = np.arange(GRID_W)[None, :]
    kc = np.arange(GRID_W)[:, None]
    cs = np.clip(c - NA_KC // 2, 0, GRID_W - NA_KC)
    ok = (kc >= cs) & (kc < cs + NA_KC)
    dc = np.clip(kc - c + NA_KC - 1, 0, 2 * NA_KC - 2)
    t = jnp.where(jnp.asarray(ok)[None, None], rpb.astype(F32)[:, :, dc], -jnp.inf)
    ninf = jnp.full((rpb.shape[0], 2, GRID_W, GRID_W), -jnp.inf, F32)
    ext = jnp.concatenate([ninf, t, ninf], axis=1)
    n_e = 2 * NA_KR - 1 + 3
    return jnp.concatenate([ext[:, 1:1 + n_e], ext[:, 0:n_e]], axis=-1)


def _na_kernel(qt_ref, k_ref, vt_ref, bias_ref, mbt_ref, o_ref, *, rows):
    n = pl.program_id(1)
    first = lax.broadcasted_iota(jnp.int32, (BLK, BLK), 0) < GRID_W
    left = lax.broadcasted_iota(jnp.int32, (BLK, BLK), 1) < GRID_W
    pad_last = jnp.zeros((META0, BLK), BF16)
    pair = 2 * HEAD_DIM

    def block(meta):
        if meta:
            rq = (0, 0)
            rs = (0, 0)
            wb = 0
            kb0 = LEAD // BLK
            start = LEAD
        else:
            r0 = (n - 1) * NA_ROWS_PER_BLK
            rq = (r0, r0 + 1)
            rs = tuple(jnp.clip(r - NA_KR // 2, 0, rows - NA_KR) for r in rq)
            wb = jnp.clip(r0 - NA_KR // 2, 0, rows - NA_WIN_ROWS)
            kb0 = LEAD // BLK + lax.shift_right_logical(wb, 1)
            start = pl.multiple_of(LEAD + wb * GRID_W, BLK)
        masks = []
        for ib in range(NA_WIN_BLKS):
            sel = []
            for kr in (wb + 2 * ib, wb + 2 * ib + 1):
                if meta:
                    oks = [jnp.int32(rs[j] <= kr < rs[j] + NA_KR) for j in range(2)]
                else:
                    oks = [jnp.logical_and(kr >= rs[j], kr < rs[j] + NA_KR).astype(jnp.int32) for j in range(2)]
                sel.append(jnp.where(left, oks[0], oks[1]))
            masks.append(jnp.where(first, sel[0], sel[1]) > 0)
        outs = []
        for pp in range(NA_HEADS // 2):
            cols = slice(pp * pair, (pp + 1) * pair)
            kblk = [k_ref[0, pl.ds(start + ib * BLK, BLK), cols] for ib in range(NA_WIN_BLKS)]
            vals = [vt_ref[0, kb0 + ib, cols, :] for ib in range(NA_WIN_BLKS)] + [vt_ref[0, 0, cols, :]]
            km = k_ref[0, META0:LEAD, cols]
            for hh in range(2):
                h = 2 * pp + hh
                qz = _head_rows(qt_ref, h, hh, 2)
                scores = []
                for ib in range(NA_WIN_BLKS):
                    e = wb + 2 * ib - rq[0] + (NA_KR - 1) - NA_E_MIN
                    if meta:
                        col = jnp.concatenate([bias_ref[h, e][:, 0:1], bias_ref[h, e + 1][:, 0:1]], axis=0)
                        bias = jnp.broadcast_to(col, (BLK, BLK))
                    else:
                        bias = jnp.concatenate([bias_ref[h, e], bias_ref[h, e + 1]], axis=0)
                    scores.append(jnp.where(masks[ib], _dot(kblk[ib], qz) + bias, -jnp.inf))
                scores.append(_dot(km, qz) + mbt_ref[h])
                o, den = _softmax_pv_t(scores, vals, pad_last)
                outs.append(o[hh * HEAD_DIM:(hh + 1) * HEAD_DIM] / den)
        o_ref[0] = jnp.concatenate(outs, axis=0).T.astype(BF16)

    @pl.when(n == 0)
    def _():
        block(True)

    @pl.when(n > 0)
    def _():
        block(False)


def _na_attention(qt, k, vt, bias, mbt):
    bsz, nb, width, _ = qt.shape
    lp = nb * BLK
    rows = (lp - LEAD) // GRID_W
    assert rows >= NA_WIN_ROWS and rows % NA_ROWS_PER_BLK == 0
    qblk = pl.BlockSpec((1, 1, width, BLK), lambda b, n: (b, n, 0, 0))
    kseq = pl.BlockSpec((1, lp, width), lambda b, n: (b, 0, 0), pipeline_mode=pl.Buffered(1))
    vseq = pl.BlockSpec((1, nb, width, BLK), lambda b, n: (b, 0, 0, 0), pipeline_mode=pl.Buffered(1))
    full = lambda arr: pl.BlockSpec(arr.shape, lambda b, n: (0,) * arr.ndim)
    return pl.pallas_call(
        functools.partial(_na_kernel, rows=rows),
        grid=(bsz, nb),
        in_specs=[qblk, kseq, vseq, full(bias), full(mbt)],
        out_specs=pl.BlockSpec((1, BLK, width), lambda b, n: (b, n, 0)),
        out_shape=jax.ShapeDtypeStruct((bsz, lp, width), BF16),
        compiler_params=_cparams(("parallel", "arbitrary")),
        name="na_attn",
    )(qt, k, vt, bias, mbt)


def _live_rows(tile_rows, tile_index):
    row = tile_index * tile_rows + lax.broadcasted_iota(jnp.int32, (tile_rows, 1), 0)
    return row >= META0


def _outproj_kernel(h_ref, ys_ref, yw_ref, yn_ref, w1_ref, w2_ref, w3_ref, nw_ref, o_ref):
    mix = _dot(ys_ref[0], w1_ref[...]) + _dot(yw_ref[0], w2_ref[...]) + _dot(yn_ref[0], w3_ref[...])
    out = h_ref[0] + _rms(mix, nw_ref[...])
    o_ref[0] = jnp.where(_live_rows(TM, pl.program_id(1)), out, 0.0)


def _outproj(h, y_ssd, y_win, y_na, w1, w2, w3, nw):
    bsz, lp, _ = h.shape
    tile = lambda n: pl.BlockSpec((1, TM, n), lambda b, i: (b, i, 0))
    full = lambda arr: pl.BlockSpec(arr.shape, lambda b, i: (0,) * arr.ndim)
    return pl.pallas_call(
        _outproj_kernel,
        grid=(bsz, lp // TM),
        in_specs=[tile(D_MODEL), tile(y_ssd.shape[-1]), tile(y_win.shape[-1]), tile(y_na.shape[-1]),
                  full(w1), full(w2), full(w3), full(nw)],
        out_specs=tile(D_MODEL),
        out_shape=jax.ShapeDtypeStruct(h.shape, F32),
        compiler_params=_cparams(("parallel", "parallel")),
        name="outproj",
    )(h, y_ssd, y_win, y_na, w1, w2, w3, nw)


def _ffn_kernel(hc_ref, hp_ref, hn_ref, nw1_ref, wup_ref, cw_ref, cb_ref, wdn_ref, nw2_ref, o_ref,
                gbuf_ref, ubuf_ref, acc_ref, *, nt):
    i = pl.program_id(1)
    nw1 = nw1_ref[...]
    f = jnp.concatenate([_rms(hp_ref[0], nw1), _rms(hc_ref[0], nw1), _rms(hn_ref[0], nw1)], axis=0).astype(BF16)
    tail = jnp.where(i < nt - 1, 1.0, 0.0)
    nch = D_FF // FF_CHUNK
    acc_ref[...] = jnp.zeros_like(acc_ref)

    def conv(buf_ref, g, c):
        buf_ref[...] = g
        buf_ref[HALO + TM:, :] = g[HALO + TM:, :] * tail
        out = buf_ref[pl.ds(HALO - 1, TM), :] * cw_ref[c, 0:1, :] + cb_ref[c]
        out = out + buf_ref[pl.ds(HALO, TM), :] * cw_ref[c, 1:2, :]
        return out + buf_ref[pl.ds(HALO + 1, TM), :] * cw_ref[c, 2:3, :]

    for c in range(nch):
        gate = conv(gbuf_ref, _dot(f, wup_ref[c]), c)
        up = conv(ubuf_ref, _dot(f, wup_ref[nch + c]), nch + c)
        th = jnp.tanh(gate * (GELU_K + (GELU_K * GELU_C) * (gate * gate)))
        act = (gate * (0.5 * th + 0.5) * up).astype(BF16)
        acc_ref[...] += _dot(act, wdn_ref[c])
    out = hc_ref[0] + _rms(acc_ref[...], nw2_ref[...])
    o_ref[0] = jnp.where(_live_rows(TM, i), out, 0.0)


def _ffn(h, nw1, wup, cw, cb, wdn, nw2):
    bsz, lp, _ = h.shape
    nt = lp // TM
    hb = TM // HALO
    full = lambda arr: pl.BlockSpec(arr.shape, lambda b, i: (0,) * arr.ndim)
    tile = pl.BlockSpec((1, TM, D_MODEL), lambda b, i: (b, i, 0))
    prev = pl.BlockSpec((1, HALO, D_MODEL), lambda b, i: (b, jnp.maximum(i * hb - 1, 0), 0))
    nxt = pl.BlockSpec((1, HALO, D_MODEL), lambda b, i: (b, jnp.minimum((i + 1) * hb, nt * hb - 1), 0))
    return pl.pallas_call(
        functools.partial(_ffn_kernel, nt=nt),
        grid=(bsz, nt),
        in_specs=[tile, prev, nxt, full(nw1), full(wup), full(cw), full(cb), full(wdn), full(nw2)],
        out_specs=tile,
        out_shape=jax.ShapeDtypeStruct(h.shape, F32),
        scratch_shapes=[pltpu.VMEM((TM + 2 * HALO, FF_CHUNK), F32),
                        pltpu.VMEM((TM + 2 * HALO, FF_CHUNK), F32),
                        pltpu.VMEM((TM, D_MODEL), F32)],
        compiler_params=_cparams(("parallel", "parallel")),
        name="ffn",
    )(h, h, h, nw1, wup, cw, cb, wdn, nw2)


def _pad_lanes(x, n):
    return jnp.pad(x, [(0, 0)] * (x.ndim - 1) + [(0, n - x.shape[-1])])


def _rope_tables(lp):
    half = ROPE_DIM // 2
    pos = jnp.maximum(jnp.arange(lp) - META0, 0).astype(F32)
    inv = jnp.power(ROPE_THETA, -jnp.arange(half, dtype=F32) / half)
    ang = pos[:, None] * inv[None, :]
    cos, sin = jnp.cos(ang), jnp.sin(ang)
    zeros, ones = jnp.zeros_like(cos), jnp.ones((lp, HEAD_DIM - ROPE_DIM), F32)
    rest = jnp.zeros((lp, HEAD_DIM - ROPE_DIM), F32)
    c = jnp.concatenate([cos, cos, ones], axis=1)
    s1 = jnp.concatenate([-sin, zeros, rest], axis=1)
    s2 = jnp.concatenate([zeros, sin, rest], axis=1)
    return tuple(jnp.tile(t, (1, 128 // HEAD_DIM)) for t in (c, s1, s2)) + (cos.T, sin.T)


def _layer_params(i, p):
    row = lambda v: v.reshape(1, -1).astype(F32)
    sizes = [SSD_INNER, SSD_XBC, 2 * SSD_HEADS, WIN_Q_HEADS * HEAD_DIM, WIN_KV_HEADS * HEAD_DIM,
             WIN_KV_HEADS * HEAD_DIM, NA_HEADS * HEAD_DIM, NA_HEADS * HEAD_DIM, NA_HEADS * HEAD_DIM]
    w_in = p['w_in'][i].astype(BF16)
    ws = jnp.split(w_in, np.cumsum(sizes)[:-1].tolist(), axis=1)
    ws[2] = _pad_lanes(ws[2], 128)
    for j in (3, 5, 6, 8):
        ws[j] = ws[j].T
    w_out = p['w_out'][i].astype(BF16)
    nch = D_FF // FF_CHUNK
    chunked = lambda m: m.reshape(m.shape[0], 2 * nch, FF_CHUNK).swapaxes(0, 1)
    return dict(
        norm_mix_pre=row(p['norm_mix_pre'][i]),
        w_in=ws,
        ssd_conv=(p['ssd_conv_w'][i].astype(F32), row(p['ssd_conv_b'][i])),
        ssd_consts=(_pad_lanes(row(p['ssd_dt_bias'][i]), 128), _pad_lanes(row(p['ssd_a_log'][i]), 128)),
        ssd_d=row(jnp.repeat(p['ssd_d'][i], HEAD_DIM)),
        ssd_norm_w=row(p['ssd_norm_w'][i]),
        win_sink=p['win_sink'][i].astype(F32),
        na_bias=_na_bias_table(p['na_rpb'][i]),
        na_meta_bias=jnp.broadcast_to(p['na_meta_bias'][i].astype(F32)[:, :, None], (NA_HEADS, N_META, BLK)),
        w_out=(w_out[:SSD_INNER], w_out[SSD_INNER:SSD_INNER + WIN_Q_HEADS * HEAD_DIM],
               w_out[SSD_INNER + WIN_Q_HEADS * HEAD_DIM:]),
        norm_mix_post=row(p['norm_mix_post'][i]),
        norm_ffn_pre=row(p['norm_ffn_pre'][i]),
        ffn_w_up=chunked(p['ffn_w_up'][i].astype(BF16)),
        ffn_conv_w=chunked(p['ffn_conv_w'][i].astype(F32)),
        ffn_conv_b=chunked(p['ffn_conv_b'][i].astype(F32).reshape(1, -1)),
        ffn_w_down=p['ffn_w_down'][i].astype(BF16).reshape(nch, FF_CHUNK, D_MODEL),
        norm_ffn_post=row(p['norm_ffn_post'][i]),
    )


def _encode(x, meta_tokens, layers):
    bsz, n_tok, _ = x.shape
    assert (n_tok + LEAD) % TM == 0 and n_tok % BLK == 0 and n_tok // GRID_W >= NA_KR
    lp = LEAD + n_tok
    meta = jnp.broadcast_to(meta_tokens.astype(F32)[None], (bsz, N_META, D_MODEL))
    h = jnp.concatenate([jnp.zeros((bsz, META0, D_MODEL), F32), meta, x.astype(F32)], axis=1)
    rope = _rope_tables(lp)
    for lw in layers:
        z, xbc, dt, wq, wk, wv, nq, nk, nv = _inproj(h, lw['norm_mix_pre'], lw['w_in'], lw['ssd_conv'], rope)
        y_fwd = _ssd_pass(xbc, dt, lw['ssd_consts'], None, reverse=False)
        y_ssd = _ssd_pass(xbc, dt, lw['ssd_consts'], (z, y_fwd, lw['ssd_d'], lw['ssd_norm_w']), reverse=True)
        y_win = _win_attention(wq, wk, wv, lw['win_sink'])
        y_na = _na_attention(nq, nk, nv, lw['na_bias'], lw['na_meta_bias'])
        h = _outproj(h, y_ssd, y_win, y_na, *lw['w_out'], lw['norm_mix_post'])
        h = _ffn(h, lw['norm_ffn_pre'], lw['ffn_w_up'], lw['ffn_conv_w'], lw['ffn_conv_b'],
                 lw['ffn_w_down'], lw['norm_ffn_post'])
    return h[:, LEAD:]


def kernel(x_prompt, x_sample, meta_tokens, norm_mix_pre, norm_mix_post, w_in, ssd_conv_w, ssd_conv_b,
           ssd_dt_bias, ssd_a_log, ssd_d, ssd_norm_w, win_sink, na_rpb, na_meta_bias, w_out, norm_ffn_pre,
           norm_ffn_post, ffn_w_up, ffn_conv_w, ffn_conv_b, ffn_w_down):
    p = dict(norm_mix_pre=norm_mix_pre, norm_mix_post=norm_mix_post, w_in=w_in, ssd_conv_w=ssd_conv_w,
             ssd_conv_b=ssd_conv_b, ssd_dt_bias=ssd_dt_bias, ssd_a_log=ssd_a_log, ssd_d=ssd_d,
             ssd_norm_w=ssd_norm_w, win_sink=win_sink, na_rpb=na_rpb, na_meta_bias=na_meta_bias, w_out=w_out,
             norm_ffn_pre=norm_ffn_pre, norm_ffn_post=norm_ffn_post, ffn_w_up=ffn_w_up, ffn_conv_w=ffn_conv_w,
             ffn_conv_b=ffn_conv_b, ffn_w_down=ffn_w_down)
    layers = [_layer_params(i, p) for i in range(w_in.shape[0])]
    return (_encode(x_prompt, meta_tokens, layers), _encode(x_sample, meta_tokens, layers))
```

```python
import functools

import jax
import jax.numpy as jnp
import numpy as np
from jax import lax
from jax.experimental import pallas as pl
from jax.experimental.pallas import tpu as pltpu

F32 = jnp.float32
BF16 = jnp.bfloat16

D_MODEL = 1024
N_META = 16
GRID_W = 64
HEAD_DIM = 64

SSD_HEADS = 16
SSD_INNER = SSD_HEADS * HEAD_DIM
SSD_GROUPS = 2
SSD_STATE = 128
SSD_XBC = SSD_INNER + 2 * SSD_GROUPS * SSD_STATE
SSD_CONV = 5

WIN_Q_HEADS = 8
WIN_KV_HEADS = 2
WIN_RADIUS = 128
ROPE_THETA = 500000.0
ROPE_DIM = HEAD_DIM // 4

NA_HEADS = 8
NA_KR = 8
NA_KC = 16

D_FF = 2816
EPS = 1e-6
GELU_K = float(np.sqrt(2.0 / np.pi))
GELU_C = 0.044715

BLK = 128
LEAD = BLK
META0 = LEAD - N_META
HALO = 8
TM = 384
FF_CHUNK = 256
VMEM_LIMIT = 56 * 1024 * 1024


def _cparams(sem):
    return pltpu.CompilerParams(dimension_semantics=sem, vmem_limit_bytes=VMEM_LIMIT)


def _rms(x, w):
    return x * lax.rsqrt(jnp.mean(x * x, axis=-1, keepdims=True) + EPS) * w


def _dot(a, b):
    return jnp.dot(a, b, preferred_element_type=F32)


def _shift_rows(x, d):
    return pltpu.roll(x, (-d) % x.shape[0], 0)[HALO:HALO + TM]


def _dot_nt(a, b):
    return lax.dot_general(a, b, (((1,), (1,)), ((), ())), preferred_element_type=F32)


def _rope128(x, c, s1, s2):
    return x * c + pltpu.roll(x, 128 - ROPE_DIM // 2, 1) * s1 + pltpu.roll(x, ROPE_DIM // 2, 1) * s2


def _inproj_kernel(h_ref, hp_ref, hn_ref, nw_ref, wz_ref, wx_ref, wdt_ref, wwq_ref, wwk_ref, wwv_ref,
                   wnq_ref, wnk_ref, wnv_ref, cw_ref, cb_ref, rc_ref, rs1_ref, rs2_ref, rct_ref, rst_ref,
                   z_ref, xt_ref, bc_ref, dt_ref, wq_ref, wk_ref, wv_ref, nq_ref, nk_ref, nv_ref, *, nt):
    nw = nw_ref[...]
    a = _rms(h_ref[0], nw).astype(BF16)
    tail = jnp.where(pl.program_id(1) < nt - 1, 1.0, 0.0)
    wx = wx_ref[...]
    xe = jnp.concatenate([_dot(_rms(hp_ref[0], nw).astype(BF16), wx), _dot(a, wx),
                          _dot(_rms(hn_ref[0], nw).astype(BF16), wx) * tail], axis=0)
    pad = SSD_CONV // 2
    acc = _shift_rows(xe, -pad) * cw_ref[0:1, :] + cb_ref[...]
    for j in range(1, SSD_CONV):
        acc = acc + _shift_rows(xe, j - pad) * cw_ref[j:j + 1, :]
    xc = acc * (1.0 / (1.0 + jnp.exp(-acc)))

    def put_blocks(ref, xt):
        for j in range(TM // BLK):
            ref[0, j] = xt[:, j * BLK:(j + 1) * BLK].astype(BF16)

    put_blocks(xt_ref, xc[:, :SSD_INNER].T)
    bc_ref[0] = xc[:, SSD_INNER:].astype(BF16)
    put_blocks(z_ref, _dot_nt(wz_ref[...], a))
    dt_ref[0] = _dot(a, wdt_ref[...])

    scale = HEAD_DIM ** -0.5
    qt = _dot_nt(wwq_ref[...], a)
    cos_t, sin_t = rct_ref[...], rst_ref[...]
    half = ROPE_DIM // 2
    parts = []
    for hh in range(WIN_Q_HEADS):
        x1 = qt[hh * HEAD_DIM:hh * HEAD_DIM + half]
        x2 = qt[hh * HEAD_DIM + half:hh * HEAD_DIM + ROPE_DIM]
        parts += [x1 * cos_t - x2 * sin_t, x2 * cos_t + x1 * sin_t, qt[hh * HEAD_DIM + ROPE_DIM:(hh + 1) * HEAD_DIM]]
    put_blocks(wq_ref, jnp.concatenate(parts, axis=0) * scale)
    wk_ref[0] = _rope128(_dot(a, wwk_ref[...]), rc_ref[...], rs1_ref[...], rs2_ref[...]).astype(BF16)
    put_blocks(wv_ref, _dot_nt(wwv_ref[...], a))
    put_blocks(nq_ref, _dot_nt(wnq_ref[...], a) * scale)
    nk_ref[0] = _dot(a, wnk_ref[...]).astype(BF16)
    put_blocks(nv_ref, _dot_nt(wnv_ref[...], a))


def _inproj(h, nw, ws, conv, rope):
    bsz, lp, _ = h.shape
    nt = lp // TM
    nb = lp // BLK
    hb = TM // HALO
    tile = lambda n: pl.BlockSpec((1, TM, n), lambda b, i: (b, i, 0))
    tblk = lambda n: pl.BlockSpec((1, TM // BLK, n, BLK), lambda b, i: (b, i, 0, 0))
    prev = pl.BlockSpec((1, HALO, D_MODEL), lambda b, i: (b, jnp.maximum(i * hb - 1, 0), 0))
    nxt = pl.BlockSpec((1, HALO, D_MODEL), lambda b, i: (b, jnp.minimum((i + 1) * hb, nt * hb - 1), 0))
    full = lambda arr: pl.BlockSpec(arr.shape, lambda b, i: (0,) * arr.ndim)
    rtab = pl.BlockSpec((TM, 128), lambda b, i: (i, 0))
    rtab_t = pl.BlockSpec((ROPE_DIM // 2, TM), lambda b, i: (0, i))
    tok = lambda n, d: (tile(n), jax.ShapeDtypeStruct((bsz, lp, n), d))
    blk = lambda n: (tblk(n), jax.ShapeDtypeStruct((bsz, nb, n, BLK), BF16))
    kvw = WIN_KV_HEADS * HEAD_DIM
    outs = [blk(SSD_INNER), blk(SSD_INNER), tok(SSD_XBC - SSD_INNER, BF16), tok(128, F32), blk(WIN_Q_HEADS * HEAD_DIM), tok(kvw, BF16),
            blk(kvw), blk(NA_HEADS * HEAD_DIM), tok(NA_HEADS * HEAD_DIM, BF16), blk(NA_HEADS * HEAD_DIM)]
    return pl.pallas_call(
        functools.partial(_inproj_kernel, nt=nt),
        grid=(bsz, nt),
        in_specs=[tile(D_MODEL), prev, nxt, full(nw)] + [full(w) for w in ws] + [full(c) for c in conv]
        + [rtab, rtab, rtab, rtab_t, rtab_t],
        out_specs=[o[0] for o in outs],
        out_shape=[o[1] for o in outs],
        compiler_params=_cparams(("parallel", "parallel")),
        name="inproj",
    )(h, h, h, nw, *ws, *conv, *rope)


def _split3(x):
    hi = x.astype(BF16)
    r1 = x - hi.astype(F32)
    mid = r1.astype(BF16)
    lo = (r1 - mid.astype(F32)).astype(BF16)
    return hi, mid, lo


def _ssd_kernel(*refs, reverse, nc, final):
    if final:
        (xt_ref, bc_ref, dt_ref, dtb_ref, alog_ref, zt_ref, yo_ref, dsk_ref, nw_ref, o_ref, state_ref, ybuf_ref) = refs
    else:
        xt_ref, bc_ref, dt_ref, dtb_ref, alog_ref, o_ref, state_ref = refs
    step = pl.program_id(1)
    chunk = (nc - 1 - step) if reverse else step

    @pl.when(step == 0)
    def _():
        state_ref[...] = jnp.zeros_like(state_ref)

    row = lax.broadcasted_iota(jnp.int32, (BLK, 1), 0)
    live = jnp.logical_or(chunk > 0, row >= META0)
    dtr = dt_ref[0] + dtb_ref[...]
    dt = jnp.maximum(dtr, 0.0) + jnp.log1p(jnp.exp(-jnp.abs(dtr)))
    dt = jnp.where(live, dt, 0.0)
    a = dt * (-jnp.exp(alog_ref[...]))

    ri = lax.broadcasted_iota(jnp.int32, (BLK, BLK), 0)
    ci = lax.broadcasted_iota(jnp.int32, (BLK, BLK), 1)
    tri = ((ci >= ri) if reverse else (ci <= ri)).astype(BF16)
    acs = sum(_dot(tri, part) for part in _split3(a))
    edge = acs[0:1, :] if reverse else acs[BLK - 1:BLK, :]
    acs_t = acs.T
    dt_t = dt.T
    dt_out_t = (dt * jnp.exp(edge - acs)).T
    e_in_t = jnp.exp(acs_t)
    e_chunk = jnp.exp(acs_t[:, 0:1] if reverse else acs_t[:, BLK - 1:BLK])
    feeds = (ri >= ci) if reverse else (ri <= ci)

    hoff = SSD_HEADS if reverse else 0
    rep = SSD_HEADS // SSD_GROUPS
    for g in range(SSD_GROUPS):
        bm = bc_ref[0, :, g * SSD_STATE:(g + 1) * SSD_STATE]
        cm = bc_ref[0, :, (SSD_GROUPS + g) * SSD_STATE:(SSD_GROUPS + g + 1) * SSD_STATE]
        cbt = _dot_nt(bm, cm)
        ht = state_ref[g]
        y_off = _dot_nt(ht.astype(BF16), cm)
        x_out, decay = [], []
        for r in range(rep):
            h = g * rep + r
            hl = hoff + h
            rows = slice(h * HEAD_DIM, (h + 1) * HEAD_DIM)
            xt = xt_ref[0, 0, rows, :].astype(F32)
            x_in = (xt * dt_t[hl:hl + 1, :]).astype(BF16)
            x_out.append((xt * dt_out_t[hl:hl + 1, :]).astype(BF16))
            decay.append(jnp.broadcast_to(e_chunk[hl:hl + 1, :], (HEAD_DIM, SSD_STATE)))
            diff = acs_t[hl:hl + 1, :] - acs[:, hl:hl + 1]
            lt = (cbt * jnp.exp(jnp.where(feeds, diff, -jnp.inf))).astype(BF16)
            y = _dot(x_in, lt) + y_off[r * HEAD_DIM:(r + 1) * HEAD_DIM] * e_in_t[hl:hl + 1, :]
            if final:
                ybuf_ref[rows, :] = y
            else:
                o_ref[0, 0, rows, :] = y
        s_new = _dot(jnp.concatenate(x_out, axis=0), bm)
        state_ref[g] = ht * jnp.concatenate(decay, axis=0) + s_new

    if final:
        y = ybuf_ref[...] + yo_ref[0, 0] + dsk_ref[...] * xt_ref[0, 0].astype(F32)
        z = zt_ref[0, 0].astype(F32)
        y = y * (z * (1.0 / (1.0 + jnp.exp(-z))))
        y = y * lax.rsqrt(jnp.mean(y * y, axis=0, keepdims=True) + EPS) * nw_ref[...]
        o_ref[0] = y.T.astype(BF16)


def _ssd_pass(xt, bc, dt, consts, final_args, *, reverse):
    bsz, nc = xt.shape[:2]
    order = (lambda c: nc - 1 - c) if reverse else (lambda c: c)
    final = final_args is not None
    tok = lambda n: pl.BlockSpec((1, BLK, n), lambda b, c: (b, order(c), 0))
    feat = pl.BlockSpec((1, 1, SSD_INNER, BLK), lambda b, c: (b, order(c), 0, 0))
    full = lambda arr: pl.BlockSpec(arr.shape, lambda b, c: (0,) * arr.ndim)
    in_specs = [feat, tok(bc.shape[-1]), tok(128)] + [full(x) for x in consts]
    args = [xt, bc, dt] + list(consts)
    scratch = [pltpu.VMEM((SSD_GROUPS, SSD_INNER // SSD_GROUPS, SSD_STATE), F32)]
    if final:
        zt, y_other, dsk, nw = final_args
        in_specs += [feat, feat, full(dsk), full(nw)]
        args += [zt, y_other, dsk, nw]
        scratch.append(pltpu.VMEM((SSD_INNER, BLK), F32))
        out_spec, out_shape = tok(SSD_INNER), jax.ShapeDtypeStruct((bsz, nc * BLK, SSD_INNER), BF16)
    else:
        out_spec, out_shape = feat, jax.ShapeDtypeStruct((bsz, nc, SSD_INNER, BLK), F32)
    return pl.pallas_call(
        functools.partial(_ssd_kernel, reverse=reverse, nc=nc, final=final),
        grid=(bsz, nc),
        in_specs=in_specs,
        out_specs=out_spec,
        out_shape=out_shape,
        scratch_shapes=scratch,
        compiler_params=_cparams(("parallel", "arbitrary")),
        name="ssd_bwd" if reverse else "ssd_fwd",
    )(*args)


def _softmax_pv_t(scores, values, pad_last, extra_logit=None):
    mx = functools.reduce(jnp.maximum, [jnp.max(s, axis=0, keepdims=True) for s in scores])
    if extra_logit is not None:
        mx = jnp.maximum(mx, extra_logit)
    ps = [jnp.exp(s - mx) for s in scores]
    den = functools.reduce(jnp.add, [jnp.sum(p, axis=0, keepdims=True) for p in ps])
    if extra_logit is not None:
        den = den + jnp.exp(extra_logit - mx)
    pb = [p.astype(BF16) for p in ps]
    pb[-1] = jnp.concatenate([pad_last, pb[-1]], axis=0)
    out = functools.reduce(jnp.add, [_dot(v, p) for v, p in zip(values, pb)])
    return out, den


def _head_rows(qt_ref, h, slot, nslots):
    qh = qt_ref[0, 0, h * HEAD_DIM:(h + 1) * HEAD_DIM, :]
    zero = jnp.zeros_like(qh)
    return jnp.concatenate([qh if s == slot else zero for s in range(nslots)], axis=0)


def _win_kernel(qt_ref, kp_ref, kc_ref, kn_ref, km_ref, vp_ref, vc_ref, vn_ref, vm_ref, sink_ref, o_ref, *, nb):
    n = pl.program_id(1)
    ki = lax.broadcasted_iota(jnp.int32, (BLK, BLK), 0)
    qi = lax.broadcasted_iota(jnp.int32, (BLK, BLK), 1)
    ok_prev = jnp.logical_and(ki >= qi, n >= 2)
    ok_cur = n >= 1
    ok_next = jnp.logical_and(ki <= qi, n + 1 <= nb - 1)
    rep = WIN_Q_HEADS // WIN_KV_HEADS
    kp, kc, kn, km = kp_ref[0], kc_ref[0], kn_ref[0], km_ref[0, META0:, :]
    vals = [vp_ref[0, 0], vc_ref[0, 0], vn_ref[0, 0], vm_ref[0, 0]]
    pad_last = jnp.zeros((META0, BLK), BF16)
    outs = []
    for h in range(WIN_Q_HEADS):
        g = h // rep
        qz = _head_rows(qt_ref, h, g, WIN_KV_HEADS)
        scores = [jnp.where(ok_prev, _dot(kp, qz), -jnp.inf),
                  jnp.where(ok_cur, _dot(kc, qz), -jnp.inf),
                  jnp.where(ok_next, _dot(kn, qz), -jnp.inf),
                  _dot(km, qz)]
        o, den = _softmax_pv_t(scores, vals, pad_last, extra_logit=sink_ref[h])
        outs.append(o[g * HEAD_DIM:(g + 1) * HEAD_DIM] / den)
    o_ref[0] = jnp.concatenate(outs, axis=0).T.astype(BF16)


def _win_attention(qt, k, vt, sink):
    bsz, nb = qt.shape[:2]
    kvw = WIN_KV_HEADS * HEAD_DIM
    assert kvw == BLK
    kspec = lambda f: pl.BlockSpec((1, BLK, kvw), lambda b, n: (b, f(n), 0))
    vspec = lambda f: pl.BlockSpec((1, 1, kvw, BLK), lambda b, n: (b, f(n), 0, 0))
    fs = [lambda n: jnp.maximum(n - 1, 0), lambda n: n, lambda n: jnp.minimum(n + 1, nb - 1), lambda n: 0]
    return pl.pallas_call(
        functools.partial(_win_kernel, nb=nb),
        grid=(bsz, nb),
        in_specs=[pl.BlockSpec((1, 1, WIN_Q_HEADS * HEAD_DIM, BLK), lambda b, n: (b, n, 0, 0))]
        + [kspec(f) for f in fs] + [vspec(f) for f in fs] + [pl.BlockSpec(memory_space=pltpu.SMEM)],
        out_specs=pl.BlockSpec((1, BLK, WIN_Q_HEADS * HEAD_DIM), lambda b, n: (b, n, 0)),
        out_shape=jax.ShapeDtypeStruct((bsz, nb * BLK, WIN_Q_HEADS * HEAD_DIM), BF16),
        compiler_params=_cparams(("parallel", "parallel")),
        name="win_attn",
    )(qt, k, k, k, k, vt, vt, vt, vt, sink)


NA_ROWS_PER_BLK = BLK // GRID_W
NA_WIN_BLKS = NA_KR // NA_ROWS_PER_BLK + 1
NA_WIN_ROWS = NA_WIN_BLKS * NA_ROWS_PER_BLK
NA_E_MIN = -1


def _na_bias_table(rpb):
    c = np.arange(GRID_W)[None, :]
    kc = np.arange(GRID_W)[:, None]
    cs = np.clip(c - NA_KC // 2, 0, GRID_W - NA_KC)
    ok = (kc >= cs) & (kc < cs + NA_KC)
    dc = np.clip(kc - c + NA_KC - 1, 0, 2 * NA_KC - 2)
    t = jnp.where(jnp.asarray(ok)[None, None], rpb.astype(F32)[:, :, dc], -jnp.inf)
    ninf = jnp.full((rpb.shape[0], 2, GRID_W, GRID_W), -jnp.inf, F32)
    ext = jnp.concatenate([ninf, t, ninf], axis=1)
    n_e = 2 * NA_KR - 1 + 3
    return jnp.concatenate([ext[:, 1:1 + n_e], ext[:, 0:n_e]], axis=-1)


def _na_kernel(qt_ref, k_ref, vt_ref, bias_ref, mbt_ref, o_ref, *, rows):
    n = pl.program_id(1)
    first = lax.broadcasted_iota(jnp.int32, (BLK, BLK), 0) < GRID_W
    left = lax.broadcasted_iota(jnp.int32, (BLK, BLK), 1) < GRID_W
    pad_last = jnp.zeros((META0, BLK), BF16)
    pair = 2 * HEAD_DIM

    def block(meta):
        if meta:
            rq = (0, 0)
            rs = (0, 0)
            wb = 0
            kb0 = LEAD // BLK
            start = LEAD
        else:
            r0 = (n - 1) * NA_ROWS_PER_BLK
            rq = (r0, r0 + 1)
            rs = tuple(jnp.clip(r - NA_KR // 2, 0, rows - NA_KR) for r in rq)
            wb = jnp.clip(r0 - NA_KR // 2, 0, rows - NA_WIN_ROWS)
            kb0 = LEAD // BLK + lax.shift_right_logical(wb, 1)
            start = pl.multiple_of(LEAD + wb * GRID_W, BLK)
        masks = []
        for ib in range(NA_WIN_BLKS):
            sel = []
            for kr in (wb + 2 * ib, wb + 2 * ib + 1):
                if meta:
                    oks = [jnp.int32(rs[j] <= kr < rs[j] + NA_KR) for j in range(2)]
                else:
                    oks = [jnp.logical_and(kr >= rs[j], kr < rs[j] + NA_KR).astype(jnp.int32) for j in range(2)]
                sel.append(jnp.where(left, oks[0], oks[1]))
            masks.append(jnp.where(first, sel[0], sel[1]) > 0)
        outs = []
        for pp in range(NA_HEADS // 2):
            cols = slice(pp * pair, (pp + 1) * pair)
            kblk = [k_ref[0, pl.ds(start + ib * BLK, BLK), cols] for ib in range(NA_WIN_BLKS)]
            vals = [vt_ref[0, kb0 + ib, cols, :] for ib in range(NA_WIN_BLKS)] + [vt_ref[0, 0, cols, :]]
            km = k_ref[0, META0:LEAD, cols]
            for hh in range(2):
                h = 2 * pp + hh
                qz = _head_rows(qt_ref, h, hh, 2)
                scores = []
                for ib in range(NA_WIN_BLKS):
                    e = wb + 2 * ib - rq[0] + (NA_KR - 1) - NA_E_MIN
                    if meta:
                        col = jnp.concatenate([bias_ref[h, e][:, 0:1], bias_ref[h, e + 1][:, 0:1]], axis=0)
                        bias = jnp.broadcast_to(col, (BLK, BLK))
                    else:
                        bias = jnp.concatenate([bias_ref[h, e], bias_ref[h, e + 1]], axis=0)
                    scores.append(jnp.where(masks[ib], _dot(kblk[ib], qz) + bias, -jnp.inf))
                scores.append(_dot(km, qz) + mbt_ref[h])
                o, den = _softmax_pv_t(scores, vals, pad_last)
                outs.append(o[hh * HEAD_DIM:(hh + 1) * HEAD_DIM] / den)
        o_ref[0] = jnp.concatenate(outs, axis=0).T.astype(BF16)

    @pl.when(n == 0)
    def _():
        block(True)

    @pl.when(n > 0)
    def _():
        block(False)


def _na_attention(qt, k, vt, bias, mbt):
    bsz, nb, width, _ = qt.shape
    lp = nb * BLK
    rows = (lp - LEAD) // GRID_W
    assert rows >= NA_WIN_ROWS and rows % NA_ROWS_PER_BLK == 0
    qblk = pl.BlockSpec((1, 1, width, BLK), lambda b, n: (b, n, 0, 0))
    kseq = pl.BlockSpec((1, lp, width), lambda b, n: (b, 0, 0), pipeline_mode=pl.Buffered(1))
    vseq = pl.BlockSpec((1, nb, width, BLK), lambda b, n: (b, 0, 0, 0), pipeline_mode=pl.Buffered(1))
    full = lambda arr: pl.BlockSpec(arr.shape, lambda b, n: (0,) * arr.ndim)
    return pl.pallas_call(
        functools.partial(_na_kernel, rows=rows),
        grid=(bsz, nb),
        in_specs=[qblk, kseq, vseq, full(bias), full(mbt)],
        out_specs=pl.BlockSpec((1, BLK, width), lambda b, n: (b, n, 0)),
        out_shape=jax.ShapeDtypeStruct((bsz, lp, width), BF16),
        compiler_params=_cparams(("parallel", "arbitrary")),
        name="na_attn",
    )(qt, k, vt, bias, mbt)


def _live_rows(tile_rows, tile_index):
    row = tile_index * tile_rows + lax.broadcasted_iota(jnp.int32, (tile_rows, 1), 0)
    return row >= META0


def _outproj_kernel(h_ref, ys_ref, yw_ref, yn_ref, w1_ref, w2_ref, w3_ref, nw_ref, o_ref):
    mix = _dot(ys_ref[0], w1_ref[...]) + _dot(yw_ref[0], w2_ref[...]) + _dot(yn_ref[0], w3_ref[...])
    out = h_ref[0] + _rms(mix, nw_ref[...])
    o_ref[0] = jnp.where(_live_rows(TM, pl.program_id(1)), out, 0.0)


def _outproj(h, y_ssd, y_win, y_na, w1, w2, w3, nw):
    bsz, lp, _ = h.shape
    tile = lambda n: pl.BlockSpec((1, TM, n), lambda b, i: (b, i, 0))
    full = lambda arr: pl.BlockSpec(arr.shape, lambda b, i: (0,) * arr.ndim)
    return pl.pallas_call(
        _outproj_kernel,
        grid=(bsz, lp // TM),
        in_specs=[tile(D_MODEL), tile(y_ssd.shape[-1]), tile(y_win.shape[-1]), tile(y_na.shape[-1]),
                  full(w1), full(w2), full(w3), full(nw)],
        out_specs=tile(D_MODEL),
        out_shape=jax.ShapeDtypeStruct(h.shape, F32),
        compiler_params=_cparams(("parallel", "parallel")),
        name="outproj",
    )(h, y_ssd, y_win, y_na, w1, w2, w3, nw)


def _ffn_kernel(hc_ref, hp_ref, hn_ref, nw1_ref, wup_ref, cw_ref, cb_ref, wdn_ref, nw2_ref, o_ref,
                act_ref, *, nt):
    i = pl.program_id(1)
    nw1 = nw1_ref[...]
    f = jnp.concatenate([_rms(hp_ref[0], nw1), _rms(hc_ref[0], nw1), _rms(hn_ref[0], nw1)], axis=0).astype(BF16)
    tail = jnp.where(i < nt - 1, 1.0, 0.0)
    nch = D_FF // FF_CHUNK

    def conv(g, c):
        g = jnp.concatenate([g[:HALO + TM], g[HALO + TM:] * tail], axis=0)
        out = _shift_rows(g, -1) * cw_ref[c, 0:1, :] + cb_ref[c]
        out = out + g[HALO:HALO + TM] * cw_ref[c, 1:2, :]
        return out + _shift_rows(g, 1) * cw_ref[c, 2:3, :]

    for c in range(nch):
        gate = conv(_dot(f, wup_ref[c]), c)
        up = conv(_dot(f, wup_ref[nch + c]), nch + c)
        th = jnp.tanh(gate * (GELU_K + (GELU_K * GELU_C) * (gate * gate)))
        act_ref[:, c * FF_CHUNK:(c + 1) * FF_CHUNK] = (gate * (0.5 * th + 0.5) * up).astype(BF16)
    out = hc_ref[0] + _rms(_dot(act_ref[...], wdn_ref[...]), nw2_ref[...])
    o_ref[0] = jnp.where(_live_rows(TM, i), out, 0.0)


def _ffn(h, nw1, wup, cw, cb, wdn, nw2):
    bsz, lp, _ = h.shape
    nt = lp // TM
    hb = TM // HALO
    full = lambda arr: pl.BlockSpec(arr.shape, lambda b, i: (0,) * arr.ndim)
    tile = pl.BlockSpec((1, TM, D_MODEL), lambda b, i: (b, i, 0))
    prev = pl.BlockSpec((1, HALO, D_MODEL), lambda b, i: (b, jnp.maximum(i * hb - 1, 0), 0))
    nxt = pl.BlockSpec((1, HALO, D_MODEL), lambda b, i: (b, jnp.minimum((i + 1) * hb, nt * hb - 1), 0))
    return pl.pallas_call(
        functools.partial(_ffn_kernel, nt=nt),
        grid=(bsz, nt),
        in_specs=[tile, prev, nxt, full(nw1), full(wup), full(cw), full(cb), full(wdn), full(nw2)],
        out_specs=tile,
        out_shape=jax.ShapeDtypeStruct(h.shape, F32),
        scratch_shapes=[pltpu.VMEM((TM, D_FF), BF16)],
        compiler_params=_cparams(("parallel", "parallel")),
        name="ffn",
    )(h, h, h, nw1, wup, cw, cb, wdn, nw2)


def _pad_lanes(x, n):
    return jnp.pad(x, [(0, 0)] * (x.ndim - 1) + [(0, n - x.shape[-1])])


def _rope_tables(lp):
    half = ROPE_DIM // 2
    pos = jnp.maximum(jnp.arange(lp) - META0, 0).astype(F32)
    inv = jnp.power(ROPE_THETA, -jnp.arange(half, dtype=F32) / half)
    ang = pos[:, None] * inv[None, :]
    cos, sin = jnp.cos(ang), jnp.sin(ang)
    zeros, ones = jnp.zeros_like(cos), jnp.ones((lp, HEAD_DIM - ROPE_DIM), F32)
    rest = jnp.zeros((lp, HEAD_DIM - ROPE_DIM), F32)
    c = jnp.concatenate([cos, cos, ones], axis=1)
    s1 = jnp.concatenate([-sin, zeros, rest], axis=1)
    s2 = jnp.concatenate([zeros, sin, rest], axis=1)
    return tuple(jnp.tile(t, (1, 128 // HEAD_DIM)) for t in (c, s1, s2)) + (cos.T, sin.T)


def _layer_params(i, p):
    row = lambda v: v.reshape(1, -1).astype(F32)
    lanes = lambda v: jnp.broadcast_to(v.astype(F32)[:, None], (v.shape[0], BLK))
    sizes = [SSD_INNER, SSD_XBC, 2 * SSD_HEADS, WIN_Q_HEADS * HEAD_DIM, WIN_KV_HEADS * HEAD_DIM,
             WIN_KV_HEADS * HEAD_DIM, NA_HEADS * HEAD_DIM, NA_HEADS * HEAD_DIM, NA_HEADS * HEAD_DIM]
    w_in = p['w_in'][i].astype(BF16)
    ws = jnp.split(w_in, np.cumsum(sizes)[:-1].tolist(), axis=1)
    ws[2] = _pad_lanes(ws[2], 128)
    for j in (0, 3, 5, 6, 8):
        ws[j] = ws[j].T
    w_out = p['w_out'][i].astype(BF16)
    nch = D_FF // FF_CHUNK
    chunked = lambda m: m.reshape(m.shape[0], 2 * nch, FF_CHUNK).swapaxes(0, 1)
    return dict(
        norm_mix_pre=row(p['norm_mix_pre'][i]),
        w_in=ws,
        ssd_conv=(p['ssd_conv_w'][i].astype(F32), row(p['ssd_conv_b'][i])),
        ssd_consts=(_pad_lanes(row(p['ssd_dt_bias'][i]), 128), _pad_lanes(row(p['ssd_a_log'][i]), 128)),
        ssd_d=lanes(jnp.repeat(p['ssd_d'][i], HEAD_DIM)),
        ssd_norm_w=lanes(p['ssd_norm_w'][i]),
        win_sink=p['win_sink'][i].astype(F32),
        na_bias=_na_bias_table(p['na_rpb'][i]),
        na_meta_bias=jnp.broadcast_to(p['na_meta_bias'][i].astype(F32)[:, :, None], (NA_HEADS, N_META, BLK)),
        w_out=(w_out[:SSD_INNER], w_out[SSD_INNER:SSD_INNER + WIN_Q_HEADS * HEAD_DIM],
               w_out[SSD_INNER + WIN_Q_HEADS * HEAD_DIM:]),
        norm_mix_post=row(p['norm_mix_post'][i]),
        norm_ffn_pre=row(p['norm_ffn_pre'][i]),
        ffn_w_up=chunked(p['ffn_w_up'][i].astype(BF16)),
        ffn_conv_w=chunked(p['ffn_conv_w'][i].astype(F32)),
        ffn_conv_b=chunked(p['ffn_conv_b'][i].astype(F32).reshape(1, -1)),
        ffn_w_down=p['ffn_w_down'][i].astype(BF16),
        norm_ffn_post=row(p['norm_ffn_post'][i]),
    )


def _encode(x, meta_tokens, layers):
    bsz, n_tok, _ = x.shape
    assert (n_tok + LEAD) % TM == 0 and n_tok % BLK == 0 and n_tok // GRID_W >= NA_KR
    lp = LEAD + n_tok
    meta = jnp.broadcast_to(meta_tokens.astype(F32)[None], (bsz, N_META, D_MODEL))
    h = jnp.concatenate([jnp.zeros((bsz, META0, D_MODEL), F32), meta, x.astype(F32)], axis=1)
    rope = _rope_tables(lp)
    for lw in layers:
        zt, xt, bc, dt, wq, wk, wv, nq, nk, nv = _inproj(h, lw['norm_mix_pre'], lw['w_in'], lw['ssd_conv'], rope)
        y_fwd = _ssd_pass(xt, bc, dt, lw['ssd_consts'], None, reverse=False)
        y_ssd = _ssd_pass(xt, bc, dt, lw['ssd_consts'], (zt, y_fwd, lw['ssd_d'], lw['ssd_norm_w']), reverse=True)
        y_win = _win_attention(wq, wk, wv, lw['win_sink'])
        y_na = _na_attention(nq, nk, nv, lw['na_bias'], lw['na_meta_bias'])
        h = _outproj(h, y_ssd, y_win, y_na, *lw['w_out'], lw['norm_mix_post'])
        h = _ffn(h, lw['norm_ffn_pre'], lw['ffn_w_up'], lw['ffn_conv_w'], lw['ffn_conv_b'],
                 lw['ffn_w_down'], lw['norm_ffn_post'])
    return h[:, LEAD:]


def kernel(x_prompt, x_sample, meta_tokens, norm_mix_pre, norm_mix_post, w_in, ssd_conv_w, ssd_conv_b,
           ssd_dt_bias, ssd_a_log, ssd_d, ssd_norm_w, win_sink, na_rpb, na_meta_bias, w_out, norm_ffn_pre,
           norm_ffn_post, ffn_w_up, ffn_conv_w, ffn_conv_b, ffn_w_down):
    p = dict(norm_mix_pre=norm_mix_pre, norm_mix_post=norm_mix_post, w_in=w_in, ssd_conv_w=ssd_conv_w,
             ssd_conv_b=ssd_conv_b, ssd_dt_bias=ssd_dt_bias, ssd_a_log=ssd_a_log, ssd_d=ssd_d,
             ssd_norm_w=ssd_norm_w, win_sink=win_sink, na_rpb=na_rpb, na_meta_bias=na_meta_bias, w_out=w_out,
             norm_ffn_pre=norm_ffn_pre, norm_ffn_post=norm_ffn_post, ffn_w_up=ffn_w_up, ffn_conv_w=ffn_conv_w,
             ffn_conv_b=ffn_conv_b, ffn_w_down=ffn_w_down)
    layers = [_layer_params(i, p) for i in range(w_in.shape[0])]
    return (_encode(x_prompt, meta_tokens, layers), _encode(x_sample, meta_tokens, layers))
```

```python
import functools

import jax
import jax.numpy as jnp
import numpy as np
from jax import lax
from jax.experimental import pallas as pl
from jax.experimental.pallas import tpu as pltpu

F32 = jnp.float32
BF16 = jnp.bfloat16

D_MODEL = 1024
N_META = 16
GRID_W = 64
HEAD_DIM = 64

SSD_HEADS = 16
SSD_INNER = SSD_HEADS * HEAD_DIM
SSD_GROUPS = 2
SSD_STATE = 128
SSD_XBC = SSD_INNER + 2 * SSD_GROUPS * SSD_STATE
SSD_CONV = 5

WIN_Q_HEADS = 8
WIN_KV_HEADS = 2
WIN_RADIUS = 128
ROPE_THETA = 500000.0
ROPE_DIM = HEAD_DIM // 4

NA_HEADS = 8
NA_KR = 8
NA_KC = 16

D_FF = 2816
EPS = 1e-6
GELU_K = float(np.sqrt(2.0 / np.pi))
GELU_C = 0.044715

BLK = 128
LEAD = BLK
META0 = LEAD - N_META
HALO = 8
TM = 384
FF_CHUNK = 256
VMEM_LIMIT = 56 * 1024 * 1024


def _cparams(sem):
    return pltpu.CompilerParams(dimension_semantics=sem, vmem_limit_bytes=VMEM_LIMIT)


def _rms(x, w):
    return x * lax.rsqrt(jnp.mean(x * x, axis=-1, keepdims=True) + EPS) * w


def _dot(a, b):
    return jnp.dot(a, b, preferred_element_type=F32)


def _shift_rows(x, d):
    return pltpu.roll(x, (-d) % x.shape[0], 0)[HALO:HALO + TM]


def _live_rows(tile_rows, tile_index):
    row = tile_index * tile_rows + lax.broadcasted_iota(jnp.int32, (tile_rows, 1), 0)
    return row >= META0


def _dot_nt(a, b):
    return lax.dot_general(a, b, (((1,), (1,)), ((), ())), preferred_element_type=F32)


def _rope128(x, c, s1, s2):
    return x * c + pltpu.roll(x, 128 - ROPE_DIM // 2, 1) * s1 + pltpu.roll(x, ROPE_DIM // 2, 1) * s2


def _scan_tables(dt_raw, dt_bias, a_log, live, acs_ref, tab_ref):
    nh = 2 * SSD_HEADS
    dtr = dt_raw + dt_bias
    dt = jnp.maximum(dtr, 0.0) + jnp.log1p(jnp.exp(-jnp.abs(dtr)))
    dt = jnp.where(live, dt, 0.0)
    a = dt * (-jnp.exp(a_log))
    ri = lax.broadcasted_iota(jnp.int32, (BLK, BLK), 0)
    ci = lax.broadcasted_iota(jnp.int32, (BLK, BLK), 1)
    tri = (ci <= ri).astype(BF16)
    fwd = ci < SSD_HEADS
    fwd_rows = ri < SSD_HEADS
    for j in range(TM // BLK):
        rows = slice(j * BLK, (j + 1) * BLK)
        aj, dtj = a[rows], dt[rows]
        pre = sum(_dot(tri, part) for part in _split3(aj))
        total = pre[BLK - 1:BLK, :]
        acs = jnp.where(fwd, pre, total - pre + aj)
        acs_ref[0, rows, :] = acs
        acs_t = acs.T[:nh]
        tab_ref[0, j, 0] = acs_t
        tab_ref[0, j, 1] = dtj.T[:nh]
        tab_ref[0, j, 2] = (dtj * jnp.exp(total - acs)).T[:nh]
        tab_ref[0, j, 3] = jnp.exp(acs_t)
        chunk_sum = jnp.where(fwd_rows[:nh, 0:1], acs_t[:, BLK - 1:BLK], acs_t[:, 0:1])
        tab_ref[0, j, 4] = jnp.broadcast_to(jnp.exp(chunk_sum), (nh, BLK))


def _inproj_kernel(h_ref, hp_ref, hn_ref, nw_ref, wz_ref, wx_ref, wdt_ref, wwq_ref, wwk_ref, wwv_ref,
                   wnq_ref, wnk_ref, wnv_ref, cw_ref, cb_ref, dtb_ref, alog_ref, rc_ref, rs1_ref, rs2_ref, rct_ref, rst_ref,
                   z_ref, xt_ref, bc_ref, acs_ref, tab_ref, wq_ref, wk_ref, wv_ref, nq_ref, nk_ref, nv_ref, *, nt):
    nw = nw_ref[...]
    a = _rms(h_ref[0], nw).astype(BF16)
    tail = jnp.where(pl.program_id(1) < nt - 1, 1.0, 0.0)
    wx = wx_ref[...]
    xe = jnp.concatenate([_dot(_rms(hp_ref[0], nw).astype(BF16), wx), _dot(a, wx),
                          _dot(_rms(hn_ref[0], nw).astype(BF16), wx) * tail], axis=0)
    pad = SSD_CONV // 2
    acc = _shift_rows(xe, -pad) * cw_ref[0:1, :] + cb_ref[...]
    for j in range(1, SSD_CONV):
        acc = acc + _shift_rows(xe, j - pad) * cw_ref[j:j + 1, :]
    xc = acc * (1.0 / (1.0 + jnp.exp(-acc)))

    def put_blocks(ref, xt):
        for j in range(TM // BLK):
            ref[0, j] = xt[:, j * BLK:(j + 1) * BLK].astype(BF16)

    put_blocks(xt_ref, xc[:, :SSD_INNER].T)
    bc_ref[0] = xc[:, SSD_INNER:].astype(BF16)
    put_blocks(z_ref, _dot_nt(wz_ref[...], a))
    _scan_tables(_dot(a, wdt_ref[...]), dtb_ref[...], alog_ref[...], _live_rows(TM, pl.program_id(1)), acs_ref, tab_ref)

    scale = HEAD_DIM ** -0.5
    qt = _dot_nt(wwq_ref[...], a)
    cos_t, sin_t = rct_ref[...], rst_ref[...]
    half = ROPE_DIM // 2
    parts = []
    for hh in range(WIN_Q_HEADS):
        x1 = qt[hh * HEAD_DIM:hh * HEAD_DIM + half]
        x2 = qt[hh * HEAD_DIM + half:hh * HEAD_DIM + ROPE_DIM]
        parts += [x1 * cos_t - x2 * sin_t, x2 * cos_t + x1 * sin_t, qt[hh * HEAD_DIM + ROPE_DIM:(hh + 1) * HEAD_DIM]]
    put_blocks(wq_ref, jnp.concatenate(parts, axis=0) * scale)
    wk_ref[0] = _rope128(_dot(a, wwk_ref[...]), rc_ref[...], rs1_ref[...], rs2_ref[...]).astype(BF16)
    put_blocks(wv_ref, _dot_nt(wwv_ref[...], a))
    put_blocks(nq_ref, _dot_nt(wnq_ref[...], a) * scale)
    nk_ref[0] = _dot(a, wnk_ref[...]).astype(BF16)
    put_blocks(nv_ref, _dot_nt(wnv_ref[...], a))


def _inproj(h, nw, ws, conv, scan, rope):
    bsz, lp, _ = h.shape
    nt = lp // TM
    nb = lp // BLK
    hb = TM // HALO
    tile = lambda n: pl.BlockSpec((1, TM, n), lambda b, i: (b, i, 0))
    tblk = lambda n: pl.BlockSpec((1, TM // BLK, n, BLK), lambda b, i: (b, i, 0, 0))
    prev = pl.BlockSpec((1, HALO, D_MODEL), lambda b, i: (b, jnp.maximum(i * hb - 1, 0), 0))
    nxt = pl.BlockSpec((1, HALO, D_MODEL), lambda b, i: (b, jnp.minimum((i + 1) * hb, nt * hb - 1), 0))
    full = lambda arr: pl.BlockSpec(arr.shape, lambda b, i: (0,) * arr.ndim)
    rtab = pl.BlockSpec((TM, 128), lambda b, i: (i, 0))
    rtab_t = pl.BlockSpec((ROPE_DIM // 2, TM), lambda b, i: (0, i))
    tok = lambda n, d: (tile(n), jax.ShapeDtypeStruct((bsz, lp, n), d))
    blk = lambda n: (tblk(n), jax.ShapeDtypeStruct((bsz, nb, n, BLK), BF16))
    kvw = WIN_KV_HEADS * HEAD_DIM
    tabs = (pl.BlockSpec((1, TM // BLK, 5, 2 * SSD_HEADS, BLK), lambda b, i: (b, i, 0, 0, 0)),
            jax.ShapeDtypeStruct((bsz, nb, 5, 2 * SSD_HEADS, BLK), F32))
    outs = [blk(SSD_INNER), blk(SSD_INNER), tok(SSD_XBC - SSD_INNER, BF16), tok(128, F32), tabs, blk(WIN_Q_HEADS * HEAD_DIM), tok(kvw, BF16),
            blk(kvw), blk(NA_HEADS * HEAD_DIM), tok(NA_HEADS * HEAD_DIM, BF16), blk(NA_HEADS * HEAD_DIM)]
    return pl.pallas_call(
        functools.partial(_inproj_kernel, nt=nt),
        grid=(bsz, nt),
        in_specs=[tile(D_MODEL), prev, nxt, full(nw)] + [full(w) for w in ws] + [full(c) for c in conv + scan]
        + [rtab, rtab, rtab, rtab_t, rtab_t],
        out_specs=[o[0] for o in outs],
        out_shape=[o[1] for o in outs],
        compiler_params=_cparams(("parallel", "parallel")),
        name="inproj",
    )(h, h, h, nw, *ws, *conv, *scan, *rope)


def _split3(x):
    hi = x.astype(BF16)
    r1 = x - hi.astype(F32)
    mid = r1.astype(BF16)
    lo = (r1 - mid.astype(F32)).astype(BF16)
    return hi, mid, lo


def _ssd_kernel(*refs, reverse, final):
    if final:
        xt_ref, bc_ref, acs_ref, tab_ref, zt_ref, yo_ref, dsk_ref, nw_ref, o_ref, state_ref, ybuf_ref = refs
    else:
        xt_ref, bc_ref, acs_ref, tab_ref, o_ref, state_ref = refs

    @pl.when(pl.program_id(1) == 0)
    def _():
        state_ref[...] = jnp.zeros_like(state_ref)

    acs = acs_ref[0]
    acs_t, dt_t, dt_out_t, e_in_t, e_chunk = (tab_ref[0, 0, k] for k in range(5))
    ri = lax.broadcasted_iota(jnp.int32, (BLK, BLK), 0)
    ci = lax.broadcasted_iota(jnp.int32, (BLK, BLK), 1)
    feeds = (ri >= ci) if reverse else (ri <= ci)

    hoff = SSD_HEADS if reverse else 0
    rep = SSD_HEADS // SSD_GROUPS
    for g in range(SSD_GROUPS):
        bm = bc_ref[0, :, g * SSD_STATE:(g + 1) * SSD_STATE]
        cm = bc_ref[0, :, (SSD_GROUPS + g) * SSD_STATE:(SSD_GROUPS + g + 1) * SSD_STATE]
        cbt = _dot_nt(bm, cm)
        ht = state_ref[g]
        y_off = _dot_nt(ht.astype(BF16), cm)
        x_out, decay = [], []
        for r in range(rep):
            h = g * rep + r
            hl = hoff + h
            rows = slice(h * HEAD_DIM, (h + 1) * HEAD_DIM)
            xt = xt_ref[0, 0, rows, :].astype(F32)
            x_in = (xt * dt_t[hl:hl + 1, :]).astype(BF16)
            x_out.append((xt * dt_out_t[hl:hl + 1, :]).astype(BF16))
            decay.append(jnp.broadcast_to(e_chunk[hl:hl + 1, :], (HEAD_DIM, SSD_STATE)))
            diff = acs_t[hl:hl + 1, :] - acs[:, hl:hl + 1]
            lt = (cbt * jnp.exp(jnp.where(feeds, diff, -jnp.inf))).astype(BF16)
            y = _dot(x_in, lt) + y_off[r * HEAD_DIM:(r + 1) * HEAD_DIM] * e_in_t[hl:hl + 1, :]
            if final:
                ybuf_ref[rows, :] = y
            else:
                o_ref[0, 0, rows, :] = y
        s_new = _dot(jnp.concatenate(x_out, axis=0), bm)
        state_ref[g] = ht * jnp.concatenate(decay, axis=0) + s_new

    if final:
        y = ybuf_ref[...] + yo_ref[0, 0] + dsk_ref[...] * xt_ref[0, 0].astype(F32)
        z = zt_ref[0, 0].astype(F32)
        y = y * (z * (1.0 / (1.0 + jnp.exp(-z))))
        y = y * lax.rsqrt(jnp.mean(y * y, axis=0, keepdims=True) + EPS) * nw_ref[...]
        o_ref[0] = y.T.astype(BF16)


def _ssd_pass(xt, bc, acs, tab, final_args, *, reverse):
    bsz, nc = xt.shape[:2]
    order = (lambda c: nc - 1 - c) if reverse else (lambda c: c)
    final = final_args is not None
    tok = lambda n: pl.BlockSpec((1, BLK, n), lambda b, c: (b, order(c), 0))
    feat = pl.BlockSpec((1, 1, SSD_INNER, BLK), lambda b, c: (b, order(c), 0, 0))
    full = lambda arr: pl.BlockSpec(arr.shape, lambda b, c: (0,) * arr.ndim)
    tabs = pl.BlockSpec((1, 1) + tab.shape[2:], lambda b, c: (b, order(c), 0, 0, 0))
    in_specs = [feat, tok(bc.shape[-1]), tok(128), tabs]
    args = [xt, bc, acs, tab]
    scratch = [pltpu.VMEM((SSD_GROUPS, SSD_INNER // SSD_GROUPS, SSD_STATE), F32)]
    if final:
        zt, y_other, dsk, nw = final_args
        in_specs += [feat, feat, full(dsk), full(nw)]
        args += [zt, y_other, dsk, nw]
        scratch.append(pltpu.VMEM((SSD_INNER, BLK), F32))
        out_spec, out_shape = tok(SSD_INNER), jax.ShapeDtypeStruct((bsz, nc * BLK, SSD_INNER), BF16)
    else:
        out_spec, out_shape = feat, jax.ShapeDtypeStruct((bsz, nc, SSD_INNER, BLK), F32)
    return pl.pallas_call(
        functools.partial(_ssd_kernel, reverse=reverse, final=final),
        grid=(bsz, nc),
        in_specs=in_specs,
        out_specs=out_spec,
        out_shape=out_shape,
        scratch_shapes=scratch,
        compiler_params=_cparams(("parallel", "arbitrary")),
        name="ssd_bwd" if reverse else "ssd_fwd",
    )(*args)


def _softmax_pv_t(scores, values, pad_last, extra_logit=None):
    mx = functools.reduce(jnp.maximum, [jnp.max(s, axis=0, keepdims=True) for s in scores])
    if extra_logit is not None:
        mx = jnp.maximum(mx, extra_logit)
    ps = [jnp.exp(s - mx) for s in scores]
    den = functools.reduce(jnp.add, [jnp.sum(p, axis=0, keepdims=True) for p in ps])
    if extra_logit is not None:
        den = den + jnp.exp(extra_logit - mx)
    pb = [p.astype(BF16) for p in ps]
    pb[-1] = jnp.concatenate([pad_last, pb[-1]], axis=0)
    out = functools.reduce(jnp.add, [_dot(v, p) for v, p in zip(values, pb)])
    return out, den


def _block_diag_rows(pieces):
    zero = jnp.zeros_like(pieces[0][0])
    rows = []
    for i in range(len(pieces)):
        row = []
        for j, ps in enumerate(pieces):
            row += [p if i == j else zero for p in ps]
        rows.append(jnp.concatenate(row, axis=1))
    return jnp.concatenate(rows, axis=0)


def _win_kernel(qt_ref, kp_ref, kc_ref, kn_ref, km_ref, vp_ref, vc_ref, vn_ref, vm_ref, sink_ref, o_ref, *, nb):
    n = pl.program_id(1)
    width = WIN_Q_HEADS * BLK
    ki = lax.broadcasted_iota(jnp.int32, (BLK, width), 0)
    qi = lax.broadcasted_iota(jnp.int32, (BLK, width), 1) & (BLK - 1)
    ok_prev = jnp.logical_and(ki >= qi, n >= 2)
    ok_cur = n >= 1
    ok_next = jnp.logical_and(ki <= qi, n + 1 <= nb - 1)
    rep = WIN_Q_HEADS // WIN_KV_HEADS
    heads = [qt_ref[0, 0, h * HEAD_DIM:(h + 1) * HEAD_DIM, :] for h in range(WIN_Q_HEADS)]
    qbd = _block_diag_rows([heads[g * rep:(g + 1) * rep] for g in range(WIN_KV_HEADS)])
    vals = [vp_ref[0, 0], vc_ref[0, 0], vn_ref[0, 0], vm_ref[0, 0]]
    scores = [jnp.where(ok_prev, _dot(kp_ref[0], qbd), -jnp.inf),
              jnp.where(ok_cur, _dot(kc_ref[0], qbd), -jnp.inf),
              jnp.where(ok_next, _dot(kn_ref[0], qbd), -jnp.inf),
              _dot(km_ref[0, META0:, :], qbd)]
    o, den = _softmax_pv_t(scores, vals, jnp.zeros((META0, width), BF16), extra_logit=sink_ref[...])
    o = o / den
    outs = [o[(h // rep) * HEAD_DIM:(h // rep + 1) * HEAD_DIM, h * BLK:(h + 1) * BLK] for h in range(WIN_Q_HEADS)]
    o_ref[0] = jnp.concatenate(outs, axis=0).T.astype(BF16)


def _win_attention(qt, k, vt, sink):
    bsz, nb = qt.shape[:2]
    kvw = WIN_KV_HEADS * HEAD_DIM
    assert kvw == BLK
    kspec = lambda f: pl.BlockSpec((1, BLK, kvw), lambda b, n: (b, f(n), 0))
    vspec = lambda f: pl.BlockSpec((1, 1, kvw, BLK), lambda b, n: (b, f(n), 0, 0))
    fs = [lambda n: jnp.maximum(n - 1, 0), lambda n: n, lambda n: jnp.minimum(n + 1, nb - 1), lambda n: 0]
    return pl.pallas_call(
        functools.partial(_win_kernel, nb=nb),
        grid=(bsz, nb),
        in_specs=[pl.BlockSpec((1, 1, WIN_Q_HEADS * HEAD_DIM, BLK), lambda b, n: (b, n, 0, 0))]
        + [kspec(f) for f in fs] + [vspec(f) for f in fs] + [pl.BlockSpec(sink.shape, lambda b, n: (0, 0))],
        out_specs=pl.BlockSpec((1, BLK, WIN_Q_HEADS * HEAD_DIM), lambda b, n: (b, n, 0)),
        out_shape=jax.ShapeDtypeStruct((bsz, nb * BLK, WIN_Q_HEADS * HEAD_DIM), BF16),
        compiler_params=_cparams(("parallel", "parallel")),
        name="win_attn",
    )(qt, k, k, k, k, vt, vt, vt, vt, sink)


NA_ROWS_PER_BLK = BLK // GRID_W
NA_WIN_BLKS = NA_KR // NA_ROWS_PER_BLK + 1
NA_WIN_ROWS = NA_WIN_BLKS * NA_ROWS_PER_BLK
NA_E_MIN = -1


def _na_bias_table(rpb):
    c = np.arange(GRID_W)[None, :]
    kc = np.arange(GRID_W)[:, None]
    cs = np.clip(c - NA_KC // 2, 0, GRID_W - NA_KC)
    ok = (kc >= cs) & (kc < cs + NA_KC)
    dc = np.clip(kc - c + NA_KC - 1, 0, 2 * NA_KC - 2)
    t = jnp.where(jnp.asarray(ok)[None, None], rpb.astype(F32)[:, :, dc], -jnp.inf)
    ninf = jnp.full((rpb.shape[0], 2, GRID_W, GRID_W), -jnp.inf, F32)
    ext = jnp.concatenate([ninf, t, ninf], axis=1)
    n_e = 2 * NA_KR - 1 + 3
    return jnp.concatenate([ext[:, 1:1 + n_e], ext[:, 0:n_e]], axis=-1)


def _na_kernel(qt_ref, k_ref, vt_ref, bias_ref, mbt_ref, o_ref, *, rows):
    n = pl.program_id(1)
    pair = 2 * HEAD_DIM
    width = 2 * BLK
    first = lax.broadcasted_iota(jnp.int32, (BLK, width), 0) < GRID_W
    left = (lax.broadcasted_iota(jnp.int32, (BLK, width), 1) & (BLK - 1)) < GRID_W
    pad_last = jnp.zeros((META0, width), BF16)

    def block(meta):
        if meta:
            rq = (0, 0)
            rs = (0, 0)
            wb = 0
            kb0 = LEAD // BLK
            start = LEAD
        else:
            r0 = (n - 1) * NA_ROWS_PER_BLK
            rq = (r0, r0 + 1)
            rs = tuple(jnp.clip(r - NA_KR // 2, 0, rows - NA_KR) for r in rq)
            wb = jnp.clip(r0 - NA_KR // 2, 0, rows - NA_WIN_ROWS)
            kb0 = LEAD // BLK + lax.shift_right_logical(wb, 1)
            start = pl.multiple_of(LEAD + wb * GRID_W, BLK)
        masks = []
        for ib in range(NA_WIN_BLKS):
            sel = []
            for kr in (wb + 2 * ib, wb + 2 * ib + 1):
                if meta:
                    oks = [jnp.int32(rs[j] <= kr < rs[j] + NA_KR) for j in range(2)]
                else:
                    oks = [jnp.logical_and(kr >= rs[j], kr < rs[j] + NA_KR).astype(jnp.int32) for j in range(2)]
                sel.append(jnp.where(left, oks[0], oks[1]))
            masks.append(jnp.where(first, sel[0], sel[1]) > 0)
        outs = []
        for pp in range(NA_HEADS // 2):
            cols = slice(pp * pair, (pp + 1) * pair)
            hs = (2 * pp, 2 * pp + 1)
            qbd = _block_diag_rows([[qt_ref[0, 0, h * HEAD_DIM:(h + 1) * HEAD_DIM, :]] for h in hs])
            vals = [vt_ref[0, kb0 + ib, cols, :] for ib in range(NA_WIN_BLKS)] + [vt_ref[0, 0, cols, :]]
            scores = []
            for ib in range(NA_WIN_BLKS):
                e = wb + 2 * ib - rq[0] + (NA_KR - 1) - NA_E_MIN
                if meta:
                    col = [jnp.concatenate([bias_ref[h, e][:, 0:1], bias_ref[h, e + 1][:, 0:1]], axis=0) for h in hs]
                    bias = jnp.concatenate([jnp.broadcast_to(c, (BLK, BLK)) for c in col], axis=1)
                else:
                    bias = jnp.concatenate(
                        [jnp.concatenate([bias_ref[h, e], bias_ref[h, e + 1]], axis=0) for h in hs], axis=1)
                s = _dot(k_ref[0, pl.ds(start + ib * BLK, BLK), cols], qbd)
                scores.append(jnp.where(masks[ib], s + bias, -jnp.inf))
            scores.append(_dot(k_ref[0, META0:LEAD, cols], qbd) + jnp.concatenate([mbt_ref[h] for h in hs], axis=1))
            o, den = _softmax_pv_t(scores, vals, pad_last)
            o = o / den
            outs += [o[hh * HEAD_DIM:(hh + 1) * HEAD_DIM, hh * BLK:(hh + 1) * BLK] for hh in range(2)]
        o_ref[0] = jnp.concatenate(outs, axis=0).T.astype(BF16)

    @pl.when(n == 0)
    def _():
        block(True)

    @pl.when(n > 0)
    def _():
        block(False)


def _na_attention(qt, k, vt, bias, mbt):
    bsz, nb, width, _ = qt.shape
    lp = nb * BLK
    rows = (lp - LEAD) // GRID_W
    assert rows >= NA_WIN_ROWS and rows % NA_ROWS_PER_BLK == 0
    qblk = pl.BlockSpec((1, 1, width, BLK), lambda b, n: (b, n, 0, 0))
    kseq = pl.BlockSpec((1, lp, width), lambda b, n: (b, 0, 0), pipeline_mode=pl.Buffered(1))
    vseq = pl.BlockSpec((1, nb, width, BLK), lambda b, n: (b, 0, 0, 0), pipeline_mode=pl.Buffered(1))
    full = lambda arr: pl.BlockSpec(arr.shape, lambda b, n: (0,) * arr.ndim)
    return pl.pallas_call(
        functools.partial(_na_kernel, rows=rows),
        grid=(bsz, nb),
        in_specs=[qblk, kseq, vseq, full(bias), full(mbt)],
        out_specs=pl.BlockSpec((1, BLK, width), lambda b, n: (b, n, 0)),
        out_shape=jax.ShapeDtypeStruct((bsz, lp, width), BF16),
        compiler_params=_cparams(("parallel", "arbitrary")),
        name="na_attn",
    )(qt, k, vt, bias, mbt)


def _outproj_kernel(h_ref, ys_ref, yw_ref, yn_ref, w1_ref, w2_ref, w3_ref, nw_ref, o_ref):
    mix = _dot(ys_ref[0], w1_ref[...]) + _dot(yw_ref[0], w2_ref[...]) + _dot(yn_ref[0], w3_ref[...])
    out = h_ref[0] + _rms(mix, nw_ref[...])
    o_ref[0] = jnp.where(_live_rows(TM, pl.program_id(1)), out, 0.0)


def _outproj(h, y_ssd, y_win, y_na, w1, w2, w3, nw):
    bsz, lp, _ = h.shape
    tile = lambda n: pl.BlockSpec((1, TM, n), lambda b, i: (b, i, 0))
    full = lambda arr: pl.BlockSpec(arr.shape, lambda b, i: (0,) * arr.ndim)
    return pl.pallas_call(
        _outproj_kernel,
        grid=(bsz, lp // TM),
        in_specs=[tile(D_MODEL), tile(y_ssd.shape[-1]), tile(y_win.shape[-1]), tile(y_na.shape[-1]),
                  full(w1), full(w2), full(w3), full(nw)],
        out_specs=tile(D_MODEL),
        out_shape=jax.ShapeDtypeStruct(h.shape, F32),
        compiler_params=_cparams(("parallel", "parallel")),
        name="outproj",
    )(h, y_ssd, y_win, y_na, w1, w2, w3, nw)


def _ffn_kernel(hc_ref, hp_ref, hn_ref, nw1_ref, wup_ref, cw_ref, cb_ref, wdn_ref, nw2_ref, o_ref,
                act_ref, *, nt):
    i = pl.program_id(1)
    nw1 = nw1_ref[...]
    f = jnp.concatenate([_rms(hp_ref[0], nw1), _rms(hc_ref[0], nw1), _rms(hn_ref[0], nw1)], axis=0).astype(BF16)
    tail = jnp.where(i < nt - 1, 1.0, 0.0)
    nch = D_FF // FF_CHUNK

    def conv(g, c):
        g = jnp.concatenate([g[:HALO + TM], g[HALO + TM:] * tail], axis=0)
        out = _shift_rows(g, -1) * cw_ref[c, 0:1, :] + cb_ref[c]
        out = out + g[HALO:HALO + TM] * cw_ref[c, 1:2, :]
        return out + _shift_rows(g, 1) * cw_ref[c, 2:3, :]

    for c in range(nch):
        gate = conv(_dot(f, wup_ref[c]), c)
        up = conv(_dot(f, wup_ref[nch + c]), nch + c)
        th = jnp.tanh(gate * (GELU_K + (GELU_K * GELU_C) * (gate * gate)))
        act_ref[:, c * FF_CHUNK:(c + 1) * FF_CHUNK] = (gate * (0.5 * th + 0.5) * up).astype(BF16)
    out = hc_ref[0] + _rms(_dot(act_ref[...], wdn_ref[...]), nw2_ref[...])
    o_ref[0] = jnp.where(_live_rows(TM, i), out, 0.0)


def _ffn(h, nw1, wup, cw, cb, wdn, nw2):
    bsz, lp, _ = h.shape
    nt = lp // TM
    hb = TM // HALO
    full = lambda arr: pl.BlockSpec(arr.shape, lambda b, i: (0,) * arr.ndim)
    tile = pl.BlockSpec((1, TM, D_MODEL), lambda b, i: (b, i, 0))
    prev = pl.BlockSpec((1, HALO, D_MODEL), lambda b, i: (b, jnp.maximum(i * hb - 1, 0), 0))
    nxt = pl.BlockSpec((1, HALO, D_MODEL), lambda b, i: (b, jnp.minimum((i + 1) * hb, nt * hb - 1), 0))
    return pl.pallas_call(
        functools.partial(_ffn_kernel, nt=nt),
        grid=(bsz, nt),
        in_specs=[tile, prev, nxt, full(nw1), full(wup), full(cw), full(cb), full(wdn), full(nw2)],
        out_specs=tile,
        out_shape=jax.ShapeDtypeStruct(h.shape, F32),
        scratch_shapes=[pltpu.VMEM((TM, D_FF), BF16)],
        compiler_params=_cparams(("parallel", "parallel")),
        name="ffn",
    )(h, h, h, nw1, wup, cw, cb, wdn, nw2)


def _pad_lanes(x, n):
    return jnp.pad(x, [(0, 0)] * (x.ndim - 1) + [(0, n - x.shape[-1])])


def _rope_tables(lp):
    half = ROPE_DIM // 2
    pos = jnp.maximum(jnp.arange(lp) - META0, 0).astype(F32)
    inv = jnp.power(ROPE_THETA, -jnp.arange(half, dtype=F32) / half)
    ang = pos[:, None] * inv[None, :]
    cos, sin = jnp.cos(ang), jnp.sin(ang)
    zeros, ones = jnp.zeros_like(cos), jnp.ones((lp, HEAD_DIM - ROPE_DIM), F32)
    rest = jnp.zeros((lp, HEAD_DIM - ROPE_DIM), F32)
    c = jnp.concatenate([cos, cos, ones], axis=1)
    s1 = jnp.concatenate([-sin, zeros, rest], axis=1)
    s2 = jnp.concatenate([zeros, sin, rest], axis=1)
    return tuple(jnp.tile(t, (1, 128 // HEAD_DIM)) for t in (c, s1, s2)) + (cos.T, sin.T)


def _layer_params(i, p):
    row = lambda v: v.reshape(1, -1).astype(F32)
    lanes = lambda v: jnp.broadcast_to(v.astype(F32)[:, None], (v.shape[0], BLK))
    sizes = [SSD_INNER, SSD_XBC, 2 * SSD_HEADS, WIN_Q_HEADS * HEAD_DIM, WIN_KV_HEADS * HEAD_DIM,
             WIN_KV_HEADS * HEAD_DIM, NA_HEADS * HEAD_DIM, NA_HEADS * HEAD_DIM, NA_HEADS * HEAD_DIM]
    w_in = p['w_in'][i].astype(BF16)
    ws = jnp.split(w_in, np.cumsum(sizes)[:-1].tolist(), axis=1)
    ws[2] = _pad_lanes(ws[2], 128)
    for j in (0, 3, 5, 6, 8):
        ws[j] = ws[j].T
    w_out = p['w_out'][i].astype(BF16)
    nch = D_FF // FF_CHUNK
    chunked = lambda m: m.reshape(m.shape[0], 2 * nch, FF_CHUNK).swapaxes(0, 1)
    return dict(
        norm_mix_pre=row(p['norm_mix_pre'][i]),
        w_in=ws,
        ssd_conv=(p['ssd_conv_w'][i].astype(F32), row(p['ssd_conv_b'][i])),
        ssd_consts=(_pad_lanes(row(p['ssd_dt_bias'][i]), 128), _pad_lanes(row(p['ssd_a_log'][i]), 128)),
        ssd_d=lanes(jnp.repeat(p['ssd_d'][i], HEAD_DIM)),
        ssd_norm_w=lanes(p['ssd_norm_w'][i]),
        win_sink=row(jnp.repeat(p['win_sink'][i], BLK)),
        na_bias=_na_bias_table(p['na_rpb'][i]),
        na_meta_bias=jnp.broadcast_to(p['na_meta_bias'][i].astype(F32)[:, :, None], (NA_HEADS, N_META, BLK)),
        w_out=(w_out[:SSD_INNER], w_out[SSD_INNER:SSD_INNER + WIN_Q_HEADS * HEAD_DIM],
               w_out[SSD_INNER + WIN_Q_HEADS * HEAD_DIM:]),
        norm_mix_post=row(p['norm_mix_post'][i]),
        norm_ffn_pre=row(p['norm_ffn_pre'][i]),
        ffn_w_up=chunked(p['ffn_w_up'][i].astype(BF16)),
        ffn_conv_w=chunked(p['ffn_conv_w'][i].astype(F32)),
        ffn_conv_b=chunked(p['ffn_conv_b'][i].astype(F32).reshape(1, -1)),
        ffn_w_down=p['ffn_w_down'][i].astype(BF16),
        norm_ffn_post=row(p['norm_ffn_post'][i]),
    )


def _encode(x, meta_tokens, layers):
    bsz, n_tok, _ = x.shape
    assert (n_tok + LEAD) % TM == 0 and n_tok % BLK == 0 and n_tok // GRID_W >= NA_KR
    lp = LEAD + n_tok
    meta = jnp.broadcast_to(meta_tokens.astype(F32)[None], (bsz, N_META, D_MODEL))
    h = jnp.concatenate([jnp.zeros((bsz, META0, D_MODEL), F32), meta, x.astype(F32)], axis=1)
    rope = _rope_tables(lp)
    for lw in layers:
        zt, xt, bc, acs, tab, wq, wk, wv, nq, nk, nv = _inproj(h, lw['norm_mix_pre'], lw['w_in'], lw['ssd_conv'],
                                                               lw['ssd_consts'], rope)
        y_fwd = _ssd_pass(xt, bc, acs, tab, None, reverse=False)
        y_ssd = _ssd_pass(xt, bc, acs, tab, (zt, y_fwd, lw['ssd_d'], lw['ssd_norm_w']), reverse=True)
        y_win = _win_attention(wq, wk, wv, lw['win_sink'])
        y_na = _na_attention(nq, nk, nv, lw['na_bias'], lw['na_meta_bias'])
        h = _outproj(h, y_ssd, y_win, y_na, *lw['w_out'], lw['norm_mix_post'])
        h = _ffn(h, lw['norm_ffn_pre'], lw['ffn_w_up'], lw['ffn_conv_w'], lw['ffn_conv_b'],
                 lw['ffn_w_down'], lw['norm_ffn_post'])
    return h[:, LEAD:]


def kernel(x_prompt, x_sample, meta_tokens, norm_mix_pre, norm_mix_post, w_in, ssd_conv_w, ssd_conv_b,
           ssd_dt_bias, ssd_a_log, ssd_d, ssd_norm_w, win_sink, na_rpb, na_meta_bias, w_out, norm_ffn_pre,
           norm_ffn_post, ffn_w_up, ffn_conv_w, ffn_conv_b, ffn_w_down):
    p = dict(norm_mix_pre=norm_mix_pre, norm_mix_post=norm_mix_post, w_in=w_in, ssd_conv_w=ssd_conv_w,
             ssd_conv_b=ssd_conv_b, ssd_dt_bias=ssd_dt_bias, ssd_a_log=ssd_a_log, ssd_d=ssd_d,
             ssd_norm_w=ssd_norm_w, win_sink=win_sink, na_rpb=na_rpb, na_meta_bias=na_meta_bias, w_out=w_out,
             norm_ffn_pre=norm_ffn_pre, norm_ffn_post=norm_ffn_post, ffn_w_up=ffn_w_up, ffn_conv_w=ffn_conv_w,
             ffn_conv_b=ffn_conv_b, ffn_w_down=ffn_w_down)
    layers = [_layer_params(i, p) for i in range(w_in.shape[0])]
    return (_encode(x_prompt, meta_tokens, layers), _encode(x_sample, meta_tokens, layers))
```

```python
import functools

import jax
import jax.numpy as jnp
import numpy as np
from jax import lax
from jax.experimental import pallas as pl
from jax.experimental.pallas import tpu as pltpu

F32 = jnp.float32
BF16 = jnp.bfloat16

D_MODEL = 1024
N_META = 16
GRID_W = 64
HEAD_DIM = 64

SSD_HEADS = 16
SSD_INNER = SSD_HEADS * HEAD_DIM
SSD_GROUPS = 2
SSD_STATE = 128
SSD_XBC = SSD_INNER + 2 * SSD_GROUPS * SSD_STATE
SSD_CONV = 5

WIN_Q_HEADS = 8
WIN_KV_HEADS = 2
WIN_RADIUS = 128
ROPE_THETA = 500000.0
ROPE_DIM = HEAD_DIM // 4

NA_HEADS = 8
NA_KR = 8
NA_KC = 16

D_FF = 2816
EPS = 1e-6
GELU_K = float(np.sqrt(2.0 / np.pi))
GELU_C = 0.044715

BLK = 128
LEAD = BLK
META0 = LEAD - N_META
HALO = 8
TM = 384
FF_CHUNK = 256
STEP_BLKS = TM // BLK
VMEM_LIMIT = 56 * 1024 * 1024


def _cparams(sem):
    return pltpu.CompilerParams(dimension_semantics=sem, vmem_limit_bytes=VMEM_LIMIT)


def _rms(x, w):
    return x * lax.rsqrt(jnp.mean(x * x, axis=-1, keepdims=True) + EPS) * w


def _dot(a, b):
    return jnp.dot(a, b, preferred_element_type=F32)


def _shift_rows(x, d):
    return pltpu.roll(x, (-d) % x.shape[0], 0)[HALO:HALO + TM]


def _live_rows(tile_rows, tile_index):
    row = tile_index * tile_rows + lax.broadcasted_iota(jnp.int32, (tile_rows, 1), 0)
    return row >= META0


def _dot_nt(a, b):
    return lax.dot_general(a, b, (((1,), (1,)), ((), ())), preferred_element_type=F32)


def _rope128(x, c, s1, s2):
    return x * c + pltpu.roll(x, 128 - ROPE_DIM // 2, 1) * s1 + pltpu.roll(x, ROPE_DIM // 2, 1) * s2


def _scan_tables(dt_raw, dt_bias, a_log, live, acs_ref, tab_ref):
    nh = 2 * SSD_HEADS
    dtr = dt_raw + dt_bias
    dt = jnp.maximum(dtr, 0.0) + jnp.log1p(jnp.exp(-jnp.abs(dtr)))
    dt = jnp.where(live, dt, 0.0)
    a = dt * (-jnp.exp(a_log))
    ri = lax.broadcasted_iota(jnp.int32, (BLK, BLK), 0)
    ci = lax.broadcasted_iota(jnp.int32, (BLK, BLK), 1)
    tri = (ci <= ri).astype(BF16)
    fwd = ci < SSD_HEADS
    fwd_rows = ri < SSD_HEADS
    for j in range(TM // BLK):
        rows = slice(j * BLK, (j + 1) * BLK)
        aj, dtj = a[rows], dt[rows]
        pre = sum(_dot(tri, part) for part in _split3(aj))
        total = pre[BLK - 1:BLK, :]
        acs = jnp.where(fwd, pre, total - pre + aj)
        acs_ref[0, rows, :] = acs
        acs_t = acs.T[:nh]
        tab_ref[0, j, 0] = acs_t
        tab_ref[0, j, 1] = dtj.T[:nh]
        tab_ref[0, j, 2] = (dtj * jnp.exp(total - acs)).T[:nh]
        tab_ref[0, j, 3] = jnp.exp(acs_t)
        chunk_sum = jnp.where(fwd_rows[:nh, 0:1], acs_t[:, BLK - 1:BLK], acs_t[:, 0:1])
        tab_ref[0, j, 4] = jnp.broadcast_to(jnp.exp(chunk_sum), (nh, BLK))


def _inproj_kernel(h_ref, hp_ref, hn_ref, nw_ref, wz_ref, wx_ref, wdt_ref, wwq_ref, wwk_ref, wwv_ref,
                   wnq_ref, wnk_ref, wnv_ref, cw_ref, cb_ref, dtb_ref, alog_ref, rc_ref, rs1_ref, rs2_ref, rct_ref, rst_ref,
                   z_ref, xt_ref, bc_ref, acs_ref, tab_ref, wq_ref, wk_ref, wv_ref, nq_ref, nk_ref, nv_ref, *, nt):
    nw = nw_ref[...]
    a = _rms(h_ref[0], nw).astype(BF16)
    tail = jnp.where(pl.program_id(1) < nt - 1, 1.0, 0.0)
    wx = wx_ref[...]
    xe = jnp.concatenate([_dot(_rms(hp_ref[0], nw).astype(BF16), wx), _dot(a, wx),
                          _dot(_rms(hn_ref[0], nw).astype(BF16), wx) * tail], axis=0)
    pad = SSD_CONV // 2
    acc = _shift_rows(xe, -pad) * cw_ref[0:1, :] + cb_ref[...]
    for j in range(1, SSD_CONV):
        acc = acc + _shift_rows(xe, j - pad) * cw_ref[j:j + 1, :]
    xc = acc * (1.0 / (1.0 + jnp.exp(-acc)))

    def put_blocks(ref, xt):
        for j in range(TM // BLK):
            ref[0, j] = xt[:, j * BLK:(j + 1) * BLK].astype(BF16)

    put_blocks(xt_ref, xc[:, :SSD_INNER].T)
    bc_ref[0] = xc[:, SSD_INNER:].astype(BF16)
    put_blocks(z_ref, _dot_nt(wz_ref[...], a))
    _scan_tables(_dot(a, wdt_ref[...]), dtb_ref[...], alog_ref[...], _live_rows(TM, pl.program_id(1)), acs_ref, tab_ref)

    scale = HEAD_DIM ** -0.5
    qt = _dot_nt(wwq_ref[...], a)
    cos_t, sin_t = rct_ref[...], rst_ref[...]
    half = ROPE_DIM // 2
    parts = []
    for hh in range(WIN_Q_HEADS):
        x1 = qt[hh * HEAD_DIM:hh * HEAD_DIM + half]
        x2 = qt[hh * HEAD_DIM + half:hh * HEAD_DIM + ROPE_DIM]
        parts += [x1 * cos_t - x2 * sin_t, x2 * cos_t + x1 * sin_t, qt[hh * HEAD_DIM + ROPE_DIM:(hh + 1) * HEAD_DIM]]
    put_blocks(wq_ref, jnp.concatenate(parts, axis=0) * scale)
    wk_ref[0] = _rope128(_dot(a, wwk_ref[...]), rc_ref[...], rs1_ref[...], rs2_ref[...]).astype(BF16)
    put_blocks(wv_ref, _dot_nt(wwv_ref[...], a))
    put_blocks(nq_ref, _dot_nt(wnq_ref[...], a) * scale)
    nk_ref[0] = _dot(a, wnk_ref[...]).astype(BF16)
    put_blocks(nv_ref, _dot_nt(wnv_ref[...], a))


def _inproj(h, nw, ws, conv, scan, rope):
    bsz, lp, _ = h.shape
    nt = lp // TM
    nb = lp // BLK
    hb = TM // HALO
    tile = lambda n: pl.BlockSpec((1, TM, n), lambda b, i: (b, i, 0))
    tblk = lambda n: pl.BlockSpec((1, TM // BLK, n, BLK), lambda b, i: (b, i, 0, 0))
    prev = pl.BlockSpec((1, HALO, D_MODEL), lambda b, i: (b, jnp.maximum(i * hb - 1, 0), 0))
    nxt = pl.BlockSpec((1, HALO, D_MODEL), lambda b, i: (b, jnp.minimum((i + 1) * hb, nt * hb - 1), 0))
    full = lambda arr: pl.BlockSpec(arr.shape, lambda b, i: (0,) * arr.ndim)
    rtab = pl.BlockSpec((TM, 128), lambda b, i: (i, 0))
    rtab_t = pl.BlockSpec((ROPE_DIM // 2, TM), lambda b, i: (0, i))
    tok = lambda n, d: (tile(n), jax.ShapeDtypeStruct((bsz, lp, n), d))
    blk = lambda n: (tblk(n), jax.ShapeDtypeStruct((bsz, nb, n, BLK), BF16))
    kvw = WIN_KV_HEADS * HEAD_DIM
    tabs = (pl.BlockSpec((1, TM // BLK, 5, 2 * SSD_HEADS, BLK), lambda b, i: (b, i, 0, 0, 0)),
            jax.ShapeDtypeStruct((bsz, nb, 5, 2 * SSD_HEADS, BLK), F32))
    outs = [blk(SSD_INNER), blk(SSD_INNER), tok(SSD_XBC - SSD_INNER, BF16), tok(128, F32), tabs, blk(WIN_Q_HEADS * HEAD_DIM), tok(kvw, BF16),
            blk(kvw), blk(NA_HEADS * HEAD_DIM), tok(NA_HEADS * HEAD_DIM, BF16), blk(NA_HEADS * HEAD_DIM)]
    return pl.pallas_call(
        functools.partial(_inproj_kernel, nt=nt),
        grid=(bsz, nt),
        in_specs=[tile(D_MODEL), prev, nxt, full(nw)] + [full(w) for w in ws] + [full(c) for c in conv + scan]
        + [rtab, rtab, rtab, rtab_t, rtab_t],
        out_specs=[o[0] for o in outs],
        out_shape=[o[1] for o in outs],
        compiler_params=_cparams(("parallel", "parallel")),
        name="inproj",
    )(h, h, h, nw, *ws, *conv, *scan, *rope)


def _split3(x):
    hi = x.astype(BF16)
    r1 = x - hi.astype(F32)
    mid = r1.astype(BF16)
    lo = (r1 - mid.astype(F32)).astype(BF16)
    return hi, mid, lo


def _ssd_kernel(*refs, reverse, final):
    if final:
        xt_ref, bc_ref, acs_ref, tab_ref, zt_ref, yo_ref, dsk_ref, nw_ref, o_ref, state_ref, ybuf_ref = refs
    else:
        xt_ref, bc_ref, acs_ref, tab_ref, o_ref, state_ref = refs

    @pl.when(pl.program_id(1) == 0)
    def _():
        state_ref[...] = jnp.zeros_like(state_ref)

    ri = lax.broadcasted_iota(jnp.int32, (BLK, BLK), 0)
    ci = lax.broadcasted_iota(jnp.int32, (BLK, BLK), 1)
    feeds = (ri >= ci) if reverse else (ri <= ci)
    hoff = SSD_HEADS if reverse else 0
    rep = SSD_HEADS // SSD_GROUPS
    for j in (reversed(range(STEP_BLKS)) if reverse else range(STEP_BLKS)):
        trows = slice(j * BLK, (j + 1) * BLK)
        acs = acs_ref[0, trows, :]
        acs_t, dt_t, dt_out_t, e_in_t, e_chunk = (tab_ref[0, j, k] for k in range(5))
        for g in range(SSD_GROUPS):
            bm = bc_ref[0, trows, g * SSD_STATE:(g + 1) * SSD_STATE]
            cm = bc_ref[0, trows, (SSD_GROUPS + g) * SSD_STATE:(SSD_GROUPS + g + 1) * SSD_STATE]
            cbt = _dot_nt(bm, cm)
            ht = state_ref[g]
            y_off = _dot_nt(ht.astype(BF16), cm)
            x_out, decay = [], []
            for r in range(rep):
                h = g * rep + r
                hl = hoff + h
                rows = slice(h * HEAD_DIM, (h + 1) * HEAD_DIM)
                xt = xt_ref[0, j, rows, :].astype(F32)
                x_in = (xt * dt_t[hl:hl + 1, :]).astype(BF16)
                x_out.append((xt * dt_out_t[hl:hl + 1, :]).astype(BF16))
                decay.append(jnp.broadcast_to(e_chunk[hl:hl + 1, :], (HEAD_DIM, SSD_STATE)))
                diff = acs_t[hl:hl + 1, :] - acs[:, hl:hl + 1]
                lt = (cbt * jnp.exp(jnp.where(feeds, diff, -jnp.inf))).astype(BF16)
                y = _dot(x_in, lt) + y_off[r * HEAD_DIM:(r + 1) * HEAD_DIM] * e_in_t[hl:hl + 1, :]
                if final:
                    ybuf_ref[rows, :] = y
                else:
                    o_ref[0, j, rows, :] = y
            s_new = _dot(jnp.concatenate(x_out, axis=0), bm)
            state_ref[g] = ht * jnp.concatenate(decay, axis=0) + s_new

        if final:
            y = ybuf_ref[...] + yo_ref[0, j] + dsk_ref[...] * xt_ref[0, j].astype(F32)
            z = zt_ref[0, j].astype(F32)
            y = y * (z * (1.0 / (1.0 + jnp.exp(-z))))
            y = y * lax.rsqrt(jnp.mean(y * y, axis=0, keepdims=True) + EPS) * nw_ref[...]
            o_ref[0, trows, :] = y.T.astype(BF16)


def _ssd_pass(xt, bc, acs, tab, final_args, *, reverse):
    bsz, nc = xt.shape[:2]
    ns = nc // STEP_BLKS
    order = (lambda c: ns - 1 - c) if reverse else (lambda c: c)
    final = final_args is not None
    tok = lambda n: pl.BlockSpec((1, TM, n), lambda b, c: (b, order(c), 0))
    feat = pl.BlockSpec((1, STEP_BLKS, SSD_INNER, BLK), lambda b, c: (b, order(c), 0, 0))
    full = lambda arr: pl.BlockSpec(arr.shape, lambda b, c: (0,) * arr.ndim)
    tabs = pl.BlockSpec((1, STEP_BLKS) + tab.shape[2:], lambda b, c: (b, order(c), 0, 0, 0))
    in_specs = [feat, tok(bc.shape[-1]), tok(128), tabs]
    args = [xt, bc, acs, tab]
    scratch = [pltpu.VMEM((SSD_GROUPS, SSD_INNER // SSD_GROUPS, SSD_STATE), F32)]
    if final:
        zt, y_other, dsk, nw = final_args
        in_specs += [feat, feat, full(dsk), full(nw)]
        args += [zt, y_other, dsk, nw]
        scratch.append(pltpu.VMEM((SSD_INNER, BLK), F32))
        out_spec, out_shape = tok(SSD_INNER), jax.ShapeDtypeStruct((bsz, nc * BLK, SSD_INNER), BF16)
    else:
        out_spec, out_shape = feat, jax.ShapeDtypeStruct((bsz, nc, SSD_INNER, BLK), F32)
    return pl.pallas_call(
        functools.partial(_ssd_kernel, reverse=reverse, final=final),
        grid=(bsz, ns),
        in_specs=in_specs,
        out_specs=out_spec,
        out_shape=out_shape,
        scratch_shapes=scratch,
        compiler_params=_cparams(("parallel", "arbitrary")),
        name="ssd_bwd" if reverse else "ssd_fwd",
    )(*args)


def _softmax_pv_t(scores, values, pad_last, extra_logit=None):
    mx = functools.reduce(jnp.maximum, [jnp.max(s, axis=0, keepdims=True) for s in scores])
    if extra_logit is not None:
        mx = jnp.maximum(mx, extra_logit)
    ps = [jnp.exp(s - mx) for s in scores]
    den = functools.reduce(jnp.add, [jnp.sum(p, axis=0, keepdims=True) for p in ps])
    if extra_logit is not None:
        den = den + jnp.exp(extra_logit - mx)
    pb = [p.astype(BF16) for p in ps]
    pb[-1] = jnp.concatenate([pad_last, pb[-1]], axis=0)
    out = functools.reduce(jnp.add, [_dot(v, p) for v, p in zip(values, pb)])
    return out, den


def _block_diag_rows(pieces):
    zero = jnp.zeros_like(pieces[0][0])
    rows = []
    for i in range(len(pieces)):
        row = []
        for j, ps in enumerate(pieces):
            row += [p if i == j else zero for p in ps]
        rows.append(jnp.concatenate(row, axis=1))
    return jnp.concatenate(rows, axis=0)


def _win_kernel(qt_ref, kp_ref, kc_ref, kn_ref, km_ref, vp_ref, vc_ref, vn_ref, vm_ref, sink_ref, o_ref, *, nb):
    width = WIN_Q_HEADS * BLK
    ki = lax.broadcasted_iota(jnp.int32, (BLK, width), 0)
    qi = lax.broadcasted_iota(jnp.int32, (BLK, width), 1) & (BLK - 1)
    rep = WIN_Q_HEADS // WIN_KV_HEADS
    keys = [kp_ref[0]] + [kc_ref[0, j * BLK:(j + 1) * BLK, :] for j in range(STEP_BLKS)] + [kn_ref[0]]
    vals = [vp_ref[0, 0]] + [vc_ref[0, j] for j in range(STEP_BLKS)] + [vn_ref[0, 0]]
    kmeta, vmeta = km_ref[0, META0:, :], vm_ref[0, 0]
    pad_last = jnp.zeros((META0, width), BF16)
    for j in range(STEP_BLKS):
        n = pl.program_id(1) * STEP_BLKS + j
        ok_prev = jnp.logical_and(ki >= qi, n >= 2)
        ok_cur = n >= 1
        ok_next = jnp.logical_and(ki <= qi, n + 1 <= nb - 1)
        heads = [qt_ref[0, j, h * HEAD_DIM:(h + 1) * HEAD_DIM, :] for h in range(WIN_Q_HEADS)]
        qbd = _block_diag_rows([heads[g * rep:(g + 1) * rep] for g in range(WIN_KV_HEADS)])
        scores = [jnp.where(ok_prev, _dot(keys[j], qbd), -jnp.inf),
                  jnp.where(ok_cur, _dot(keys[j + 1], qbd), -jnp.inf),
                  jnp.where(ok_next, _dot(keys[j + 2], qbd), -jnp.inf),
                  _dot(kmeta, qbd)]
        o, den = _softmax_pv_t(scores, vals[j:j + 3] + [vmeta], pad_last, extra_logit=sink_ref[...])
        o = o / den
        outs = [o[(h // rep) * HEAD_DIM:(h // rep + 1) * HEAD_DIM, h * BLK:(h + 1) * BLK] for h in range(WIN_Q_HEADS)]
        o_ref[0, j * BLK:(j + 1) * BLK, :] = jnp.concatenate(outs, axis=0).T.astype(BF16)


def _win_attention(qt, k, vt, sink):
    bsz, nb = qt.shape[:2]
    ns = nb // STEP_BLKS
    kvw = WIN_KV_HEADS * HEAD_DIM
    assert kvw == BLK
    kedge = lambda f: pl.BlockSpec((1, BLK, kvw), lambda b, i: (b, f(i), 0))
    vedge = lambda f: pl.BlockSpec((1, 1, kvw, BLK), lambda b, i: (b, f(i), 0, 0))
    before = lambda i: jnp.maximum(i * STEP_BLKS - 1, 0)
    after = lambda i: jnp.minimum((i + 1) * STEP_BLKS, nb - 1)
    first = lambda i: 0
    return pl.pallas_call(
        functools.partial(_win_kernel, nb=nb),
        grid=(bsz, ns),
        in_specs=[pl.BlockSpec((1, STEP_BLKS, WIN_Q_HEADS * HEAD_DIM, BLK), lambda b, i: (b, i, 0, 0)),
                  kedge(before), pl.BlockSpec((1, TM, kvw), lambda b, i: (b, i, 0)), kedge(after), kedge(first),
                  vedge(before), pl.BlockSpec((1, STEP_BLKS, kvw, BLK), lambda b, i: (b, i, 0, 0)), vedge(after),
                  vedge(first), pl.BlockSpec(sink.shape, lambda b, i: (0, 0))],
        out_specs=pl.BlockSpec((1, TM, WIN_Q_HEADS * HEAD_DIM), lambda b, i: (b, i, 0)),
        out_shape=jax.ShapeDtypeStruct((bsz, nb * BLK, WIN_Q_HEADS * HEAD_DIM), BF16),
        compiler_params=_cparams(("parallel", "parallel")),
        name="win_attn",
    )(qt, k, k, k, k, vt, vt, vt, vt, sink)


NA_ROWS_PER_BLK = BLK // GRID_W
NA_WIN_BLKS = NA_KR // NA_ROWS_PER_BLK + 1
NA_WIN_ROWS = NA_WIN_BLKS * NA_ROWS_PER_BLK
NA_E_MIN = -1


def _na_bias_table(rpb):
    c = np.arange(GRID_W)[None, :]
    kc = np.arange(GRID_W)[:, None]
    cs = np.clip(c - NA_KC // 2, 0, GRID_W - NA_KC)
    ok = (kc >= cs) & (kc < cs + NA_KC)
    dc = np.clip(kc - c + NA_KC - 1, 0, 2 * NA_KC - 2)
    t = jnp.where(jnp.asarray(ok)[None, None], rpb.astype(F32)[:, :, dc], -jnp.inf)
    ninf = jnp.full((rpb.shape[0], 2, GRID_W, GRID_W), -jnp.inf, F32)
    ext = jnp.concatenate([ninf, t, ninf], axis=1)
    n_e = 2 * NA_KR - 1 + 3
    return jnp.concatenate([ext[:, 1:1 + n_e], ext[:, 0:n_e]], axis=-1)


def _na_kernel(qt_ref, k_ref, vt_ref, bias_ref, mbt_ref, o_ref, *, rows):
    pair = 2 * HEAD_DIM
    width = 2 * BLK
    first = lax.broadcasted_iota(jnp.int32, (BLK, width), 0) < GRID_W
    left = (lax.broadcasted_iota(jnp.int32, (BLK, width), 1) & (BLK - 1)) < GRID_W
    pad_last = jnp.zeros((META0, width), BF16)

    def block(j, meta):
        n = pl.program_id(1) * STEP_BLKS + j
        if meta:
            rq = (0, 0)
            rs = (0, 0)
            wb = 0
            kb0 = LEAD // BLK
            start = LEAD
        else:
            r0 = (n - 1) * NA_ROWS_PER_BLK
            rq = (r0, r0 + 1)
            rs = tuple(jnp.clip(r - NA_KR // 2, 0, rows - NA_KR) for r in rq)
            wb = jnp.clip(r0 - NA_KR // 2, 0, rows - NA_WIN_ROWS)
            kb0 = LEAD // BLK + lax.shift_right_logical(wb, 1)
            start = pl.multiple_of(LEAD + wb * GRID_W, BLK)
        masks = []
        for ib in range(NA_WIN_BLKS):
            sel = []
            for kr in (wb + 2 * ib, wb + 2 * ib + 1):
                if meta:
                    oks = [jnp.int32(rs[j] <= kr < rs[j] + NA_KR) for j in range(2)]
                else:
                    oks = [jnp.logical_and(kr >= rs[j], kr < rs[j] + NA_KR).astype(jnp.int32) for j in range(2)]
                sel.append(jnp.where(left, oks[0], oks[1]))
            masks.append(jnp.where(first, sel[0], sel[1]) > 0)
        outs = []
        for pp in range(NA_HEADS // 2):
            cols = slice(pp * pair, (pp + 1) * pair)
            hs = (2 * pp, 2 * pp + 1)
            qbd = _block_diag_rows([[qt_ref[0, j, h * HEAD_DIM:(h + 1) * HEAD_DIM, :]] for h in hs])
            vals = [vt_ref[0, kb0 + ib, cols, :] for ib in range(NA_WIN_BLKS)] + [vt_ref[0, 0, cols, :]]
            scores = []
            for ib in range(NA_WIN_BLKS):
                e = wb + 2 * ib - rq[0] + (NA_KR - 1) - NA_E_MIN
                if meta:
                    col = [jnp.concatenate([bias_ref[h, e][:, 0:1], bias_ref[h, e + 1][:, 0:1]], axis=0) for h in hs]
                    bias = jnp.concatenate([jnp.broadcast_to(c, (BLK, BLK)) for c in col], axis=1)
                else:
                    bias = jnp.concatenate(
                        [jnp.concatenate([bias_ref[h, e], bias_ref[h, e + 1]], axis=0) for h in hs], axis=1)
                s = _dot(k_ref[0, pl.ds(start + ib * BLK, BLK), cols], qbd)
                scores.append(jnp.where(masks[ib], s + bias, -jnp.inf))
            scores.append(_dot(k_ref[0, META0:LEAD, cols], qbd) + jnp.concatenate([mbt_ref[h] for h in hs], axis=1))
            o, den = _softmax_pv_t(scores, vals, pad_last)
            o = o / den
            outs += [o[hh * HEAD_DIM:(hh + 1) * HEAD_DIM, hh * BLK:(hh + 1) * BLK] for hh in range(2)]
        o_ref[0, j * BLK:(j + 1) * BLK, :] = jnp.concatenate(outs, axis=0).T.astype(BF16)

    @pl.when(pl.program_id(1) == 0)
    def _():
        block(0, True)

    @pl.when(pl.program_id(1) > 0)
    def _():
        block(0, False)

    for j in range(1, STEP_BLKS):
        block(j, False)


def _na_attention(qt, k, vt, bias, mbt):
    bsz, nb, width, _ = qt.shape
    lp = nb * BLK
    rows = (lp - LEAD) // GRID_W
    assert rows >= NA_WIN_ROWS and rows % NA_ROWS_PER_BLK == 0
    qblk = pl.BlockSpec((1, STEP_BLKS, width, BLK), lambda b, n: (b, n, 0, 0))
    kseq = pl.BlockSpec((1, lp, width), lambda b, n: (b, 0, 0), pipeline_mode=pl.Buffered(1))
    vseq = pl.BlockSpec((1, nb, width, BLK), lambda b, n: (b, 0, 0, 0), pipeline_mode=pl.Buffered(1))
    full = lambda arr: pl.BlockSpec(arr.shape, lambda b, n: (0,) * arr.ndim)
    return pl.pallas_call(
        functools.partial(_na_kernel, rows=rows),
        grid=(bsz, nb // STEP_BLKS),
        in_specs=[qblk, kseq, vseq, full(bias), full(mbt)],
        out_specs=pl.BlockSpec((1, TM, width), lambda b, n: (b, n, 0)),
        out_shape=jax.ShapeDtypeStruct((bsz, lp, width), BF16),
        compiler_params=_cparams(("parallel", "arbitrary")),
        name="na_attn",
    )(qt, k, vt, bias, mbt)


def _outproj_kernel(h_ref, ys_ref, yw_ref, yn_ref, w1_ref, w2_ref, w3_ref, nw_ref, o_ref):
    mix = _dot(ys_ref[0], w1_ref[...]) + _dot(yw_ref[0], w2_ref[...]) + _dot(yn_ref[0], w3_ref[...])
    out = h_ref[0] + _rms(mix, nw_ref[...])
    o_ref[0] = jnp.where(_live_rows(TM, pl.program_id(1)), out, 0.0)


def _outproj(h, y_ssd, y_win, y_na, w1, w2, w3, nw):
    bsz, lp, _ = h.shape
    tile = lambda n: pl.BlockSpec((1, TM, n), lambda b, i: (b, i, 0))
    full = lambda arr: pl.BlockSpec(arr.shape, lambda b, i: (0,) * arr.ndim)
    return pl.pallas_call(
        _outproj_kernel,
        grid=(bsz, lp // TM),
        in_specs=[tile(D_MODEL), tile(y_ssd.shape[-1]), tile(y_win.shape[-1]), tile(y_na.shape[-1]),
                  full(w1), full(w2), full(w3), full(nw)],
        out_specs=tile(D_MODEL),
        out_shape=jax.ShapeDtypeStruct(h.shape, F32),
        compiler_params=_cparams(("parallel", "parallel")),
        name="outproj",
    )(h, y_ssd, y_win, y_na, w1, w2, w3, nw)


def _ffn_kernel(hc_ref, hp_ref, hn_ref, nw1_ref, wup_ref, cw_ref, cb_ref, wdn_ref, nw2_ref, o_ref,
                act_ref, *, nt):
    i = pl.program_id(1)
    nw1 = nw1_ref[...]
    f = jnp.concatenate([_rms(hp_ref[0], nw1), _rms(hc_ref[0], nw1), _rms(hn_ref[0], nw1)], axis=0).astype(BF16)
    tail = jnp.where(i < nt - 1, 1.0, 0.0)
    nch = D_FF // FF_CHUNK

    def conv(g, c):
        g = jnp.concatenate([g[:HALO + TM], g[HALO + TM:] * tail], axis=0)
        out = _shift_rows(g, -1) * cw_ref[c, 0:1, :] + cb_ref[c]
        out = out + g[HALO:HALO + TM] * cw_ref[c, 1:2, :]
        return out + _shift_rows(g, 1) * cw_ref[c, 2:3, :]

    for c in range(nch):
        gate = conv(_dot(f, wup_ref[c]), c)
        up = conv(_dot(f, wup_ref[nch + c]), nch + c)
        th = jnp.tanh(gate * (GELU_K + (GELU_K * GELU_C) * (gate * gate)))
        act_ref[:, c * FF_CHUNK:(c + 1) * FF_CHUNK] = (gate * (0.5 * th + 0.5) * up).astype(BF16)
    out = hc_ref[0] + _rms(_dot(act_ref[...], wdn_ref[...]), nw2_ref[...])
    o_ref[0] = jnp.where(_live_rows(TM, i), out, 0.0)


def _ffn(h, nw1, wup, cw, cb, wdn, nw2):
    bsz, lp, _ = h.shape
    nt = lp // TM
    hb = TM // HALO
    full = lambda arr: pl.BlockSpec(arr.shape, lambda b, i: (0,) * arr.ndim)
    tile = pl.BlockSpec((1, TM, D_MODEL), lambda b, i: (b, i, 0))
    prev = pl.BlockSpec((1, HALO, D_MODEL), lambda b, i: (b, jnp.maximum(i * hb - 1, 0), 0))
    nxt = pl.BlockSpec((1, HALO, D_MODEL), lambda b, i: (b, jnp.minimum((i + 1) * hb, nt * hb - 1), 0))
    return pl.pallas_call(
        functools.partial(_ffn_kernel, nt=nt),
        grid=(bsz, nt),
        in_specs=[tile, prev, nxt, full(nw1), full(wup), full(cw), full(cb), full(wdn), full(nw2)],
        out_specs=tile,
        out_shape=jax.ShapeDtypeStruct(h.shape, F32),
        scratch_shapes=[pltpu.VMEM((TM, D_FF), BF16)],
        compiler_params=_cparams(("parallel", "parallel")),
        name="ffn",
    )(h, h, h, nw1, wup, cw, cb, wdn, nw2)


def _pad_lanes(x, n):
    return jnp.pad(x, [(0, 0)] * (x.ndim - 1) + [(0, n - x.shape[-1])])


def _rope_tables(lp):
    half = ROPE_DIM // 2
    pos = jnp.maximum(jnp.arange(lp) - META0, 0).astype(F32)
    inv = jnp.power(ROPE_THETA, -jnp.arange(half, dtype=F32) / half)
    ang = pos[:, None] * inv[None, :]
    cos, sin = jnp.cos(ang), jnp.sin(ang)
    zeros, ones = jnp.zeros_like(cos), jnp.ones((lp, HEAD_DIM - ROPE_DIM), F32)
    rest = jnp.zeros((lp, HEAD_DIM - ROPE_DIM), F32)
    c = jnp.concatenate([cos, cos, ones], axis=1)
    s1 = jnp.concatenate([-sin, zeros, rest], axis=1)
    s2 = jnp.concatenate([zeros, sin, rest], axis=1)
    return tuple(jnp.tile(t, (1, 128 // HEAD_DIM)) for t in (c, s1, s2)) + (cos.T, sin.T)


def _layer_params(i, p):
    row = lambda v: v.reshape(1, -1).astype(F32)
    lanes = lambda v: jnp.broadcast_to(v.astype(F32)[:, None], (v.shape[0], BLK))
    sizes = [SSD_INNER, SSD_XBC, 2 * SSD_HEADS, WIN_Q_HEADS * HEAD_DIM, WIN_KV_HEADS * HEAD_DIM,
             WIN_KV_HEADS * HEAD_DIM, NA_HEADS * HEAD_DIM, NA_HEADS * HEAD_DIM, NA_HEADS * HEAD_DIM]
    w_in = p['w_in'][i].astype(BF16)
    ws = jnp.split(w_in, np.cumsum(sizes)[:-1].tolist(), axis=1)
    ws[2] = _pad_lanes(ws[2], 128)
    for j in (0, 3, 5, 6, 8):
        ws[j] = ws[j].T
    w_out = p['w_out'][i].astype(BF16)
    nch = D_FF // FF_CHUNK
    chunked = lambda m: m.reshape(m.shape[0], 2 * nch, FF_CHUNK).swapaxes(0, 1)
    return dict(
        norm_mix_pre=row(p['norm_mix_pre'][i]),
        w_in=ws,
        ssd_conv=(p['ssd_conv_w'][i].astype(F32), row(p['ssd_conv_b'][i])),
        ssd_consts=(_pad_lanes(row(p['ssd_dt_bias'][i]), 128), _pad_lanes(row(p['ssd_a_log'][i]), 128)),
        ssd_d=lanes(jnp.repeat(p['ssd_d'][i], HEAD_DIM)),
        ssd_norm_w=lanes(p['ssd_norm_w'][i]),
        win_sink=row(jnp.repeat(p['win_sink'][i], BLK)),
        na_bias=_na_bias_table(p['na_rpb'][i]),
        na_meta_bias=jnp.broadcast_to(p['na_meta_bias'][i].astype(F32)[:, :, None], (NA_HEADS, N_META, BLK)),
        w_out=(w_out[:SSD_INNER], w_out[SSD_INNER:SSD_INNER + WIN_Q_HEADS * HEAD_DIM],
               w_out[SSD_INNER + WIN_Q_HEADS * HEAD_DIM:]),
        norm_mix_post=row(p['norm_mix_post'][i]),
        norm_ffn_pre=row(p['norm_ffn_pre'][i]),
        ffn_w_up=chunked(p['ffn_w_up'][i].astype(BF16)),
        ffn_conv_w=chunked(p['ffn_conv_w'][i].astype(F32)),
        ffn_conv_b=chunked(p['ffn_conv_b'][i].astype(F32).reshape(1, -1)),
        ffn_w_down=p['ffn_w_down'][i].astype(BF16),
        norm_ffn_post=row(p['norm_ffn_post'][i]),
    )


def _encode(x, meta_tokens, layers):
    bsz, n_tok, _ = x.shape
    assert (n_tok + LEAD) % TM == 0 and n_tok % BLK == 0 and n_tok // GRID_W >= NA_KR
    lp = LEAD + n_tok
    meta = jnp.broadcast_to(meta_tokens.astype(F32)[None], (bsz, N_META, D_MODEL))
    h = jnp.concatenate([jnp.zeros((bsz, META0, D_MODEL), F32), meta, x.astype(F32)], axis=1)
    rope = _rope_tables(lp)
    for lw in layers:
        zt, xt, bc, acs, tab, wq, wk, wv, nq, nk, nv = _inproj(h, lw['norm_mix_pre'], lw['w_in'], lw['ssd_conv'],
                                                               lw['ssd_consts'], rope)
        y_fwd = _ssd_pass(xt, bc, acs, tab, None, reverse=False)
        y_ssd = _ssd_pass(xt, bc, acs, tab, (zt, y_fwd, lw['ssd_d'], lw['ssd_norm_w']), reverse=True)
        y_win = _win_attention(wq, wk, wv, lw['win_sink'])
        y_na = _na_attention(nq, nk, nv, lw['na_bias'], lw['na_meta_bias'])
        h = _outproj(h, y_ssd, y_win, y_na, *lw['w_out'], lw['norm_mix_post'])
        h = _ffn(h, lw['norm_ffn_pre'], lw['ffn_w_up'], lw['ffn_conv_w'], lw['ffn_conv_b'],
                 lw['ffn_w_down'], lw['norm_ffn_post'])
    return h[:, LEAD:]


def kernel(x_prompt, x_sample, meta_tokens, norm_mix_pre, norm_mix_post, w_in, ssd_conv_w, ssd_conv_b,
           ssd_dt_bias, ssd_a_log, ssd_d, ssd_norm_w, win_sink, na_rpb, na_meta_bias, w_out, norm_ffn_pre,
           norm_ffn_post, ffn_w_up, ffn_conv_w, ffn_conv_b, ffn_w_down):
    p = dict(norm_mix_pre=norm_mix_pre, norm_mix_post=norm_mix_post, w_in=w_in, ssd_conv_w=ssd_conv_w,
             ssd_conv_b=ssd_conv_b, ssd_dt_bias=ssd_dt_bias, ssd_a_log=ssd_a_log, ssd_d=ssd_d,
             ssd_norm_w=ssd_norm_w, win_sink=win_sink, na_rpb=na_rpb, na_meta_bias=na_meta_bias, w_out=w_out,
             norm_ffn_pre=norm_ffn_pre, norm_ffn_post=norm_ffn_post, ffn_w_up=ffn_w_up, ffn_conv_w=ffn_conv_w,
             ffn_conv_b=ffn_conv_b, ffn_w_down=ffn_w_down)
    layers = [_layer_params(i, p) for i in range(w_in.shape[0])]
    return (_encode(x_prompt, meta_tokens, layers), _encode(x_sample, meta_tokens, layers))
```

```python
import functools

import jax
import jax.numpy as jnp
import numpy as np
from jax import lax
from jax.experimental import pallas as pl
from jax.experimental.pallas import tpu as pltpu

F32 = jnp.float32
BF16 = jnp.bfloat16

D_MODEL = 1024
N_META = 16
GRID_W = 64
HEAD_DIM = 64

SSD_HEADS = 16
SSD_INNER = SSD_HEADS * HEAD_DIM
SSD_GROUPS = 2
SSD_STATE = 128
SSD_XBC = SSD_INNER + 2 * SSD_GROUPS * SSD_STATE
SSD_CONV = 5

WIN_Q_HEADS = 8
WIN_KV_HEADS = 2
WIN_RADIUS = 128
ROPE_THETA = 500000.0
ROPE_DIM = HEAD_DIM // 4

NA_HEADS = 8
NA_KR = 8
NA_KC = 16

D_FF = 2816
EPS = 1e-6
GELU_K = float(np.sqrt(2.0 / np.pi))
GELU_C = 0.044715

BLK = 128
LEAD = BLK
META0 = LEAD - N_META
HALO = 8
TM = 384
TM_OUT = 512
FF_CHUNK = 256
STEP_BLKS = TM // BLK
VMEM_LIMIT = 56 * 1024 * 1024


def _cparams(sem):
    return pltpu.CompilerParams(dimension_semantics=sem, vmem_limit_bytes=VMEM_LIMIT)


def _rms(x, w):
    return x * lax.rsqrt(jnp.mean(x * x, axis=-1, keepdims=True) + EPS) * w


def _dot(a, b):
    return jnp.dot(a, b, preferred_element_type=F32)


def _shift_rows(x, d):
    return pltpu.roll(x, (-d) % x.shape[0], 0)[HALO:x.shape[0] - HALO]


def _load_tile(refs, lead_ref, i):
    if len(refs) == 1:
        return refs[0][0]
    blocks = [r[0, 0] for r in refs]
    if lead_ref is not None:
        blocks[0] = jnp.where(i == 0, lead_ref[...], blocks[0])
    return jnp.concatenate(blocks, axis=0)


def _tile_specs(src, tm):
    if src[0] == 'h':
        return [pl.BlockSpec((1, tm, D_MODEL), lambda b, i: (b, i, 0))], [src[1]]
    x, lead = src[1], src[2]
    x4 = x.reshape(x.shape[0], x.shape[1] // BLK, BLK, D_MODEL)
    k = tm // BLK
    specs = [pl.BlockSpec((1, 1, BLK, D_MODEL), lambda b, i, j=j: (b, jnp.maximum(i * k + j - LEAD // BLK, 0), 0, 0))
             for j in range(k)]
    return [pl.BlockSpec(lead.shape, lambda b, i: (0, 0))] + specs, [lead] + [x4] * k


def _halo_specs(src, tm):
    arr = src[1]
    off = 0 if src[0] == 'h' else LEAD // HALO
    last = arr.shape[1] // HALO - 1
    hb = tm // HALO
    prev = pl.BlockSpec((1, HALO, D_MODEL), lambda b, i: (b, jnp.clip(i * hb - 1 - off, 0, last), 0))
    nxt = pl.BlockSpec((1, HALO, D_MODEL), lambda b, i: (b, jnp.clip((i + 1) * hb - off, 0, last), 0))
    return [prev, nxt], [arr, arr]


def _live_rows(tile_rows, tile_index):
    row = tile_index * tile_rows + lax.broadcasted_iota(jnp.int32, (tile_rows, 1), 0)
    return row >= META0


def _dot_nt(a, b):
    return lax.dot_general(a, b, (((1,), (1,)), ((), ())), preferred_element_type=F32)


def _rope128(x, c, s1, s2):
    return x * c + pltpu.roll(x, 128 - ROPE_DIM // 2, 1) * s1 + pltpu.roll(x, ROPE_DIM // 2, 1) * s2


def _scan_tables(dt_raw, dt_bias, a_log, live, acs_ref, tab_ref):
    nh = 2 * SSD_HEADS
    dtr = dt_raw + dt_bias
    dt = jnp.maximum(dtr, 0.0) + jnp.log1p(jnp.exp(-jnp.abs(dtr)))
    dt = jnp.where(live, dt, 0.0)
    a = dt * (-jnp.exp(a_log))
    ri = lax.broadcasted_iota(jnp.int32, (BLK, BLK), 0)
    ci = lax.broadcasted_iota(jnp.int32, (BLK, BLK), 1)
    tri = (ci <= ri).astype(BF16)
    fwd = ci < SSD_HEADS
    fwd_rows = ri < SSD_HEADS
    for j in range(TM // BLK):
        rows = slice(j * BLK, (j + 1) * BLK)
        aj, dtj = a[rows], dt[rows]
        pre = sum(_dot(tri, part) for part in _split3(aj))
        total = pre[BLK - 1:BLK, :]
        acs = jnp.where(fwd, pre, total - pre + aj)
        acs_ref[0, rows, :] = acs
        acs_t = acs.T[:nh]
        tab_ref[0, j, 0] = acs_t
        tab_ref[0, j, 1] = dtj.T[:nh]
        tab_ref[0, j, 2] = (dtj * jnp.exp(total - acs)).T[:nh]
        tab_ref[0, j, 3] = jnp.exp(acs_t)
        chunk_sum = jnp.where(fwd_rows[:nh, 0:1], acs_t[:, BLK - 1:BLK], acs_t[:, 0:1])
        tab_ref[0, j, 4] = jnp.broadcast_to(jnp.exp(chunk_sum), (nh, BLK))


def _inproj_kernel(*refs, nt, ntile):
    lead_ref = refs[0] if ntile > 1 else None
    tile_refs = refs[ntile > 1:ntile + (ntile > 1)]
    (hp_ref, hn_ref, nw_ref, wz_ref, wx_ref, wdt_ref, wwq_ref, wwk_ref, wwv_ref,
     wnq_ref, wnk_ref, wnv_ref, cw_ref, cb_ref, dtb_ref, alog_ref, rc_ref, rs1_ref, rs2_ref, rct_ref, rst_ref,
     z_ref, xt_ref, bc_ref, acs_ref, tab_ref, wq_ref, wk_ref, wv_ref, nq_ref, nk_ref, nv_ref) = refs[ntile + (ntile > 1):]
    i = pl.program_id(1)
    nw = nw_ref[...]
    a = _rms(_load_tile(tile_refs, lead_ref, i), nw).astype(BF16)
    head = jnp.where(i > 0, 1.0, 0.0)
    tail = jnp.where(i < nt - 1, 1.0, 0.0)
    wx = wx_ref[...]
    xe = jnp.concatenate([_dot(_rms(hp_ref[0], nw).astype(BF16), wx) * head, _dot(a, wx),
                          _dot(_rms(hn_ref[0], nw).astype(BF16), wx) * tail], axis=0)
    pad = SSD_CONV // 2
    acc = _shift_rows(xe, -pad) * cw_ref[0:1, :] + cb_ref[...]
    for j in range(1, SSD_CONV):
        acc = acc + _shift_rows(xe, j - pad) * cw_ref[j:j + 1, :]
    xc = acc * (1.0 / (1.0 + jnp.exp(-acc)))

    def put_blocks(ref, xt):
        for j in range(TM // BLK):
            ref[0, j] = xt[:, j * BLK:(j + 1) * BLK].astype(BF16)

    put_blocks(xt_ref, xc[:, :SSD_INNER].T)
    bc_ref[0] = xc[:, SSD_INNER:].astype(BF16)
    put_blocks(z_ref, _dot(a, wz_ref[...]).T)
    _scan_tables(_dot(a, wdt_ref[...]), dtb_ref[...], alog_ref[...], _live_rows(TM, i), acs_ref, tab_ref)

    scale = HEAD_DIM ** -0.5
    qt = _dot(a, wwq_ref[...]).T
    cos_t, sin_t = rct_ref[...], rst_ref[...]
    half = ROPE_DIM // 2
    parts = []
    for hh in range(WIN_Q_HEADS):
        x1 = qt[hh * HEAD_DIM:hh * HEAD_DIM + half]
        x2 = qt[hh * HEAD_DIM + half:hh * HEAD_DIM + ROPE_DIM]
        parts += [x1 * cos_t - x2 * sin_t, x2 * cos_t + x1 * sin_t, qt[hh * HEAD_DIM + ROPE_DIM:(hh + 1) * HEAD_DIM]]
    put_blocks(wq_ref, jnp.concatenate(parts, axis=0) * scale)
    wk_ref[0] = _rope128(_dot(a, wwk_ref[...]), rc_ref[...], rs1_ref[...], rs2_ref[...]).astype(BF16)
    put_blocks(wv_ref, _dot(a, wwv_ref[...]).T)
    put_blocks(nq_ref, (_dot(a, wnq_ref[...]) * scale).T)
    nk_ref[0] = _dot(a, wnk_ref[...]).astype(BF16)
    put_blocks(nv_ref, _dot(a, wnv_ref[...]).T)


def _inproj(src, lp, nw, ws, conv, scan, rope):
    bsz = src[1].shape[0]
    nt = lp // TM
    nb = lp // BLK
    tile = lambda n: pl.BlockSpec((1, TM, n), lambda b, i: (b, i, 0))
    tblk = lambda n: pl.BlockSpec((1, TM // BLK, n, BLK), lambda b, i: (b, i, 0, 0))
    tile_specs, tile_args = _tile_specs(src, TM)
    halo_specs, halo_args = _halo_specs(src, TM)
    full = lambda arr: pl.BlockSpec(arr.shape, lambda b, i: (0,) * arr.ndim)
    rtab = pl.BlockSpec((TM, 128), lambda b, i: (i, 0))
    rtab_t = pl.BlockSpec((ROPE_DIM // 2, TM), lambda b, i: (0, i))
    tok = lambda n, d: (tile(n), jax.ShapeDtypeStruct((bsz, lp, n), d))
    blk = lambda n: (tblk(n), jax.ShapeDtypeStruct((bsz, nb, n, BLK), BF16))
    kvw = WIN_KV_HEADS * HEAD_DIM
    tabs = (pl.BlockSpec((1, TM // BLK, 5, 2 * SSD_HEADS, BLK), lambda b, i: (b, i, 0, 0, 0)),
            jax.ShapeDtypeStruct((bsz, nb, 5, 2 * SSD_HEADS, BLK), F32))
    outs = [blk(SSD_INNER), blk(SSD_INNER), tok(SSD_XBC - SSD_INNER, BF16), tok(128, F32), tabs, blk(WIN_Q_HEADS * HEAD_DIM), tok(kvw, BF16),
            blk(kvw), blk(NA_HEADS * HEAD_DIM), tok(NA_HEADS * HEAD_DIM, BF16), blk(NA_HEADS * HEAD_DIM)]
    return pl.pallas_call(
        functools.partial(_inproj_kernel, nt=nt, ntile=len(tile_specs) - (src[0] == 'x')),
        grid=(bsz, nt),
        in_specs=tile_specs + halo_specs + [full(nw)] + [full(w) for w in ws] + [full(c) for c in conv + scan]
        + [rtab, rtab, rtab, rtab_t, rtab_t],
        out_specs=[o[0] for o in outs],
        out_shape=[o[1] for o in outs],
        compiler_params=_cparams(("parallel", "parallel")),
        name="inproj",
    )(*tile_args, *halo_args, nw, *ws, *conv, *scan, *rope)


def _split3(x):
    hi = x.astype(BF16)
    r1 = x - hi.astype(F32)
    mid = r1.astype(BF16)
    lo = (r1 - mid.astype(F32)).astype(BF16)
    return hi, mid, lo


def _ssd_kernel(*refs, reverse, final):
    if final:
        xt_ref, bc_ref, acs_ref, tab_ref, zt_ref, yo_ref, dsk_ref, nw_ref, o_ref, state_ref, ybuf_ref = refs
    else:
        xt_ref, bc_ref, acs_ref, tab_ref, o_ref, state_ref = refs

    @pl.when(pl.program_id(1) == 0)
    def _():
        state_ref[...] = jnp.zeros_like(state_ref)

    ri = lax.broadcasted_iota(jnp.int32, (BLK, BLK), 0)
    ci = lax.broadcasted_iota(jnp.int32, (BLK, BLK), 1)
    feeds = (ri >= ci) if reverse else (ri <= ci)
    hoff = SSD_HEADS if reverse else 0
    rep = SSD_HEADS // SSD_GROUPS
    for j in (reversed(range(STEP_BLKS)) if reverse else range(STEP_BLKS)):
        trows = slice(j * BLK, (j + 1) * BLK)
        acs = acs_ref[0, trows, :]
        acs_t, dt_t, dt_out_t, e_in_t, e_chunk = (tab_ref[0, j, k] for k in range(5))
        for g in range(SSD_GROUPS):
            bm = bc_ref[0, trows, g * SSD_STATE:(g + 1) * SSD_STATE]
            cm = bc_ref[0, trows, (SSD_GROUPS + g) * SSD_STATE:(SSD_GROUPS + g + 1) * SSD_STATE]
            cbt = _dot_nt(bm, cm)
            ht = state_ref[g]
            y_off = _dot_nt(ht.astype(BF16), cm)
            x_out, decay = [], []
            for r in range(rep):
                h = g * rep + r
                hl = hoff + h
                rows = slice(h * HEAD_DIM, (h + 1) * HEAD_DIM)
                xt = xt_ref[0, j, rows, :].astype(F32)
                x_in = (xt * dt_t[hl:hl + 1, :]).astype(BF16)
                x_out.append((xt * dt_out_t[hl:hl + 1, :]).astype(BF16))
                decay.append(jnp.broadcast_to(e_chunk[hl:hl + 1, :], (HEAD_DIM, SSD_STATE)))
                diff = acs_t[hl:hl + 1, :] - acs[:, hl:hl + 1]
                lt = (cbt * jnp.exp(jnp.where(feeds, diff, -jnp.inf))).astype(BF16)
                y = _dot(x_in, lt) + y_off[r * HEAD_DIM:(r + 1) * HEAD_DIM] * e_in_t[hl:hl + 1, :]
                if final:
                    ybuf_ref[rows, :] = y
                else:
                    o_ref[0, j, rows, :] = y
            s_new = _dot(jnp.concatenate(x_out, axis=0), bm)
            state_ref[g] = ht * jnp.concatenate(decay, axis=0) + s_new

        if final:
            y = ybuf_ref[...] + yo_ref[0, j] + dsk_ref[...] * xt_ref[0, j].astype(F32)
            z = zt_ref[0, j].astype(F32)
            y = y * (z * (1.0 / (1.0 + jnp.exp(-z))))
            y = y * lax.rsqrt(jnp.mean(y * y, axis=0, keepdims=True) + EPS) * nw_ref[...]
            o_ref[0, trows, :] = y.T.astype(BF16)


def _ssd_pass(xt, bc, acs, tab, final_args, *, reverse):
    bsz, nc = xt.shape[:2]
    ns = nc // STEP_BLKS
    order = (lambda c: ns - 1 - c) if reverse else (lambda c: c)
    final = final_args is not None
    tok = lambda n: pl.BlockSpec((1, TM, n), lambda b, c: (b, order(c), 0))
    feat = pl.BlockSpec((1, STEP_BLKS, SSD_INNER, BLK), lambda b, c: (b, order(c), 0, 0))
    full = lambda arr: pl.BlockSpec(arr.shape, lambda b, c: (0,) * arr.ndim)
    tabs = pl.BlockSpec((1, STEP_BLKS) + tab.shape[2:], lambda b, c: (b, order(c), 0, 0, 0))
    in_specs = [feat, tok(bc.shape[-1]), tok(128), tabs]
    args = [xt, bc, acs, tab]
    scratch = [pltpu.VMEM((SSD_GROUPS, SSD_INNER // SSD_GROUPS, SSD_STATE), F32)]
    if final:
        zt, y_other, dsk, nw = final_args
        in_specs += [feat, feat, full(dsk), full(nw)]
        args += [zt, y_other, dsk, nw]
        scratch.append(pltpu.VMEM((SSD_INNER, BLK), F32))
        out_spec, out_shape = tok(SSD_INNER), jax.ShapeDtypeStruct((bsz, nc * BLK, SSD_INNER), BF16)
    else:
        out_spec, out_shape = feat, jax.ShapeDtypeStruct((bsz, nc, SSD_INNER, BLK), F32)
    return pl.pallas_call(
        functools.partial(_ssd_kernel, reverse=reverse, final=final),
        grid=(bsz, ns),
        in_specs=in_specs,
        out_specs=out_spec,
        out_shape=out_shape,
        scratch_shapes=scratch,
        compiler_params=_cparams(("parallel", "arbitrary")),
        name="ssd_bwd" if reverse else "ssd_fwd",
    )(*args)


def _softmax_pv_t(scores, values, pad_last, extra_logit=None, merge_blocks=False):
    mx = functools.reduce(jnp.maximum, [jnp.max(s, axis=0, keepdims=True) for s in scores])
    if extra_logit is not None:
        mx = jnp.maximum(mx, extra_logit)
    ps = [jnp.exp(s - mx) for s in scores]
    den = functools.reduce(jnp.add, [jnp.sum(p, axis=0, keepdims=True) for p in ps])
    if extra_logit is not None:
        den = den + jnp.exp(extra_logit - mx)
    pb = [p.astype(BF16) for p in ps]
    pb[-1] = jnp.concatenate([pad_last, pb[-1]], axis=0)
    if merge_blocks:
        return _dot(jnp.concatenate(values, axis=1), jnp.concatenate(pb, axis=0)), den
    return functools.reduce(jnp.add, [_dot(v, p) for v, p in zip(values, pb)]), den


def _block_diag_rows(pieces):
    zero = jnp.zeros_like(pieces[0][0])
    rows = []
    for i in range(len(pieces)):
        row = []
        for j, ps in enumerate(pieces):
            row += [p if i == j else zero for p in ps]
        rows.append(jnp.concatenate(row, axis=1))
    return jnp.concatenate(rows, axis=0)


def _win_kernel(qt_ref, kp_ref, kc_ref, kn_ref, km_ref, vp_ref, vc_ref, vn_ref, vm_ref, sink_ref, o_ref, *, nb):
    width = WIN_Q_HEADS * BLK
    ki = lax.broadcasted_iota(jnp.int32, (BLK, width), 0)
    qi = lax.broadcasted_iota(jnp.int32, (BLK, width), 1) & (BLK - 1)
    rep = WIN_Q_HEADS // WIN_KV_HEADS
    keys = [kp_ref[0]] + [kc_ref[0, j * BLK:(j + 1) * BLK, :] for j in range(STEP_BLKS)] + [kn_ref[0]]
    vals = [vp_ref[0, 0]] + [vc_ref[0, j] for j in range(STEP_BLKS)] + [vn_ref[0, 0]]
    kmeta, vmeta = km_ref[0, META0:, :], vm_ref[0, 0]
    pad_last = jnp.zeros((META0, width), BF16)
    for j in range(STEP_BLKS):
        n = pl.program_id(1) * STEP_BLKS + j
        ok_prev = jnp.logical_and(ki >= qi, n >= 2)
        ok_cur = n >= 1
        ok_next = jnp.logical_and(ki <= qi, n + 1 <= nb - 1)
        heads = [qt_ref[0, j, h * HEAD_DIM:(h + 1) * HEAD_DIM, :] for h in range(WIN_Q_HEADS)]
        qbd = _block_diag_rows([heads[g * rep:(g + 1) * rep] for g in range(WIN_KV_HEADS)])
        scores = [jnp.where(ok_prev, _dot(keys[j], qbd), -jnp.inf),
                  jnp.where(ok_cur, _dot(keys[j + 1], qbd), -jnp.inf),
                  jnp.where(ok_next, _dot(keys[j + 2], qbd), -jnp.inf),
                  _dot(kmeta, qbd)]
        o, den = _softmax_pv_t(scores, vals[j:j + 3] + [vmeta], pad_last, extra_logit=sink_ref[...],
                               merge_blocks=True)
        o = o / den
        outs = [o[(h // rep) * HEAD_DIM:(h // rep + 1) * HEAD_DIM, h * BLK:(h + 1) * BLK] for h in range(WIN_Q_HEADS)]
        o_ref[0, j * BLK:(j + 1) * BLK, :] = jnp.concatenate(outs, axis=0).T.astype(BF16)


def _win_attention(qt, k, vt, sink):
    bsz, nb = qt.shape[:2]
    ns = nb // STEP_BLKS
    kvw = WIN_KV_HEADS * HEAD_DIM
    assert kvw == BLK
    kedge = lambda f: pl.BlockSpec((1, BLK, kvw), lambda b, i: (b, f(i), 0))
    vedge = lambda f: pl.BlockSpec((1, 1, kvw, BLK), lambda b, i: (b, f(i), 0, 0))
    before = lambda i: jnp.maximum(i * STEP_BLKS - 1, 0)
    after = lambda i: jnp.minimum((i + 1) * STEP_BLKS, nb - 1)
    first = lambda i: 0
    return pl.pallas_call(
        functools.partial(_win_kernel, nb=nb),
        grid=(bsz, ns),
        in_specs=[pl.BlockSpec((1, STEP_BLKS, WIN_Q_HEADS * HEAD_DIM, BLK), lambda b, i: (b, i, 0, 0)),
                  kedge(before), pl.BlockSpec((1, TM, kvw), lambda b, i: (b, i, 0)), kedge(after), kedge(first),
                  vedge(before), pl.BlockSpec((1, STEP_BLKS, kvw, BLK), lambda b, i: (b, i, 0, 0)), vedge(after),
                  vedge(first), pl.BlockSpec(sink.shape, lambda b, i: (0, 0))],
        out_specs=pl.BlockSpec((1, TM, WIN_Q_HEADS * HEAD_DIM), lambda b, i: (b, i, 0)),
        out_shape=jax.ShapeDtypeStruct((bsz, nb * BLK, WIN_Q_HEADS * HEAD_DIM), BF16),
        compiler_params=_cparams(("parallel", "parallel")),
        name="win_attn",
    )(qt, k, k, k, k, vt, vt, vt, vt, sink)


NA_ROWS_PER_BLK = BLK // GRID_W
NA_WIN_BLKS = NA_KR // NA_ROWS_PER_BLK + 1
NA_WIN_ROWS = NA_WIN_BLKS * NA_ROWS_PER_BLK
NA_E_MIN = -1


def _na_bias_table(rpb):
    c = np.arange(GRID_W)[None, :]
    kc = np.arange(GRID_W)[:, None]
    cs = np.clip(c - NA_KC // 2, 0, GRID_W - NA_KC)
    ok = (kc >= cs) & (kc < cs + NA_KC)
    dc = np.clip(kc - c + NA_KC - 1, 0, 2 * NA_KC - 2)
    t = jnp.where(jnp.asarray(ok)[None, None], rpb.astype(F32)[:, :, dc], -jnp.inf)
    ninf = jnp.full((rpb.shape[0], 2, GRID_W, GRID_W), -jnp.inf, F32)
    ext = jnp.concatenate([ninf, t, ninf], axis=1)
    n_e = 2 * NA_KR - 1 + 3
    return jnp.concatenate([ext[:, 1:1 + n_e], ext[:, 0:n_e]], axis=-1)


def _na_kernel(qt_ref, k_ref, vt_ref, bias_ref, mbt_ref, o_ref, *, rows):
    pair = 2 * HEAD_DIM
    width = 2 * BLK
    first = lax.broadcasted_iota(jnp.int32, (BLK, width), 0) < GRID_W
    left = (lax.broadcasted_iota(jnp.int32, (BLK, width), 1) & (BLK - 1)) < GRID_W
    pad_last = jnp.zeros((META0, width), BF16)

    def block(j, meta):
        n = pl.program_id(1) * STEP_BLKS + j
        if meta:
            rq = (0, 0)
            rs = (0, 0)
            wb = 0
            kb0 = LEAD // BLK
            start = LEAD
        else:
            r0 = (n - 1) * NA_ROWS_PER_BLK
            rq = (r0, r0 + 1)
            rs = tuple(jnp.clip(r - NA_KR // 2, 0, rows - NA_KR) for r in rq)
            wb = jnp.clip(r0 - NA_KR // 2, 0, rows - NA_WIN_ROWS)
            kb0 = LEAD // BLK + lax.shift_right_logical(wb, 1)
            start = pl.multiple_of(LEAD + wb * GRID_W, BLK)
        masks = []
        for ib in range(NA_WIN_BLKS):
            sel = []
            for kr in (wb + 2 * ib, wb + 2 * ib + 1):
                if meta:
                    oks = [jnp.int32(rs[j] <= kr < rs[j] + NA_KR) for j in range(2)]
                else:
                    oks = [jnp.logical_and(kr >= rs[j], kr < rs[j] + NA_KR).astype(jnp.int32) for j in range(2)]
                sel.append(jnp.where(left, oks[0], oks[1]))
            masks.append(jnp.where(first, sel[0], sel[1]) > 0)
        outs = []
        for pp in range(NA_HEADS // 2):
            cols = slice(pp * pair, (pp + 1) * pair)
            hs = (2 * pp, 2 * pp + 1)
            qbd = _block_diag_rows([[qt_ref[0, j, h * HEAD_DIM:(h + 1) * HEAD_DIM, :]] for h in hs])
            vals = [vt_ref[0, kb0 + ib, cols, :] for ib in range(NA_WIN_BLKS)] + [vt_ref[0, 0, cols, :]]
            scores = []
            for ib in range(NA_WIN_BLKS):
                e = wb + 2 * ib - rq[0] + (NA_KR - 1) - NA_E_MIN
                if meta:
                    col = [jnp.concatenate([bias_ref[h, e][:, 0:1], bias_ref[h, e + 1][:, 0:1]], axis=0) for h in hs]
                    bias = jnp.concatenate([jnp.broadcast_to(c, (BLK, BLK)) for c in col], axis=1)
                else:
                    bias = jnp.concatenate(
                        [jnp.concatenate([bias_ref[h, e], bias_ref[h, e + 1]], axis=0) for h in hs], axis=1)
                s = _dot(k_ref[0, pl.ds(start + ib * BLK, BLK), cols], qbd)
                scores.append(jnp.where(masks[ib], s + bias, -jnp.inf))
            scores.append(_dot(k_ref[0, META0:LEAD, cols], qbd) + jnp.concatenate([mbt_ref[h] for h in hs], axis=1))
            o, den = _softmax_pv_t(scores, vals, pad_last)
            o = o / den
            outs += [o[hh * HEAD_DIM:(hh + 1) * HEAD_DIM, hh * BLK:(hh + 1) * BLK] for hh in range(2)]
        o_ref[0, j * BLK:(j + 1) * BLK, :] = jnp.concatenate(outs, axis=0).T.astype(BF16)

    @pl.when(pl.program_id(1) == 0)
    def _():
        block(0, True)

    @pl.when(pl.program_id(1) > 0)
    def _():
        block(0, False)

    for j in range(1, STEP_BLKS):
        block(j, False)


def _na_attention(qt, k, vt, bias, mbt):
    bsz, nb, width, _ = qt.shape
    lp = nb * BLK
    rows = (lp - LEAD) // GRID_W
    assert rows >= NA_WIN_ROWS and rows % NA_ROWS_PER_BLK == 0
    qblk = pl.BlockSpec((1, STEP_BLKS, width, BLK), lambda b, n: (b, n, 0, 0))
    kseq = pl.BlockSpec((1, lp, width), lambda b, n: (b, 0, 0), pipeline_mode=pl.Buffered(1))
    vseq = pl.BlockSpec((1, nb, width, BLK), lambda b, n: (b, 0, 0, 0), pipeline_mode=pl.Buffered(1))
    full = lambda arr: pl.BlockSpec(arr.shape, lambda b, n: (0,) * arr.ndim)
    return pl.pallas_call(
        functools.partial(_na_kernel, rows=rows),
        grid=(bsz, nb // STEP_BLKS),
        in_specs=[qblk, kseq, vseq, full(bias), full(mbt)],
        out_specs=pl.BlockSpec((1, TM, width), lambda b, n: (b, n, 0)),
        out_shape=jax.ShapeDtypeStruct((bsz, lp, width), BF16),
        compiler_params=_cparams(("parallel", "arbitrary")),
        name="na_attn",
    )(qt, k, vt, bias, mbt)


def _outproj_kernel(*refs, ntile):
    lead_ref = refs[0] if ntile > 1 else None
    tile_refs = refs[ntile > 1:ntile + (ntile > 1)]
    ys_ref, yw_ref, yn_ref, w1_ref, w2_ref, w3_ref, nw_ref, o_ref = refs[ntile + (ntile > 1):]
    i = pl.program_id(1)
    mix = _dot(ys_ref[0], w1_ref[...]) + _dot(yw_ref[0], w2_ref[...]) + _dot(yn_ref[0], w3_ref[...])
    out = _load_tile(tile_refs, lead_ref, i) + _rms(mix, nw_ref[...])
    o_ref[0] = jnp.where(_live_rows(TM, i), out, 0.0)


def _outproj(src, y_ssd, y_win, y_na, w1, w2, w3, nw):
    bsz, lp, _ = y_ssd.shape
    tile = lambda n: pl.BlockSpec((1, TM, n), lambda b, i: (b, i, 0))
    full = lambda arr: pl.BlockSpec(arr.shape, lambda b, i: (0,) * arr.ndim)
    tile_specs, tile_args = _tile_specs(src, TM)
    return pl.pallas_call(
        functools.partial(_outproj_kernel, ntile=len(tile_specs) - (src[0] == 'x')),
        grid=(bsz, lp // TM),
        in_specs=tile_specs + [tile(y_ssd.shape[-1]), tile(y_win.shape[-1]), tile(y_na.shape[-1]),
                               full(w1), full(w2), full(w3), full(nw)],
        out_specs=tile(D_MODEL),
        out_shape=jax.ShapeDtypeStruct((bsz, lp, D_MODEL), F32),
        compiler_params=_cparams(("parallel", "parallel")),
        name="outproj",
    )(*tile_args, y_ssd, y_win, y_na, w1, w2, w3, nw)


def _ffn_kernel(*refs, nt, ntile, padded):
    tile_refs = refs[:ntile]
    hp_ref, hn_ref, nw1_ref, wup_ref, cw_ref, cb_ref, wdn_ref, nw2_ref, o_ref, act_ref = refs[ntile:]
    i = pl.program_id(1)
    rows = act_ref.shape[0]
    nw1 = nw1_ref[...]
    hc = _load_tile(tile_refs, None, i)
    f = jnp.concatenate([_rms(hp_ref[0], nw1), _rms(hc, nw1), _rms(hn_ref[0], nw1)], axis=0).astype(BF16)
    tail = jnp.where(i < nt - 1, 1.0, 0.0)
    nch = D_FF // FF_CHUNK

    def conv(g, c):
        g = jnp.concatenate([g[:HALO + rows], g[HALO + rows:] * tail], axis=0)
        out = _shift_rows(g, -1) * cw_ref[c, 0:1, :] + cb_ref[c]
        out = out + g[HALO:HALO + rows] * cw_ref[c, 1:2, :]
        return out + _shift_rows(g, 1) * cw_ref[c, 2:3, :]

    for c in range(nch):
        gate = conv(_dot(f, wup_ref[c]), c)
        up = conv(_dot(f, wup_ref[nch + c]), nch + c)
        th = jnp.tanh(gate * (GELU_K + (GELU_K * GELU_C) * (gate * gate)))
        act_ref[:, c * FF_CHUNK:(c + 1) * FF_CHUNK] = (gate * (0.5 * th + 0.5) * up).astype(BF16)
    out = hc + _rms(_dot(act_ref[...], wdn_ref[...]), nw2_ref[...])
    o_ref[0] = jnp.where(_live_rows(rows, i), out, 0.0) if padded else out


def _ffn(h, nw1, wup, cw, cb, wdn, nw2, *, last):
    bsz, lp, _ = h.shape
    full = lambda arr: pl.BlockSpec(arr.shape, lambda b, i: (0,) * arr.ndim)
    if last:
        tm, n_out = TM_OUT, lp - LEAD
        h4 = h.reshape(bsz, lp // BLK, BLK, D_MODEL)
        k = tm // BLK
        tile_specs = [pl.BlockSpec((1, 1, BLK, D_MODEL), lambda b, i, j=j: (b, LEAD // BLK + i * k + j, 0, 0))
                      for j in range(k)]
        tile_args = [h4] * k
        off = LEAD // HALO
    else:
        tm, n_out = TM, lp
        tile_specs, tile_args = _tile_specs(('h', h), tm)
        off = 0
    nt = n_out // tm
    hb = tm // HALO
    last_blk = lp // HALO - 1
    prev = pl.BlockSpec((1, HALO, D_MODEL), lambda b, i: (b, jnp.clip(off + i * hb - 1, 0, last_blk), 0))
    nxt = pl.BlockSpec((1, HALO, D_MODEL), lambda b, i: (b, jnp.clip(off + (i + 1) * hb, 0, last_blk), 0))
    return pl.pallas_call(
        functools.partial(_ffn_kernel, nt=nt, ntile=len(tile_specs), padded=not last),
        grid=(bsz, nt),
        in_specs=tile_specs + [prev, nxt, full(nw1), full(wup), full(cw), full(cb), full(wdn), full(nw2)],
        out_specs=pl.BlockSpec((1, tm, D_MODEL), lambda b, i: (b, i, 0)),
        out_shape=jax.ShapeDtypeStruct((bsz, n_out, D_MODEL), F32),
        scratch_shapes=[pltpu.VMEM((tm, D_FF), BF16)],
        compiler_params=_cparams(("parallel", "parallel")),
        name="ffn",
    )(*tile_args, h, h, nw1, wup, cw, cb, wdn, nw2)


def _pad_lanes(x, n):
    return jnp.pad(x, [(0, 0)] * (x.ndim - 1) + [(0, n - x.shape[-1])])


def _rope_tables(lp):
    half = ROPE_DIM // 2
    pos = jnp.maximum(jnp.arange(lp) - META0, 0).astype(F32)
    inv = jnp.power(ROPE_THETA, -jnp.arange(half, dtype=F32) / half)
    ang = pos[:, None] * inv[None, :]
    cos, sin = jnp.cos(ang), jnp.sin(ang)
    zeros, ones = jnp.zeros_like(cos), jnp.ones((lp, HEAD_DIM - ROPE_DIM), F32)
    rest = jnp.zeros((lp, HEAD_DIM - ROPE_DIM), F32)
    c = jnp.concatenate([cos, cos, ones], axis=1)
    s1 = jnp.concatenate([-sin, zeros, rest], axis=1)
    s2 = jnp.concatenate([zeros, sin, rest], axis=1)
    return tuple(jnp.tile(t, (1, 128 // HEAD_DIM)) for t in (c, s1, s2)) + (cos.T, sin.T)


def _layer_params(i, p):
    row = lambda v: v.reshape(1, -1).astype(F32)
    lanes = lambda v: jnp.broadcast_to(v.astype(F32)[:, None], (v.shape[0], BLK))
    sizes = [SSD_INNER, SSD_XBC, 2 * SSD_HEADS, WIN_Q_HEADS * HEAD_DIM, WIN_KV_HEADS * HEAD_DIM,
             WIN_KV_HEADS * HEAD_DIM, NA_HEADS * HEAD_DIM, NA_HEADS * HEAD_DIM, NA_HEADS * HEAD_DIM]
    w_in = p['w_in'][i].astype(BF16)
    ws = jnp.split(w_in, np.cumsum(sizes)[:-1].tolist(), axis=1)
    ws[2] = _pad_lanes(ws[2], 128)
    w_out = p['w_out'][i].astype(BF16)
    nch = D_FF // FF_CHUNK
    chunked = lambda m: m.reshape(m.shape[0], 2 * nch, FF_CHUNK).swapaxes(0, 1)
    return dict(
        norm_mix_pre=row(p['norm_mix_pre'][i]),
        w_in=ws,
        ssd_conv=(p['ssd_conv_w'][i].astype(F32), row(p['ssd_conv_b'][i])),
        ssd_consts=(_pad_lanes(row(p['ssd_dt_bias'][i]), 128), _pad_lanes(row(p['ssd_a_log'][i]), 128)),
        ssd_d=lanes(jnp.repeat(p['ssd_d'][i], HEAD_DIM)),
        ssd_norm_w=lanes(p['ssd_norm_w'][i]),
        win_sink=row(jnp.repeat(p['win_sink'][i], BLK)),
        na_bias=_na_bias_table(p['na_rpb'][i]),
        na_meta_bias=jnp.broadcast_to(p['na_meta_bias'][i].astype(F32)[:, :, None], (NA_HEADS, N_META, BLK)),
        w_out=(w_out[:SSD_INNER], w_out[SSD_INNER:SSD_INNER + WIN_Q_HEADS * HEAD_DIM],
               w_out[SSD_INNER + WIN_Q_HEADS * HEAD_DIM:]),
        norm_mix_post=row(p['norm_mix_post'][i]),
        norm_ffn_pre=row(p['norm_ffn_pre'][i]),
        ffn_w_up=chunked(p['ffn_w_up'][i].astype(BF16)),
        ffn_conv_w=chunked(p['ffn_conv_w'][i].astype(F32)),
        ffn_conv_b=chunked(p['ffn_conv_b'][i].astype(F32).reshape(1, -1)),
        ffn_w_down=p['ffn_w_down'][i].astype(BF16),
        norm_ffn_post=row(p['norm_ffn_post'][i]),
    )


def _encode(x, meta_tokens, layers):
    bsz, n_tok, _ = x.shape
    assert (n_tok + LEAD) % TM == 0 and n_tok % TM_OUT == 0 and n_tok // GRID_W >= NA_KR
    lp = LEAD + n_tok
    lead = jnp.concatenate([jnp.zeros((META0, D_MODEL), F32), meta_tokens.astype(F32)], axis=0)
    src = ('x', x.astype(F32), lead)
    rope = _rope_tables(lp)
    for li, lw in enumerate(layers):
        zt, xt, bc, acs, tab, wq, wk, wv, nq, nk, nv = _inproj(src, lp, lw['norm_mix_pre'], lw['w_in'], lw['ssd_conv'],
                                                               lw['ssd_consts'], rope)
        y_fwd = _ssd_pass(xt, bc, acs, tab, None, reverse=False)
        y_ssd = _ssd_pass(xt, bc, acs, tab, (zt, y_fwd, lw['ssd_d'], lw['ssd_norm_w']), reverse=True)
        y_win = _win_attention(wq, wk, wv, lw['win_sink'])
        y_na = _na_attention(nq, nk, nv, lw['na_bias'], lw['na_meta_bias'])
        h = _outproj(src, y_ssd, y_win, y_na, *lw['w_out'], lw['norm_mix_post'])
        h = _ffn(h, lw['norm_ffn_pre'], lw['ffn_w_up'], lw['ffn_conv_w'], lw['ffn_conv_b'],
                 lw['ffn_w_down'], lw['norm_ffn_post'], last=li == len(layers) - 1)
        src = ('h', h)
    return h


def kernel(x_prompt, x_sample, meta_tokens, norm_mix_pre, norm_mix_post, w_in, ssd_conv_w, ssd_conv_b,
           ssd_dt_bias, ssd_a_log, ssd_d, ssd_norm_w, win_sink, na_rpb, na_meta_bias, w_out, norm_ffn_pre,
           norm_ffn_post, ffn_w_up, ffn_conv_w, ffn_conv_b, ffn_w_down):
    p = dict(norm_mix_pre=norm_mix_pre, norm_mix_post=norm_mix_post, w_in=w_in, ssd_conv_w=ssd_conv_w,
             ssd_conv_b=ssd_conv_b, ssd_dt_bias=ssd_dt_bias, ssd_a_log=ssd_a_log, ssd_d=ssd_d,
             ssd_norm_w=ssd_norm_w, win_sink=win_sink, na_rpb=na_rpb, na_meta_bias=na_meta_bias, w_out=w_out,
             norm_ffn_pre=norm_ffn_pre, norm_ffn_post=norm_ffn_post, ffn_w_up=ffn_w_up, ffn_conv_w=ffn_conv_w,
             ffn_conv_b=ffn_conv_b, ffn_w_down=ffn_w_down)
    layers = [_layer_params(i, p) for i in range(w_in.shape[0])]
    return (_encode(x_prompt, meta_tokens, layers), _encode(x_sample, meta_tokens, layers))
```

```python
import functools

import jax
import jax.numpy as jnp
import numpy as np
from jax import lax
from jax.experimental import pallas as pl
from jax.experimental.pallas import tpu as pltpu

F32 = jnp.float32
BF16 = jnp.bfloat16

D_MODEL = 1024
N_META = 16
GRID_W = 64
HEAD_DIM = 64

SSD_HEADS = 16
SSD_INNER = SSD_HEADS * HEAD_DIM
SSD_GROUPS = 2
SSD_STATE = 128
SSD_XBC = SSD_INNER + 2 * SSD_GROUPS * SSD_STATE
SSD_CONV = 5

WIN_Q_HEADS = 8
WIN_KV_HEADS = 2
WIN_RADIUS = 128
ROPE_THETA = 500000.0
ROPE_DIM = HEAD_DIM // 4

NA_HEADS = 8
NA_KR = 8
NA_KC = 16

D_FF = 2816
EPS = 1e-6
GELU_K = float(np.sqrt(2.0 / np.pi))
GELU_C = 0.044715

BLK = 128
LEAD = BLK
META0 = LEAD - N_META
HALO = 8
TM = 384
FFN_OUT_TILE_CAP = 720
OUTPROJ_TILE_CAP = 1100
FF_CHUNK = 256
STEP_BLKS = TM // BLK
VMEM_LIMIT = 56 * 1024 * 1024


def _tile_rows(n, cap, unit=HALO):
    return max(t for t in range(unit, cap + 1, unit) if n % t == 0)


def _resident(arr):
    return pl.BlockSpec(arr.shape, lambda *_: (0,) * arr.ndim, pipeline_mode=pl.Buffered(1))


def _cparams(sem):
    return pltpu.CompilerParams(dimension_semantics=sem, vmem_limit_bytes=VMEM_LIMIT)


def _rms(x, w):
    return x * lax.rsqrt(jnp.mean(x * x, axis=-1, keepdims=True) + EPS) * w


def _dot(a, b):
    return jnp.dot(a, b, preferred_element_type=F32)


def _shift_rows(x, d):
    return pltpu.roll(x, (-d) % x.shape[0], 0)[HALO:x.shape[0] - HALO]


def _load_tile(refs, lead_ref, i):
    if len(refs) == 1:
        return refs[0][0]
    blocks = [r[0, 0] for r in refs]
    if lead_ref is not None:
        blocks[0] = jnp.where(i == 0, lead_ref[...], blocks[0])
    return jnp.concatenate(blocks, axis=0)


def _tile_specs(src, tm):
    if src[0] == 'h':
        return [pl.BlockSpec((1, tm, D_MODEL), lambda b, i: (b, i, 0))], [src[1]]
    x, lead = src[1], src[2]
    x4 = x.reshape(x.shape[0], x.shape[1] // BLK, BLK, D_MODEL)
    k = tm // BLK
    specs = [pl.BlockSpec((1, 1, BLK, D_MODEL), lambda b, i, j=j: (b, jnp.maximum(i * k + j - LEAD // BLK, 0), 0, 0))
             for j in range(k)]
    return [pl.BlockSpec(lead.shape, lambda b, i: (0, 0))] + specs, [lead] + [x4] * k


def _halo_specs(src, tm):
    arr = src[1]
    off = 0 if src[0] == 'h' else LEAD // HALO
    last = arr.shape[1] // HALO - 1
    hb = tm // HALO
    prev = pl.BlockSpec((1, HALO, D_MODEL), lambda b, i: (b, jnp.clip(i * hb - 1 - off, 0, last), 0))
    nxt = pl.BlockSpec((1, HALO, D_MODEL), lambda b, i: (b, jnp.clip((i + 1) * hb - off, 0, last), 0))
    return [prev, nxt], [arr, arr]


def _live_rows(tile_rows, tile_index):
    row = tile_index * tile_rows + lax.broadcasted_iota(jnp.int32, (tile_rows, 1), 0)
    return row >= META0


def _dot_nt(a, b):
    return lax.dot_general(a, b, (((1,), (1,)), ((), ())), preferred_element_type=F32)


def _rope128(x, c, s1, s2):
    return x * c + pltpu.roll(x, 128 - ROPE_DIM // 2, 1) * s1 + pltpu.roll(x, ROPE_DIM // 2, 1) * s2


def _scan_tables(dt_raw, dt_bias, a_log, live, acs_ref, tab_ref):
    nh = 2 * SSD_HEADS
    dtr = dt_raw + dt_bias
    dt = jnp.maximum(dtr, 0.0) + jnp.log1p(jnp.exp(-jnp.abs(dtr)))
    dt = jnp.where(live, dt, 0.0)
    a = dt * (-jnp.exp(a_log))
    ri = lax.broadcasted_iota(jnp.int32, (BLK, BLK), 0)
    ci = lax.broadcasted_iota(jnp.int32, (BLK, BLK), 1)
    tri = (ci <= ri).astype(BF16)
    fwd = ci < SSD_HEADS
    fwd_rows = ri < SSD_HEADS
    for j in range(TM // BLK):
        rows = slice(j * BLK, (j + 1) * BLK)
        aj, dtj = a[rows], dt[rows]
        pre = sum(_dot(tri, part) for part in _split3(aj))
        total = pre[BLK - 1:BLK, :]
        acs = jnp.where(fwd, pre, total - pre + aj)
        acs_ref[0, rows, :] = acs
        acs_t = acs.T[:nh]
        tab_ref[0, j, 0] = acs_t
        tab_ref[0, j, 1] = dtj.T[:nh]
        tab_ref[0, j, 2] = (dtj * jnp.exp(total - acs)).T[:nh]
        tab_ref[0, j, 3] = jnp.exp(acs_t)
        chunk_sum = jnp.where(fwd_rows[:nh, 0:1], acs_t[:, BLK - 1:BLK], acs_t[:, 0:1])
        tab_ref[0, j, 4] = jnp.broadcast_to(jnp.exp(chunk_sum), (nh, BLK))


def _inproj_kernel(*refs, nt, ntile):
    lead_ref = refs[0] if ntile > 1 else None
    tile_refs = refs[ntile > 1:ntile + (ntile > 1)]
    (hp_ref, hn_ref, nw_ref, wz_ref, wx_ref, wdt_ref, wwq_ref, wwk_ref, wwv_ref,
     wnq_ref, wnk_ref, wnv_ref, cw_ref, cb_ref, dtb_ref, alog_ref, rc_ref, rs1_ref, rs2_ref, rct_ref, rst_ref,
     z_ref, xt_ref, bc_ref, acs_ref, tab_ref, wq_ref, wk_ref, wv_ref, nq_ref, nk_ref, nv_ref) = refs[ntile + (ntile > 1):]
    i = pl.program_id(1)
    nw = nw_ref[...]
    a = _rms(_load_tile(tile_refs, lead_ref, i), nw).astype(BF16)
    head = jnp.where(i > 0, 1.0, 0.0)
    tail = jnp.where(i < nt - 1, 1.0, 0.0)
    wx = wx_ref[...]
    xe = jnp.concatenate([_dot(_rms(hp_ref[0], nw).astype(BF16), wx) * head, _dot(a, wx),
                          _dot(_rms(hn_ref[0], nw).astype(BF16), wx) * tail], axis=0)
    pad = SSD_CONV // 2
    acc = _shift_rows(xe, -pad) * cw_ref[0:1, :] + cb_ref[...]
    for j in range(1, SSD_CONV):
        acc = acc + _shift_rows(xe, j - pad) * cw_ref[j:j + 1, :]
    xc = acc * (1.0 / (1.0 + jnp.exp(-acc)))

    def put_blocks(ref, xt):
        for j in range(TM // BLK):
            ref[0, j] = xt[:, j * BLK:(j + 1) * BLK].astype(BF16)

    put_blocks(xt_ref, xc[:, :SSD_INNER].T)
    bc_ref[0] = xc[:, SSD_INNER:].astype(BF16)
    put_blocks(z_ref, _dot(a, wz_ref[...]).T)
    _scan_tables(_dot(a, wdt_ref[...]), dtb_ref[...], alog_ref[...], _live_rows(TM, i), acs_ref, tab_ref)

    scale = HEAD_DIM ** -0.5
    qt = _dot(a, wwq_ref[...]).T
    cos_t, sin_t = rct_ref[...], rst_ref[...]
    half = ROPE_DIM // 2
    parts = []
    for hh in range(WIN_Q_HEADS):
        x1 = qt[hh * HEAD_DIM:hh * HEAD_DIM + half]
        x2 = qt[hh * HEAD_DIM + half:hh * HEAD_DIM + ROPE_DIM]
        parts += [x1 * cos_t - x2 * sin_t, x2 * cos_t + x1 * sin_t, qt[hh * HEAD_DIM + ROPE_DIM:(hh + 1) * HEAD_DIM]]
    put_blocks(wq_ref, jnp.concatenate(parts, axis=0) * scale)
    wk_ref[0] = _rope128(_dot(a, wwk_ref[...]), rc_ref[...], rs1_ref[...], rs2_ref[...]).astype(BF16)
    put_blocks(wv_ref, _dot(a, wwv_ref[...]).T)
    put_blocks(nq_ref, (_dot(a, wnq_ref[...]) * scale).T)
    nk_ref[0] = _dot(a, wnk_ref[...]).astype(BF16)
    put_blocks(nv_ref, _dot(a, wnv_ref[...]).T)


def _inproj(src, lp, nw, ws, conv, scan, rope):
    bsz = src[1].shape[0]
    nt = lp // TM
    nb = lp // BLK
    tile = lambda n: pl.BlockSpec((1, TM, n), lambda b, i: (b, i, 0))
    tblk = lambda n: pl.BlockSpec((1, TM // BLK, n, BLK), lambda b, i: (b, i, 0, 0))
    tile_specs, tile_args = _tile_specs(src, TM)
    halo_specs, halo_args = _halo_specs(src, TM)
    full = lambda arr: pl.BlockSpec(arr.shape, lambda b, i: (0,) * arr.ndim)
    rtab = pl.BlockSpec((TM, 128), lambda b, i: (i, 0))
    rtab_t = pl.BlockSpec((ROPE_DIM // 2, TM), lambda b, i: (0, i))
    tok = lambda n, d: (tile(n), jax.ShapeDtypeStruct((bsz, lp, n), d))
    blk = lambda n: (tblk(n), jax.ShapeDtypeStruct((bsz, nb, n, BLK), BF16))
    kvw = WIN_KV_HEADS * HEAD_DIM
    tabs = (pl.BlockSpec((1, TM // BLK, 5, 2 * SSD_HEADS, BLK), lambda b, i: (b, i, 0, 0, 0)),
            jax.ShapeDtypeStruct((bsz, nb, 5, 2 * SSD_HEADS, BLK), F32))
    outs = [blk(SSD_INNER), blk(SSD_INNER), tok(SSD_XBC - SSD_INNER, BF16), tok(128, F32), tabs, blk(WIN_Q_HEADS * HEAD_DIM), tok(kvw, BF16),
            blk(kvw), blk(NA_HEADS * HEAD_DIM), tok(NA_HEADS * HEAD_DIM, BF16), blk(NA_HEADS * HEAD_DIM)]
    return pl.pallas_call(
        functools.partial(_inproj_kernel, nt=nt, ntile=len(tile_specs) - (src[0] == 'x')),
        grid=(bsz, nt),
        in_specs=tile_specs + halo_specs + [full(nw)] + [full(w) for w in ws] + [full(c) for c in conv + scan]
        + [rtab, rtab, rtab, rtab_t, rtab_t],
        out_specs=[o[0] for o in outs],
        out_shape=[o[1] for o in outs],
        compiler_params=_cparams(("parallel", "parallel")),
        name="inproj",
    )(*tile_args, *halo_args, nw, *ws, *conv, *scan, *rope)


def _split3(x):
    hi = x.astype(BF16)
    r1 = x - hi.astype(F32)
    mid = r1.astype(BF16)
    lo = (r1 - mid.astype(F32)).astype(BF16)
    return hi, mid, lo


def _ssd_kernel(*refs, reverse, final):
    if final:
        xt_ref, bc_ref, acs_ref, tab_ref, zt_ref, yo_ref, dsk_ref, nw_ref, o_ref, state_ref, ybuf_ref = refs
    else:
        xt_ref, bc_ref, acs_ref, tab_ref, o_ref, state_ref = refs

    @pl.when(pl.program_id(1) == 0)
    def _():
        state_ref[...] = jnp.zeros_like(state_ref)

    ri = lax.broadcasted_iota(jnp.int32, (BLK, BLK), 0)
    ci = lax.broadcasted_iota(jnp.int32, (BLK, BLK), 1)
    feeds = (ri >= ci) if reverse else (ri <= ci)
    hoff = SSD_HEADS if reverse else 0
    rep = SSD_HEADS // SSD_GROUPS
    for j in (reversed(range(STEP_BLKS)) if reverse else range(STEP_BLKS)):
        trows = slice(j * BLK, (j + 1) * BLK)
        acs = acs_ref[0, trows, :]
        acs_t, dt_t, dt_out_t, e_in_t, e_chunk = (tab_ref[0, j, k] for k in range(5))
        for g in range(SSD_GROUPS):
            bm = bc_ref[0, trows, g * SSD_STATE:(g + 1) * SSD_STATE]
            cm = bc_ref[0, trows, (SSD_GROUPS + g) * SSD_STATE:(SSD_GROUPS + g + 1) * SSD_STATE]
            cbt = _dot_nt(bm, cm)
            ht = state_ref[g]
            y_off = _dot_nt(ht.astype(BF16), cm)
            x_out, decay = [], []
            for r in range(rep):
                h = g * rep + r
                hl = hoff + h
                rows = slice(h * HEAD_DIM, (h + 1) * HEAD_DIM)
                xt = xt_ref[0, j, rows, :].astype(F32)
                x_in = (xt * dt_t[hl:hl + 1, :]).astype(BF16)
                x_out.append((xt * dt_out_t[hl:hl + 1, :]).astype(BF16))
                decay.append(jnp.broadcast_to(e_chunk[hl:hl + 1, :], (HEAD_DIM, SSD_STATE)))
                diff = acs_t[hl:hl + 1, :] - acs[:, hl:hl + 1]
                lt = (cbt * jnp.exp(jnp.where(feeds, diff, -jnp.inf))).astype(BF16)
                y = _dot(x_in, lt) + y_off[r * HEAD_DIM:(r + 1) * HEAD_DIM] * e_in_t[hl:hl + 1, :]
                if final:
                    ybuf_ref[rows, :] = y
                else:
                    o_ref[0, j, rows, :] = y
            s_new = _dot(jnp.concatenate(x_out, axis=0), bm)
            state_ref[g] = ht * jnp.concatenate(decay, axis=0) + s_new

        if final:
            y = ybuf_ref[...] + yo_ref[0, j] + dsk_ref[...] * xt_ref[0, j].astype(F32)
            z = zt_ref[0, j].astype(F32)
            y = y * (z * (1.0 / (1.0 + jnp.exp(-z))))
            y = y * lax.rsqrt(jnp.mean(y * y, axis=0, keepdims=True) + EPS) * nw_ref[...]
            o_ref[0, trows, :] = y.T.astype(BF16)


def _ssd_pass(xt, bc, acs, tab, final_args, *, reverse):
    bsz, nc = xt.shape[:2]
    ns = nc // STEP_BLKS
    order = (lambda c: ns - 1 - c) if reverse else (lambda c: c)
    final = final_args is not None
    tok = lambda n: pl.BlockSpec((1, TM, n), lambda b, c: (b, order(c), 0))
    feat = pl.BlockSpec((1, STEP_BLKS, SSD_INNER, BLK), lambda b, c: (b, order(c), 0, 0))
    full = lambda arr: pl.BlockSpec(arr.shape, lambda b, c: (0,) * arr.ndim)
    tabs = pl.BlockSpec((1, STEP_BLKS) + tab.shape[2:], lambda b, c: (b, order(c), 0, 0, 0))
    in_specs = [feat, tok(bc.shape[-1]), tok(128), tabs]
    args = [xt, bc, acs, tab]
    scratch = [pltpu.VMEM((SSD_GROUPS, SSD_INNER // SSD_GROUPS, SSD_STATE), F32)]
    if final:
        zt, y_other, dsk, nw = final_args
        in_specs += [feat, feat, full(dsk), full(nw)]
        args += [zt, y_other, dsk, nw]
        scratch.append(pltpu.VMEM((SSD_INNER, BLK), F32))
        out_spec, out_shape = tok(SSD_INNER), jax.ShapeDtypeStruct((bsz, nc * BLK, SSD_INNER), BF16)
    else:
        out_spec, out_shape = feat, jax.ShapeDtypeStruct((bsz, nc, SSD_INNER, BLK), F32)
    return pl.pallas_call(
        functools.partial(_ssd_kernel, reverse=reverse, final=final),
        grid=(bsz, ns),
        in_specs=in_specs,
        out_specs=out_spec,
        out_shape=out_shape,
        scratch_shapes=scratch,
        compiler_params=_cparams(("parallel", "arbitrary")),
        name="ssd_bwd" if reverse else "ssd_fwd",
    )(*args)


def _softmax_pv_t(scores, values, pad_last, extra_logit=None, merge_blocks=False):
    mx = functools.reduce(jnp.maximum, [jnp.max(s, axis=0, keepdims=True) for s in scores])
    if extra_logit is not None:
        mx = jnp.maximum(mx, extra_logit)
    ps = [jnp.exp(s - mx) for s in scores]
    den = functools.reduce(jnp.add, [jnp.sum(p, axis=0, keepdims=True) for p in ps])
    if extra_logit is not None:
        den = den + jnp.exp(extra_logit - mx)
    pb = [p.astype(BF16) for p in ps]
    pb[-1] = jnp.concatenate([pad_last, pb[-1]], axis=0)
    if merge_blocks:
        return _dot(jnp.concatenate(values, axis=1), jnp.concatenate(pb, axis=0)), den
    return functools.reduce(jnp.add, [_dot(v, p) for v, p in zip(values, pb)]), den


def _block_diag_rows(pieces):
    zero = jnp.zeros_like(pieces[0][0])
    rows = []
    for i in range(len(pieces)):
        row = []
        for j, ps in enumerate(pieces):
            row += [p if i == j else zero for p in ps]
        rows.append(jnp.concatenate(row, axis=1))
    return jnp.concatenate(rows, axis=0)


def _win_kernel(qt_ref, kp_ref, kc_ref, kn_ref, km_ref, vp_ref, vc_ref, vn_ref, vm_ref, sink_ref, o_ref, *, nb):
    width = WIN_Q_HEADS * BLK
    ki = lax.broadcasted_iota(jnp.int32, (BLK, width), 0)
    qi = lax.broadcasted_iota(jnp.int32, (BLK, width), 1) & (BLK - 1)
    rep = WIN_Q_HEADS // WIN_KV_HEADS
    keys = [kp_ref[0]] + [kc_ref[0, j * BLK:(j + 1) * BLK, :] for j in range(STEP_BLKS)] + [kn_ref[0]]
    vals = [vp_ref[0, 0]] + [vc_ref[0, j] for j in range(STEP_BLKS)] + [vn_ref[0, 0]]
    kmeta, vmeta = km_ref[0, META0:, :], vm_ref[0, 0]
    pad_last = jnp.zeros((META0, width), BF16)
    for j in range(STEP_BLKS):
        n = pl.program_id(1) * STEP_BLKS + j
        ok_prev = jnp.logical_and(ki >= qi, n >= 2)
        ok_cur = n >= 1
        ok_next = jnp.logical_and(ki <= qi, n + 1 <= nb - 1)
        heads = [qt_ref[0, j, h * HEAD_DIM:(h + 1) * HEAD_DIM, :] for h in range(WIN_Q_HEADS)]
        qbd = _block_diag_rows([heads[g * rep:(g + 1) * rep] for g in range(WIN_KV_HEADS)])
        scores = [jnp.where(ok_prev, _dot(keys[j], qbd), -jnp.inf),
                  jnp.where(ok_cur, _dot(keys[j + 1], qbd), -jnp.inf),
                  jnp.where(ok_next, _dot(keys[j + 2], qbd), -jnp.inf),
                  _dot(kmeta, qbd)]
        o, den = _softmax_pv_t(scores, vals[j:j + 3] + [vmeta], pad_last, extra_logit=sink_ref[...],
                               merge_blocks=True)
        o = o / den
        outs = [o[(h // rep) * HEAD_DIM:(h // rep + 1) * HEAD_DIM, h * BLK:(h + 1) * BLK] for h in range(WIN_Q_HEADS)]
        o_ref[0, j * BLK:(j + 1) * BLK, :] = jnp.concatenate(outs, axis=0).T.astype(BF16)


def _win_attention(qt, k, vt, sink):
    bsz, nb = qt.shape[:2]
    ns = nb // STEP_BLKS
    kvw = WIN_KV_HEADS * HEAD_DIM
    assert kvw == BLK
    kedge = lambda f: pl.BlockSpec((1, BLK, kvw), lambda b, i: (b, f(i), 0))
    vedge = lambda f: pl.BlockSpec((1, 1, kvw, BLK), lambda b, i: (b, f(i), 0, 0))
    before = lambda i: jnp.maximum(i * STEP_BLKS - 1, 0)
    after = lambda i: jnp.minimum((i + 1) * STEP_BLKS, nb - 1)
    first = lambda i: 0
    return pl.pallas_call(
        functools.partial(_win_kernel, nb=nb),
        grid=(bsz, ns),
        in_specs=[pl.BlockSpec((1, STEP_BLKS, WIN_Q_HEADS * HEAD_DIM, BLK), lambda b, i: (b, i, 0, 0)),
                  kedge(before), pl.BlockSpec((1, TM, kvw), lambda b, i: (b, i, 0)), kedge(after), kedge(first),
                  vedge(before), pl.BlockSpec((1, STEP_BLKS, kvw, BLK), lambda b, i: (b, i, 0, 0)), vedge(after),
                  vedge(first), pl.BlockSpec(sink.shape, lambda b, i: (0, 0))],
        out_specs=pl.BlockSpec((1, TM, WIN_Q_HEADS * HEAD_DIM), lambda b, i: (b, i, 0)),
        out_shape=jax.ShapeDtypeStruct((bsz, nb * BLK, WIN_Q_HEADS * HEAD_DIM), BF16),
        compiler_params=_cparams(("parallel", "parallel")),
        name="win_attn",
    )(qt, k, k, k, k, vt, vt, vt, vt, sink)


NA_ROWS_PER_BLK = BLK // GRID_W
NA_WIN_BLKS = NA_KR // NA_ROWS_PER_BLK + 1
NA_WIN_ROWS = NA_WIN_BLKS * NA_ROWS_PER_BLK
NA_E_MIN = -1


def _na_bias_table(rpb):
    c = np.arange(GRID_W)[None, :]
    kc = np.arange(GRID_W)[:, None]
    cs = np.clip(c - NA_KC // 2, 0, GRID_W - NA_KC)
    ok = (kc >= cs) & (kc < cs + NA_KC)
    dc = np.clip(kc - c + NA_KC - 1, 0, 2 * NA_KC - 2)
    t = jnp.where(jnp.asarray(ok)[None, None], rpb.astype(F32)[:, :, dc], -jnp.inf)
    ninf = jnp.full((rpb.shape[0], 2, GRID_W, GRID_W), -jnp.inf, F32)
    ext = jnp.concatenate([ninf, t, ninf], axis=1)
    n_e = 2 * NA_KR - 1 + 3
    return jnp.concatenate([ext[:, 1:1 + n_e], ext[:, 0:n_e]], axis=-1)


def _na_kernel(qt_ref, k_ref, vt_ref, bias_ref, mbt_ref, o_ref, *, rows):
    pair = 2 * HEAD_DIM
    width = 2 * BLK
    first = lax.broadcasted_iota(jnp.int32, (BLK, width), 0) < GRID_W
    left = (lax.broadcasted_iota(jnp.int32, (BLK, width), 1) & (BLK - 1)) < GRID_W
    pad_last = jnp.zeros((META0, width), BF16)

    def block(j, meta):
        n = pl.program_id(1) * STEP_BLKS + j
        if meta:
            rq = (0, 0)
            rs = (0, 0)
            wb = 0
            kb0 = LEAD // BLK
            start = LEAD
        else:
            r0 = (n - 1) * NA_ROWS_PER_BLK
            rq = (r0, r0 + 1)
            rs = tuple(jnp.clip(r - NA_KR // 2, 0, rows - NA_KR) for r in rq)
            wb = jnp.clip(r0 - NA_KR // 2, 0, rows - NA_WIN_ROWS)
            kb0 = LEAD // BLK + lax.shift_right_logical(wb, 1)
            start = pl.multiple_of(LEAD + wb * GRID_W, BLK)
        masks = []
        for ib in range(NA_WIN_BLKS):
            sel = []
            for kr in (wb + 2 * ib, wb + 2 * ib + 1):
                if meta:
                    oks = [jnp.int32(rs[j] <= kr < rs[j] + NA_KR) for j in range(2)]
                else:
                    oks = [jnp.logical_and(kr >= rs[j], kr < rs[j] + NA_KR).astype(jnp.int32) for j in range(2)]
                sel.append(jnp.where(left, oks[0], oks[1]))
            masks.append(jnp.where(first, sel[0], sel[1]) > 0)
        outs = []
        for pp in range(NA_HEADS // 2):
            cols = slice(pp * pair, (pp + 1) * pair)
            hs = (2 * pp, 2 * pp + 1)
            qbd = _block_diag_rows([[qt_ref[0, j, h * HEAD_DIM:(h + 1) * HEAD_DIM, :]] for h in hs])
            vals = [vt_ref[0, kb0 + ib, cols, :] for ib in range(NA_WIN_BLKS)] + [vt_ref[0, 0, cols, :]]
            scores = []
            for ib in range(NA_WIN_BLKS):
                e = wb + 2 * ib - rq[0] + (NA_KR - 1) - NA_E_MIN
                if meta:
                    col = [jnp.concatenate([bias_ref[h, e][:, 0:1], bias_ref[h, e + 1][:, 0:1]], axis=0) for h in hs]
                    bias = jnp.concatenate([jnp.broadcast_to(c, (BLK, BLK)) for c in col], axis=1)
                else:
                    bias = jnp.concatenate(
                        [jnp.concatenate([bias_ref[h, e], bias_ref[h, e + 1]], axis=0) for h in hs], axis=1)
                s = _dot(k_ref[0, pl.ds(start + ib * BLK, BLK), cols], qbd)
                scores.append(jnp.where(masks[ib], s + bias, -jnp.inf))
            scores.append(_dot(k_ref[0, META0:LEAD, cols], qbd) + jnp.concatenate([mbt_ref[h] for h in hs], axis=1))
            o, den = _softmax_pv_t(scores, vals, pad_last)
            o = o / den
            outs += [o[hh * HEAD_DIM:(hh + 1) * HEAD_DIM, hh * BLK:(hh + 1) * BLK] for hh in range(2)]
        o_ref[0, j * BLK:(j + 1) * BLK, :] = jnp.concatenate(outs, axis=0).T.astype(BF16)

    @pl.when(pl.program_id(1) == 0)
    def _():
        block(0, True)

    @pl.when(pl.program_id(1) > 0)
    def _():
        block(0, False)

    for j in range(1, STEP_BLKS):
        block(j, False)


def _na_attention(qt, k, vt, bias, mbt):
    bsz, nb, width, _ = qt.shape
    lp = nb * BLK
    rows = (lp - LEAD) // GRID_W
    assert rows >= NA_WIN_ROWS and rows % NA_ROWS_PER_BLK == 0
    qblk = pl.BlockSpec((1, STEP_BLKS, width, BLK), lambda b, n: (b, n, 0, 0))
    kseq = pl.BlockSpec((1, lp, width), lambda b, n: (b, 0, 0), pipeline_mode=pl.Buffered(1))
    vseq = pl.BlockSpec((1, nb, width, BLK), lambda b, n: (b, 0, 0, 0), pipeline_mode=pl.Buffered(1))
    full = lambda arr: pl.BlockSpec(arr.shape, lambda b, n: (0,) * arr.ndim)
    return pl.pallas_call(
        functools.partial(_na_kernel, rows=rows),
        grid=(bsz, nb // STEP_BLKS),
        in_specs=[qblk, kseq, vseq, full(bias), full(mbt)],
        out_specs=pl.BlockSpec((1, TM, width), lambda b, n: (b, n, 0)),
        out_shape=jax.ShapeDtypeStruct((bsz, lp, width), BF16),
        compiler_params=_cparams(("parallel", "arbitrary")),
        name="na_attn",
    )(qt, k, vt, bias, mbt)


def _outproj_kernel(*refs, ntile):
    lead_ref = refs[0] if ntile > 1 else None
    tile_refs = refs[ntile > 1:ntile + (ntile > 1)]
    ys_ref, yw_ref, yn_ref, w1_ref, w2_ref, w3_ref, nw_ref, o_ref = refs[ntile + (ntile > 1):]
    i = pl.program_id(1)
    mix = _dot(ys_ref[0], w1_ref[...]) + _dot(yw_ref[0], w2_ref[...]) + _dot(yn_ref[0], w3_ref[...])
    out = _load_tile(tile_refs, lead_ref, i) + _rms(mix, nw_ref[...])
    o_ref[0] = jnp.where(_live_rows(o_ref.shape[1], i), out, 0.0)


def _outproj(src, y_ssd, y_win, y_na, w1, w2, w3, nw):
    bsz, lp, _ = y_ssd.shape
    tm = TM if src[0] == 'x' else _tile_rows(lp, OUTPROJ_TILE_CAP)
    tile = lambda n: pl.BlockSpec((1, tm, n), lambda b, i: (b, i, 0))
    full = _resident
    tile_specs, tile_args = _tile_specs(src, tm)
    return pl.pallas_call(
        functools.partial(_outproj_kernel, ntile=len(tile_specs) - (src[0] == 'x')),
        grid=(bsz, lp // tm),
        in_specs=tile_specs + [tile(y_ssd.shape[-1]), tile(y_win.shape[-1]), tile(y_na.shape[-1]),
                               full(w1), full(w2), full(w3), full(nw)],
        out_specs=tile(D_MODEL),
        out_shape=jax.ShapeDtypeStruct((bsz, lp, D_MODEL), F32),
        compiler_params=_cparams(("parallel", "parallel")),
        name="outproj",
    )(*tile_args, y_ssd, y_win, y_na, w1, w2, w3, nw)


def _ffn_kernel(*refs, nt, ntile, padded):
    tile_refs = refs[:ntile]
    hp_ref, hn_ref, nw1_ref, wup_ref, cw_ref, cb_ref, wdn_ref, nw2_ref, o_ref, act_ref = refs[ntile:]
    i = pl.program_id(1)
    rows = act_ref.shape[0]
    nw1 = nw1_ref[...]
    hc = _load_tile(tile_refs, None, i)
    f = jnp.concatenate([_rms(hp_ref[0], nw1), _rms(hc, nw1), _rms(hn_ref[0], nw1)], axis=0).astype(BF16)
    tail = jnp.where(i < nt - 1, 1.0, 0.0)
    nch = D_FF // FF_CHUNK

    def conv(g, c):
        g = jnp.concatenate([g[:HALO + rows], g[HALO + rows:] * tail], axis=0)
        out = _shift_rows(g, -1) * cw_ref[c, 0:1, :] + cb_ref[c]
        out = out + g[HALO:HALO + rows] * cw_ref[c, 1:2, :]
        return out + _shift_rows(g, 1) * cw_ref[c, 2:3, :]

    for c in range(nch):
        gate = conv(_dot(f, wup_ref[c]), c)
        up = conv(_dot(f, wup_ref[nch + c]), nch + c)
        th = jnp.tanh(gate * (GELU_K + (GELU_K * GELU_C) * (gate * gate)))
        act_ref[:, c * FF_CHUNK:(c + 1) * FF_CHUNK] = (gate * (0.5 * th + 0.5) * up).astype(BF16)
    out = hc + _rms(_dot(act_ref[...], wdn_ref[...]), nw2_ref[...])
    o_ref[0] = jnp.where(_live_rows(rows, i), out, 0.0) if padded else out


def _ffn(h, nw1, wup, cw, cb, wdn, nw2, *, last):
    bsz, lp, _ = h.shape
    full = _resident
    if last:
        n_out = lp - LEAD
        tm = _tile_rows(n_out, FFN_OUT_TILE_CAP, BLK)
        h4 = h.reshape(bsz, lp // BLK, BLK, D_MODEL)
        k = tm // BLK
        tile_specs = [pl.BlockSpec((1, 1, BLK, D_MODEL), lambda b, i, j=j: (b, LEAD // BLK + i * k + j, 0, 0))
                      for j in range(k)]
        tile_args = [h4] * k
        off = LEAD // HALO
    else:
        tm, n_out = TM, lp
        tile_specs, tile_args = _tile_specs(('h', h), tm)
        off = 0
    nt = n_out // tm
    hb = tm // HALO
    last_blk = lp // HALO - 1
    prev = pl.BlockSpec((1, HALO, D_MODEL), lambda b, i: (b, jnp.clip(off + i * hb - 1, 0, last_blk), 0))
    nxt = pl.BlockSpec((1, HALO, D_MODEL), lambda b, i: (b, jnp.clip(off + (i + 1) * hb, 0, last_blk), 0))
    return pl.pallas_call(
        functools.partial(_ffn_kernel, nt=nt, ntile=len(tile_specs), padded=not last),
        grid=(bsz, nt),
        in_specs=tile_specs + [prev, nxt, full(nw1), full(wup), full(cw), full(cb), full(wdn), full(nw2)],
        out_specs=pl.BlockSpec((1, tm, D_MODEL), lambda b, i: (b, i, 0)),
        out_shape=jax.ShapeDtypeStruct((bsz, n_out, D_MODEL), F32),
        scratch_shapes=[pltpu.VMEM((tm, D_FF), BF16)],
        compiler_params=_cparams(("parallel", "parallel")),
        name="ffn",
    )(*tile_args, h, h, nw1, wup, cw, cb, wdn, nw2)


def _pad_lanes(x, n):
    return jnp.pad(x, [(0, 0)] * (x.ndim - 1) + [(0, n - x.shape[-1])])


def _rope_tables(lp):
    half = ROPE_DIM // 2
    pos = jnp.maximum(jnp.arange(lp) - META0, 0).astype(F32)
    inv = jnp.power(ROPE_THETA, -jnp.arange(half, dtype=F32) / half)
    ang = pos[:, None] * inv[None, :]
    cos, sin = jnp.cos(ang), jnp.sin(ang)
    zeros, ones = jnp.zeros_like(cos), jnp.ones((lp, HEAD_DIM - ROPE_DIM), F32)
    rest = jnp.zeros((lp, HEAD_DIM - ROPE_DIM), F32)
    c = jnp.concatenate([cos, cos, ones], axis=1)
    s1 = jnp.concatenate([-sin, zeros, rest], axis=1)
    s2 = jnp.concatenate([zeros, sin, rest], axis=1)
    return tuple(jnp.tile(t, (1, 128 // HEAD_DIM)) for t in (c, s1, s2)) + (cos.T, sin.T)


def _layer_params(i, p):
    row = lambda v: v.reshape(1, -1).astype(F32)
    lanes = lambda v: jnp.broadcast_to(v.astype(F32)[:, None], (v.shape[0], BLK))
    sizes = [SSD_INNER, SSD_XBC, 2 * SSD_HEADS, WIN_Q_HEADS * HEAD_DIM, WIN_KV_HEADS * HEAD_DIM,
             WIN_KV_HEADS * HEAD_DIM, NA_HEADS * HEAD_DIM, NA_HEADS * HEAD_DIM, NA_HEADS * HEAD_DIM]
    w_in = p['w_in'][i].astype(BF16)
    ws = jnp.split(w_in, np.cumsum(sizes)[:-1].tolist(), axis=1)
    ws[2] = _pad_lanes(ws[2], 128)
    w_out = p['w_out'][i].astype(BF16)
    nch = D_FF // FF_CHUNK
    chunked = lambda m: m.reshape(m.shape[0], 2 * nch, FF_CHUNK).swapaxes(0, 1)
    return dict(
        norm_mix_pre=row(p['norm_mix_pre'][i]),
        w_in=ws,
        ssd_conv=(p['ssd_conv_w'][i].astype(F32), row(p['ssd_conv_b'][i])),
        ssd_consts=(_pad_lanes(row(p['ssd_dt_bias'][i]), 128), _pad_lanes(row(p['ssd_a_log'][i]), 128)),
        ssd_d=lanes(jnp.repeat(p['ssd_d'][i], HEAD_DIM)),
        ssd_norm_w=lanes(p['ssd_norm_w'][i]),
        win_sink=row(jnp.repeat(p['win_sink'][i], BLK)),
        na_bias=_na_bias_table(p['na_rpb'][i]),
        na_meta_bias=jnp.broadcast_to(p['na_meta_bias'][i].astype(F32)[:, :, None], (NA_HEADS, N_META, BLK)),
        w_out=(w_out[:SSD_INNER], w_out[SSD_INNER:SSD_INNER + WIN_Q_HEADS * HEAD_DIM],
               w_out[SSD_INNER + WIN_Q_HEADS * HEAD_DIM:]),
        norm_mix_post=row(p['norm_mix_post'][i]),
        norm_ffn_pre=row(p['norm_ffn_pre'][i]),
        ffn_w_up=chunked(p['ffn_w_up'][i].astype(BF16)),
        ffn_conv_w=chunked(p['ffn_conv_w'][i].astype(F32)),
        ffn_conv_b=chunked(p['ffn_conv_b'][i].astype(F32).reshape(1, -1)),
        ffn_w_down=p['ffn_w_down'][i].astype(BF16),
        norm_ffn_post=row(p['norm_ffn_post'][i]),
    )


def _encode(x, meta_tokens, layers):
    bsz, n_tok, _ = x.shape
    assert (n_tok + LEAD) % TM == 0 and n_tok // GRID_W >= NA_KR
    lp = LEAD + n_tok
    lead = jnp.concatenate([jnp.zeros((META0, D_MODEL), F32), meta_tokens.astype(F32)], axis=0)
    src = ('x', x.astype(F32), lead)
    rope = _rope_tables(lp)
    for li, lw in enumerate(layers):
        zt, xt, bc, acs, tab, wq, wk, wv, nq, nk, nv = _inproj(src, lp, lw['norm_mix_pre'], lw['w_in'], lw['ssd_conv'],
                                                               lw['ssd_consts'], rope)
        y_fwd = _ssd_pass(xt, bc, acs, tab, None, reverse=False)
        y_ssd = _ssd_pass(xt, bc, acs, tab, (zt, y_fwd, lw['ssd_d'], lw['ssd_norm_w']), reverse=True)
        y_win = _win_attention(wq, wk, wv, lw['win_sink'])
        y_na = _na_attention(nq, nk, nv, lw['na_bias'], lw['na_meta_bias'])
        h = _outproj(src, y_ssd, y_win, y_na, *lw['w_out'], lw['norm_mix_post'])
        h = _ffn(h, lw['norm_ffn_pre'], lw['ffn_w_up'], lw['ffn_conv_w'], lw['ffn_conv_b'],
                 lw['ffn_w_down'], lw['norm_ffn_post'], last=li == len(layers) - 1)
        src = ('h', h)
    return h


def kernel(x_prompt, x_sample, meta_tokens, norm_mix_pre, norm_mix_post, w_in, ssd_conv_w, ssd_conv_b,
           ssd_dt_bias, ssd_a_log, ssd_d, ssd_norm_w, win_sink, na_rpb, na_meta_bias, w_out, norm_ffn_pre,
           norm_ffn_post, ffn_w_up, ffn_conv_w, ffn_conv_b, ffn_w_down):
    p = dict(norm_mix_pre=norm_mix_pre, norm_mix_post=norm_mix_post, w_in=w_in, ssd_conv_w=ssd_conv_w,
             ssd_conv_b=ssd_conv_b, ssd_dt_bias=ssd_dt_bias, ssd_a_log=ssd_a_log, ssd_d=ssd_d,
             ssd_norm_w=ssd_norm_w, win_sink=win_sink, na_rpb=na_rpb, na_meta_bias=na_meta_bias, w_out=w_out,
             norm_ffn_pre=norm_ffn_pre, norm_ffn_post=norm_ffn_post, ffn_w_up=ffn_w_up, ffn_conv_w=ffn_conv_w,
             ffn_conv_b=ffn_conv_b, ffn_w_down=ffn_w_down)
    layers = [_layer_params(i, p) for i in range(w_in.shape[0])]
    return (_encode(x_prompt, meta_tokens, layers), _encode(x_sample, meta_tokens, layers))
```

```python
import functools

import jax
import jax.numpy as jnp
import numpy as np
from jax import lax
from jax.experimental import pallas as pl
from jax.experimental.pallas import tpu as pltpu

F32 = jnp.float32
BF16 = jnp.bfloat16

D_MODEL = 1024
N_META = 16
GRID_W = 64
HEAD_DIM = 64

SSD_HEADS = 16
SSD_INNER = SSD_HEADS * HEAD_DIM
SSD_GROUPS = 2
SSD_STATE = 128
SSD_XBC = SSD_INNER + 2 * SSD_GROUPS * SSD_STATE
SSD_CONV = 5

WIN_Q_HEADS = 8
WIN_KV_HEADS = 2
WIN_RADIUS = 128
ROPE_THETA = 500000.0
ROPE_DIM = HEAD_DIM // 4

NA_HEADS = 8
NA_KR = 8
NA_KC = 16

D_FF = 2816
EPS = 1e-6
GELU_K = float(np.sqrt(2.0 / np.pi))
GELU_C = 0.044715

BLK = 128
LEAD = BLK
META0 = LEAD - N_META
HALO = 8
TM = 384
FFN_OUT_TILE_CAP = 720
OUTPROJ_TILE_CAP = 1100
FF_CHUNK = 256
STEP_BLKS = TM // BLK
VMEM_LIMIT = 56 * 1024 * 1024


def _tile_rows(n, cap, unit=HALO):
    return max(t for t in range(unit, cap + 1, unit) if n % t == 0)


def _resident(arr):
    return pl.BlockSpec(arr.shape, lambda *_: (0,) * arr.ndim, pipeline_mode=pl.Buffered(1))


def _cparams(sem):
    return pltpu.CompilerParams(dimension_semantics=sem, vmem_limit_bytes=VMEM_LIMIT)


def _rms(x, w):
    return x * lax.rsqrt(jnp.mean(x * x, axis=-1, keepdims=True) + EPS) * w


def _dot(a, b):
    return jnp.dot(a, b, preferred_element_type=F32)


def _shift_rows(x, d):
    return pltpu.roll(x, (-d) % x.shape[0], 0)[HALO:x.shape[0] - HALO]


def _load_tile(refs, lead_ref, i):
    if len(refs) == 1:
        return refs[0][0]
    blocks = [r[0, 0] for r in refs]
    if lead_ref is not None:
        blocks[0] = jnp.where(i == 0, lead_ref[...], blocks[0])
    return jnp.concatenate(blocks, axis=0)


def _tile_specs(src, tm):
    if src[0] == 'h':
        return [pl.BlockSpec((1, tm, D_MODEL), lambda b, i: (b, i, 0))], [src[1]]
    x, lead = src[1], src[2]
    x4 = x.reshape(x.shape[0], x.shape[1] // BLK, BLK, D_MODEL)
    k = tm // BLK
    specs = [pl.BlockSpec((1, 1, BLK, D_MODEL), lambda b, i, j=j: (b, jnp.maximum(i * k + j - LEAD // BLK, 0), 0, 0))
             for j in range(k)]
    return [pl.BlockSpec(lead.shape, lambda b, i: (0, 0))] + specs, [lead] + [x4] * k


def _halo_specs(src, tm):
    arr = src[1]
    off = 0 if src[0] == 'h' else LEAD // HALO
    last = arr.shape[1] // HALO - 1
    hb = tm // HALO
    prev = pl.BlockSpec((1, HALO, D_MODEL), lambda b, i: (b, jnp.clip(i * hb - 1 - off, 0, last), 0))
    nxt = pl.BlockSpec((1, HALO, D_MODEL), lambda b, i: (b, jnp.clip((i + 1) * hb - off, 0, last), 0))
    return [prev, nxt], [arr, arr]


def _live_rows(tile_rows, tile_index):
    row = tile_index * tile_rows + lax.broadcasted_iota(jnp.int32, (tile_rows, 1), 0)
    return row >= META0


def _dot_nt(a, b):
    return lax.dot_general(a, b, (((1,), (1,)), ((), ())), preferred_element_type=F32)


def _rope128(x, c, s1, s2):
    return x * c + pltpu.roll(x, 128 - ROPE_DIM // 2, 1) * s1 + pltpu.roll(x, ROPE_DIM // 2, 1) * s2


def _scan_tables(dt_raw, dt_bias, a_log, live, acs_ref, tab_ref):
    nh = 2 * SSD_HEADS
    dtr = dt_raw + dt_bias
    dt = jnp.maximum(dtr, 0.0) + jnp.log1p(jnp.exp(-jnp.abs(dtr)))
    dt = jnp.where(live, dt, 0.0)
    a = dt * (-jnp.exp(a_log))
    ri = lax.broadcasted_iota(jnp.int32, (BLK, BLK), 0)
    ci = lax.broadcasted_iota(jnp.int32, (BLK, BLK), 1)
    tri = (ci <= ri).astype(BF16)
    fwd = ci < SSD_HEADS
    fwd_rows = ri < SSD_HEADS
    for j in range(TM // BLK):
        rows = slice(j * BLK, (j + 1) * BLK)
        aj, dtj = a[rows], dt[rows]
        pre = sum(_dot(tri, part) for part in _split3(aj))
        total = pre[BLK - 1:BLK, :]
        acs = jnp.where(fwd, pre, total - pre + aj)
        acs_ref[0, rows, :] = acs
        acs_t = acs.T[:nh]
        tab_ref[0, j, 0] = acs_t
        tab_ref[0, j, 1] = dtj.T[:nh]
        tab_ref[0, j, 2] = (dtj * jnp.exp(total - acs)).T[:nh]
        tab_ref[0, j, 3] = jnp.exp(acs_t)
        chunk_sum = jnp.where(fwd_rows[:nh, 0:1], acs_t[:, BLK - 1:BLK], acs_t[:, 0:1])
        tab_ref[0, j, 4] = jnp.broadcast_to(jnp.exp(chunk_sum), (nh, BLK))


def _inproj_kernel(*refs, nt, ntile):
    lead_ref = refs[0] if ntile > 1 else None
    tile_refs = refs[ntile > 1:ntile + (ntile > 1)]
    (hp_ref, hn_ref, nw_ref, wz_ref, wx_ref, wdt_ref, wwq_ref, wwk_ref, wwv_ref,
     wnq_ref, wnk_ref, wnv_ref, cw_ref, cb_ref, dtb_ref, alog_ref, rc_ref, rs1_ref, rs2_ref, rct_ref, rst_ref,
     z_ref, xt_ref, bc_ref, acs_ref, tab_ref, wq_ref, wk_ref, wv_ref, nq_ref, nk_ref, nv_ref) = refs[ntile + (ntile > 1):]
    i = pl.program_id(1)
    nw = nw_ref[...]
    a32 = _rms(_load_tile(tile_refs, lead_ref, i), nw)
    a = a32.astype(BF16)
    head = jnp.where(i > 0, 1.0, 0.0)
    tail = jnp.where(i < nt - 1, 1.0, 0.0)
    a_ext = jnp.concatenate([_rms(hp_ref[0], nw) * head, a32, _rms(hn_ref[0], nw) * tail], axis=0).astype(BF16)
    xe = _dot(a_ext, wx_ref[...])
    pad = SSD_CONV // 2
    acc = _shift_rows(xe, -pad) * cw_ref[0:1, :] + cb_ref[...]
    for j in range(1, SSD_CONV):
        acc = acc + _shift_rows(xe, j - pad) * cw_ref[j:j + 1, :]
    xc = acc * (1.0 / (1.0 + jnp.exp(-acc)))

    def put_blocks(ref, xt):
        for j in range(TM // BLK):
            ref[0, j] = xt[:, j * BLK:(j + 1) * BLK].astype(BF16)

    put_blocks(xt_ref, xc[:, :SSD_INNER].T)
    bc_ref[0] = xc[:, SSD_INNER:].astype(BF16)
    put_blocks(z_ref, _dot(a, wz_ref[...]).T)
    _scan_tables(_dot(a, wdt_ref[...]), dtb_ref[...], alog_ref[...], _live_rows(TM, i), acs_ref, tab_ref)

    scale = HEAD_DIM ** -0.5
    qt = _dot(a, wwq_ref[...]).T
    cos_t, sin_t = rct_ref[...], rst_ref[...]
    half = ROPE_DIM // 2
    parts = []
    for hh in range(WIN_Q_HEADS):
        x1 = qt[hh * HEAD_DIM:hh * HEAD_DIM + half]
        x2 = qt[hh * HEAD_DIM + half:hh * HEAD_DIM + ROPE_DIM]
        parts += [x1 * cos_t - x2 * sin_t, x2 * cos_t + x1 * sin_t, qt[hh * HEAD_DIM + ROPE_DIM:(hh + 1) * HEAD_DIM]]
    put_blocks(wq_ref, jnp.concatenate(parts, axis=0) * scale)
    wk_ref[0] = _rope128(_dot(a, wwk_ref[...]), rc_ref[...], rs1_ref[...], rs2_ref[...]).astype(BF16)
    put_blocks(wv_ref, _dot(a, wwv_ref[...]).T)
    put_blocks(nq_ref, (_dot(a, wnq_ref[...]) * scale).T)
    nk_ref[0] = _dot(a, wnk_ref[...]).astype(BF16)
    put_blocks(nv_ref, _dot(a, wnv_ref[...]).T)


def _inproj(src, lp, nw, ws, conv, scan, rope):
    bsz = src[1].shape[0]
    nt = lp // TM
    nb = lp // BLK
    tile = lambda n: pl.BlockSpec((1, TM, n), lambda b, i: (b, i, 0))
    tblk = lambda n: pl.BlockSpec((1, TM // BLK, n, BLK), lambda b, i: (b, i, 0, 0))
    tile_specs, tile_args = _tile_specs(src, TM)
    halo_specs, halo_args = _halo_specs(src, TM)
    full = lambda arr: pl.BlockSpec(arr.shape, lambda b, i: (0,) * arr.ndim)
    rtab = pl.BlockSpec((TM, 128), lambda b, i: (i, 0))
    rtab_t = pl.BlockSpec((ROPE_DIM // 2, TM), lambda b, i: (0, i))
    tok = lambda n, d: (tile(n), jax.ShapeDtypeStruct((bsz, lp, n), d))
    blk = lambda n: (tblk(n), jax.ShapeDtypeStruct((bsz, nb, n, BLK), BF16))
    kvw = WIN_KV_HEADS * HEAD_DIM
    tabs = (pl.BlockSpec((1, TM // BLK, 5, 2 * SSD_HEADS, BLK), lambda b, i: (b, i, 0, 0, 0)),
            jax.ShapeDtypeStruct((bsz, nb, 5, 2 * SSD_HEADS, BLK), F32))
    outs = [blk(SSD_INNER), blk(SSD_INNER), tok(SSD_XBC - SSD_INNER, BF16), tok(128, F32), tabs, blk(WIN_Q_HEADS * HEAD_DIM), tok(kvw, BF16),
            blk(kvw), blk(NA_HEADS * HEAD_DIM), tok(NA_HEADS * HEAD_DIM, BF16), blk(NA_HEADS * HEAD_DIM)]
    return pl.pallas_call(
        functools.partial(_inproj_kernel, nt=nt, ntile=len(tile_specs) - (src[0] == 'x')),
        grid=(bsz, nt),
        in_specs=tile_specs + halo_specs + [full(nw)] + [full(w) for w in ws] + [full(c) for c in conv + scan]
        + [rtab, rtab, rtab, rtab_t, rtab_t],
        out_specs=[o[0] for o in outs],
        out_shape=[o[1] for o in outs],
        compiler_params=_cparams(("parallel", "parallel")),
        name="inproj",
    )(*tile_args, *halo_args, nw, *ws, *conv, *scan, *rope)


def _split3(x):
    hi = x.astype(BF16)
    r1 = x - hi.astype(F32)
    mid = r1.astype(BF16)
    lo = (r1 - mid.astype(F32)).astype(BF16)
    return hi, mid, lo


def _ssd_kernel(*refs, reverse, final):
    if final:
        xt_ref, bc_ref, acs_ref, tab_ref, zt_ref, yo_ref, dsk_ref, nw_ref, o_ref, state_ref, ybuf_ref = refs
    else:
        xt_ref, bc_ref, acs_ref, tab_ref, o_ref, state_ref = refs

    @pl.when(pl.program_id(1) == 0)
    def _():
        state_ref[...] = jnp.zeros_like(state_ref)

    ri = lax.broadcasted_iota(jnp.int32, (BLK, BLK), 0)
    ci = lax.broadcasted_iota(jnp.int32, (BLK, BLK), 1)
    feeds = (ri >= ci) if reverse else (ri <= ci)
    hoff = SSD_HEADS if reverse else 0
    rep = SSD_HEADS // SSD_GROUPS
    for j in (reversed(range(STEP_BLKS)) if reverse else range(STEP_BLKS)):
        trows = slice(j * BLK, (j + 1) * BLK)
        acs = acs_ref[0, trows, :]
        acs_t, dt_t, dt_out_t, e_in_t, e_chunk = (tab_ref[0, j, k] for k in range(5))
        for g in range(SSD_GROUPS):
            bm = bc_ref[0, trows, g * SSD_STATE:(g + 1) * SSD_STATE]
            cm = bc_ref[0, trows, (SSD_GROUPS + g) * SSD_STATE:(SSD_GROUPS + g + 1) * SSD_STATE]
            cbt = _dot_nt(bm, cm)
            ht = state_ref[g]
            y_off = _dot_nt(ht.astype(BF16), cm)
            x_out, decay = [], []
            for r in range(rep):
                h = g * rep + r
                hl = hoff + h
                rows = slice(h * HEAD_DIM, (h + 1) * HEAD_DIM)
                xt = xt_ref[0, j, rows, :].astype(F32)
                x_in = (xt * dt_t[hl:hl + 1, :]).astype(BF16)
                x_out.append((xt * dt_out_t[hl:hl + 1, :]).astype(BF16))
                decay.append(jnp.broadcast_to(e_chunk[hl:hl + 1, :], (HEAD_DIM, SSD_STATE)))
                diff = acs_t[hl:hl + 1, :] - acs[:, hl:hl + 1]
                lt = (cbt * jnp.exp(jnp.where(feeds, diff, -jnp.inf))).astype(BF16)
                y = _dot(x_in, lt) + y_off[r * HEAD_DIM:(r + 1) * HEAD_DIM] * e_in_t[hl:hl + 1, :]
                if final:
                    ybuf_ref[rows, :] = y
                else:
                    o_ref[0, j, rows, :] = y
            s_new = _dot(jnp.concatenate(x_out, axis=0), bm)
            state_ref[g] = ht * jnp.concatenate(decay, axis=0) + s_new

        if final:
            y = ybuf_ref[...] + yo_ref[0, j] + dsk_ref[...] * xt_ref[0, j].astype(F32)
            z = zt_ref[0, j].astype(F32)
            y = y * (z * (1.0 / (1.0 + jnp.exp(-z))))
            y = y * lax.rsqrt(jnp.mean(y * y, axis=0, keepdims=True) + EPS) * nw_ref[...]
            o_ref[0, trows, :] = y.T.astype(BF16)


def _ssd_pass(xt, bc, acs, tab, final_args, *, reverse):
    bsz, nc = xt.shape[:2]
    ns = nc // STEP_BLKS
    order = (lambda c: ns - 1 - c) if reverse else (lambda c: c)
    final = final_args is not None
    tok = lambda n: pl.BlockSpec((1, TM, n), lambda b, c: (b, order(c), 0))
    feat = pl.BlockSpec((1, STEP_BLKS, SSD_INNER, BLK), lambda b, c: (b, order(c), 0, 0))
    full = lambda arr: pl.BlockSpec(arr.shape, lambda b, c: (0,) * arr.ndim)
    tabs = pl.BlockSpec((1, STEP_BLKS) + tab.shape[2:], lambda b, c: (b, order(c), 0, 0, 0))
    in_specs = [feat, tok(bc.shape[-1]), tok(128), tabs]
    args = [xt, bc, acs, tab]
    scratch = [pltpu.VMEM((SSD_GROUPS, SSD_INNER // SSD_GROUPS, SSD_STATE), F32)]
    if final:
        zt, y_other, dsk, nw = final_args
        in_specs += [feat, feat, full(dsk), full(nw)]
        args += [zt, y_other, dsk, nw]
        scratch.append(pltpu.VMEM((SSD_INNER, BLK), F32))
        out_spec, out_shape = tok(SSD_INNER), jax.ShapeDtypeStruct((bsz, nc * BLK, SSD_INNER), BF16)
    else:
        out_spec, out_shape = feat, jax.ShapeDtypeStruct((bsz, nc, SSD_INNER, BLK), F32)
    return pl.pallas_call(
        functools.partial(_ssd_kernel, reverse=reverse, final=final),
        grid=(bsz, ns),
        in_specs=in_specs,
        out_specs=out_spec,
        out_shape=out_shape,
        scratch_shapes=scratch,
        compiler_params=_cparams(("parallel", "arbitrary")),
        name="ssd_bwd" if reverse else "ssd_fwd",
    )(*args)


def _softmax_pv_t(scores, values, pad_last, extra_logit=None, merge_blocks=False):
    mx = functools.reduce(jnp.maximum, [jnp.max(s, axis=0, keepdims=True) for s in scores])
    if extra_logit is not None:
        mx = jnp.maximum(mx, extra_logit)
    ps = [jnp.exp(s - mx) for s in scores]
    den = functools.reduce(jnp.add, [jnp.sum(p, axis=0, keepdims=True) for p in ps])
    if extra_logit is not None:
        den = den + jnp.exp(extra_logit - mx)
    pb = [p.astype(BF16) for p in ps]
    pb[-1] = jnp.concatenate([pad_last, pb[-1]], axis=0)
    if merge_blocks:
        return _dot(jnp.concatenate(values, axis=1), jnp.concatenate(pb, axis=0)), den
    return functools.reduce(jnp.add, [_dot(v, p) for v, p in zip(values, pb)]), den


def _block_diag_rows(pieces):
    zero = jnp.zeros_like(pieces[0][0])
    rows = []
    for i in range(len(pieces)):
        row = []
        for j, ps in enumerate(pieces):
            row += [p if i == j else zero for p in ps]
        rows.append(jnp.concatenate(row, axis=1))
    return jnp.concatenate(rows, axis=0)


def _win_kernel(qt_ref, kp_ref, kc_ref, kn_ref, km_ref, vp_ref, vc_ref, vn_ref, vm_ref, sink_ref, o_ref, *, nb):
    width = WIN_Q_HEADS * BLK
    ki = lax.broadcasted_iota(jnp.int32, (BLK, width), 0)
    qi = lax.broadcasted_iota(jnp.int32, (BLK, width), 1) & (BLK - 1)
    rep = WIN_Q_HEADS // WIN_KV_HEADS
    keys = [kp_ref[0]] + [kc_ref[0, j * BLK:(j + 1) * BLK, :] for j in range(STEP_BLKS)] + [kn_ref[0]]
    vals = [vp_ref[0, 0]] + [vc_ref[0, j] for j in range(STEP_BLKS)] + [vn_ref[0, 0]]
    kmeta, vmeta = km_ref[0, META0:, :], vm_ref[0, 0]
    pad_last = jnp.zeros((META0, width), BF16)
    for j in range(STEP_BLKS):
        n = pl.program_id(1) * STEP_BLKS + j
        ok_prev = jnp.logical_and(ki >= qi, n >= 2)
        ok_cur = n >= 1
        ok_next = jnp.logical_and(ki <= qi, n + 1 <= nb - 1)
        heads = [qt_ref[0, j, h * HEAD_DIM:(h + 1) * HEAD_DIM, :] for h in range(WIN_Q_HEADS)]
        qbd = _block_diag_rows([heads[g * rep:(g + 1) * rep] for g in range(WIN_KV_HEADS)])
        s = _dot(jnp.concatenate(keys[j:j + 3] + [kmeta], axis=0), qbd)
        scores = [jnp.where(ok_prev, s[:BLK], -jnp.inf),
                  jnp.where(ok_cur, s[BLK:2 * BLK], -jnp.inf),
                  jnp.where(ok_next, s[2 * BLK:3 * BLK], -jnp.inf),
                  s[3 * BLK:]]
        o, den = _softmax_pv_t(scores, vals[j:j + 3] + [vmeta], pad_last, extra_logit=sink_ref[...],
                               merge_blocks=True)
        o = o / den
        outs = [o[(h // rep) * HEAD_DIM:(h // rep + 1) * HEAD_DIM, h * BLK:(h + 1) * BLK] for h in range(WIN_Q_HEADS)]
        o_ref[0, j * BLK:(j + 1) * BLK, :] = jnp.concatenate(outs, axis=0).T.astype(BF16)


def _win_attention(qt, k, vt, sink):
    bsz, nb = qt.shape[:2]
    ns = nb // STEP_BLKS
    kvw = WIN_KV_HEADS * HEAD_DIM
    assert kvw == BLK
    kedge = lambda f: pl.BlockSpec((1, BLK, kvw), lambda b, i: (b, f(i), 0))
    vedge = lambda f: pl.BlockSpec((1, 1, kvw, BLK), lambda b, i: (b, f(i), 0, 0))
    before = lambda i: jnp.maximum(i * STEP_BLKS - 1, 0)
    after = lambda i: jnp.minimum((i + 1) * STEP_BLKS, nb - 1)
    first = lambda i: 0
    return pl.pallas_call(
        functools.partial(_win_kernel, nb=nb),
        grid=(bsz, ns),
        in_specs=[pl.BlockSpec((1, STEP_BLKS, WIN_Q_HEADS * HEAD_DIM, BLK), lambda b, i: (b, i, 0, 0)),
                  kedge(before), pl.BlockSpec((1, TM, kvw), lambda b, i: (b, i, 0)), kedge(after), kedge(first),
                  vedge(before), pl.BlockSpec((1, STEP_BLKS, kvw, BLK), lambda b, i: (b, i, 0, 0)), vedge(after),
                  vedge(first), pl.BlockSpec(sink.shape, lambda b, i: (0, 0))],
        out_specs=pl.BlockSpec((1, TM, WIN_Q_HEADS * HEAD_DIM), lambda b, i: (b, i, 0)),
        out_shape=jax.ShapeDtypeStruct((bsz, nb * BLK, WIN_Q_HEADS * HEAD_DIM), BF16),
        compiler_params=_cparams(("parallel", "parallel")),
        name="win_attn",
    )(qt, k, k, k, k, vt, vt, vt, vt, sink)


NA_ROWS_PER_BLK = BLK // GRID_W
NA_WIN_BLKS = NA_KR // NA_ROWS_PER_BLK + 1
NA_WIN_ROWS = NA_WIN_BLKS * NA_ROWS_PER_BLK
NA_E_MIN = -1


def _na_bias_table(rpb):
    c = np.arange(GRID_W)[None, :]
    kc = np.arange(GRID_W)[:, None]
    cs = np.clip(c - NA_KC // 2, 0, GRID_W - NA_KC)
    ok = (kc >= cs) & (kc < cs + NA_KC)
    pick = jnp.asarray((kc - c + NA_KC - 1)[None] == np.arange(2 * NA_KC - 1)[:, None, None], F32)
    t = jnp.einsum('hed,dkc->hekc', rpb.astype(F32), pick, precision=lax.Precision.HIGHEST)
    t = jnp.where(jnp.asarray(ok)[None, None], t, -jnp.inf)
    ninf = jnp.full((rpb.shape[0], 2, GRID_W, GRID_W), -jnp.inf, F32)
    ext = jnp.concatenate([ninf, t, ninf], axis=1)
    n_e = 2 * NA_KR - 1 + 3
    return jnp.concatenate([ext[:, 1:1 + n_e], ext[:, 0:n_e]], axis=-1)


def _na_kernel(qt_ref, k_ref, vt_ref, bias_ref, mbt_ref, o_ref, *, rows):
    pair = 2 * HEAD_DIM
    width = 2 * BLK
    first = lax.broadcasted_iota(jnp.int32, (BLK, width), 0) < GRID_W
    left = (lax.broadcasted_iota(jnp.int32, (BLK, width), 1) & (BLK - 1)) < GRID_W
    pad_last = jnp.zeros((META0, width), BF16)

    def block(j, meta):
        n = pl.program_id(1) * STEP_BLKS + j
        if meta:
            rq = (0, 0)
            rs = (0, 0)
            wb = 0
            kb0 = LEAD // BLK
            start = LEAD
        else:
            r0 = (n - 1) * NA_ROWS_PER_BLK
            rq = (r0, r0 + 1)
            rs = tuple(jnp.clip(r - NA_KR // 2, 0, rows - NA_KR) for r in rq)
            wb = jnp.clip(r0 - NA_KR // 2, 0, rows - NA_WIN_ROWS)
            kb0 = LEAD // BLK + lax.shift_right_logical(wb, 1)
            start = pl.multiple_of(LEAD + wb * GRID_W, BLK)
        masks = []
        for ib in range(NA_WIN_BLKS):
            sel = []
            for kr in (wb + 2 * ib, wb + 2 * ib + 1):
                if meta:
                    oks = [jnp.int32(rs[j] <= kr < rs[j] + NA_KR) for j in range(2)]
                else:
                    oks = [jnp.logical_and(kr >= rs[j], kr < rs[j] + NA_KR).astype(jnp.int32) for j in range(2)]
                sel.append(jnp.where(left, oks[0], oks[1]))
            masks.append(jnp.where(first, sel[0], sel[1]) > 0)
        outs = []
        for pp in range(NA_HEADS // 2):
            cols = slice(pp * pair, (pp + 1) * pair)
            hs = (2 * pp, 2 * pp + 1)
            qbd = _block_diag_rows([[qt_ref[0, j, h * HEAD_DIM:(h + 1) * HEAD_DIM, :]] for h in hs])
            vals = [vt_ref[0, kb0 + ib, cols, :] for ib in range(NA_WIN_BLKS)] + [vt_ref[0, 0, cols, :]]
            kwin = jnp.concatenate([k_ref[0, pl.ds(start, NA_WIN_BLKS * BLK), cols], k_ref[0, META0:LEAD, cols]], axis=0)
            s_all = _dot(kwin, qbd)
            scores = []
            for ib in range(NA_WIN_BLKS):
                e = wb + 2 * ib - rq[0] + (NA_KR - 1) - NA_E_MIN
                if meta:
                    col = [jnp.concatenate([bias_ref[h, e][:, 0:1], bias_ref[h, e + 1][:, 0:1]], axis=0) for h in hs]
                    bias = jnp.concatenate([jnp.broadcast_to(c, (BLK, BLK)) for c in col], axis=1)
                else:
                    bias = jnp.concatenate(
                        [jnp.concatenate([bias_ref[h, e], bias_ref[h, e + 1]], axis=0) for h in hs], axis=1)
                scores.append(jnp.where(masks[ib], s_all[ib * BLK:(ib + 1) * BLK] + bias, -jnp.inf))
            scores.append(s_all[NA_WIN_BLKS * BLK:] + jnp.concatenate([mbt_ref[h] for h in hs], axis=1))
            o, den = _softmax_pv_t(scores, vals, pad_last)
            o = o / den
            outs += [o[hh * HEAD_DIM:(hh + 1) * HEAD_DIM, hh * BLK:(hh + 1) * BLK] for hh in range(2)]
        o_ref[0, j * BLK:(j + 1) * BLK, :] = jnp.concatenate(outs, axis=0).T.astype(BF16)

    @pl.when(pl.program_id(1) == 0)
    def _():
        block(0, True)

    @pl.when(pl.program_id(1) > 0)
    def _():
        block(0, False)

    for j in range(1, STEP_BLKS):
        block(j, False)


def _na_attention(qt, k, vt, bias, mbt):
    bsz, nb, width, _ = qt.shape
    lp = nb * BLK
    rows = (lp - LEAD) // GRID_W
    assert rows >= NA_WIN_ROWS and rows % NA_ROWS_PER_BLK == 0
    qblk = pl.BlockSpec((1, STEP_BLKS, width, BLK), lambda b, n: (b, n, 0, 0))
    kseq = pl.BlockSpec((1, lp, width), lambda b, n: (b, 0, 0), pipeline_mode=pl.Buffered(1))
    vseq = pl.BlockSpec((1, nb, width, BLK), lambda b, n: (b, 0, 0, 0), pipeline_mode=pl.Buffered(1))
    full = lambda arr: pl.BlockSpec(arr.shape, lambda b, n: (0,) * arr.ndim)
    return pl.pallas_call(
        functools.partial(_na_kernel, rows=rows),
        grid=(bsz, nb // STEP_BLKS),
        in_specs=[qblk, kseq, vseq, full(bias), full(mbt)],
        out_specs=pl.BlockSpec((1, TM, width), lambda b, n: (b, n, 0)),
        out_shape=jax.ShapeDtypeStruct((bsz, lp, width), BF16),
        compiler_params=_cparams(("parallel", "arbitrary")),
        name="na_attn",
    )(qt, k, vt, bias, mbt)


def _outproj_kernel(*refs, ntile):
    lead_ref = refs[0] if ntile > 1 else None
    tile_refs = refs[ntile > 1:ntile + (ntile > 1)]
    ys_ref, yw_ref, yn_ref, w1_ref, w2_ref, w3_ref, nw_ref, o_ref = refs[ntile + (ntile > 1):]
    i = pl.program_id(1)
    mix = _dot(ys_ref[0], w1_ref[...]) + _dot(yw_ref[0], w2_ref[...]) + _dot(yn_ref[0], w3_ref[...])
    out = _load_tile(tile_refs, lead_ref, i) + _rms(mix, nw_ref[...])
    o_ref[0] = jnp.where(_live_rows(o_ref.shape[1], i), out, 0.0)


def _outproj(src, y_ssd, y_win, y_na, w1, w2, w3, nw):
    bsz, lp, _ = y_ssd.shape
    tm = TM if src[0] == 'x' else _tile_rows(lp, OUTPROJ_TILE_CAP)
    tile = lambda n: pl.BlockSpec((1, tm, n), lambda b, i: (b, i, 0))
    full = _resident
    tile_specs, tile_args = _tile_specs(src, tm)
    return pl.pallas_call(
        functools.partial(_outproj_kernel, ntile=len(tile_specs) - (src[0] == 'x')),
        grid=(bsz, lp // tm),
        in_specs=tile_specs + [tile(y_ssd.shape[-1]), tile(y_win.shape[-1]), tile(y_na.shape[-1]),
                               full(w1), full(w2), full(w3), full(nw)],
        out_specs=tile(D_MODEL),
        out_shape=jax.ShapeDtypeStruct((bsz, lp, D_MODEL), F32),
        compiler_params=_cparams(("parallel", "parallel")),
        name="outproj",
    )(*tile_args, y_ssd, y_win, y_na, w1, w2, w3, nw)


def _ffn_kernel(*refs, nt, ntile, padded):
    tile_refs = refs[:ntile]
    hp_ref, hn_ref, nw1_ref, wup_ref, cw_ref, cb_ref, wdn_ref, nw2_ref, o_ref, act_ref = refs[ntile:]
    i = pl.program_id(1)
    rows = act_ref.shape[0]
    nw1 = nw1_ref[...]
    hc = _load_tile(tile_refs, None, i)
    f = jnp.concatenate([_rms(hp_ref[0], nw1), _rms(hc, nw1), _rms(hn_ref[0], nw1)], axis=0).astype(BF16)
    tail = jnp.where(i < nt - 1, 1.0, 0.0)
    nch = D_FF // FF_CHUNK

    def conv(g, c):
        g = jnp.concatenate([g[:HALO + rows], g[HALO + rows:] * tail], axis=0)
        out = _shift_rows(g, -1) * cw_ref[c, 0:1, :] + cb_ref[c]
        out = out + g[HALO:HALO + rows] * cw_ref[c, 1:2, :]
        return out + _shift_rows(g, 1) * cw_ref[c, 2:3, :]

    for c in range(nch):
        gate = conv(_dot(f, wup_ref[c]), c)
        up = conv(_dot(f, wup_ref[nch + c]), nch + c)
        th = jnp.tanh(gate * (GELU_K + (GELU_K * GELU_C) * (gate * gate)))
        act_ref[:, c * FF_CHUNK:(c + 1) * FF_CHUNK] = (gate * (0.5 * th + 0.5) * up).astype(BF16)
    out = hc + _rms(_dot(act_ref[...], wdn_ref[...]), nw2_ref[...])
    o_ref[0] = jnp.where(_live_rows(rows, i), out, 0.0) if padded else out


def _ffn(h, nw1, wup, cw, cb, wdn, nw2, *, last):
    bsz, lp, _ = h.shape
    full = _resident
    if last:
        n_out = lp - LEAD
        tm = _tile_rows(n_out, FFN_OUT_TILE_CAP, BLK)
        h4 = h.reshape(bsz, lp // BLK, BLK, D_MODEL)
        k = tm // BLK
        tile_specs = [pl.BlockSpec((1, 1, BLK, D_MODEL), lambda b, i, j=j: (b, LEAD // BLK + i * k + j, 0, 0))
                      for j in range(k)]
        tile_args = [h4] * k
        off = LEAD // HALO
    else:
        tm, n_out = TM, lp
        tile_specs, tile_args = _tile_specs(('h', h), tm)
        off = 0
    nt = n_out // tm
    hb = tm // HALO
    last_blk = lp // HALO - 1
    prev = pl.BlockSpec((1, HALO, D_MODEL), lambda b, i: (b, jnp.clip(off + i * hb - 1, 0, last_blk), 0))
    nxt = pl.BlockSpec((1, HALO, D_MODEL), lambda b, i: (b, jnp.clip(off + (i + 1) * hb, 0, last_blk), 0))
    return pl.pallas_call(
        functools.partial(_ffn_kernel, nt=nt, ntile=len(tile_specs), padded=not last),
        grid=(bsz, nt),
        in_specs=tile_specs + [prev, nxt, full(nw1), full(wup), full(cw), full(cb), full(wdn), full(nw2)],
        out_specs=pl.BlockSpec((1, tm, D_MODEL), lambda b, i: (b, i, 0)),
        out_shape=jax.ShapeDtypeStruct((bsz, n_out, D_MODEL), F32),
        scratch_shapes=[pltpu.VMEM((tm, D_FF), BF16)],
        compiler_params=_cparams(("parallel", "parallel")),
        name="ffn",
    )(*tile_args, h, h, nw1, wup, cw, cb, wdn, nw2)


def _pad_lanes(x, n):
    return jnp.pad(x, [(0, 0)] * (x.ndim - 1) + [(0, n - x.shape[-1])])


def _rope_tables(lp):
    half = ROPE_DIM // 2
    pos = jnp.maximum(jnp.arange(lp) - META0, 0).astype(F32)
    inv = jnp.power(ROPE_THETA, -jnp.arange(half, dtype=F32) / half)
    ang = pos[:, None] * inv[None, :]
    cos, sin = jnp.cos(ang), jnp.sin(ang)
    zeros, ones = jnp.zeros_like(cos), jnp.ones((lp, HEAD_DIM - ROPE_DIM), F32)
    rest = jnp.zeros((lp, HEAD_DIM - ROPE_DIM), F32)
    c = jnp.concatenate([cos, cos, ones], axis=1)
    s1 = jnp.concatenate([-sin, zeros, rest], axis=1)
    s2 = jnp.concatenate([zeros, sin, rest], axis=1)
    return tuple(jnp.tile(t, (1, 128 // HEAD_DIM)) for t in (c, s1, s2)) + (cos.T, sin.T)


def _layer_params(i, p):
    row = lambda v: v.reshape(1, -1).astype(F32)
    lanes = lambda v: jnp.broadcast_to(v.astype(F32)[:, None], (v.shape[0], BLK))
    sizes = [SSD_INNER, SSD_XBC, 2 * SSD_HEADS, WIN_Q_HEADS * HEAD_DIM, WIN_KV_HEADS * HEAD_DIM,
             WIN_KV_HEADS * HEAD_DIM, NA_HEADS * HEAD_DIM, NA_HEADS * HEAD_DIM, NA_HEADS * HEAD_DIM]
    w_in = p['w_in'][i].astype(BF16)
    ws = jnp.split(w_in, np.cumsum(sizes)[:-1].tolist(), axis=1)
    ws[2] = _pad_lanes(ws[2], 128)
    w_out = p['w_out'][i].astype(BF16)
    nch = D_FF // FF_CHUNK
    chunked = lambda m: m.reshape(m.shape[0], 2 * nch, FF_CHUNK).swapaxes(0, 1)
    return dict(
        norm_mix_pre=row(p['norm_mix_pre'][i]),
        w_in=ws,
        ssd_conv=(p['ssd_conv_w'][i].astype(F32), row(p['ssd_conv_b'][i])),
        ssd_consts=(_pad_lanes(row(p['ssd_dt_bias'][i]), 128), _pad_lanes(row(p['ssd_a_log'][i]), 128)),
        ssd_d=lanes(jnp.repeat(p['ssd_d'][i], HEAD_DIM)),
        ssd_norm_w=lanes(p['ssd_norm_w'][i]),
        win_sink=row(jnp.repeat(p['win_sink'][i], BLK)),
        na_bias=_na_bias_table(p['na_rpb'][i]),
        na_meta_bias=jnp.broadcast_to(p['na_meta_bias'][i].astype(F32)[:, :, None], (NA_HEADS, N_META, BLK)),
        w_out=(w_out[:SSD_INNER], w_out[SSD_INNER:SSD_INNER + WIN_Q_HEADS * HEAD_DIM],
               w_out[SSD_INNER + WIN_Q_HEADS * HEAD_DIM:]),
        norm_mix_post=row(p['norm_mix_post'][i]),
        norm_ffn_pre=row(p['norm_ffn_pre'][i]),
        ffn_w_up=chunked(p['ffn_w_up'][i].astype(BF16)),
        ffn_conv_w=chunked(p['ffn_conv_w'][i].astype(F32)),
        ffn_conv_b=chunked(p['ffn_conv_b'][i].astype(F32).reshape(1, -1)),
        ffn_w_down=p['ffn_w_down'][i].astype(BF16),
        norm_ffn_post=row(p['norm_ffn_post'][i]),
    )


def _encode(x, meta_tokens, layers):
    bsz, n_tok, _ = x.shape
    assert (n_tok + LEAD) % TM == 0 and n_tok // GRID_W >= NA_KR
    lp = LEAD + n_tok
    lead = jnp.concatenate([jnp.zeros((META0, D_MODEL), F32), meta_tokens.astype(F32)], axis=0)
    src = ('x', x.astype(F32), lead)
    rope = _rope_tables(lp)
    for li, lw in enumerate(layers):
        zt, xt, bc, acs, tab, wq, wk, wv, nq, nk, nv = _inproj(src, lp, lw['norm_mix_pre'], lw['w_in'], lw['ssd_conv'],
                                                               lw['ssd_consts'], rope)
        y_fwd = _ssd_pass(xt, bc, acs, tab, None, reverse=False)
        y_ssd = _ssd_pass(xt, bc, acs, tab, (zt, y_fwd, lw['ssd_d'], lw['ssd_norm_w']), reverse=True)
        y_win = _win_attention(wq, wk, wv, lw['win_sink'])
        y_na = _na_attention(nq, nk, nv, lw['na_bias'], lw['na_meta_bias'])
        h = _outproj(src, y_ssd, y_win, y_na, *lw['w_out'], lw['norm_mix_post'])
        h = _ffn(h, lw['norm_ffn_pre'], lw['ffn_w_up'], lw['ffn_conv_w'], lw['ffn_conv_b'],
                 lw['ffn_w_down'], lw['norm_ffn_post'], last=li == len(layers) - 1)
        src = ('h', h)
    return h


def kernel(x_prompt, x_sample, meta_tokens, norm_mix_pre, norm_mix_post, w_in, ssd_conv_w, ssd_conv_b,
           ssd_dt_bias, ssd_a_log, ssd_d, ssd_norm_w, win_sink, na_rpb, na_meta_bias, w_out, norm_ffn_pre,
           norm_ffn_post, ffn_w_up, ffn_conv_w, ffn_conv_b, ffn_w_down):
    p = dict(norm_mix_pre=norm_mix_pre, norm_mix_post=norm_mix_post, w_in=w_in, ssd_conv_w=ssd_conv_w,
             ssd_conv_b=ssd_conv_b, ssd_dt_bias=ssd_dt_bias, ssd_a_log=ssd_a_log, ssd_d=ssd_d,
             ssd_norm_w=ssd_norm_w, win_sink=win_sink, na_rpb=na_rpb, na_meta_bias=na_meta_bias, w_out=w_out,
             norm_ffn_pre=norm_ffn_pre, norm_ffn_post=norm_ffn_post, ffn_w_up=ffn_w_up, ffn_conv_w=ffn_conv_w,
             ffn_conv_b=ffn_conv_b, ffn_w_down=ffn_w_down)
    layers = [_layer_params(i, p) for i in range(w_in.shape[0])]
    return (_encode(x_prompt, meta_tokens, layers), _encode(x_sample, meta_tokens, layers))
```

```python
import functools

import jax
import jax.numpy as jnp
import numpy as np
from jax import lax
from jax.experimental import pallas as pl
from jax.experimental.pallas import tpu as pltpu

F32 = jnp.float32
BF16 = jnp.bfloat16

D_MODEL = 1024
N_META = 16
GRID_W = 64
HEAD_DIM = 64

SSD_HEADS = 16
SSD_INNER = SSD_HEADS * HEAD_DIM
SSD_GROUPS = 2
SSD_STATE = 128
SSD_XBC = SSD_INNER + 2 * SSD_GROUPS * SSD_STATE
SSD_CONV = 5

WIN_Q_HEADS = 8
WIN_KV_HEADS = 2
WIN_RADIUS = 128
ROPE_THETA = 500000.0
ROPE_DIM = HEAD_DIM // 4

NA_HEADS = 8
NA_KR = 8
NA_KC = 16

D_FF = 2816
EPS = 1e-6
GELU_K = float(np.sqrt(2.0 / np.pi))
GELU_C = 0.044715

BLK = 128
LEAD = BLK
META0 = LEAD - N_META
HALO = 8
TM = 384
FFN_OUT_TILE_CAP = 720
OUTPROJ_TILE_CAP = 1100
FF_CHUNK = 256
STEP_BLKS = TM // BLK
VMEM_LIMIT = 56 * 1024 * 1024


def _tile_rows(n, cap, unit=HALO):
    return max(t for t in range(unit, cap + 1, unit) if n % t == 0)


def _resident(arr):
    return pl.BlockSpec(arr.shape, lambda *_: (0,) * arr.ndim, pipeline_mode=pl.Buffered(1))


def _cparams(sem):
    return pltpu.CompilerParams(dimension_semantics=sem, vmem_limit_bytes=VMEM_LIMIT)


def _rms(x, w):
    return x * lax.rsqrt(jnp.mean(x * x, axis=-1, keepdims=True) + EPS) * w


def _dot(a, b):
    return jnp.dot(a, b, preferred_element_type=F32)


def _shift_rows(x, d):
    return pltpu.roll(x, (-d) % x.shape[0], 0)[HALO:x.shape[0] - HALO]


def _load_tile(refs, lead_ref, i):
    if len(refs) == 1:
        return refs[0][0]
    blocks = [r[0, 0] for r in refs]
    if lead_ref is not None:
        blocks[0] = jnp.where(i == 0, lead_ref[...], blocks[0])
    return jnp.concatenate(blocks, axis=0)


def _tile_specs(src, tm):
    if src[0] == 'h':
        return [pl.BlockSpec((1, tm, D_MODEL), lambda b, i: (b, i, 0))], [src[1]]
    x, lead = src[1], src[2]
    x4 = x.reshape(x.shape[0], x.shape[1] // BLK, BLK, D_MODEL)
    k = tm // BLK
    specs = [pl.BlockSpec((1, 1, BLK, D_MODEL), lambda b, i, j=j: (b, jnp.maximum(i * k + j - LEAD // BLK, 0), 0, 0))
             for j in range(k)]
    return [pl.BlockSpec(lead.shape, lambda b, i: (0, 0))] + specs, [lead] + [x4] * k


def _halo_specs(src, tm):
    arr = src[1]
    off = 0 if src[0] == 'h' else LEAD // HALO
    last = arr.shape[1] // HALO - 1
    hb = tm // HALO
    prev = pl.BlockSpec((1, HALO, D_MODEL), lambda b, i: (b, jnp.clip(i * hb - 1 - off, 0, last), 0))
    nxt = pl.BlockSpec((1, HALO, D_MODEL), lambda b, i: (b, jnp.clip((i + 1) * hb - off, 0, last), 0))
    return [prev, nxt], [arr, arr]


def _live_rows(tile_rows, tile_index):
    row = tile_index * tile_rows + lax.broadcasted_iota(jnp.int32, (tile_rows, 1), 0)
    return row >= META0


def _dot_nt(a, b):
    return lax.dot_general(a, b, (((1,), (1,)), ((), ())), preferred_element_type=F32)


def _rope128(x, c, s1, s2):
    return x * c + pltpu.roll(x, 128 - ROPE_DIM // 2, 1) * s1 + pltpu.roll(x, ROPE_DIM // 2, 1) * s2


def _scan_tables(dt_raw, dt_bias, a_log, live, acs_ref, tab_ref):
    nh = 2 * SSD_HEADS
    dtr = dt_raw + dt_bias
    dt = jnp.maximum(dtr, 0.0) + jnp.log1p(jnp.exp(-jnp.abs(dtr)))
    dt = jnp.where(live, dt, 0.0)
    a = dt * (-jnp.exp(a_log))
    ri = lax.broadcasted_iota(jnp.int32, (BLK, BLK), 0)
    ci = lax.broadcasted_iota(jnp.int32, (BLK, BLK), 1)
    tri = (ci <= ri).astype(BF16)
    fwd = ci < SSD_HEADS
    fwd_rows = ri < SSD_HEADS
    for j in range(TM // BLK):
        rows = slice(j * BLK, (j + 1) * BLK)
        aj, dtj = a[rows], dt[rows]
        pre = sum(_dot(tri, part) for part in _split3(aj))
        total = pre[BLK - 1:BLK, :]
        acs = jnp.where(fwd, pre, total - pre + aj)
        acs_ref[0, rows, :] = acs
        acs_t = acs.T[:nh]
        tab_ref[0, j, 0] = acs_t
        tab_ref[0, j, 1] = dtj.T[:nh]
        tab_ref[0, j, 2] = (dtj * jnp.exp(total - acs)).T[:nh]
        tab_ref[0, j, 3] = jnp.exp(acs_t)
        chunk_sum = jnp.where(fwd_rows[:nh, 0:1], acs_t[:, BLK - 1:BLK], acs_t[:, 0:1])
        tab_ref[0, j, 4] = jnp.broadcast_to(jnp.exp(chunk_sum), (nh, BLK))


def _inproj_kernel(*refs, nt, ntile):
    lead_ref = refs[0] if ntile > 1 else None
    tile_refs = refs[ntile > 1:ntile + (ntile > 1)]
    (hp_ref, hn_ref, nw_ref, wz_ref, wx_ref, wdt_ref, wwq_ref, wwk_ref, wwv_ref,
     wnq_ref, wnk_ref, wnv_ref, cw_ref, cb_ref, dtb_ref, alog_ref, rc_ref, rs1_ref, rs2_ref, rct_ref, rst_ref,
     z_ref, xt_ref, bc_ref, acs_ref, tab_ref, wq_ref, wk_ref, wv_ref, nq_ref, nk_ref, nv_ref) = refs[ntile + (ntile > 1):]
    i = pl.program_id(1)
    nw = nw_ref[...]
    a32 = _rms(_load_tile(tile_refs, lead_ref, i), nw)
    a = a32.astype(BF16)
    head = jnp.where(i > 0, 1.0, 0.0)
    tail = jnp.where(i < nt - 1, 1.0, 0.0)
    a_ext = jnp.concatenate([_rms(hp_ref[0], nw) * head, a32, _rms(hn_ref[0], nw) * tail], axis=0).astype(BF16)
    xe = _dot(a_ext, wx_ref[...])
    pad = SSD_CONV // 2
    acc = _shift_rows(xe, -pad) * cw_ref[0:1, :] + cb_ref[...]
    for j in range(1, SSD_CONV):
        acc = acc + _shift_rows(xe, j - pad) * cw_ref[j:j + 1, :]
    xc = acc * (1.0 / (1.0 + jnp.exp(-acc)))

    def put_blocks(ref, xt):
        for j in range(TM // BLK):
            ref[0, j] = xt[:, j * BLK:(j + 1) * BLK].astype(BF16)

    put_blocks(xt_ref, xc[:, :SSD_INNER].T)
    bc_ref[0] = xc[:, SSD_INNER:].astype(BF16)
    put_blocks(z_ref, _dot(a, wz_ref[...]).T)
    _scan_tables(_dot(a, wdt_ref[...]), dtb_ref[...], alog_ref[...], _live_rows(TM, i), acs_ref, tab_ref)

    scale = HEAD_DIM ** -0.5
    qt = _dot(a, wwq_ref[...]).T
    cos_t, sin_t = rct_ref[...], rst_ref[...]
    half = ROPE_DIM // 2
    parts = []
    for hh in range(WIN_Q_HEADS):
        x1 = qt[hh * HEAD_DIM:hh * HEAD_DIM + half]
        x2 = qt[hh * HEAD_DIM + half:hh * HEAD_DIM + ROPE_DIM]
        parts += [x1 * cos_t - x2 * sin_t, x2 * cos_t + x1 * sin_t, qt[hh * HEAD_DIM + ROPE_DIM:(hh + 1) * HEAD_DIM]]
    put_blocks(wq_ref, jnp.concatenate(parts, axis=0) * scale)
    wk_ref[0] = _rope128(_dot(a, wwk_ref[...]), rc_ref[...], rs1_ref[...], rs2_ref[...]).astype(BF16)
    put_blocks(wv_ref, _dot(a, wwv_ref[...]).T)
    put_blocks(nq_ref, (_dot(a, wnq_ref[...]) * scale).T)
    nk_ref[0] = _dot(a, wnk_ref[...]).astype(BF16)
    nv_ref[0] = _dot(a, wnv_ref[...]).astype(BF16)


def _inproj(src, lp, nw, ws, conv, scan, rope):
    bsz = src[1].shape[0]
    nt = lp // TM
    nb = lp // BLK
    tile = lambda n: pl.BlockSpec((1, TM, n), lambda b, i: (b, i, 0))
    tblk = lambda n: pl.BlockSpec((1, TM // BLK, n, BLK), lambda b, i: (b, i, 0, 0))
    tile_specs, tile_args = _tile_specs(src, TM)
    halo_specs, halo_args = _halo_specs(src, TM)
    full = lambda arr: pl.BlockSpec(arr.shape, lambda b, i: (0,) * arr.ndim)
    rtab = pl.BlockSpec((TM, 128), lambda b, i: (i, 0))
    rtab_t = pl.BlockSpec((ROPE_DIM // 2, TM), lambda b, i: (0, i))
    tok = lambda n, d: (tile(n), jax.ShapeDtypeStruct((bsz, lp, n), d))
    blk = lambda n: (tblk(n), jax.ShapeDtypeStruct((bsz, nb, n, BLK), BF16))
    kvw = WIN_KV_HEADS * HEAD_DIM
    tabs = (pl.BlockSpec((1, TM // BLK, 5, 2 * SSD_HEADS, BLK), lambda b, i: (b, i, 0, 0, 0)),
            jax.ShapeDtypeStruct((bsz, nb, 5, 2 * SSD_HEADS, BLK), F32))
    outs = [blk(SSD_INNER), blk(SSD_INNER), tok(SSD_XBC - SSD_INNER, BF16), tok(128, F32), tabs, blk(WIN_Q_HEADS * HEAD_DIM), tok(kvw, BF16),
            blk(kvw), blk(NA_HEADS * HEAD_DIM), tok(NA_HEADS * HEAD_DIM, BF16), tok(NA_HEADS * HEAD_DIM, BF16)]
    return pl.pallas_call(
        functools.partial(_inproj_kernel, nt=nt, ntile=len(tile_specs) - (src[0] == 'x')),
        grid=(bsz, nt),
        in_specs=tile_specs + halo_specs + [full(nw)] + [full(w) for w in ws] + [full(c) for c in conv + scan]
        + [rtab, rtab, rtab, rtab_t, rtab_t],
        out_specs=[o[0] for o in outs],
        out_shape=[o[1] for o in outs],
        compiler_params=_cparams(("parallel", "parallel")),
        name="inproj",
    )(*tile_args, *halo_args, nw, *ws, *conv, *scan, *rope)


def _split3(x):
    hi = x.astype(BF16)
    r1 = x - hi.astype(F32)
    mid = r1.astype(BF16)
    lo = (r1 - mid.astype(F32)).astype(BF16)
    return hi, mid, lo


def _ssd_kernel(*refs, reverse, final):
    if final:
        xt_ref, bc_ref, acs_ref, tab_ref, zt_ref, yo_ref, dsk_ref, nw_ref, o_ref, state_ref, ybuf_ref = refs
    else:
        xt_ref, bc_ref, acs_ref, tab_ref, o_ref, state_ref = refs

    @pl.when(pl.program_id(1) == 0)
    def _():
        state_ref[...] = jnp.zeros_like(state_ref)

    ri = lax.broadcasted_iota(jnp.int32, (BLK, BLK), 0)
    ci = lax.broadcasted_iota(jnp.int32, (BLK, BLK), 1)
    feeds = (ri >= ci) if reverse else (ri <= ci)
    hoff = SSD_HEADS if reverse else 0
    rep = SSD_HEADS // SSD_GROUPS
    for j in (reversed(range(STEP_BLKS)) if reverse else range(STEP_BLKS)):
        trows = slice(j * BLK, (j + 1) * BLK)
        acs = acs_ref[0, trows, :]
        acs_t, dt_t, dt_out_t, e_in_t, e_chunk = (tab_ref[0, j, k] for k in range(5))
        for g in range(SSD_GROUPS):
            bm = bc_ref[0, trows, g * SSD_STATE:(g + 1) * SSD_STATE]
            cm = bc_ref[0, trows, (SSD_GROUPS + g) * SSD_STATE:(SSD_GROUPS + g + 1) * SSD_STATE]
            cbt = _dot_nt(bm, cm)
            ht = state_ref[g]
            y_off = _dot_nt(ht.astype(BF16), cm)
            x_out, decay = [], []
            for r in range(rep):
                h = g * rep + r
                hl = hoff + h
                rows = slice(h * HEAD_DIM, (h + 1) * HEAD_DIM)
                xt = xt_ref[0, j, rows, :].astype(F32)
                x_in = (xt * dt_t[hl:hl + 1, :]).astype(BF16)
                x_out.append((xt * dt_out_t[hl:hl + 1, :]).astype(BF16))
                decay.append(jnp.broadcast_to(e_chunk[hl:hl + 1, :], (HEAD_DIM, SSD_STATE)))
                diff = acs_t[hl:hl + 1, :] - acs[:, hl:hl + 1]
                lt = (cbt * jnp.exp(jnp.where(feeds, diff, -jnp.inf))).astype(BF16)
                y = _dot(x_in, lt) + y_off[r * HEAD_DIM:(r + 1) * HEAD_DIM] * e_in_t[hl:hl + 1, :]
                if final:
                    ybuf_ref[rows, :] = y
                else:
                    o_ref[0, j, rows, :] = y
            s_new = _dot(jnp.concatenate(x_out, axis=0), bm)
            state_ref[g] = ht * jnp.concatenate(decay, axis=0) + s_new

        if final:
            y = ybuf_ref[...] + yo_ref[0, j] + dsk_ref[...] * xt_ref[0, j].astype(F32)
            z = zt_ref[0, j].astype(F32)
            y = y * (z * (1.0 / (1.0 + jnp.exp(-z))))
            y = y * lax.rsqrt(jnp.mean(y * y, axis=0, keepdims=True) + EPS) * nw_ref[...]
            o_ref[0, trows, :] = y.T.astype(BF16)


def _ssd_pass(xt, bc, acs, tab, final_args, *, reverse):
    bsz, nc = xt.shape[:2]
    ns = nc // STEP_BLKS
    order = (lambda c: ns - 1 - c) if reverse else (lambda c: c)
    final = final_args is not None
    tok = lambda n: pl.BlockSpec((1, TM, n), lambda b, c: (b, order(c), 0))
    feat = pl.BlockSpec((1, STEP_BLKS, SSD_INNER, BLK), lambda b, c: (b, order(c), 0, 0))
    full = lambda arr: pl.BlockSpec(arr.shape, lambda b, c: (0,) * arr.ndim)
    tabs = pl.BlockSpec((1, STEP_BLKS) + tab.shape[2:], lambda b, c: (b, order(c), 0, 0, 0))
    in_specs = [feat, tok(bc.shape[-1]), tok(128), tabs]
    args = [xt, bc, acs, tab]
    scratch = [pltpu.VMEM((SSD_GROUPS, SSD_INNER // SSD_GROUPS, SSD_STATE), F32)]
    if final:
        zt, y_other, dsk, nw = final_args
        in_specs += [feat, feat, full(dsk), full(nw)]
        args += [zt, y_other, dsk, nw]
        scratch.append(pltpu.VMEM((SSD_INNER, BLK), F32))
        out_spec, out_shape = tok(SSD_INNER), jax.ShapeDtypeStruct((bsz, nc * BLK, SSD_INNER), BF16)
    else:
        out_spec, out_shape = feat, jax.ShapeDtypeStruct((bsz, nc, SSD_INNER, BLK), F32)
    return pl.pallas_call(
        functools.partial(_ssd_kernel, reverse=reverse, final=final),
        grid=(bsz, ns),
        in_specs=in_specs,
        out_specs=out_spec,
        out_shape=out_shape,
        scratch_shapes=scratch,
        compiler_params=_cparams(("parallel", "arbitrary")),
        name="ssd_bwd" if reverse else "ssd_fwd",
    )(*args)


def _softmax_pv_t(scores, values, pad_last, extra_logit=None, merge_blocks=False):
    mx = functools.reduce(jnp.maximum, [jnp.max(s, axis=0, keepdims=True) for s in scores])
    if extra_logit is not None:
        mx = jnp.maximum(mx, extra_logit)
    ps = [jnp.exp(s - mx) for s in scores]
    den = functools.reduce(jnp.add, [jnp.sum(p, axis=0, keepdims=True) for p in ps])
    if extra_logit is not None:
        den = den + jnp.exp(extra_logit - mx)
    pb = [p.astype(BF16) for p in ps]
    pb[-1] = jnp.concatenate([pad_last, pb[-1]], axis=0)
    if merge_blocks:
        return _dot(jnp.concatenate(values, axis=1), jnp.concatenate(pb, axis=0)), den
    return functools.reduce(jnp.add, [_dot(v, p) for v, p in zip(values, pb)]), den


def _block_diag_rows(pieces):
    zero = jnp.zeros_like(pieces[0][0])
    rows = []
    for i in range(len(pieces)):
        row = []
        for j, ps in enumerate(pieces):
            row += [p if i == j else zero for p in ps]
        rows.append(jnp.concatenate(row, axis=1))
    return jnp.concatenate(rows, axis=0)


def _win_kernel(qt_ref, kp_ref, kc_ref, kn_ref, km_ref, vp_ref, vc_ref, vn_ref, vm_ref, sink_ref, o_ref, *, nb):
    width = WIN_Q_HEADS * BLK
    ki = lax.broadcasted_iota(jnp.int32, (BLK, width), 0)
    qi = lax.broadcasted_iota(jnp.int32, (BLK, width), 1) & (BLK - 1)
    rep = WIN_Q_HEADS // WIN_KV_HEADS
    keys = [kp_ref[0]] + [kc_ref[0, j * BLK:(j + 1) * BLK, :] for j in range(STEP_BLKS)] + [kn_ref[0]]
    vals = [vp_ref[0, 0]] + [vc_ref[0, j] for j in range(STEP_BLKS)] + [vn_ref[0, 0]]
    kmeta, vmeta = km_ref[0, META0:, :], vm_ref[0, 0]
    pad_last = jnp.zeros((META0, width), BF16)
    for j in range(STEP_BLKS):
        n = pl.program_id(1) * STEP_BLKS + j
        ok_prev = jnp.logical_and(ki >= qi, n >= 2)
        ok_cur = n >= 1
        ok_next = jnp.logical_and(ki <= qi, n + 1 <= nb - 1)
        heads = [qt_ref[0, j, h * HEAD_DIM:(h + 1) * HEAD_DIM, :] for h in range(WIN_Q_HEADS)]
        qbd = _block_diag_rows([heads[g * rep:(g + 1) * rep] for g in range(WIN_KV_HEADS)])
        s = _dot(jnp.concatenate(keys[j:j + 3] + [kmeta], axis=0), qbd)
        scores = [jnp.where(ok_prev, s[:BLK], -jnp.inf),
                  jnp.where(ok_cur, s[BLK:2 * BLK], -jnp.inf),
                  jnp.where(ok_next, s[2 * BLK:3 * BLK], -jnp.inf),
                  s[3 * BLK:]]
        o, den = _softmax_pv_t(scores, vals[j:j + 3] + [vmeta], pad_last, extra_logit=sink_ref[...],
                               merge_blocks=True)
        o = o / den
        outs = [o[(h // rep) * HEAD_DIM:(h // rep + 1) * HEAD_DIM, h * BLK:(h + 1) * BLK] for h in range(WIN_Q_HEADS)]
        o_ref[0, j * BLK:(j + 1) * BLK, :] = jnp.concatenate(outs, axis=0).T.astype(BF16)


def _win_attention(qt, k, vt, sink):
    bsz, nb = qt.shape[:2]
    ns = nb // STEP_BLKS
    kvw = WIN_KV_HEADS * HEAD_DIM
    assert kvw == BLK
    kedge = lambda f: pl.BlockSpec((1, BLK, kvw), lambda b, i: (b, f(i), 0))
    vedge = lambda f: pl.BlockSpec((1, 1, kvw, BLK), lambda b, i: (b, f(i), 0, 0))
    before = lambda i: jnp.maximum(i * STEP_BLKS - 1, 0)
    after = lambda i: jnp.minimum((i + 1) * STEP_BLKS, nb - 1)
    first = lambda i: 0
    return pl.pallas_call(
        functools.partial(_win_kernel, nb=nb),
        grid=(bsz, ns),
        in_specs=[pl.BlockSpec((1, STEP_BLKS, WIN_Q_HEADS * HEAD_DIM, BLK), lambda b, i: (b, i, 0, 0)),
                  kedge(before), pl.BlockSpec((1, TM, kvw), lambda b, i: (b, i, 0)), kedge(after), kedge(first),
                  vedge(before), pl.BlockSpec((1, STEP_BLKS, kvw, BLK), lambda b, i: (b, i, 0, 0)), vedge(after),
                  vedge(first), pl.BlockSpec(sink.shape, lambda b, i: (0, 0))],
        out_specs=pl.BlockSpec((1, TM, WIN_Q_HEADS * HEAD_DIM), lambda b, i: (b, i, 0)),
        out_shape=jax.ShapeDtypeStruct((bsz, nb * BLK, WIN_Q_HEADS * HEAD_DIM), BF16),
        compiler_params=_cparams(("parallel", "parallel")),
        name="win_attn",
    )(qt, k, k, k, k, vt, vt, vt, vt, sink)


NA_QUAD = 4
NA_QW = NA_QUAD * HEAD_DIM
NA_WIN = NA_KR * GRID_W


def _na_bias_tables(rpb):
    c = np.arange(GRID_W)[None, :]
    kc = np.arange(GRID_W)[:, None]
    cs = np.clip(c - NA_KC // 2, 0, GRID_W - NA_KC)
    ok = (kc >= cs) & (kc < cs + NA_KC)
    pick = jnp.asarray((kc - c + NA_KC - 1)[None] == np.arange(2 * NA_KC - 1)[:, None, None], F32)
    t = jnp.einsum('hed,dkc->hekc', rpb.astype(F32), pick, precision=lax.Precision.HIGHEST)
    t = jnp.where(jnp.asarray(ok)[None, None], t, -jnp.inf)

    def pack(x):
        h, e = x.shape[:2]
        x = x.reshape(h // NA_QUAD, NA_QUAD, e, GRID_W, GRID_W)
        return x.transpose(0, 2, 3, 1, 4).reshape(h // NA_QUAD, e, GRID_W, NA_QW)

    meta = jnp.broadcast_to(t[:, NA_KR - 1:, :, 0:1], (t.shape[0], NA_KR, GRID_W, GRID_W))
    return pack(t), pack(meta)


def _na_kernel(qt_ref, k_ref, v_ref, bias_ref, mbias_ref, mb_ref, o_ref, *, rows):
    width = NA_HEADS * HEAD_DIM
    first_half = lax.broadcasted_iota(jnp.int32, (width, BLK), 1) < GRID_W
    ri = lax.broadcasted_iota(jnp.int32, (NA_QW, NA_QW), 0)
    ci = lax.broadcasted_iota(jnp.int32, (NA_QW, NA_QW), 1)
    diag = lax.shift_right_logical(ri, 6) == lax.shift_right_logical(ci, 6)
    head_of_lane = lax.shift_right_logical(lax.broadcasted_iota(jnp.int32, (GRID_W, NA_QW), 1), 6)
    pad_meta = jnp.zeros((META0, NA_QW), F32)

    def query_row(j, half, meta):
        qf = qt_ref[0, j].astype(F32)
        qsw = pltpu.roll(qf, GRID_W, 1)
        dup = jnp.where(first_half, qf, qsw) if half == 0 else jnp.where(first_half, qsw, qf)
        if meta:
            start = LEAD
        else:
            r = (pl.program_id(1) * STEP_BLKS + j - 1) * (BLK // GRID_W) + half
            rs = jnp.clip(r - NA_KR // 2, 0, rows - NA_KR)
            start = pl.multiple_of(LEAD + rs * GRID_W, GRID_W)
            e0 = rs - r + (NA_KR - 1)
        for u in range(NA_HEADS // NA_QUAD):
            cols = slice(u * NA_QW, (u + 1) * NA_QW)
            base = dup[u * NA_QW:(u + 1) * NA_QW]
            qbd = jnp.where(diag, jnp.concatenate([base, base], axis=1), 0.0).astype(BF16)
            keys = jnp.concatenate([k_ref[0, pl.ds(start, NA_WIN), cols], k_ref[0, META0:LEAD, cols]], axis=0)
            s = _dot(keys, qbd)
            if meta:
                bias = jnp.concatenate([mbias_ref[u, i] for i in range(NA_KR)], axis=0)
            else:
                bias = jnp.concatenate([bias_ref[u, e0 + i] for i in range(NA_KR)], axis=0)
            sw = s[:NA_WIN] + bias
            sm = s[NA_WIN:] + mb_ref[u]
            mx = jnp.maximum(jnp.max(sw, axis=0, keepdims=True), jnp.max(sm, axis=0, keepdims=True))
            pw, pm = jnp.exp(sw - mx), jnp.exp(sm - mx)
            rden = 1.0 / (jnp.sum(pw, axis=0, keepdims=True) + jnp.sum(pm, axis=0, keepdims=True))
            p_t = jnp.concatenate([pw * rden, pad_meta, pm * rden], axis=0).T.astype(BF16)
            vals = jnp.concatenate([v_ref[0, pl.ds(start, NA_WIN), cols], v_ref[0, 0:LEAD, cols]], axis=0)
            o = _dot(p_t, vals)
            out = o[(NA_QUAD - 1) * GRID_W:]
            for hq in range(NA_QUAD - 2, -1, -1):
                out = jnp.where(head_of_lane == hq, o[hq * GRID_W:(hq + 1) * GRID_W], out)
            o_ref[0, pl.ds(j * BLK + half * GRID_W, GRID_W), cols] = out.astype(BF16)

    @pl.when(pl.program_id(1) == 0)
    def _():
        o_ref[0, 0:GRID_W, :] = jnp.zeros((GRID_W, width), BF16)
        query_row(0, 1, True)

    @pl.when(pl.program_id(1) > 0)
    def _():
        query_row(0, 0, False)
        query_row(0, 1, False)

    for j in range(1, STEP_BLKS):
        query_row(j, 0, False)
        query_row(j, 1, False)


def _na_attention(qt, k, v, bias, mbias, mb):
    bsz, nb, width, _ = qt.shape
    lp = nb * BLK
    rows = (lp - LEAD) // GRID_W
    assert rows >= NA_KR and N_META <= GRID_W and LEAD - GRID_W <= META0
    qblk = pl.BlockSpec((1, STEP_BLKS, width, BLK), lambda b, n: (b, n, 0, 0))
    seq = pl.BlockSpec((1, lp, width), lambda b, n: (b, 0, 0), pipeline_mode=pl.Buffered(1))
    return pl.pallas_call(
        functools.partial(_na_kernel, rows=rows),
        grid=(bsz, nb // STEP_BLKS),
        in_specs=[qblk, seq, seq, _resident(bias), _resident(mbias), _resident(mb)],
        out_specs=pl.BlockSpec((1, TM, width), lambda b, n: (b, n, 0)),
        out_shape=jax.ShapeDtypeStruct((bsz, lp, width), BF16),
        compiler_params=_cparams(("parallel", "arbitrary")),
        name="na_attn",
    )(qt, k, v, bias, mbias, mb)


def _outproj_kernel(*refs, ntile):
    lead_ref = refs[0] if ntile > 1 else None
    tile_refs = refs[ntile > 1:ntile + (ntile > 1)]
    ys_ref, yw_ref, yn_ref, w1_ref, w2_ref, w3_ref, nw_ref, o_ref = refs[ntile + (ntile > 1):]
    i = pl.program_id(1)
    mix = _dot(ys_ref[0], w1_ref[...]) + _dot(yw_ref[0], w2_ref[...]) + _dot(yn_ref[0], w3_ref[...])
    out = _load_tile(tile_refs, lead_ref, i) + _rms(mix, nw_ref[...])
    o_ref[0] = jnp.where(_live_rows(o_ref.shape[1], i), out, 0.0)


def _outproj(src, y_ssd, y_win, y_na, w1, w2, w3, nw):
    bsz, lp, _ = y_ssd.shape
    tm = TM if src[0] == 'x' else _tile_rows(lp, OUTPROJ_TILE_CAP)
    tile = lambda n: pl.BlockSpec((1, tm, n), lambda b, i: (b, i, 0))
    full = _resident
    tile_specs, tile_args = _tile_specs(src, tm)
    return pl.pallas_call(
        functools.partial(_outproj_kernel, ntile=len(tile_specs) - (src[0] == 'x')),
        grid=(bsz, lp // tm),
        in_specs=tile_specs + [tile(y_ssd.shape[-1]), tile(y_win.shape[-1]), tile(y_na.shape[-1]),
                               full(w1), full(w2), full(w3), full(nw)],
        out_specs=tile(D_MODEL),
        out_shape=jax.ShapeDtypeStruct((bsz, lp, D_MODEL), F32),
        compiler_params=_cparams(("parallel", "parallel")),
        name="outproj",
    )(*tile_args, y_ssd, y_win, y_na, w1, w2, w3, nw)


def _ffn_kernel(*refs, nt, ntile, padded):
    tile_refs = refs[:ntile]
    hp_ref, hn_ref, nw1_ref, wup_ref, cw_ref, cb_ref, wdn_ref, nw2_ref, o_ref, act_ref = refs[ntile:]
    i = pl.program_id(1)
    rows = act_ref.shape[0]
    nw1 = nw1_ref[...]
    hc = _load_tile(tile_refs, None, i)
    f = jnp.concatenate([_rms(hp_ref[0], nw1), _rms(hc, nw1), _rms(hn_ref[0], nw1)], axis=0).astype(BF16)
    tail = jnp.where(i < nt - 1, 1.0, 0.0)
    nch = D_FF // FF_CHUNK

    def conv(g, c):
        g = jnp.concatenate([g[:HALO + rows], g[HALO + rows:] * tail], axis=0)
        out = _shift_rows(g, -1) * cw_ref[c, 0:1, :] + cb_ref[c]
        out = out + g[HALO:HALO + rows] * cw_ref[c, 1:2, :]
        return out + _shift_rows(g, 1) * cw_ref[c, 2:3, :]

    for c in range(nch):
        gate = conv(_dot(f, wup_ref[c]), c)
        up = conv(_dot(f, wup_ref[nch + c]), nch + c)
        th = jnp.tanh(gate * (GELU_K + (GELU_K * GELU_C) * (gate * gate)))
        act_ref[:, c * FF_CHUNK:(c + 1) * FF_CHUNK] = (gate * (0.5 * th + 0.5) * up).astype(BF16)
    out = hc + _rms(_dot(act_ref[...], wdn_ref[...]), nw2_ref[...])
    o_ref[0] = jnp.where(_live_rows(rows, i), out, 0.0) if padded else out


def _ffn(h, nw1, wup, cw, cb, wdn, nw2, *, last):
    bsz, lp, _ = h.shape
    full = _resident
    if last:
        n_out = lp - LEAD
        tm = _tile_rows(n_out, FFN_OUT_TILE_CAP, BLK)
        h4 = h.reshape(bsz, lp // BLK, BLK, D_MODEL)
        k = tm // BLK
        tile_specs = [pl.BlockSpec((1, 1, BLK, D_MODEL), lambda b, i, j=j: (b, LEAD // BLK + i * k + j, 0, 0))
                      for j in range(k)]
        tile_args = [h4] * k
        off = LEAD // HALO
    else:
        tm, n_out = TM, lp
        tile_specs, tile_args = _tile_specs(('h', h), tm)
        off = 0
    nt = n_out // tm
    hb = tm // HALO
    last_blk = lp // HALO - 1
    prev = pl.BlockSpec((1, HALO, D_MODEL), lambda b, i: (b, jnp.clip(off + i * hb - 1, 0, last_blk), 0))
    nxt = pl.BlockSpec((1, HALO, D_MODEL), lambda b, i: (b, jnp.clip(off + (i + 1) * hb, 0, last_blk), 0))
    return pl.pallas_call(
        functools.partial(_ffn_kernel, nt=nt, ntile=len(tile_specs), padded=not last),
        grid=(bsz, nt),
        in_specs=tile_specs + [prev, nxt, full(nw1), full(wup), full(cw), full(cb), full(wdn), full(nw2)],
        out_specs=pl.BlockSpec((1, tm, D_MODEL), lambda b, i: (b, i, 0)),
        out_shape=jax.ShapeDtypeStruct((bsz, n_out, D_MODEL), F32),
        scratch_shapes=[pltpu.VMEM((tm, D_FF), BF16)],
        compiler_params=_cparams(("parallel", "parallel")),
        name="ffn",
    )(*tile_args, h, h, nw1, wup, cw, cb, wdn, nw2)


def _pad_lanes(x, n):
    return jnp.pad(x, [(0, 0)] * (x.ndim - 1) + [(0, n - x.shape[-1])])


def _rope_tables(lp):
    half = ROPE_DIM // 2
    pos = jnp.maximum(jnp.arange(lp) - META0, 0).astype(F32)
    inv = jnp.power(ROPE_THETA, -jnp.arange(half, dtype=F32) / half)
    ang = pos[:, None] * inv[None, :]
    cos, sin = jnp.cos(ang), jnp.sin(ang)
    zeros, ones = jnp.zeros_like(cos), jnp.ones((lp, HEAD_DIM - ROPE_DIM), F32)
    rest = jnp.zeros((lp, HEAD_DIM - ROPE_DIM), F32)
    c = jnp.concatenate([cos, cos, ones], axis=1)
    s1 = jnp.concatenate([-sin, zeros, rest], axis=1)
    s2 = jnp.concatenate([zeros, sin, rest], axis=1)
    return tuple(jnp.tile(t, (1, 128 // HEAD_DIM)) for t in (c, s1, s2)) + (cos.T, sin.T)


def _layer_params(i, p):
    row = lambda v: v.reshape(1, -1).astype(F32)
    lanes = lambda v: jnp.broadcast_to(v.astype(F32)[:, None], (v.shape[0], BLK))
    sizes = [SSD_INNER, SSD_XBC, 2 * SSD_HEADS, WIN_Q_HEADS * HEAD_DIM, WIN_KV_HEADS * HEAD_DIM,
             WIN_KV_HEADS * HEAD_DIM, NA_HEADS * HEAD_DIM, NA_HEADS * HEAD_DIM, NA_HEADS * HEAD_DIM]
    w_in = p['w_in'][i].astype(BF16)
    ws = jnp.split(w_in, np.cumsum(sizes)[:-1].tolist(), axis=1)
    ws[2] = _pad_lanes(ws[2], 128)
    w_out = p['w_out'][i].astype(BF16)
    nch = D_FF // FF_CHUNK
    chunked = lambda m: m.reshape(m.shape[0], 2 * nch, FF_CHUNK).swapaxes(0, 1)
    return dict(
        norm_mix_pre=row(p['norm_mix_pre'][i]),
        w_in=ws,
        ssd_conv=(p['ssd_conv_w'][i].astype(F32), row(p['ssd_conv_b'][i])),
        ssd_consts=(_pad_lanes(row(p['ssd_dt_bias'][i]), 128), _pad_lanes(row(p['ssd_a_log'][i]), 128)),
        ssd_d=lanes(jnp.repeat(p['ssd_d'][i], HEAD_DIM)),
        ssd_norm_w=lanes(p['ssd_norm_w'][i]),
        win_sink=row(jnp.repeat(p['win_sink'][i], BLK)),
        na_bias=_na_bias_tables(p['na_rpb'][i]),
        na_meta_bias=jnp.broadcast_to(p['na_meta_bias'][i].astype(F32).T.reshape(N_META, NA_HEADS // NA_QUAD, NA_QUAD, 1),
                                      (N_META, NA_HEADS // NA_QUAD, NA_QUAD, GRID_W)).transpose(1, 0, 2, 3).reshape(
            NA_HEADS // NA_QUAD, N_META, NA_QW),
        w_out=(w_out[:SSD_INNER], w_out[SSD_INNER:SSD_INNER + WIN_Q_HEADS * HEAD_DIM],
               w_out[SSD_INNER + WIN_Q_HEADS * HEAD_DIM:]),
        norm_mix_post=row(p['norm_mix_post'][i]),
        norm_ffn_pre=row(p['norm_ffn_pre'][i]),
        ffn_w_up=chunked(p['ffn_w_up'][i].astype(BF16)),
        ffn_conv_w=chunked(p['ffn_conv_w'][i].astype(F32)),
        ffn_conv_b=chunked(p['ffn_conv_b'][i].astype(F32).reshape(1, -1)),
        ffn_w_down=p['ffn_w_down'][i].astype(BF16),
        norm_ffn_post=row(p['norm_ffn_post'][i]),
    )


def _encode(x, meta_tokens, layers):
    bsz, n_tok, _ = x.shape
    assert (n_tok + LEAD) % TM == 0 and n_tok // GRID_W >= NA_KR
    lp = LEAD + n_tok
    lead = jnp.concatenate([jnp.zeros((META0, D_MODEL), F32), meta_tokens.astype(F32)], axis=0)
    src = ('x', x.astype(F32), lead)
    rope = _rope_tables(lp)
    for li, lw in enumerate(layers):
        zt, xt, bc, acs, tab, wq, wk, wv, nq, nk, nv = _inproj(src, lp, lw['norm_mix_pre'], lw['w_in'], lw['ssd_conv'],
                                                               lw['ssd_consts'], rope)
        y_fwd = _ssd_pass(xt, bc, acs, tab, None, reverse=False)
        y_ssd = _ssd_pass(xt, bc, acs, tab, (zt, y_fwd, lw['ssd_d'], lw['ssd_norm_w']), reverse=True)
        y_win = _win_attention(wq, wk, wv, lw['win_sink'])
        y_na = _na_attention(nq, nk, nv, *lw['na_bias'], lw['na_meta_bias'])
        h = _outproj(src, y_ssd, y_win, y_na, *lw['w_out'], lw['norm_mix_post'])
        h = _ffn(h, lw['norm_ffn_pre'], lw['ffn_w_up'], lw['ffn_conv_w'], lw['ffn_conv_b'],
                 lw['ffn_w_down'], lw['norm_ffn_post'], last=li == len(layers) - 1)
        src = ('h', h)
    return h


def kernel(x_prompt, x_sample, meta_tokens, norm_mix_pre, norm_mix_post, w_in, ssd_conv_w, ssd_conv_b,
           ssd_dt_bias, ssd_a_log, ssd_d, ssd_norm_w, win_sink, na_rpb, na_meta_bias, w_out, norm_ffn_pre,
           norm_ffn_post, ffn_w_up, ffn_conv_w, ffn_conv_b, ffn_w_down):
    p = dict(norm_mix_pre=norm_mix_pre, norm_mix_post=norm_mix_post, w_in=w_in, ssd_conv_w=ssd_conv_w,
             ssd_conv_b=ssd_conv_b, ssd_dt_bias=ssd_dt_bias, ssd_a_log=ssd_a_log, ssd_d=ssd_d,
             ssd_norm_w=ssd_norm_w, win_sink=win_sink, na_rpb=na_rpb, na_meta_bias=na_meta_bias, w_out=w_out,
             norm_ffn_pre=norm_ffn_pre, norm_ffn_post=norm_ffn_post, ffn_w_up=ffn_w_up, ffn_conv_w=ffn_conv_w,
             ffn_conv_b=ffn_conv_b, ffn_w_down=ffn_w_down)
    layers = [_layer_params(i, p) for i in range(w_in.shape[0])]
    return (_encode(x_prompt, meta_tokens, layers), _encode(x_sample, meta_tokens, layers))
```

```python
import functools

import jax
import jax.numpy as jnp
import numpy as np
from jax import lax
from jax.experimental import pallas as pl
from jax.experimental.pallas import tpu as pltpu

F32 = jnp.float32
BF16 = jnp.bfloat16

D_MODEL = 1024
N_META = 16
GRID_W = 64
HEAD_DIM = 64

SSD_HEADS = 16
SSD_INNER = SSD_HEADS * HEAD_DIM
SSD_GROUPS = 2
SSD_STATE = 128
SSD_XBC = SSD_INNER + 2 * SSD_GROUPS * SSD_STATE
SSD_CONV = 5

WIN_Q_HEADS = 8
WIN_KV_HEADS = 2
WIN_RADIUS = 128
ROPE_THETA = 500000.0
ROPE_DIM = HEAD_DIM // 4

NA_HEADS = 8
NA_KR = 8
NA_KC = 16

D_FF = 2816
EPS = 1e-6
GELU_K = float(np.sqrt(2.0 / np.pi))
GELU_C = 0.044715

BLK = 128
LEAD = BLK
META0 = LEAD - N_META
HALO = 8
TM = 384
FFN_OUT_TILE_CAP = 720
FF_CHUNK = 256
STEP_BLKS = TM // BLK
VMEM_LIMIT = 56 * 1024 * 1024


def _tile_rows(n, cap, unit=HALO):
    return max(t for t in range(unit, cap + 1, unit) if n % t == 0)


def _resident(arr):
    return pl.BlockSpec(arr.shape, lambda *_: (0,) * arr.ndim, pipeline_mode=pl.Buffered(1))


def _cparams(sem):
    return pltpu.CompilerParams(dimension_semantics=sem, vmem_limit_bytes=VMEM_LIMIT)


def _rms(x, w):
    return x * lax.rsqrt(jnp.mean(x * x, axis=-1, keepdims=True) + EPS) * w


def _dot(a, b):
    return jnp.dot(a, b, preferred_element_type=F32)


def _shift_rows(x, d):
    return pltpu.roll(x, (-d) % x.shape[0], 0)[HALO:x.shape[0] - HALO]


def _load_tile(refs, lead_ref, i):
    if len(refs) == 1:
        return refs[0][0]
    blocks = [r[0, 0] for r in refs]
    if lead_ref is not None:
        blocks[0] = jnp.where(i == 0, lead_ref[...], blocks[0])
    return jnp.concatenate(blocks, axis=0)


def _tile_specs(src, tm, order=lambda i: i):
    if src[0] == 'h':
        return [pl.BlockSpec((1, tm, D_MODEL), lambda b, i: (b, order(i), 0))], [src[1]]
    x, lead = src[1], src[2]
    x4 = x.reshape(x.shape[0], x.shape[1] // BLK, BLK, D_MODEL)
    k = tm // BLK
    specs = [pl.BlockSpec((1, 1, BLK, D_MODEL),
                          lambda b, i, j=j: (b, jnp.maximum(order(i) * k + j - LEAD // BLK, 0), 0, 0))
             for j in range(k)]
    return [pl.BlockSpec(lead.shape, lambda b, i: (0, 0))] + specs, [lead] + [x4] * k


def _halo_specs(src, tm):
    arr = src[1]
    off = 0 if src[0] == 'h' else LEAD // HALO
    last = arr.shape[1] // HALO - 1
    hb = tm // HALO
    prev = pl.BlockSpec((1, HALO, D_MODEL), lambda b, i: (b, jnp.clip(i * hb - 1 - off, 0, last), 0))
    nxt = pl.BlockSpec((1, HALO, D_MODEL), lambda b, i: (b, jnp.clip((i + 1) * hb - off, 0, last), 0))
    return [prev, nxt], [arr, arr]


def _live_rows(tile_rows, tile_index):
    row = tile_index * tile_rows + lax.broadcasted_iota(jnp.int32, (tile_rows, 1), 0)
    return row >= META0


def _dot_nt(a, b):
    return lax.dot_general(a, b, (((1,), (1,)), ((), ())), preferred_element_type=F32)


def _rope128(x, c, s1, s2):
    return x * c + pltpu.roll(x, 128 - ROPE_DIM // 2, 1) * s1 + pltpu.roll(x, ROPE_DIM // 2, 1) * s2


def _scan_tables(dt_raw, dt_bias, a_log, live, acs_ref, tab_ref):
    nh = 2 * SSD_HEADS
    dtr = dt_raw + dt_bias
    dt = jnp.maximum(dtr, 0.0) + jnp.log1p(jnp.exp(-jnp.abs(dtr)))
    dt = jnp.where(live, dt, 0.0)
    a = dt * (-jnp.exp(a_log))
    ri = lax.broadcasted_iota(jnp.int32, (BLK, BLK), 0)
    ci = lax.broadcasted_iota(jnp.int32, (BLK, BLK), 1)
    tri = (ci <= ri).astype(BF16)
    fwd = ci < SSD_HEADS
    fwd_rows = ri < SSD_HEADS
    acs_chunks, tab_chunks = [], []
    for j in range(TM // BLK):
        rows = slice(j * BLK, (j + 1) * BLK)
        aj, dtj = a[rows], dt[rows]
        pre = sum(_dot(tri, part) for part in _split3(aj))
        total = pre[BLK - 1:BLK, :]
        acs = jnp.where(fwd, pre, total - pre + aj)
        acs_ref[0, rows, :] = acs
        acs_t = acs.T[:nh]
        chunk_sum = jnp.where(fwd_rows[:nh, 0:1], acs_t[:, BLK - 1:BLK], acs_t[:, 0:1])
        tabs = [acs_t, dtj.T[:nh],
                (dtj * jnp.exp(total - acs)).T[:nh],
                jnp.exp(acs_t), jnp.broadcast_to(jnp.exp(chunk_sum), (nh, BLK))]
        for k, t in enumerate(tabs):
            tab_ref[0, j, k] = t
        acs_chunks.append(acs)
        tab_chunks.append(tabs)
    return acs_chunks, tab_chunks


def _inproj_kernel(*refs, nt, ntile):
    lead_ref = refs[0] if ntile > 1 else None
    tile_refs = refs[ntile > 1:ntile + (ntile > 1)]
    (hp_ref, hn_ref, nw_ref, wz_ref, wx_ref, wdt_ref, wwq_ref, wwk_ref, wwv_ref,
     wnq_ref, wnk_ref, wnv_ref, cw_ref, cb_ref, dtb_ref, alog_ref, rc_ref, rs1_ref, rs2_ref, rct_ref, rst_ref,
     z_ref, xt_ref, bc_ref, acs_ref, tab_ref, wq_ref, wk_ref, wv_ref, nq_ref, nk_ref, nv_ref, yf_ref,
     state_ref) = refs[ntile + (ntile > 1):]
    i = pl.program_id(1)

    @pl.when(i == 0)
    def _():
        state_ref[...] = jnp.zeros_like(state_ref)

    nw = nw_ref[...]
    a32 = _rms(_load_tile(tile_refs, lead_ref, i), nw)
    a = a32.astype(BF16)
    head = jnp.where(i > 0, 1.0, 0.0)
    tail = jnp.where(i < nt - 1, 1.0, 0.0)
    a_ext = jnp.concatenate([_rms(hp_ref[0], nw) * head, a32, _rms(hn_ref[0], nw) * tail], axis=0).astype(BF16)
    xe = _dot(a_ext, wx_ref[...])

    def put_blocks(ref, xt):
        for j in range(TM // BLK):
            ref[0, j] = xt[:, j * BLK:(j + 1) * BLK].astype(BF16)

    put_blocks(z_ref, _dot(a, wz_ref[...]).T)
    dt_raw = _dot(a, wdt_ref[...])
    pad = SSD_CONV // 2
    acc = _shift_rows(xe, -pad) * cw_ref[0:1, :] + cb_ref[...]
    for j in range(1, SSD_CONV):
        acc = acc + _shift_rows(xe, j - pad) * cw_ref[j:j + 1, :]
    xc = acc * (1.0 / (1.0 + jnp.exp(-acc)))
    xt_all = xc[:, :SSD_INNER].T
    xt_chunks = [xt_all[:, j * BLK:(j + 1) * BLK].astype(BF16) for j in range(TM // BLK)]
    for j in range(TM // BLK):
        xt_ref[0, j] = xt_chunks[j]
    bc = xc[:, SSD_INNER:].astype(BF16)
    bc_ref[0] = bc
    acs_chunks, tab_chunks = _scan_tables(dt_raw, dtb_ref[...], alog_ref[...], _live_rows(TM, i),
                                          acs_ref, tab_ref)

    def put_y(j, rows, y):
        yf_ref[0, j, rows, :] = y

    def scan_chunk(j):
        _ssd_scan(lambda j, rows: xt_chunks[j][rows],
                  lambda j, k: bc[j * BLK:(j + 1) * BLK, k * SSD_STATE:(k + 1) * SSD_STATE],
                  lambda j: acs_chunks[j], lambda j, k: tab_chunks[j][k], state_ref, put_y, None,
                  reverse=False, chunks=[j])

    scan_chunk(0)
    scale = HEAD_DIM ** -0.5
    qt = _dot(a, wwq_ref[...]).T
    cos_t, sin_t = rct_ref[...], rst_ref[...]
    half = ROPE_DIM // 2
    parts = []
    for hh in range(WIN_Q_HEADS):
        x1 = qt[hh * HEAD_DIM:hh * HEAD_DIM + half]
        x2 = qt[hh * HEAD_DIM + half:hh * HEAD_DIM + ROPE_DIM]
        parts += [x1 * cos_t - x2 * sin_t, x2 * cos_t + x1 * sin_t, qt[hh * HEAD_DIM + ROPE_DIM:(hh + 1) * HEAD_DIM]]
    put_blocks(wq_ref, jnp.concatenate(parts, axis=0) * scale)
    wk_ref[0] = _rope128(_dot(a, wwk_ref[...]), rc_ref[...], rs1_ref[...], rs2_ref[...]).astype(BF16)
    put_blocks(wv_ref, _dot(a, wwv_ref[...]).T)
    scan_chunk(1)
    put_blocks(nq_ref, (_dot(a, wnq_ref[...]) * scale).T)
    nk_ref[0] = _dot(a, wnk_ref[...]).astype(BF16)
    scan_chunk(2)
    nv_ref[0] = _dot(a, wnv_ref[...]).astype(BF16)


def _inproj(src, lp, nw, ws, conv, scan, rope):
    bsz = src[1].shape[0]
    nt = lp // TM
    nb = lp // BLK
    tile = lambda n: pl.BlockSpec((1, TM, n), lambda b, i: (b, i, 0))
    tblk = lambda n: pl.BlockSpec((1, TM // BLK, n, BLK), lambda b, i: (b, i, 0, 0))
    tile_specs, tile_args = _tile_specs(src, TM)
    halo_specs, halo_args = _halo_specs(src, TM)
    full = lambda arr: pl.BlockSpec(arr.shape, lambda b, i: (0,) * arr.ndim)
    rtab = pl.BlockSpec((TM, 128), lambda b, i: (i, 0))
    rtab_t = pl.BlockSpec((ROPE_DIM // 2, TM), lambda b, i: (0, i))
    tok = lambda n, d: (tile(n), jax.ShapeDtypeStruct((bsz, lp, n), d))
    blk = lambda n: (tblk(n), jax.ShapeDtypeStruct((bsz, nb, n, BLK), BF16))
    kvw = WIN_KV_HEADS * HEAD_DIM
    tabs = (pl.BlockSpec((1, TM // BLK, 5, 2 * SSD_HEADS, BLK), lambda b, i: (b, i, 0, 0, 0)),
            jax.ShapeDtypeStruct((bsz, nb, 5, 2 * SSD_HEADS, BLK), F32))
    outs = [blk(SSD_INNER), blk(SSD_INNER), tok(SSD_XBC - SSD_INNER, BF16), tok(128, F32), tabs, blk(WIN_Q_HEADS * HEAD_DIM), tok(kvw, BF16),
            blk(kvw), blk(NA_HEADS * HEAD_DIM), tok(NA_HEADS * HEAD_DIM, BF16), tok(NA_HEADS * HEAD_DIM, BF16),
            (tblk(SSD_INNER), jax.ShapeDtypeStruct((bsz, nb, SSD_INNER, BLK), F32))]
    return pl.pallas_call(
        functools.partial(_inproj_kernel, nt=nt, ntile=len(tile_specs) - (src[0] == 'x')),
        grid=(bsz, nt),
        in_specs=tile_specs + halo_specs + [full(nw)] + [full(w) for w in ws] + [full(c) for c in conv + scan]
        + [rtab, rtab, rtab, rtab_t, rtab_t],
        out_specs=[o[0] for o in outs],
        out_shape=[o[1] for o in outs],
        scratch_shapes=[pltpu.VMEM((SSD_GROUPS, SSD_INNER // SSD_GROUPS, SSD_STATE), F32)],
        compiler_params=_cparams(("parallel", "arbitrary")),
        name="inproj",
    )(*tile_args, *halo_args, nw, *ws, *conv, *scan, *rope)


def _split3(x):
    hi = x.astype(BF16)
    r1 = x - hi.astype(F32)
    mid = r1.astype(BF16)
    lo = (r1 - mid.astype(F32)).astype(BF16)
    return hi, mid, lo


def _ssd_scan(get_x, get_bc, get_acs, get_tab, state_ref, put_y, after_chunk, *, reverse, chunks=None):
    ri = lax.broadcasted_iota(jnp.int32, (BLK, BLK), 0)
    ci = lax.broadcasted_iota(jnp.int32, (BLK, BLK), 1)
    feeds = (ri >= ci) if reverse else (ri <= ci)
    hoff = SSD_HEADS if reverse else 0
    rep = SSD_HEADS // SSD_GROUPS
    if chunks is None:
        chunks = reversed(range(STEP_BLKS)) if reverse else range(STEP_BLKS)
    for j in chunks:
        acs = get_acs(j)
        acs_t, dt_t, dt_out_t, e_in_t, e_chunk = (get_tab(j, k) for k in range(5))
        for g in range(SSD_GROUPS):
            bm = get_bc(j, g)
            cm = get_bc(j, SSD_GROUPS + g)
            cbt = _dot_nt(bm, cm)
            ht = state_ref[g]
            y_off = _dot_nt(ht.astype(BF16), cm)
            x_out, decay = [], []
            for r in range(rep):
                h = g * rep + r
                hl = hoff + h
                rows = slice(h * HEAD_DIM, (h + 1) * HEAD_DIM)
                xt = get_x(j, rows).astype(F32)
                x_in = (xt * dt_t[hl:hl + 1, :]).astype(BF16)
                x_out.append((xt * dt_out_t[hl:hl + 1, :]).astype(BF16))
                decay.append(jnp.broadcast_to(e_chunk[hl:hl + 1, :], (HEAD_DIM, SSD_STATE)))
                diff = acs_t[hl:hl + 1, :] - acs[:, hl:hl + 1]
                lt = (cbt * jnp.exp(jnp.where(feeds, diff, -jnp.inf))).astype(BF16)
                put_y(j, rows, _dot(x_in, lt) + y_off[r * HEAD_DIM:(r + 1) * HEAD_DIM] * e_in_t[hl:hl + 1, :])
            s_new = _dot(jnp.concatenate(x_out, axis=0), bm)
            state_ref[g] = ht * jnp.concatenate(decay, axis=0) + s_new
        if after_chunk is not None:
            after_chunk(j)


def _softmax_pv_t(scores, values, pad_last, extra_logit=None, merge_blocks=False):
    mx = functools.reduce(jnp.maximum, [jnp.max(s, axis=0, keepdims=True) for s in scores])
    if extra_logit is not None:
        mx = jnp.maximum(mx, extra_logit)
    ps = [jnp.exp(s - mx) for s in scores]
    den = functools.reduce(jnp.add, [jnp.sum(p, axis=0, keepdims=True) for p in ps])
    if extra_logit is not None:
        den = den + jnp.exp(extra_logit - mx)
    pb = [p.astype(BF16) for p in ps]
    pb[-1] = jnp.concatenate([pad_last, pb[-1]], axis=0)
    if merge_blocks:
        return _dot(jnp.concatenate(values, axis=1), jnp.concatenate(pb, axis=0)), den
    return functools.reduce(jnp.add, [_dot(v, p) for v, p in zip(values, pb)]), den


def _block_diag_rows(pieces):
    zero = jnp.zeros_like(pieces[0][0])
    rows = []
    for i in range(len(pieces)):
        row = []
        for j, ps in enumerate(pieces):
            row += [p if i == j else zero for p in ps]
        rows.append(jnp.concatenate(row, axis=1))
    return jnp.concatenate(rows, axis=0)


def _win_kernel(qt_ref, kp_ref, kc_ref, kn_ref, km_ref, vp_ref, vc_ref, vn_ref, vm_ref, sink_ref, o_ref, *, nb):
    width = WIN_Q_HEADS * BLK
    ki = lax.broadcasted_iota(jnp.int32, (BLK, width), 0)
    qi = lax.broadcasted_iota(jnp.int32, (BLK, width), 1) & (BLK - 1)
    rep = WIN_Q_HEADS // WIN_KV_HEADS
    keys = [kp_ref[0]] + [kc_ref[0, j * BLK:(j + 1) * BLK, :] for j in range(STEP_BLKS)] + [kn_ref[0]]
    vals = [vp_ref[0, 0]] + [vc_ref[0, j] for j in range(STEP_BLKS)] + [vn_ref[0, 0]]
    kmeta, vmeta = km_ref[0, META0:, :], vm_ref[0, 0]
    pad_last = jnp.zeros((META0, width), BF16)
    for j in range(STEP_BLKS):
        n = pl.program_id(1) * STEP_BLKS + j
        ok_prev = jnp.logical_and(ki >= qi, n >= 2)
        ok_cur = n >= 1
        ok_next = jnp.logical_and(ki <= qi, n + 1 <= nb - 1)
        heads = [qt_ref[0, j, h * HEAD_DIM:(h + 1) * HEAD_DIM, :] for h in range(WIN_Q_HEADS)]
        qbd = _block_diag_rows([heads[g * rep:(g + 1) * rep] for g in range(WIN_KV_HEADS)])
        s = _dot(jnp.concatenate(keys[j:j + 3] + [kmeta], axis=0), qbd)
        scores = [jnp.where(ok_prev, s[:BLK], -jnp.inf),
                  jnp.where(ok_cur, s[BLK:2 * BLK], -jnp.inf),
                  jnp.where(ok_next, s[2 * BLK:3 * BLK], -jnp.inf),
                  s[3 * BLK:]]
        o, den = _softmax_pv_t(scores, vals[j:j + 3] + [vmeta], pad_last, extra_logit=sink_ref[...],
                               merge_blocks=True)
        o = o / den
        outs = [o[(h // rep) * HEAD_DIM:(h // rep + 1) * HEAD_DIM, h * BLK:(h + 1) * BLK] for h in range(WIN_Q_HEADS)]
        o_ref[0, j * BLK:(j + 1) * BLK, :] = jnp.concatenate(outs, axis=0).T.astype(BF16)


def _win_attention(qt, k, vt, sink):
    bsz, nb = qt.shape[:2]
    ns = nb // STEP_BLKS
    kvw = WIN_KV_HEADS * HEAD_DIM
    assert kvw == BLK
    kedge = lambda f: pl.BlockSpec((1, BLK, kvw), lambda b, i: (b, f(i), 0))
    vedge = lambda f: pl.BlockSpec((1, 1, kvw, BLK), lambda b, i: (b, f(i), 0, 0))
    before = lambda i: jnp.maximum(i * STEP_BLKS - 1, 0)
    after = lambda i: jnp.minimum((i + 1) * STEP_BLKS, nb - 1)
    first = lambda i: 0
    return pl.pallas_call(
        functools.partial(_win_kernel, nb=nb),
        grid=(bsz, ns),
        in_specs=[pl.BlockSpec((1, STEP_BLKS, WIN_Q_HEADS * HEAD_DIM, BLK), lambda b, i: (b, i, 0, 0)),
                  kedge(before), pl.BlockSpec((1, TM, kvw), lambda b, i: (b, i, 0)), kedge(after), kedge(first),
                  vedge(before), pl.BlockSpec((1, STEP_BLKS, kvw, BLK), lambda b, i: (b, i, 0, 0)), vedge(after),
                  vedge(first), pl.BlockSpec(sink.shape, lambda b, i: (0, 0))],
        out_specs=pl.BlockSpec((1, TM, WIN_Q_HEADS * HEAD_DIM), lambda b, i: (b, i, 0)),
        out_shape=jax.ShapeDtypeStruct((bsz, nb * BLK, WIN_Q_HEADS * HEAD_DIM), BF16),
        compiler_params=_cparams(("parallel", "parallel")),
        name="win_attn",
    )(qt, k, k, k, k, vt, vt, vt, vt, sink)


NA_QUAD = 4
NA_QW = NA_QUAD * HEAD_DIM
NA_WIN = NA_KR * GRID_W


def _na_bias_tables(rpb):
    c = np.arange(GRID_W)[None, :]
    kc = np.arange(GRID_W)[:, None]
    cs = np.clip(c - NA_KC // 2, 0, GRID_W - NA_KC)
    ok = (kc >= cs) & (kc < cs + NA_KC)
    pick = jnp.asarray((kc - c + NA_KC - 1)[None] == np.arange(2 * NA_KC - 1)[:, None, None], F32)
    t = jnp.einsum('hed,dkc->hekc', rpb.astype(F32), pick, precision=lax.Precision.HIGHEST)
    t = jnp.where(jnp.asarray(ok)[None, None], t, -jnp.inf)

    def pack(x):
        h, e = x.shape[:2]
        x = x.reshape(h // NA_QUAD, NA_QUAD, e, GRID_W, GRID_W)
        return x.transpose(0, 2, 3, 1, 4).reshape(h // NA_QUAD, e, GRID_W, NA_QW)

    meta = jnp.broadcast_to(t[:, NA_KR - 1:, :, 0:1], (t.shape[0], NA_KR, GRID_W, GRID_W))
    return pack(t), pack(meta)


def _na_kernel(qt_ref, k_ref, v_ref, bias_ref, mbias_ref, mb_ref, o_ref, *, rows):
    width = NA_HEADS * HEAD_DIM
    first_half = lax.broadcasted_iota(jnp.int32, (width, BLK), 1) < GRID_W
    ri = lax.broadcasted_iota(jnp.int32, (NA_QW, NA_QW), 0)
    ci = lax.broadcasted_iota(jnp.int32, (NA_QW, NA_QW), 1)
    diag = lax.shift_right_logical(ri, 6) == lax.shift_right_logical(ci, 6)
    head_of_lane = lax.shift_right_logical(lax.broadcasted_iota(jnp.int32, (GRID_W, NA_QW), 1), 6)
    pad_meta = jnp.zeros((META0, NA_QW), F32)

    def query_row(j, half, meta):
        qf = qt_ref[0, j].astype(F32)
        qsw = pltpu.roll(qf, GRID_W, 1)
        dup = jnp.where(first_half, qf, qsw) if half == 0 else jnp.where(first_half, qsw, qf)
        if meta:
            start = LEAD
        else:
            r = (pl.program_id(1) * STEP_BLKS + j - 1) * (BLK // GRID_W) + half
            rs = jnp.clip(r - NA_KR // 2, 0, rows - NA_KR)
            start = pl.multiple_of(LEAD + rs * GRID_W, GRID_W)
            e0 = rs - r + (NA_KR - 1)
        for u in range(NA_HEADS // NA_QUAD):
            cols = slice(u * NA_QW, (u + 1) * NA_QW)
            base = dup[u * NA_QW:(u + 1) * NA_QW]
            qbd = jnp.where(diag, jnp.concatenate([base, base], axis=1), 0.0).astype(BF16)
            keys = jnp.concatenate([k_ref[0, pl.ds(start, NA_WIN), cols], k_ref[0, META0:LEAD, cols]], axis=0)
            s = _dot(keys, qbd)
            if meta:
                bias = jnp.concatenate([mbias_ref[u, i] for i in range(NA_KR)], axis=0)
            else:
                bias = jnp.concatenate([bias_ref[u, e0 + i] for i in range(NA_KR)], axis=0)
            sw = s[:NA_WIN] + bias
            sm = s[NA_WIN:] + mb_ref[u]
            mx = jnp.maximum(jnp.max(sw, axis=0, keepdims=True), jnp.max(sm, axis=0, keepdims=True))
            pw, pm = jnp.exp(sw - mx), jnp.exp(sm - mx)
            rden = 1.0 / (jnp.sum(pw, axis=0, keepdims=True) + jnp.sum(pm, axis=0, keepdims=True))
            p_t = jnp.concatenate([pw * rden, pad_meta, pm * rden], axis=0).T.astype(BF16)
            vals = jnp.concatenate([v_ref[0, pl.ds(start, NA_WIN), cols], v_ref[0, 0:LEAD, cols]], axis=0)
            o = _dot(p_t, vals)
            out = o[(NA_QUAD - 1) * GRID_W:]
            for hq in range(NA_QUAD - 2, -1, -1):
                out = jnp.where(head_of_lane == hq, o[hq * GRID_W:(hq + 1) * GRID_W], out)
            o_ref[0, pl.ds(j * BLK + half * GRID_W, GRID_W), cols] = out.astype(BF16)

    @pl.when(pl.program_id(1) == 0)
    def _():
        o_ref[0, 0:GRID_W, :] = jnp.zeros((GRID_W, width), BF16)
        query_row(0, 1, True)

    @pl.when(pl.program_id(1) > 0)
    def _():
        query_row(0, 0, False)
        query_row(0, 1, False)

    for j in range(1, STEP_BLKS):
        query_row(j, 0, False)
        query_row(j, 1, False)


def _na_attention(qt, k, v, bias, mbias, mb):
    bsz, nb, width, _ = qt.shape
    lp = nb * BLK
    rows = (lp - LEAD) // GRID_W
    assert rows >= NA_KR and N_META <= GRID_W and LEAD - GRID_W <= META0
    qblk = pl.BlockSpec((1, STEP_BLKS, width, BLK), lambda b, n: (b, n, 0, 0))
    seq = pl.BlockSpec((1, lp, width), lambda b, n: (b, 0, 0), pipeline_mode=pl.Buffered(1))
    return pl.pallas_call(
        functools.partial(_na_kernel, rows=rows),
        grid=(bsz, nb // STEP_BLKS),
        in_specs=[qblk, seq, seq, _resident(bias), _resident(mbias), _resident(mb)],
        out_specs=pl.BlockSpec((1, TM, width), lambda b, n: (b, n, 0)),
        out_shape=jax.ShapeDtypeStruct((bsz, lp, width), BF16),
        compiler_params=_cparams(("parallel", "arbitrary")),
        name="na_attn",
    )(qt, k, v, bias, mbias, mb)


def _mixout_kernel(*refs, ns, ntile):
    lead_ref = refs[0] if ntile > 1 else None
    tile_refs = refs[ntile > 1:ntile + (ntile > 1)]
    (xt_ref, bc_ref, acs_ref, tab_ref, zt_ref, yf_ref, yw_ref, yn_ref, dsk_ref, nws_ref, w1_ref, w2_ref, w3_ref,
     nw_ref, o_ref, state_ref, ybuf_ref, ys_ref) = refs[ntile + (ntile > 1):]
    i = ns - 1 - pl.program_id(1)

    @pl.when(pl.program_id(1) == 0)
    def _():
        state_ref[...] = jnp.zeros_like(state_ref)

    def put_y(j, rows, y):
        ybuf_ref[rows, :] = y

    def finish(j):
        y = ybuf_ref[...] + yf_ref[0, j] + dsk_ref[...] * xt_ref[0, j].astype(F32)
        z = zt_ref[0, j].astype(F32)
        y = y * (z * (1.0 / (1.0 + jnp.exp(-z))))
        y = y * lax.rsqrt(jnp.mean(y * y, axis=0, keepdims=True) + EPS) * nws_ref[...]
        ys_ref[j * BLK:(j + 1) * BLK, :] = y.T.astype(BF16)

    def scan_chunk(j):
        _ssd_scan(lambda j, rows: xt_ref[0, j, rows, :],
                  lambda j, k: bc_ref[0, j * BLK:(j + 1) * BLK, k * SSD_STATE:(k + 1) * SSD_STATE],
                  lambda j: acs_ref[0, j * BLK:(j + 1) * BLK, :],
                  lambda j, k: tab_ref[0, j, k], state_ref, put_y, finish, reverse=True, chunks=[j])

    scan_chunk(STEP_BLKS - 1)
    mix = _dot(yw_ref[0], w2_ref[...])
    for j in reversed(range(STEP_BLKS - 1)):
        scan_chunk(j)
        if j == STEP_BLKS - 2:
            mix = mix + _dot(yn_ref[0], w3_ref[...])
    mix = mix + _dot(ys_ref[...], w1_ref[...])
    out = _load_tile(tile_refs, lead_ref, i) + _rms(mix, nw_ref[...])
    o_ref[0] = jnp.where(_live_rows(TM, i), out, 0.0)


def _mixout(src, xt, bc, acs, tab, zt, y_fwd, y_win, y_na, dsk, nws, w1, w2, w3, nw):
    bsz, nc = xt.shape[:2]
    ns = nc // STEP_BLKS
    order = lambda c: ns - 1 - c
    tok = lambda n: pl.BlockSpec((1, TM, n), lambda b, c: (b, order(c), 0))
    feat = pl.BlockSpec((1, STEP_BLKS, SSD_INNER, BLK), lambda b, c: (b, order(c), 0, 0))
    tabs = pl.BlockSpec((1, STEP_BLKS) + tab.shape[2:], lambda b, c: (b, order(c), 0, 0, 0))
    tile_specs, tile_args = _tile_specs(src, TM, order)
    consts = [dsk, nws, w1, w2, w3, nw]
    return pl.pallas_call(
        functools.partial(_mixout_kernel, ns=ns, ntile=len(tile_specs) - (src[0] == 'x')),
        grid=(bsz, ns),
        in_specs=tile_specs + [feat, tok(bc.shape[-1]), tok(128), tabs, feat, feat, tok(y_win.shape[-1]),
                               tok(y_na.shape[-1])] + [_resident(c) for c in consts],
        out_specs=tok(D_MODEL),
        out_shape=jax.ShapeDtypeStruct((bsz, nc * BLK, D_MODEL), F32),
        scratch_shapes=[pltpu.VMEM((SSD_GROUPS, SSD_INNER // SSD_GROUPS, SSD_STATE), F32),
                        pltpu.VMEM((SSD_INNER, BLK), F32), pltpu.VMEM((TM, SSD_INNER), BF16)],
        compiler_params=_cparams(("parallel", "arbitrary")),
        name="mixout",
    )(*tile_args, xt, bc, acs, tab, zt, y_fwd, y_win, y_na, *consts)


def _ffn_kernel(*refs, nt, ntile, padded):
    tile_refs = refs[:ntile]
    hp_ref, hn_ref, nw1_ref, wup_ref, cw_ref, cb_ref, wdn_ref, nw2_ref, o_ref, act_ref = refs[ntile:]
    i = pl.program_id(1)
    rows = act_ref.shape[0]
    nw1 = nw1_ref[...]
    hc = _load_tile(tile_refs, None, i)
    f = jnp.concatenate([_rms(hp_ref[0], nw1), _rms(hc, nw1), _rms(hn_ref[0], nw1)], axis=0).astype(BF16)
    tail = jnp.where(i < nt - 1, 1.0, 0.0)
    nch = D_FF // FF_CHUNK

    def conv(g, c):
        g = jnp.concatenate([g[:HALO + rows], g[HALO + rows:] * tail], axis=0)
        out = _shift_rows(g, -1) * cw_ref[c, 0:1, :] + cb_ref[c]
        out = out + g[HALO:HALO + rows] * cw_ref[c, 1:2, :]
        return out + _shift_rows(g, 1) * cw_ref[c, 2:3, :]

    for c in range(nch):
        gate = conv(_dot(f, wup_ref[c]), c)
        up = conv(_dot(f, wup_ref[nch + c]), nch + c)
        th = jnp.tanh(gate * (GELU_K + (GELU_K * GELU_C) * (gate * gate)))
        act_ref[:, c * FF_CHUNK:(c + 1) * FF_CHUNK] = (gate * (0.5 * th + 0.5) * up).astype(BF16)
    out = hc + _rms(_dot(act_ref[...], wdn_ref[...]), nw2_ref[...])
    o_ref[0] = jnp.where(_live_rows(rows, i), out, 0.0) if padded else out


def _ffn(h, nw1, wup, cw, cb, wdn, nw2, *, last):
    bsz, lp, _ = h.shape
    full = _resident
    if last:
        n_out = lp - LEAD
        tm = _tile_rows(n_out, FFN_OUT_TILE_CAP, BLK)
        h4 = h.reshape(bsz, lp // BLK, BLK, D_MODEL)
        k = tm // BLK
        tile_specs = [pl.BlockSpec((1, 1, BLK, D_MODEL), lambda b, i, j=j: (b, LEAD // BLK + i * k + j, 0, 0))
                      for j in range(k)]
        tile_args = [h4] * k
        off = LEAD // HALO
    else:
        tm, n_out = TM, lp
        tile_specs, tile_args = _tile_specs(('h', h), tm)
        off = 0
    nt = n_out // tm
    hb = tm // HALO
    last_blk = lp // HALO - 1
    prev = pl.BlockSpec((1, HALO, D_MODEL), lambda b, i: (b, jnp.clip(off + i * hb - 1, 0, last_blk), 0))
    nxt = pl.BlockSpec((1, HALO, D_MODEL), lambda b, i: (b, jnp.clip(off + (i + 1) * hb, 0, last_blk), 0))
    return pl.pallas_call(
        functools.partial(_ffn_kernel, nt=nt, ntile=len(tile_specs), padded=not last),
        grid=(bsz, nt),
        in_specs=tile_specs + [prev, nxt, full(nw1), full(wup), full(cw), full(cb), full(wdn), full(nw2)],
        out_specs=pl.BlockSpec((1, tm, D_MODEL), lambda b, i: (b, i, 0)),
        out_shape=jax.ShapeDtypeStruct((bsz, n_out, D_MODEL), F32),
        scratch_shapes=[pltpu.VMEM((tm, D_FF), BF16)],
        compiler_params=_cparams(("parallel", "parallel")),
        name="ffn",
    )(*tile_args, h, h, nw1, wup, cw, cb, wdn, nw2)


def _pad_lanes(x, n):
    return jnp.pad(x, [(0, 0)] * (x.ndim - 1) + [(0, n - x.shape[-1])])


def _rope_tables(lp):
    half = ROPE_DIM // 2
    pos = jnp.maximum(jnp.arange(lp) - META0, 0).astype(F32)
    inv = jnp.power(ROPE_THETA, -jnp.arange(half, dtype=F32) / half)
    ang = pos[:, None] * inv[None, :]
    cos, sin = jnp.cos(ang), jnp.sin(ang)
    zeros, ones = jnp.zeros_like(cos), jnp.ones((lp, HEAD_DIM - ROPE_DIM), F32)
    rest = jnp.zeros((lp, HEAD_DIM - ROPE_DIM), F32)
    c = jnp.concatenate([cos, cos, ones], axis=1)
    s1 = jnp.concatenate([-sin, zeros, rest], axis=1)
    s2 = jnp.concatenate([zeros, sin, rest], axis=1)
    return tuple(jnp.tile(t, (1, 128 // HEAD_DIM)) for t in (c, s1, s2)) + (cos.T, sin.T)


def _layer_params(i, p):
    row = lambda v: v.reshape(1, -1).astype(F32)
    lanes = lambda v: jnp.broadcast_to(v.astype(F32)[:, None], (v.shape[0], BLK))
    sizes = [SSD_INNER, SSD_XBC, 2 * SSD_HEADS, WIN_Q_HEADS * HEAD_DIM, WIN_KV_HEADS * HEAD_DIM,
             WIN_KV_HEADS * HEAD_DIM, NA_HEADS * HEAD_DIM, NA_HEADS * HEAD_DIM, NA_HEADS * HEAD_DIM]
    w_in = p['w_in'][i].astype(BF16)
    ws = jnp.split(w_in, np.cumsum(sizes)[:-1].tolist(), axis=1)
    ws[2] = _pad_lanes(ws[2], 128)
    w_out = p['w_out'][i].astype(BF16)
    nch = D_FF // FF_CHUNK
    chunked = lambda m: m.reshape(m.shape[0], 2 * nch, FF_CHUNK).swapaxes(0, 1)
    return dict(
        norm_mix_pre=row(p['norm_mix_pre'][i]),
        w_in=ws,
        ssd_conv=(p['ssd_conv_w'][i].astype(F32), row(p['ssd_conv_b'][i])),
        ssd_consts=(_pad_lanes(row(p['ssd_dt_bias'][i]), 128), _pad_lanes(row(p['ssd_a_log'][i]), 128)),
        ssd_d=lanes(jnp.repeat(p['ssd_d'][i], HEAD_DIM)),
        ssd_norm_w=lanes(p['ssd_norm_w'][i]),
        win_sink=row(jnp.repeat(p['win_sink'][i], BLK)),
        na_bias=_na_bias_tables(p['na_rpb'][i]),
        na_meta_bias=jnp.broadcast_to(p['na_meta_bias'][i].astype(F32).T.reshape(N_META, NA_HEADS // NA_QUAD, NA_QUAD, 1),
                                      (N_META, NA_HEADS // NA_QUAD, NA_QUAD, GRID_W)).transpose(1, 0, 2, 3).reshape(
            NA_HEADS // NA_QUAD, N_META, NA_QW),
        w_out=(w_out[:SSD_INNER], w_out[SSD_INNER:SSD_INNER + WIN_Q_HEADS * HEAD_DIM],
               w_out[SSD_INNER + WIN_Q_HEADS * HEAD_DIM:]),
        norm_mix_post=row(p['norm_mix_post'][i]),
        norm_ffn_pre=row(p['norm_ffn_pre'][i]),
        ffn_w_up=chunked(p['ffn_w_up'][i].astype(BF16)),
        ffn_conv_w=chunked(p['ffn_conv_w'][i].astype(F32)),
        ffn_conv_b=chunked(p['ffn_conv_b'][i].astype(F32).reshape(1, -1)),
        ffn_w_down=p['ffn_w_down'][i].astype(BF16),
        norm_ffn_post=row(p['norm_ffn_post'][i]),
    )


def _encode(x, meta_tokens, layers):
    bsz, n_tok, _ = x.shape
    assert (n_tok + LEAD) % TM == 0 and n_tok // GRID_W >= NA_KR
    lp = LEAD + n_tok
    lead = jnp.concatenate([jnp.zeros((META0, D_MODEL), F32), meta_tokens.astype(F32)], axis=0)
    src = ('x', x.astype(F32), lead)
    rope = _rope_tables(lp)
    for li, lw in enumerate(layers):
        zt, xt, bc, acs, tab, wq, wk, wv, nq, nk, nv, y_fwd = _inproj(src, lp, lw['norm_mix_pre'], lw['w_in'],
                                                                      lw['ssd_conv'], lw['ssd_consts'], rope)
        y_win = _win_attention(wq, wk, wv, lw['win_sink'])
        y_na = _na_attention(nq, nk, nv, *lw['na_bias'], lw['na_meta_bias'])
        h = _mixout(src, xt, bc, acs, tab, zt, y_fwd, y_win, y_na, lw['ssd_d'], lw['ssd_norm_w'], *lw['w_out'],
                    lw['norm_mix_post'])
        h = _ffn(h, lw['norm_ffn_pre'], lw['ffn_w_up'], lw['ffn_conv_w'], lw['ffn_conv_b'],
                 lw['ffn_w_down'], lw['norm_ffn_post'], last=li == len(layers) - 1)
        src = ('h', h)
    return h


def kernel(x_prompt, x_sample, meta_tokens, norm_mix_pre, norm_mix_post, w_in, ssd_conv_w, ssd_conv_b,
           ssd_dt_bias, ssd_a_log, ssd_d, ssd_norm_w, win_sink, na_rpb, na_meta_bias, w_out, norm_ffn_pre,
           norm_ffn_post, ffn_w_up, ffn_conv_w, ffn_conv_b, ffn_w_down):
    p = dict(norm_mix_pre=norm_mix_pre, norm_mix_post=norm_mix_post, w_in=w_in, ssd_conv_w=ssd_conv_w,
             ssd_conv_b=ssd_conv_b, ssd_dt_bias=ssd_dt_bias, ssd_a_log=ssd_a_log, ssd_d=ssd_d,
             ssd_norm_w=ssd_norm_w, win_sink=win_sink, na_rpb=na_rpb, na_meta_bias=na_meta_bias, w_out=w_out,
             norm_ffn_pre=norm_ffn_pre, norm_ffn_post=norm_ffn_post, ffn_w_up=ffn_w_up, ffn_conv_w=ffn_conv_w,
             ffn_conv_b=ffn_conv_b, ffn_w_down=ffn_w_down)
    layers = [_layer_params(i, p) for i in range(w_in.shape[0])]
    return (_encode(x_prompt, meta_tokens, layers), _encode(x_sample, meta_tokens, layers))
```

```python
import functools

import jax
import jax.numpy as jnp
import numpy as np
from jax import lax
from jax.experimental import pallas as pl
from jax.experimental.pallas import tpu as pltpu

F32 = jnp.float32
BF16 = jnp.bfloat16

D_MODEL = 1024
N_META = 16
GRID_W = 64
HEAD_DIM = 64

SSD_HEADS = 16
SSD_INNER = SSD_HEADS * HEAD_DIM
SSD_GROUPS = 2
SSD_STATE = 128
SSD_XBC = SSD_INNER + 2 * SSD_GROUPS * SSD_STATE
SSD_CONV = 5

WIN_Q_HEADS = 8
WIN_KV_HEADS = 2
WIN_RADIUS = 128
ROPE_THETA = 500000.0
ROPE_DIM = HEAD_DIM // 4

NA_HEADS = 8
NA_KR = 8
NA_KC = 16

D_FF = 2816
EPS = 1e-6
GELU_K = float(np.sqrt(2.0 / np.pi))
GELU_C = 0.044715

BLK = 128
LEAD = BLK
META0 = LEAD - N_META
HALO = 8
TM = 384
FFN_OUT_TILE_CAP = 720
FF_CHUNK = 256
STEP_BLKS = TM // BLK
VMEM_LIMIT = 56 * 1024 * 1024


def _tile_rows(n, cap, unit=HALO):
    return max(t for t in range(unit, cap + 1, unit) if n % t == 0)


def _resident(arr):
    return pl.BlockSpec(arr.shape, lambda *_: (0,) * arr.ndim, pipeline_mode=pl.Buffered(1))


def _cparams(sem):
    return pltpu.CompilerParams(dimension_semantics=sem, vmem_limit_bytes=VMEM_LIMIT)


def _rms(x, w):
    return x * lax.rsqrt(jnp.mean(x * x, axis=-1, keepdims=True) + EPS) * w


def _dot(a, b):
    return jnp.dot(a, b, preferred_element_type=F32)


def _shift_rows(x, d):
    return pltpu.roll(x, (-d) % x.shape[0], 0)[HALO:x.shape[0] - HALO]


def _load_tile(refs, lead_ref, i):
    if len(refs) == 1:
        return refs[0][0]
    blocks = [r[0, 0] for r in refs]
    if lead_ref is not None:
        blocks[0] = jnp.where(i == 0, lead_ref[...], blocks[0])
    return jnp.concatenate(blocks, axis=0)


def _tile_specs(src, tm, order=lambda i: i):
    if src[0] == 'h':
        return [pl.BlockSpec((1, tm, D_MODEL), lambda b, i: (b, order(i), 0))], [src[1]]
    x, lead = src[1], src[2]
    x4 = x.reshape(x.shape[0], x.shape[1] // BLK, BLK, D_MODEL)
    k = tm // BLK
    specs = [pl.BlockSpec((1, 1, BLK, D_MODEL),
                          lambda b, i, j=j: (b, jnp.maximum(order(i) * k + j - LEAD // BLK, 0), 0, 0))
             for j in range(k)]
    return [pl.BlockSpec(lead.shape, lambda b, i: (0, 0))] + specs, [lead] + [x4] * k


def _halo_specs(src, tm):
    arr = src[1]
    off = 0 if src[0] == 'h' else LEAD // HALO
    last = arr.shape[1] // HALO - 1
    hb = tm // HALO
    prev = pl.BlockSpec((1, HALO, D_MODEL), lambda b, i: (b, jnp.clip(i * hb - 1 - off, 0, last), 0))
    nxt = pl.BlockSpec((1, HALO, D_MODEL), lambda b, i: (b, jnp.clip((i + 1) * hb - off, 0, last), 0))
    return [prev, nxt], [arr, arr]


def _live_rows(tile_rows, tile_index):
    row = tile_index * tile_rows + lax.broadcasted_iota(jnp.int32, (tile_rows, 1), 0)
    return row >= META0


def _dot_nt(a, b):
    return lax.dot_general(a, b, (((1,), (1,)), ((), ())), preferred_element_type=F32)


def _rope128(x, c, s1, s2):
    return x * c + pltpu.roll(x, 128 - ROPE_DIM // 2, 1) * s1 + pltpu.roll(x, ROPE_DIM // 2, 1) * s2


def _scan_tables(dt_raw, dt_bias, a_log, live, acs_ref, tab_ref):
    nh = 2 * SSD_HEADS
    dtr = dt_raw + dt_bias
    dt = jnp.maximum(dtr, 0.0) + jnp.log1p(jnp.exp(-jnp.abs(dtr)))
    dt = jnp.where(live, dt, 0.0)
    a = dt * (-jnp.exp(a_log))
    ri = lax.broadcasted_iota(jnp.int32, (BLK, BLK), 0)
    ci = lax.broadcasted_iota(jnp.int32, (BLK, BLK), 1)
    tri = (ci <= ri).astype(BF16)
    fwd = ci < SSD_HEADS
    fwd_rows = ri < SSD_HEADS
    acs_chunks, tab_chunks = [], []
    for j in range(TM // BLK):
        rows = slice(j * BLK, (j + 1) * BLK)
        aj, dtj = a[rows], dt[rows]
        pre = sum(_dot(tri, part) for part in _split3(aj))
        total = pre[BLK - 1:BLK, :]
        acs = jnp.where(fwd, pre, total - pre + aj)
        acs_ref[0, rows, :] = acs
        acs_t = acs.T[:nh]
        chunk_sum = jnp.where(fwd_rows[:nh, 0:1], acs_t[:, BLK - 1:BLK], acs_t[:, 0:1])
        tabs = [acs_t, dtj.T[:nh],
                (dtj * jnp.exp(total - acs)).T[:nh],
                jnp.exp(acs_t), jnp.broadcast_to(jnp.exp(chunk_sum), (nh, BLK))]
        for k, t in enumerate(tabs):
            tab_ref[0, j, k] = t
        acs_chunks.append(acs)
        tab_chunks.append(tabs)
    return acs_chunks, tab_chunks


def _inproj_kernel(*refs, nt, ntile):
    lead_ref = refs[0] if ntile > 1 else None
    tile_refs = refs[ntile > 1:ntile + (ntile > 1)]
    (hp_ref, hn_ref, nw_ref, wz_ref, wx_ref, wdt_ref, wwq_ref, wwk_ref, wwv_ref,
     wnq_ref, wnk_ref, wnv_ref, cw_ref, cb_ref, dtb_ref, alog_ref, rc_ref, rs1_ref, rs2_ref, rct_ref, rst_ref,
     z_ref, xt_ref, bc_ref, acs_ref, tab_ref, wq_ref, wk_ref, wv_ref, nq_ref, nk_ref, nv_ref, yf_ref,
     state_ref) = refs[ntile + (ntile > 1):]
    i = pl.program_id(1)

    @pl.when(i == 0)
    def _():
        state_ref[...] = jnp.zeros_like(state_ref)

    nw = nw_ref[...]
    a32 = _rms(_load_tile(tile_refs, lead_ref, i), nw)
    a = a32.astype(BF16)
    head = jnp.where(i > 0, 1.0, 0.0)
    tail = jnp.where(i < nt - 1, 1.0, 0.0)
    a_ext = jnp.concatenate([_rms(hp_ref[0], nw) * head, a32, _rms(hn_ref[0], nw) * tail], axis=0).astype(BF16)
    xe = _dot(a_ext, wx_ref[...])

    def put_blocks(ref, xt):
        for j in range(TM // BLK):
            ref[0, j] = xt[:, j * BLK:(j + 1) * BLK].astype(BF16)

    put_blocks(z_ref, _dot(a, wz_ref[...]).T)
    dt_raw = _dot(a, wdt_ref[...])
    pad = SSD_CONV // 2
    acc = _shift_rows(xe, -pad) * cw_ref[0:1, :] + cb_ref[...]
    for j in range(1, SSD_CONV):
        acc = acc + _shift_rows(xe, j - pad) * cw_ref[j:j + 1, :]
    xc = acc * (1.0 / (1.0 + jnp.exp(-acc)))
    xt_all = xc[:, :SSD_INNER].T
    xt_chunks = [xt_all[:, j * BLK:(j + 1) * BLK].astype(BF16) for j in range(TM // BLK)]
    for j in range(TM // BLK):
        xt_ref[0, j] = xt_chunks[j]
    bc = xc[:, SSD_INNER:].astype(BF16)
    bc_ref[0] = bc
    acs_chunks, tab_chunks = _scan_tables(dt_raw, dtb_ref[...], alog_ref[...], _live_rows(TM, i),
                                          acs_ref, tab_ref)

    def put_y(j, rows, y):
        yf_ref[0, j, rows, :] = y

    def scan_chunk(j):
        _ssd_scan(lambda j, rows: xt_chunks[j][rows],
                  lambda j, k: bc[j * BLK:(j + 1) * BLK, k * SSD_STATE:(k + 1) * SSD_STATE],
                  lambda j: acs_chunks[j], lambda j, k: tab_chunks[j][k], state_ref, put_y, None,
                  reverse=False, chunks=[j])

    scan_chunk(0)
    scale = HEAD_DIM ** -0.5
    qt = _dot(a, wwq_ref[...]).T
    cos_t, sin_t = rct_ref[...], rst_ref[...]
    half = ROPE_DIM // 2
    parts = []
    for hh in range(WIN_Q_HEADS):
        x1 = qt[hh * HEAD_DIM:hh * HEAD_DIM + half]
        x2 = qt[hh * HEAD_DIM + half:hh * HEAD_DIM + ROPE_DIM]
        parts += [x1 * cos_t - x2 * sin_t, x2 * cos_t + x1 * sin_t, qt[hh * HEAD_DIM + ROPE_DIM:(hh + 1) * HEAD_DIM]]
    put_blocks(wq_ref, jnp.concatenate(parts, axis=0) * scale)
    wk_ref[0] = _rope128(_dot(a, wwk_ref[...]), rc_ref[...], rs1_ref[...], rs2_ref[...]).astype(BF16)
    put_blocks(wv_ref, _dot(a, wwv_ref[...]).T)
    scan_chunk(1)
    put_blocks(nq_ref, (_dot(a, wnq_ref[...]) * scale).T)
    nk_ref[0] = _dot(a, wnk_ref[...]).astype(BF16)
    scan_chunk(2)
    nv_ref[0] = _dot(a, wnv_ref[...]).astype(BF16)


def _inproj(src, lp, nw, ws, conv, scan, rope):
    bsz = src[1].shape[0]
    nt = lp // TM
    nb = lp // BLK
    tile = lambda n: pl.BlockSpec((1, TM, n), lambda b, i: (b, i, 0))
    tblk = lambda n: pl.BlockSpec((1, TM // BLK, n, BLK), lambda b, i: (b, i, 0, 0))
    tile_specs, tile_args = _tile_specs(src, TM)
    halo_specs, halo_args = _halo_specs(src, TM)
    full = lambda arr: pl.BlockSpec(arr.shape, lambda b, i: (0,) * arr.ndim)
    rtab = pl.BlockSpec((TM, 128), lambda b, i: (i, 0))
    rtab_t = pl.BlockSpec((ROPE_DIM // 2, TM), lambda b, i: (0, i))
    tok = lambda n, d: (tile(n), jax.ShapeDtypeStruct((bsz, lp, n), d))
    blk = lambda n: (tblk(n), jax.ShapeDtypeStruct((bsz, nb, n, BLK), BF16))
    kvw = WIN_KV_HEADS * HEAD_DIM
    tabs = (pl.BlockSpec((1, TM // BLK, 5, 2 * SSD_HEADS, BLK), lambda b, i: (b, i, 0, 0, 0)),
            jax.ShapeDtypeStruct((bsz, nb, 5, 2 * SSD_HEADS, BLK), F32))
    outs = [blk(SSD_INNER), blk(SSD_INNER), tok(SSD_XBC - SSD_INNER, BF16), tok(128, F32), tabs, blk(WIN_Q_HEADS * HEAD_DIM), tok(kvw, BF16),
            blk(kvw), blk(NA_HEADS * HEAD_DIM), tok(NA_HEADS * HEAD_DIM, BF16), tok(NA_HEADS * HEAD_DIM, BF16),
            (tblk(SSD_INNER), jax.ShapeDtypeStruct((bsz, nb, SSD_INNER, BLK), F32))]
    return pl.pallas_call(
        functools.partial(_inproj_kernel, nt=nt, ntile=len(tile_specs) - (src[0] == 'x')),
        grid=(bsz, nt),
        in_specs=tile_specs + halo_specs + [full(nw)] + [full(w) for w in ws] + [full(c) for c in conv + scan]
        + [rtab, rtab, rtab, rtab_t, rtab_t],
        out_specs=[o[0] for o in outs],
        out_shape=[o[1] for o in outs],
        scratch_shapes=[pltpu.VMEM((SSD_GROUPS, SSD_INNER // SSD_GROUPS, SSD_STATE), F32)],
        compiler_params=_cparams(("parallel", "arbitrary")),
        name="inproj",
    )(*tile_args, *halo_args, nw, *ws, *conv, *scan, *rope)


def _split3(x):
    hi = x.astype(BF16)
    r1 = x - hi.astype(F32)
    mid = r1.astype(BF16)
    lo = (r1 - mid.astype(F32)).astype(BF16)
    return hi, mid, lo


def _ssd_scan(get_x, get_bc, get_acs, get_tab, state_ref, put_y, after_chunk, *, reverse, chunks=None):
    ri = lax.broadcasted_iota(jnp.int32, (BLK, BLK), 0)
    ci = lax.broadcasted_iota(jnp.int32, (BLK, BLK), 1)
    feeds = (ri >= ci) if reverse else (ri <= ci)
    hoff = SSD_HEADS if reverse else 0
    rep = SSD_HEADS // SSD_GROUPS
    if chunks is None:
        chunks = reversed(range(STEP_BLKS)) if reverse else range(STEP_BLKS)
    for j in chunks:
        acs = get_acs(j)
        acs_t, dt_t, dt_out_t, e_in_t, e_chunk = (get_tab(j, k) for k in range(5))
        for g in range(SSD_GROUPS):
            bm = get_bc(j, g)
            cm = get_bc(j, SSD_GROUPS + g)
            cbt = _dot_nt(bm, cm)
            ht = state_ref[g]
            y_off = _dot_nt(ht.astype(BF16), cm)
            x_out, decay = [], []
            for r in range(rep):
                h = g * rep + r
                hl = hoff + h
                rows = slice(h * HEAD_DIM, (h + 1) * HEAD_DIM)
                xt = get_x(j, rows).astype(F32)
                x_in = (xt * dt_t[hl:hl + 1, :]).astype(BF16)
                x_out.append((xt * dt_out_t[hl:hl + 1, :]).astype(BF16))
                decay.append(jnp.broadcast_to(e_chunk[hl:hl + 1, :], (HEAD_DIM, SSD_STATE)))
                diff = acs_t[hl:hl + 1, :] - acs[:, hl:hl + 1]
                lt = (cbt * jnp.exp(jnp.where(feeds, diff, -jnp.inf))).astype(BF16)
                put_y(j, rows, _dot(x_in, lt) + y_off[r * HEAD_DIM:(r + 1) * HEAD_DIM] * e_in_t[hl:hl + 1, :])
            s_new = _dot(jnp.concatenate(x_out, axis=0), bm)
            state_ref[g] = ht * jnp.concatenate(decay, axis=0) + s_new
        if after_chunk is not None:
            after_chunk(j)


def _softmax_pv_t(scores, values, pad_last, sink_logit):
    mx = functools.reduce(jnp.maximum, [jnp.max(s, axis=0, keepdims=True) for s in scores] + [sink_logit])
    ps = [jnp.exp(s - mx) for s in scores]
    den = functools.reduce(jnp.add, [jnp.sum(p, axis=0, keepdims=True) for p in ps]) + jnp.exp(sink_logit - mx)
    pb = [p.astype(BF16) for p in ps]
    pb[-1] = jnp.concatenate([pad_last, pb[-1]], axis=0)
    return _dot(jnp.concatenate(values, axis=1), jnp.concatenate(pb, axis=0)), den


def _block_diag_rows(pieces):
    zero = jnp.zeros_like(pieces[0][0])
    rows = []
    for i in range(len(pieces)):
        row = []
        for j, ps in enumerate(pieces):
            row += [p if i == j else zero for p in ps]
        rows.append(jnp.concatenate(row, axis=1))
    return jnp.concatenate(rows, axis=0)


def _win_block(win_refs, j, tile, nb, put):
    qt_ref, kp_ref, kc_ref, kn_ref, km_ref, vp_ref, vc_ref, vn_ref, vm_ref, sink_ref = win_refs
    width = WIN_Q_HEADS * BLK
    ki = lax.broadcasted_iota(jnp.int32, (BLK, width), 0)
    qi = lax.broadcasted_iota(jnp.int32, (BLK, width), 1) & (BLK - 1)
    rep = WIN_Q_HEADS // WIN_KV_HEADS
    keys = [kp_ref[0]] + [kc_ref[0, jj * BLK:(jj + 1) * BLK, :] for jj in range(STEP_BLKS)] + [kn_ref[0]]
    vals = [vp_ref[0, 0]] + [vc_ref[0, jj] for jj in range(STEP_BLKS)] + [vn_ref[0, 0]]
    kmeta, vmeta = km_ref[0, META0:, :], vm_ref[0, 0]
    pad_last = jnp.zeros((META0, width), BF16)
    n = tile * STEP_BLKS + j
    ok_prev = jnp.logical_and(ki >= qi, n >= 2)
    ok_cur = n >= 1
    ok_next = jnp.logical_and(ki <= qi, n + 1 <= nb - 1)
    heads = [qt_ref[0, j, h * HEAD_DIM:(h + 1) * HEAD_DIM, :] for h in range(WIN_Q_HEADS)]
    qbd = _block_diag_rows([heads[g * rep:(g + 1) * rep] for g in range(WIN_KV_HEADS)])
    s = _dot(jnp.concatenate(keys[j:j + 3] + [kmeta], axis=0), qbd)
    scores = [jnp.where(ok_prev, s[:BLK], -jnp.inf),
              jnp.where(ok_cur, s[BLK:2 * BLK], -jnp.inf),
              jnp.where(ok_next, s[2 * BLK:3 * BLK], -jnp.inf),
              s[3 * BLK:]]
    o, den = _softmax_pv_t(scores, vals[j:j + 3] + [vmeta], pad_last, sink_ref[...])
    o = o / den
    outs = [o[(h // rep) * HEAD_DIM:(h // rep + 1) * HEAD_DIM, h * BLK:(h + 1) * BLK] for h in range(WIN_Q_HEADS)]
    put(j, jnp.concatenate(outs, axis=0).T.astype(BF16))


def _win_specs(qt, k, vt, sink, order):
    nb = qt.shape[1]
    kvw = WIN_KV_HEADS * HEAD_DIM
    assert kvw == BLK and WIN_RADIUS == BLK
    kedge = lambda f: pl.BlockSpec((1, BLK, kvw), lambda b, i: (b, f(order(i)), 0))
    vedge = lambda f: pl.BlockSpec((1, 1, kvw, BLK), lambda b, i: (b, f(order(i)), 0, 0))
    before = lambda t: jnp.maximum(t * STEP_BLKS - 1, 0)
    after = lambda t: jnp.minimum((t + 1) * STEP_BLKS, nb - 1)
    first = lambda t: 0
    specs = [pl.BlockSpec((1, STEP_BLKS, WIN_Q_HEADS * HEAD_DIM, BLK), lambda b, i: (b, order(i), 0, 0)),
             kedge(before), pl.BlockSpec((1, TM, kvw), lambda b, i: (b, order(i), 0)), kedge(after), kedge(first),
             vedge(before), pl.BlockSpec((1, STEP_BLKS, kvw, BLK), lambda b, i: (b, order(i), 0, 0)), vedge(after),
             vedge(first), pl.BlockSpec(sink.shape, lambda b, i: (0, 0))]
    return specs, [qt, k, k, k, k, vt, vt, vt, vt, sink]


NA_QUAD = 4
NA_QW = NA_QUAD * HEAD_DIM
NA_WIN = NA_KR * GRID_W


def _na_bias_tables(rpb):
    c = np.arange(GRID_W)[None, :]
    kc = np.arange(GRID_W)[:, None]
    cs = np.clip(c - NA_KC // 2, 0, GRID_W - NA_KC)
    ok = (kc >= cs) & (kc < cs + NA_KC)
    pick = jnp.asarray((kc - c + NA_KC - 1)[None] == np.arange(2 * NA_KC - 1)[:, None, None], F32)
    t = jnp.einsum('hed,dkc->hekc', rpb.astype(F32), pick, precision=lax.Precision.HIGHEST)
    t = jnp.where(jnp.asarray(ok)[None, None], t, -jnp.inf)

    def pack(x):
        h, e = x.shape[:2]
        x = x.reshape(h // NA_QUAD, NA_QUAD, e, GRID_W, GRID_W)
        return x.transpose(0, 2, 3, 1, 4).reshape(h // NA_QUAD, e, GRID_W, NA_QW)

    meta = jnp.broadcast_to(t[:, NA_KR - 1:, :, 0:1], (t.shape[0], NA_KR, GRID_W, GRID_W))
    return pack(t), pack(meta)


def _na_kernel(qt_ref, k_ref, v_ref, bias_ref, mbias_ref, mb_ref, o_ref, *, rows):
    width = NA_HEADS * HEAD_DIM
    first_half = lax.broadcasted_iota(jnp.int32, (width, BLK), 1) < GRID_W
    ri = lax.broadcasted_iota(jnp.int32, (NA_QW, NA_QW), 0)
    ci = lax.broadcasted_iota(jnp.int32, (NA_QW, NA_QW), 1)
    diag = lax.shift_right_logical(ri, 6) == lax.shift_right_logical(ci, 6)
    head_of_lane = lax.shift_right_logical(lax.broadcasted_iota(jnp.int32, (GRID_W, NA_QW), 1), 6)
    pad_meta = jnp.zeros((META0, NA_QW), F32)

    def query_row(j, half, meta):
        qf = qt_ref[0, j].astype(F32)
        qsw = pltpu.roll(qf, GRID_W, 1)
        dup = jnp.where(first_half, qf, qsw) if half == 0 else jnp.where(first_half, qsw, qf)
        if meta:
            start = LEAD
        else:
            r = (pl.program_id(1) * STEP_BLKS + j - 1) * (BLK // GRID_W) + half
            rs = jnp.clip(r - NA_KR // 2, 0, rows - NA_KR)
            start = pl.multiple_of(LEAD + rs * GRID_W, GRID_W)
            e0 = rs - r + (NA_KR - 1)
        for u in range(NA_HEADS // NA_QUAD):
            cols = slice(u * NA_QW, (u + 1) * NA_QW)
            base = dup[u * NA_QW:(u + 1) * NA_QW]
            qbd = jnp.where(diag, jnp.concatenate([base, base], axis=1), 0.0).astype(BF16)
            keys = jnp.concatenate([k_ref[0, pl.ds(start, NA_WIN), cols], k_ref[0, META0:LEAD, cols]], axis=0)
            s = _dot(keys, qbd)
            if meta:
                bias = jnp.concatenate([mbias_ref[u, i] for i in range(NA_KR)], axis=0)
            else:
                bias = jnp.concatenate([bias_ref[u, e0 + i] for i in range(NA_KR)], axis=0)
            sw = s[:NA_WIN] + bias
            sm = s[NA_WIN:] + mb_ref[u]
            mx = jnp.maximum(jnp.max(sw, axis=0, keepdims=True), jnp.max(sm, axis=0, keepdims=True))
            pw, pm = jnp.exp(sw - mx), jnp.exp(sm - mx)
            rden = 1.0 / (jnp.sum(pw, axis=0, keepdims=True) + jnp.sum(pm, axis=0, keepdims=True))
            p_t = jnp.concatenate([pw * rden, pad_meta, pm * rden], axis=0).T.astype(BF16)
            vals = jnp.concatenate([v_ref[0, pl.ds(start, NA_WIN), cols], v_ref[0, 0:LEAD, cols]], axis=0)
            o = _dot(p_t, vals)
            out = o[(NA_QUAD - 1) * GRID_W:]
            for hq in range(NA_QUAD - 2, -1, -1):
                out = jnp.where(head_of_lane == hq, o[hq * GRID_W:(hq + 1) * GRID_W], out)
            o_ref[0, pl.ds(j * BLK + half * GRID_W, GRID_W), cols] = out.astype(BF16)

    @pl.when(pl.program_id(1) == 0)
    def _():
        o_ref[0, 0:GRID_W, :] = jnp.zeros((GRID_W, width), BF16)
        query_row(0, 1, True)

    @pl.when(pl.program_id(1) > 0)
    def _():
        query_row(0, 0, False)
        query_row(0, 1, False)

    for j in range(1, STEP_BLKS):
        query_row(j, 0, False)
        query_row(j, 1, False)


def _na_attention(qt, k, v, bias, mbias, mb):
    bsz, nb, width, _ = qt.shape
    lp = nb * BLK
    rows = (lp - LEAD) // GRID_W
    assert rows >= NA_KR and N_META <= GRID_W and LEAD - GRID_W <= META0
    qblk = pl.BlockSpec((1, STEP_BLKS, width, BLK), lambda b, n: (b, n, 0, 0))
    seq = pl.BlockSpec((1, lp, width), lambda b, n: (b, 0, 0), pipeline_mode=pl.Buffered(1))
    return pl.pallas_call(
        functools.partial(_na_kernel, rows=rows),
        grid=(bsz, nb // STEP_BLKS),
        in_specs=[qblk, seq, seq, _resident(bias), _resident(mbias), _resident(mb)],
        out_specs=pl.BlockSpec((1, TM, width), lambda b, n: (b, n, 0)),
        out_shape=jax.ShapeDtypeStruct((bsz, lp, width), BF16),
        compiler_params=_cparams(("parallel", "arbitrary")),
        name="na_attn",
    )(qt, k, v, bias, mbias, mb)


def _mixout_kernel(*refs, ns, ntile):
    lead_ref = refs[0] if ntile > 1 else None
    tile_refs = refs[ntile > 1:ntile + (ntile > 1)]
    rest = refs[ntile + (ntile > 1):]
    win_refs, rest = rest[:N_WIN_REFS], rest[N_WIN_REFS:]
    (xt_ref, bc_ref, acs_ref, tab_ref, zt_ref, yf_ref, yn_ref, dsk_ref, nws_ref, w1_ref, w2_ref, w3_ref,
     nw_ref, o_ref, state_ref, ybuf_ref, ys_ref, yw_ref) = rest
    i = ns - 1 - pl.program_id(1)

    @pl.when(pl.program_id(1) == 0)
    def _():
        state_ref[...] = jnp.zeros_like(state_ref)

    def put_y(j, rows, y):
        ybuf_ref[rows, :] = y

    def finish(j):
        y = ybuf_ref[...] + yf_ref[0, j] + dsk_ref[...] * xt_ref[0, j].astype(F32)
        z = zt_ref[0, j].astype(F32)
        y = y * (z * (1.0 / (1.0 + jnp.exp(-z))))
        y = y * lax.rsqrt(jnp.mean(y * y, axis=0, keepdims=True) + EPS) * nws_ref[...]
        ys_ref[j * BLK:(j + 1) * BLK, :] = y.T.astype(BF16)

    def scan_chunk(j):
        _ssd_scan(lambda j, rows: xt_ref[0, j, rows, :],
                  lambda j, k: bc_ref[0, j * BLK:(j + 1) * BLK, k * SSD_STATE:(k + 1) * SSD_STATE],
                  lambda j: acs_ref[0, j * BLK:(j + 1) * BLK, :],
                  lambda j, k: tab_ref[0, j, k], state_ref, put_y, finish, reverse=True, chunks=[j])

    def put_win(j, y):
        yw_ref[j * BLK:(j + 1) * BLK, :] = y

    for j in reversed(range(STEP_BLKS)):
        scan_chunk(j)
        _win_block(win_refs, j, i, ns * STEP_BLKS, put_win)
        if j == STEP_BLKS - 1:
            mix = _dot(yn_ref[0], w3_ref[...])
    mix = mix + _dot(yw_ref[...], w2_ref[...]) + _dot(ys_ref[...], w1_ref[...])
    out = _load_tile(tile_refs, lead_ref, i) + _rms(mix, nw_ref[...])
    o_ref[0] = jnp.where(_live_rows(TM, i), out, 0.0)


N_WIN_REFS = 10


def _mixout(src, win_args, xt, bc, acs, tab, zt, y_fwd, y_na, dsk, nws, w1, w2, w3, nw):
    bsz, nc = xt.shape[:2]
    ns = nc // STEP_BLKS
    order = lambda c: ns - 1 - c
    tok = lambda n: pl.BlockSpec((1, TM, n), lambda b, c: (b, order(c), 0))
    feat = pl.BlockSpec((1, STEP_BLKS, SSD_INNER, BLK), lambda b, c: (b, order(c), 0, 0))
    tabs = pl.BlockSpec((1, STEP_BLKS) + tab.shape[2:], lambda b, c: (b, order(c), 0, 0, 0))
    tile_specs, tile_args = _tile_specs(src, TM, order)
    win_specs, win_ops = _win_specs(*win_args, order)
    assert len(win_specs) == N_WIN_REFS
    consts = [dsk, nws, w1, w2, w3, nw]
    return pl.pallas_call(
        functools.partial(_mixout_kernel, ns=ns, ntile=len(tile_specs) - (src[0] == 'x')),
        grid=(bsz, ns),
        in_specs=tile_specs + win_specs + [feat, tok(bc.shape[-1]), tok(128), tabs, feat, feat,
                                           tok(y_na.shape[-1])] + [_resident(c) for c in consts],
        out_specs=tok(D_MODEL),
        out_shape=jax.ShapeDtypeStruct((bsz, nc * BLK, D_MODEL), F32),
        scratch_shapes=[pltpu.VMEM((SSD_GROUPS, SSD_INNER // SSD_GROUPS, SSD_STATE), F32),
                        pltpu.VMEM((SSD_INNER, BLK), F32), pltpu.VMEM((TM, SSD_INNER), BF16),
                        pltpu.VMEM((TM, WIN_Q_HEADS * HEAD_DIM), BF16)],
        compiler_params=_cparams(("parallel", "arbitrary")),
        name="mixout",
    )(*tile_args, *win_ops, xt, bc, acs, tab, zt, y_fwd, y_na, *consts)


def _ffn_kernel(*refs, nt, ntile, padded):
    tile_refs = refs[:ntile]
    hp_ref, hn_ref, nw1_ref, wup_ref, cw_ref, cb_ref, wdn_ref, nw2_ref, o_ref, act_ref = refs[ntile:]
    i = pl.program_id(1)
    rows = act_ref.shape[0]
    nw1 = nw1_ref[...]
    hc = _load_tile(tile_refs, None, i)
    f = jnp.concatenate([_rms(hp_ref[0], nw1), _rms(hc, nw1), _rms(hn_ref[0], nw1)], axis=0).astype(BF16)
    tail = jnp.where(i < nt - 1, 1.0, 0.0)
    nch = D_FF // FF_CHUNK

    def conv(g, c):
        g = jnp.concatenate([g[:HALO + rows], g[HALO + rows:] * tail], axis=0)
        cols = slice(c * FF_CHUNK, (c + 1) * FF_CHUNK)
        out = _shift_rows(g, -1) * cw_ref[0:1, cols] + cb_ref[:, cols]
        out = out + g[HALO:HALO + rows] * cw_ref[1:2, cols]
        return out + _shift_rows(g, 1) * cw_ref[2:3, cols]

    for c in range(nch):
        gate = conv(_dot(f, wup_ref[:, c * FF_CHUNK:(c + 1) * FF_CHUNK]), c)
        up = conv(_dot(f, wup_ref[:, D_FF + c * FF_CHUNK:D_FF + (c + 1) * FF_CHUNK]), nch + c)
        th = jnp.tanh(gate * (GELU_K + (GELU_K * GELU_C) * (gate * gate)))
        act_ref[:, c * FF_CHUNK:(c + 1) * FF_CHUNK] = (gate * (0.5 * th + 0.5) * up).astype(BF16)
    out = hc + _rms(_dot(act_ref[...], wdn_ref[...]), nw2_ref[...])
    o_ref[0] = jnp.where(_live_rows(rows, i), out, 0.0) if padded else out


def _ffn(h, nw1, wup, cw, cb, wdn, nw2, *, last):
    bsz, lp, _ = h.shape
    full = _resident
    if last:
        n_out = lp - LEAD
        tm = _tile_rows(n_out, FFN_OUT_TILE_CAP, BLK)
        h4 = h.reshape(bsz, lp // BLK, BLK, D_MODEL)
        k = tm // BLK
        tile_specs = [pl.BlockSpec((1, 1, BLK, D_MODEL), lambda b, i, j=j: (b, LEAD // BLK + i * k + j, 0, 0))
                      for j in range(k)]
        tile_args = [h4] * k
        off = LEAD // HALO
    else:
        tm, n_out = TM, lp
        tile_specs, tile_args = _tile_specs(('h', h), tm)
        off = 0
    nt = n_out // tm
    hb = tm // HALO
    last_blk = lp // HALO - 1
    prev = pl.BlockSpec((1, HALO, D_MODEL), lambda b, i: (b, jnp.clip(off + i * hb - 1, 0, last_blk), 0))
    nxt = pl.BlockSpec((1, HALO, D_MODEL), lambda b, i: (b, jnp.clip(off + (i + 1) * hb, 0, last_blk), 0))
    return pl.pallas_call(
        functools.partial(_ffn_kernel, nt=nt, ntile=len(tile_specs), padded=not last),
        grid=(bsz, nt),
        in_specs=tile_specs + [prev, nxt, full(nw1), full(wup), full(cw), full(cb), full(wdn), full(nw2)],
        out_specs=pl.BlockSpec((1, tm, D_MODEL), lambda b, i: (b, i, 0)),
        out_shape=jax.ShapeDtypeStruct((bsz, n_out, D_MODEL), F32),
        scratch_shapes=[pltpu.VMEM((tm, D_FF), BF16)],
        compiler_params=_cparams(("parallel", "parallel")),
        name="ffn",
    )(*tile_args, h, h, nw1, wup, cw, cb, wdn, nw2)


def _pad_lanes(x, n):
    return jnp.pad(x, [(0, 0)] * (x.ndim - 1) + [(0, n - x.shape[-1])])


def _rope_tables(lp):
    half = ROPE_DIM // 2
    pos = jnp.maximum(jnp.arange(lp) - META0, 0).astype(F32)
    inv = jnp.power(ROPE_THETA, -jnp.arange(half, dtype=F32) / half)
    ang = pos[:, None] * inv[None, :]
    cos, sin = jnp.cos(ang), jnp.sin(ang)
    zeros, ones = jnp.zeros_like(cos), jnp.ones((lp, HEAD_DIM - ROPE_DIM), F32)
    rest = jnp.zeros((lp, HEAD_DIM - ROPE_DIM), F32)
    c = jnp.concatenate([cos, cos, ones], axis=1)
    s1 = jnp.concatenate([-sin, zeros, rest], axis=1)
    s2 = jnp.concatenate([zeros, sin, rest], axis=1)
    return tuple(jnp.tile(t, (1, 128 // HEAD_DIM)) for t in (c, s1, s2)) + (cos.T, sin.T)


def _layer_params(i, p):
    row = lambda v: v.reshape(1, -1).astype(F32)
    lanes = lambda v: jnp.broadcast_to(v.astype(F32)[:, None], (v.shape[0], BLK))
    sizes = [SSD_INNER, SSD_XBC, 2 * SSD_HEADS, WIN_Q_HEADS * HEAD_DIM, WIN_KV_HEADS * HEAD_DIM,
             WIN_KV_HEADS * HEAD_DIM, NA_HEADS * HEAD_DIM, NA_HEADS * HEAD_DIM, NA_HEADS * HEAD_DIM]
    w_in = p['w_in'][i].astype(BF16)
    ws = jnp.split(w_in, np.cumsum(sizes)[:-1].tolist(), axis=1)
    ws[2] = _pad_lanes(ws[2], 128)
    w_out = p['w_out'][i].astype(BF16)
    return dict(
        norm_mix_pre=row(p['norm_mix_pre'][i]),
        w_in=ws,
        ssd_conv=(p['ssd_conv_w'][i].astype(F32), row(p['ssd_conv_b'][i])),
        ssd_consts=(_pad_lanes(row(p['ssd_dt_bias'][i]), 128), _pad_lanes(row(p['ssd_a_log'][i]), 128)),
        ssd_d=lanes(jnp.repeat(p['ssd_d'][i], HEAD_DIM)),
        ssd_norm_w=lanes(p['ssd_norm_w'][i]),
        win_sink=row(jnp.repeat(p['win_sink'][i], BLK)),
        na_bias=_na_bias_tables(p['na_rpb'][i]),
        na_meta_bias=jnp.broadcast_to(p['na_meta_bias'][i].astype(F32).T.reshape(N_META, NA_HEADS // NA_QUAD, NA_QUAD, 1),
                                      (N_META, NA_HEADS // NA_QUAD, NA_QUAD, GRID_W)).transpose(1, 0, 2, 3).reshape(
            NA_HEADS // NA_QUAD, N_META, NA_QW),
        w_out=(w_out[:SSD_INNER], w_out[SSD_INNER:SSD_INNER + WIN_Q_HEADS * HEAD_DIM],
               w_out[SSD_INNER + WIN_Q_HEADS * HEAD_DIM:]),
        norm_mix_post=row(p['norm_mix_post'][i]),
        norm_ffn_pre=row(p['norm_ffn_pre'][i]),
        ffn_w_up=p['ffn_w_up'][i].astype(BF16),
        ffn_conv_w=p['ffn_conv_w'][i].astype(F32),
        ffn_conv_b=row(p['ffn_conv_b'][i]),
        ffn_w_down=p['ffn_w_down'][i].astype(BF16),
        norm_ffn_post=row(p['norm_ffn_post'][i]),
    )


def _encode(x, meta_tokens, layers):
    bsz, n_tok, _ = x.shape
    assert (n_tok + LEAD) % TM == 0 and n_tok // GRID_W >= NA_KR
    lp = LEAD + n_tok
    lead = jnp.concatenate([jnp.zeros((META0, D_MODEL), F32), meta_tokens.astype(F32)], axis=0)
    src = ('x', x.astype(F32), lead)
    rope = _rope_tables(lp)
    for li, lw in enumerate(layers):
        zt, xt, bc, acs, tab, wq, wk, wv, nq, nk, nv, y_fwd = _inproj(src, lp, lw['norm_mix_pre'], lw['w_in'],
                                                                      lw['ssd_conv'], lw['ssd_consts'], rope)
        y_na = _na_attention(nq, nk, nv, *lw['na_bias'], lw['na_meta_bias'])
        h = _mixout(src, (wq, wk, wv, lw['win_sink']), xt, bc, acs, tab, zt, y_fwd, y_na, lw['ssd_d'],
                    lw['ssd_norm_w'], *lw['w_out'], lw['norm_mix_post'])
        h = _ffn(h, lw['norm_ffn_pre'], lw['ffn_w_up'], lw['ffn_conv_w'], lw['ffn_conv_b'],
                 lw['ffn_w_down'], lw['norm_ffn_post'], last=li == len(layers) - 1)
        src = ('h', h)
    return h


def kernel(x_prompt, x_sample, meta_tokens, norm_mix_pre, norm_mix_post, w_in, ssd_conv_w, ssd_conv_b,
           ssd_dt_bias, ssd_a_log, ssd_d, ssd_norm_w, win_sink, na_rpb, na_meta_bias, w_out, norm_ffn_pre,
           norm_ffn_post, ffn_w_up, ffn_conv_w, ffn_conv_b, ffn_w_down):
    p = dict(norm_mix_pre=norm_mix_pre, norm_mix_post=norm_mix_post, w_in=w_in, ssd_conv_w=ssd_conv_w,
             ssd_conv_b=ssd_conv_b, ssd_dt_bias=ssd_dt_bias, ssd_a_log=ssd_a_log, ssd_d=ssd_d,
             ssd_norm_w=ssd_norm_w, win_sink=win_sink, na_rpb=na_rpb, na_meta_bias=na_meta_bias, w_out=w_out,
             norm_ffn_pre=norm_ffn_pre, norm_ffn_post=norm_ffn_post, ffn_w_up=ffn_w_up, ffn_conv_w=ffn_conv_w,
             ffn_conv_b=ffn_conv_b, ffn_w_down=ffn_w_down)
    layers = [_layer_params(i, p) for i in range(w_in.shape[0])]
    return (_encode(x_prompt, meta_tokens, layers), _encode(x_sample, meta_tokens, layers))
```

```python
import functools

import jax
import jax.numpy as jnp
import numpy as np
from jax import lax
from jax.experimental import pallas as pl
from jax.experimental.pallas import tpu as pltpu

F32 = jnp.float32
BF16 = jnp.bfloat16

D_MODEL = 1024
N_META = 16
GRID_W = 64
HEAD_DIM = 64

SSD_HEADS = 16
SSD_INNER = SSD_HEADS * HEAD_DIM
SSD_GROUPS = 2
SSD_STATE = 128
SSD_XBC = SSD_INNER + 2 * SSD_GROUPS * SSD_STATE
SSD_CONV = 5

WIN_Q_HEADS = 8
WIN_KV_HEADS = 2
WIN_RADIUS = 128
ROPE_THETA = 500000.0
ROPE_DIM = HEAD_DIM // 4

NA_HEADS = 8
NA_KR = 8
NA_KC = 16

D_FF = 2816
EPS = 1e-6
GELU_K = float(np.sqrt(2.0 / np.pi))
GELU_C = 0.044715

BLK = 128
LEAD = BLK
META0 = LEAD - N_META
HALO = 8
TM = 384
FFN_OUT_TILE_CAP = 720
FF_CHUNK = 256
STEP_BLKS = TM // BLK
VMEM_LIMIT = 56 * 1024 * 1024


def _tile_rows(n, cap, unit=HALO):
    return max(t for t in range(unit, cap + 1, unit) if n % t == 0)


def _resident(arr):
    return pl.BlockSpec(arr.shape, lambda *_: (0,) * arr.ndim, pipeline_mode=pl.Buffered(1))


def _cparams(sem):
    return pltpu.CompilerParams(dimension_semantics=sem, vmem_limit_bytes=VMEM_LIMIT)


def _rms(x, w):
    return x * lax.rsqrt(jnp.mean(x * x, axis=-1, keepdims=True) + EPS) * w


def _dot(a, b):
    return jnp.dot(a, b, preferred_element_type=F32)


def _shift_rows(x, d):
    return pltpu.roll(x, (-d) % x.shape[0], 0)[HALO:x.shape[0] - HALO]


def _load_tile(refs, lead_ref, i):
    if len(refs) == 1:
        return refs[0][0]
    blocks = [r[0, 0] for r in refs]
    if lead_ref is not None:
        blocks[0] = jnp.where(i == 0, lead_ref[...], blocks[0])
    return jnp.concatenate(blocks, axis=0)


def _tile_specs(src, tm, order=lambda i: i):
    if src[0] == 'h':
        return [pl.BlockSpec((1, tm, D_MODEL), lambda b, i: (b, order(i), 0))], [src[1]]
    x, lead = src[1], src[2]
    x4 = x.reshape(x.shape[0], x.shape[1] // BLK, BLK, D_MODEL)
    k = tm // BLK
    specs = [pl.BlockSpec((1, 1, BLK, D_MODEL),
                          lambda b, i, j=j: (b, jnp.maximum(order(i) * k + j - LEAD // BLK, 0), 0, 0))
             for j in range(k)]
    return [pl.BlockSpec(lead.shape, lambda b, i: (0, 0))] + specs, [lead] + [x4] * k


def _halo_specs(src, tm):
    arr = src[1]
    off = 0 if src[0] == 'h' else LEAD // HALO
    last = arr.shape[1] // HALO - 1
    hb = tm // HALO
    prev = pl.BlockSpec((1, HALO, D_MODEL), lambda b, i: (b, jnp.clip(i * hb - 1 - off, 0, last), 0))
    nxt = pl.BlockSpec((1, HALO, D_MODEL), lambda b, i: (b, jnp.clip((i + 1) * hb - off, 0, last), 0))
    return [prev, nxt], [arr, arr]


def _live_rows(tile_rows, tile_index):
    row = tile_index * tile_rows + lax.broadcasted_iota(jnp.int32, (tile_rows, 1), 0)
    return row >= META0


def _dot_nt(a, b):
    return lax.dot_general(a, b, (((1,), (1,)), ((), ())), preferred_element_type=F32)


def _rope128(x, c, s1, s2):
    return x * c + pltpu.roll(x, 128 - ROPE_DIM // 2, 1) * s1 + pltpu.roll(x, ROPE_DIM // 2, 1) * s2


def _scan_tables(dt_raw, dt_bias, a_log, live, acs_ref, tab_ref):
    nh = 2 * SSD_HEADS
    dtr = dt_raw + dt_bias
    dt = jnp.maximum(dtr, 0.0) + jnp.log1p(jnp.exp(-jnp.abs(dtr)))
    dt = jnp.where(live, dt, 0.0)
    a = dt * (-jnp.exp(a_log))
    ri = lax.broadcasted_iota(jnp.int32, (BLK, BLK), 0)
    ci = lax.broadcasted_iota(jnp.int32, (BLK, BLK), 1)
    tri = (ci <= ri).astype(BF16)
    fwd = ci < SSD_HEADS
    fwd_rows = ri < SSD_HEADS
    acs_chunks, tab_chunks = [], []
    for j in range(TM // BLK):
        rows = slice(j * BLK, (j + 1) * BLK)
        aj, dtj = a[rows], dt[rows]
        pre = sum(_dot(tri, part) for part in _split3(aj))
        total = pre[BLK - 1:BLK, :]
        acs = jnp.where(fwd, pre, total - pre + aj)
        acs_ref[0, rows, :] = acs
        acs_t = acs.T[:nh]
        chunk_sum = jnp.where(fwd_rows[:nh, 0:1], acs_t[:, BLK - 1:BLK], acs_t[:, 0:1])
        tabs = [acs_t, dtj.T[:nh],
                (dtj * jnp.exp(total - acs)).T[:nh],
                jnp.exp(acs_t), jnp.broadcast_to(jnp.exp(chunk_sum), (nh, BLK))]
        for k, t in enumerate(tabs):
            tab_ref[0, j, k] = t
        acs_chunks.append(acs)
        tab_chunks.append(tabs)
    return acs_chunks, tab_chunks


def _inproj_kernel(*refs, nt, ntile):
    lead_ref = refs[0] if ntile > 1 else None
    tile_refs = refs[ntile > 1:ntile + (ntile > 1)]
    (hp_ref, hn_ref, nw_ref, wz_ref, wx_ref, wdt_ref, wwq_ref, wwk_ref, wwv_ref,
     wnq_ref, wnk_ref, wnv_ref, cw_ref, cb_ref, dtb_ref, alog_ref, rc_ref, rs1_ref, rs2_ref, rct_ref, rst_ref,
     z_ref, xt_ref, bc_ref, acs_ref, tab_ref, wq_ref, wk_ref, wv_ref, nq_ref, nk_ref, nv_ref, yf_ref,
     state_ref) = refs[ntile + (ntile > 1):]
    i = pl.program_id(1)

    @pl.when(i == 0)
    def _():
        state_ref[...] = jnp.zeros_like(state_ref)

    nw = nw_ref[...]
    a32 = _rms(_load_tile(tile_refs, lead_ref, i), nw)
    a = a32.astype(BF16)
    head = jnp.where(i > 0, 1.0, 0.0)
    tail = jnp.where(i < nt - 1, 1.0, 0.0)
    a_ext = jnp.concatenate([_rms(hp_ref[0], nw) * head, a32, _rms(hn_ref[0], nw) * tail], axis=0).astype(BF16)
    xe = _dot(a_ext, wx_ref[...])

    def put_blocks(ref, xt):
        for j in range(TM // BLK):
            ref[0, j] = xt[:, j * BLK:(j + 1) * BLK].astype(BF16)

    put_blocks(z_ref, _dot(a, wz_ref[...]).astype(BF16).T)
    dt_raw = _dot(a, wdt_ref[...])
    pad = SSD_CONV // 2
    acc = _shift_rows(xe, -pad) * cw_ref[0:1, :] + cb_ref[...]
    for j in range(1, SSD_CONV):
        acc = acc + _shift_rows(xe, j - pad) * cw_ref[j:j + 1, :]
    xc = acc * (1.0 / (1.0 + jnp.exp(-acc)))
    xt_all = xc[:, :SSD_INNER].astype(BF16).T
    xt_chunks = [xt_all[:, j * BLK:(j + 1) * BLK] for j in range(TM // BLK)]
    for j in range(TM // BLK):
        xt_ref[0, j] = xt_chunks[j]
    bc = xc[:, SSD_INNER:].astype(BF16)
    bc_ref[0] = bc
    acs_chunks, tab_chunks = _scan_tables(dt_raw, dtb_ref[...], alog_ref[...], _live_rows(TM, i),
                                          acs_ref, tab_ref)

    def put_y(j, rows, y):
        yf_ref[0, j, rows, :] = y

    def scan_chunk(j):
        _ssd_scan(lambda j, rows: xt_chunks[j][rows],
                  lambda j, k: bc[j * BLK:(j + 1) * BLK, k * SSD_STATE:(k + 1) * SSD_STATE],
                  lambda j: acs_chunks[j], lambda j, k: tab_chunks[j][k], state_ref, put_y, None,
                  reverse=False, chunks=[j])

    scan_chunk(0)
    scale = HEAD_DIM ** -0.5
    qt = _dot(a, wwq_ref[...]).T
    cos_t, sin_t = rct_ref[...], rst_ref[...]
    half = ROPE_DIM // 2
    parts = []
    for hh in range(WIN_Q_HEADS):
        x1 = qt[hh * HEAD_DIM:hh * HEAD_DIM + half]
        x2 = qt[hh * HEAD_DIM + half:hh * HEAD_DIM + ROPE_DIM]
        parts += [x1 * cos_t - x2 * sin_t, x2 * cos_t + x1 * sin_t, qt[hh * HEAD_DIM + ROPE_DIM:(hh + 1) * HEAD_DIM]]
    put_blocks(wq_ref, jnp.concatenate(parts, axis=0) * scale)
    wk_ref[0] = _rope128(_dot(a, wwk_ref[...]), rc_ref[...], rs1_ref[...], rs2_ref[...]).astype(BF16)
    put_blocks(wv_ref, _dot(a, wwv_ref[...]).astype(BF16).T)
    scan_chunk(1)
    put_blocks(nq_ref, (_dot(a, wnq_ref[...]) * scale).astype(BF16).T)
    nk_ref[0] = _dot(a, wnk_ref[...]).astype(BF16)
    scan_chunk(2)
    nv_ref[0] = _dot(a, wnv_ref[...]).astype(BF16)


def _inproj(src, lp, nw, ws, conv, scan, rope):
    bsz = src[1].shape[0]
    nt = lp // TM
    nb = lp // BLK
    tile = lambda n: pl.BlockSpec((1, TM, n), lambda b, i: (b, i, 0))
    tblk = lambda n: pl.BlockSpec((1, TM // BLK, n, BLK), lambda b, i: (b, i, 0, 0))
    tile_specs, tile_args = _tile_specs(src, TM)
    halo_specs, halo_args = _halo_specs(src, TM)
    full = lambda arr: pl.BlockSpec(arr.shape, lambda b, i: (0,) * arr.ndim)
    rtab = pl.BlockSpec((TM, 128), lambda b, i: (i, 0))
    rtab_t = pl.BlockSpec((ROPE_DIM // 2, TM), lambda b, i: (0, i))
    tok = lambda n, d: (tile(n), jax.ShapeDtypeStruct((bsz, lp, n), d))
    blk = lambda n: (tblk(n), jax.ShapeDtypeStruct((bsz, nb, n, BLK), BF16))
    kvw = WIN_KV_HEADS * HEAD_DIM
    tabs = (pl.BlockSpec((1, TM // BLK, 5, 2 * SSD_HEADS, BLK), lambda b, i: (b, i, 0, 0, 0)),
            jax.ShapeDtypeStruct((bsz, nb, 5, 2 * SSD_HEADS, BLK), F32))
    outs = [blk(SSD_INNER), blk(SSD_INNER), tok(SSD_XBC - SSD_INNER, BF16), tok(128, F32), tabs, blk(WIN_Q_HEADS * HEAD_DIM), tok(kvw, BF16),
            blk(kvw), blk(NA_HEADS * HEAD_DIM), tok(NA_HEADS * HEAD_DIM, BF16), tok(NA_HEADS * HEAD_DIM, BF16),
            (tblk(SSD_INNER), jax.ShapeDtypeStruct((bsz, nb, SSD_INNER, BLK), F32))]
    return pl.pallas_call(
        functools.partial(_inproj_kernel, nt=nt, ntile=len(tile_specs) - (src[0] == 'x')),
        grid=(bsz, nt),
        in_specs=tile_specs + halo_specs + [full(nw)] + [full(w) for w in ws] + [full(c) for c in conv + scan]
        + [rtab, rtab, rtab, rtab_t, rtab_t],
        out_specs=[o[0] for o in outs],
        out_shape=[o[1] for o in outs],
        scratch_shapes=[pltpu.VMEM((SSD_GROUPS, SSD_INNER // SSD_GROUPS, SSD_STATE), F32)],
        compiler_params=_cparams(("parallel", "arbitrary")),
        name="inproj",
    )(*tile_args, *halo_args, nw, *ws, *conv, *scan, *rope)


def _split3(x):
    hi = x.astype(BF16)
    r1 = x - hi.astype(F32)
    mid = r1.astype(BF16)
    lo = (r1 - mid.astype(F32)).astype(BF16)
    return hi, mid, lo


def _ssd_scan(get_x, get_bc, get_acs, get_tab, state_ref, put_y, after_chunk, *, reverse, chunks=None):
    ri = lax.broadcasted_iota(jnp.int32, (BLK, BLK), 0)
    ci = lax.broadcasted_iota(jnp.int32, (BLK, BLK), 1)
    feeds = (ri >= ci) if reverse else (ri <= ci)
    hoff = SSD_HEADS if reverse else 0
    rep = SSD_HEADS // SSD_GROUPS
    if chunks is None:
        chunks = reversed(range(STEP_BLKS)) if reverse else range(STEP_BLKS)
    for j in chunks:
        acs = get_acs(j)
        acs_t, dt_t, dt_out_t, e_in_t, e_chunk = (get_tab(j, k) for k in range(5))
        for g in range(SSD_GROUPS):
            bm = get_bc(j, g)
            cm = get_bc(j, SSD_GROUPS + g)
            cbt = _dot_nt(bm, cm)
            ht = state_ref[g]
            y_off = _dot_nt(ht.astype(BF16), cm)
            x_out, decay = [], []
            for r in range(rep):
                h = g * rep + r
                hl = hoff + h
                rows = slice(h * HEAD_DIM, (h + 1) * HEAD_DIM)
                xt = get_x(j, rows).astype(F32)
                x_in = (xt * dt_t[hl:hl + 1, :]).astype(BF16)
                x_out.append((xt * dt_out_t[hl:hl + 1, :]).astype(BF16))
                decay.append(jnp.broadcast_to(e_chunk[hl:hl + 1, :], (HEAD_DIM, SSD_STATE)))
                diff = acs_t[hl:hl + 1, :] - acs[:, hl:hl + 1]
                lt = (cbt * jnp.exp(jnp.where(feeds, diff, -jnp.inf))).astype(BF16)
                put_y(j, rows, _dot(x_in, lt) + y_off[r * HEAD_DIM:(r + 1) * HEAD_DIM] * e_in_t[hl:hl + 1, :])
            s_new = _dot(jnp.concatenate(x_out, axis=0), bm)
            state_ref[g] = ht * jnp.concatenate(decay, axis=0) + s_new
        if after_chunk is not None:
            after_chunk(j)


def _softmax_pv_t(scores, values, pad_last, sink_logit):
    mx = functools.reduce(jnp.maximum, [jnp.max(s, axis=0, keepdims=True) for s in scores] + [sink_logit])
    ps = [jnp.exp(s - mx) for s in scores]
    den = functools.reduce(jnp.add, [jnp.sum(p, axis=0, keepdims=True) for p in ps]) + jnp.exp(sink_logit - mx)
    pb = [p.astype(BF16) for p in ps]
    pb[-1] = jnp.concatenate([pad_last, pb[-1]], axis=0)
    return _dot(jnp.concatenate(values, axis=1), jnp.concatenate(pb, axis=0)), den


def _block_diag_rows(pieces):
    zero = jnp.zeros_like(pieces[0][0])
    rows = []
    for i in range(len(pieces)):
        row = []
        for j, ps in enumerate(pieces):
            row += [p if i == j else zero for p in ps]
        rows.append(jnp.concatenate(row, axis=1))
    return jnp.concatenate(rows, axis=0)


def _win_block(win_refs, j, tile, nb, put):
    qt_ref, kp_ref, kc_ref, kn_ref, km_ref, vp_ref, vc_ref, vn_ref, vm_ref, sink_ref = win_refs
    width = WIN_Q_HEADS * BLK
    ki = lax.broadcasted_iota(jnp.int32, (BLK, width), 0)
    qi = lax.broadcasted_iota(jnp.int32, (BLK, width), 1) & (BLK - 1)
    rep = WIN_Q_HEADS // WIN_KV_HEADS
    keys = [kp_ref[0]] + [kc_ref[0, jj * BLK:(jj + 1) * BLK, :] for jj in range(STEP_BLKS)] + [kn_ref[0]]
    vals = [vp_ref[0, 0]] + [vc_ref[0, jj] for jj in range(STEP_BLKS)] + [vn_ref[0, 0]]
    kmeta, vmeta = km_ref[0, META0:, :], vm_ref[0, 0]
    pad_last = jnp.zeros((META0, width), BF16)
    n = tile * STEP_BLKS + j
    ok_prev = jnp.logical_and(ki >= qi, n >= 2)
    ok_cur = n >= 1
    ok_next = jnp.logical_and(ki <= qi, n + 1 <= nb - 1)
    heads = [qt_ref[0, j, h * HEAD_DIM:(h + 1) * HEAD_DIM, :] for h in range(WIN_Q_HEADS)]
    qbd = _block_diag_rows([heads[g * rep:(g + 1) * rep] for g in range(WIN_KV_HEADS)])
    s = _dot(jnp.concatenate(keys[j:j + 3] + [kmeta], axis=0), qbd)
    scores = [jnp.where(ok_prev, s[:BLK], -jnp.inf),
              jnp.where(ok_cur, s[BLK:2 * BLK], -jnp.inf),
              jnp.where(ok_next, s[2 * BLK:3 * BLK], -jnp.inf),
              s[3 * BLK:]]
    o, den = _softmax_pv_t(scores, vals[j:j + 3] + [vmeta], pad_last, sink_ref[...])
    o = o / den
    outs = [o[(h // rep) * HEAD_DIM:(h // rep + 1) * HEAD_DIM, h * BLK:(h + 1) * BLK] for h in range(WIN_Q_HEADS)]
    put(j, jnp.concatenate(outs, axis=0).astype(BF16).T)


def _win_specs(qt, k, vt, sink, order):
    nb = qt.shape[1]
    kvw = WIN_KV_HEADS * HEAD_DIM
    assert kvw == BLK and WIN_RADIUS == BLK
    kedge = lambda f: pl.BlockSpec((1, BLK, kvw), lambda b, i: (b, f(order(i)), 0))
    vedge = lambda f: pl.BlockSpec((1, 1, kvw, BLK), lambda b, i: (b, f(order(i)), 0, 0))
    before = lambda t: jnp.maximum(t * STEP_BLKS - 1, 0)
    after = lambda t: jnp.minimum((t + 1) * STEP_BLKS, nb - 1)
    first = lambda t: 0
    specs = [pl.BlockSpec((1, STEP_BLKS, WIN_Q_HEADS * HEAD_DIM, BLK), lambda b, i: (b, order(i), 0, 0)),
             kedge(before), pl.BlockSpec((1, TM, kvw), lambda b, i: (b, order(i), 0)), kedge(after), kedge(first),
             vedge(before), pl.BlockSpec((1, STEP_BLKS, kvw, BLK), lambda b, i: (b, order(i), 0, 0)), vedge(after),
             vedge(first), pl.BlockSpec(sink.shape, lambda b, i: (0, 0))]
    return specs, [qt, k, k, k, k, vt, vt, vt, vt, sink]


NA_QUAD = 4
NA_QW = NA_QUAD * HEAD_DIM
NA_WIN = NA_KR * GRID_W


def _na_bias_tables(rpb):
    c = np.arange(GRID_W)[None, :]
    kc = np.arange(GRID_W)[:, None]
    cs = np.clip(c - NA_KC // 2, 0, GRID_W - NA_KC)
    ok = (kc >= cs) & (kc < cs + NA_KC)
    pick = jnp.asarray((kc - c + NA_KC - 1)[None] == np.arange(2 * NA_KC - 1)[:, None, None], F32)
    t = jnp.einsum('hed,dkc->hekc', rpb.astype(F32), pick, precision=lax.Precision.HIGHEST)
    t = jnp.where(jnp.asarray(ok)[None, None], t, -jnp.inf)

    def pack(x):
        h, e = x.shape[:2]
        x = x.reshape(h // NA_QUAD, NA_QUAD, e, GRID_W, GRID_W)
        return x.transpose(0, 2, 3, 1, 4).reshape(h // NA_QUAD, e, GRID_W, NA_QW)

    meta = jnp.broadcast_to(t[:, NA_KR - 1:, :, 0:1], (t.shape[0], NA_KR, GRID_W, GRID_W))
    return pack(t), pack(meta)


def _na_kernel(qt_ref, k_ref, v_ref, bias_ref, mbias_ref, mb_ref, o_ref, *, rows):
    width = NA_HEADS * HEAD_DIM
    first_half = lax.broadcasted_iota(jnp.int32, (width, BLK), 1) < GRID_W
    ri = lax.broadcasted_iota(jnp.int32, (NA_QW, NA_QW), 0)
    ci = lax.broadcasted_iota(jnp.int32, (NA_QW, NA_QW), 1)
    diag = lax.shift_right_logical(ri, 6) == lax.shift_right_logical(ci, 6)
    head_of_lane = lax.shift_right_logical(lax.broadcasted_iota(jnp.int32, (GRID_W, NA_QW), 1), 6)
    pad_meta = jnp.zeros((META0, NA_QW), F32)

    def scores(j, half, u, meta):
        qf = qt_ref[0, j].astype(F32)
        qsw = pltpu.roll(qf, GRID_W, 1)
        dup = jnp.where(first_half, qf, qsw) if half == 0 else jnp.where(first_half, qsw, qf)
        if meta:
            start, e0 = LEAD, None
        else:
            r = (pl.program_id(1) * STEP_BLKS + j - 1) * (BLK // GRID_W) + half
            rs = jnp.clip(r - NA_KR // 2, 0, rows - NA_KR)
            start = pl.multiple_of(LEAD + rs * GRID_W, GRID_W)
            e0 = rs - r + (NA_KR - 1)
        cols = slice(u * NA_QW, (u + 1) * NA_QW)
        base = dup[u * NA_QW:(u + 1) * NA_QW]
        qbd = jnp.where(diag, jnp.concatenate([base, base], axis=1), 0.0).astype(BF16)
        keys = jnp.concatenate([k_ref[0, pl.ds(start, NA_WIN), cols], k_ref[0, META0:LEAD, cols]], axis=0)
        return _dot(keys, qbd), (j, half, u, start, e0, cols)

    def finish(s, unit):
        j, half, u, start, e0, cols = unit
        if e0 is None:
            bias = jnp.concatenate([mbias_ref[u, i] for i in range(NA_KR)], axis=0)
        else:
            bias = jnp.concatenate([bias_ref[u, e0 + i] for i in range(NA_KR)], axis=0)
        sw = s[:NA_WIN] + bias
        sm = s[NA_WIN:] + mb_ref[u]
        mx = jnp.maximum(jnp.max(sw, axis=0, keepdims=True), jnp.max(sm, axis=0, keepdims=True))
        pw, pm = jnp.exp(sw - mx), jnp.exp(sm - mx)
        rden = 1.0 / (jnp.sum(pw, axis=0, keepdims=True) + jnp.sum(pm, axis=0, keepdims=True))
        p_t = jnp.concatenate([pw * rden, pad_meta, pm * rden], axis=0).astype(BF16).T
        vals = jnp.concatenate([v_ref[0, pl.ds(start, NA_WIN), cols], v_ref[0, 0:LEAD, cols]], axis=0)
        o = _dot(p_t, vals)
        out = o[(NA_QUAD - 1) * GRID_W:]
        for hq in range(NA_QUAD - 2, -1, -1):
            out = jnp.where(head_of_lane == hq, o[hq * GRID_W:(hq + 1) * GRID_W], out)
        o_ref[0, pl.ds(j * BLK + half * GRID_W, GRID_W), cols] = out.astype(BF16)

    def run(units):
        pending = scores(*units[0])
        for nxt in units[1:]:
            ahead = scores(*nxt)
            finish(*pending)
            pending = ahead
        finish(*pending)

    quads = range(NA_HEADS // NA_QUAD)

    @pl.when(pl.program_id(1) == 0)
    def _():
        o_ref[0, 0:GRID_W, :] = jnp.zeros((GRID_W, width), BF16)
        run([(0, 1, u, True) for u in quads])

    @pl.when(pl.program_id(1) > 0)
    def _():
        run([(0, half, u, False) for half in range(2) for u in quads])

    run([(j, half, u, False) for j in range(1, STEP_BLKS) for half in range(2) for u in quads])


def _na_attention(qt, k, v, bias, mbias, mb):
    bsz, nb, width, _ = qt.shape
    lp = nb * BLK
    rows = (lp - LEAD) // GRID_W
    assert rows >= NA_KR and N_META <= GRID_W and LEAD - GRID_W <= META0
    qblk = pl.BlockSpec((1, STEP_BLKS, width, BLK), lambda b, n: (b, n, 0, 0))
    seq = pl.BlockSpec((1, lp, width), lambda b, n: (b, 0, 0), pipeline_mode=pl.Buffered(1))
    return pl.pallas_call(
        functools.partial(_na_kernel, rows=rows),
        grid=(bsz, nb // STEP_BLKS),
        in_specs=[qblk, seq, seq, _resident(bias), _resident(mbias), _resident(mb)],
        out_specs=pl.BlockSpec((1, TM, width), lambda b, n: (b, n, 0)),
        out_shape=jax.ShapeDtypeStruct((bsz, lp, width), BF16),
        compiler_params=_cparams(("parallel", "arbitrary")),
        name="na_attn",
    )(qt, k, v, bias, mbias, mb)


def _mixout_kernel(*refs, ns, ntile):
    lead_ref = refs[0] if ntile > 1 else None
    tile_refs = refs[ntile > 1:ntile + (ntile > 1)]
    rest = refs[ntile + (ntile > 1):]
    win_refs, rest = rest[:N_WIN_REFS], rest[N_WIN_REFS:]
    (xt_ref, bc_ref, acs_ref, tab_ref, zt_ref, yf_ref, yn_ref, dsk_ref, nws_ref, w1_ref, w2_ref, w3_ref,
     nw_ref, o_ref, state_ref, ybuf_ref, ys_ref, yw_ref) = rest
    i = ns - 1 - pl.program_id(1)

    @pl.when(pl.program_id(1) == 0)
    def _():
        state_ref[...] = jnp.zeros_like(state_ref)

    def put_y(j, rows, y):
        ybuf_ref[rows, :] = y

    def finish(j):
        y = ybuf_ref[...] + yf_ref[0, j] + dsk_ref[...] * xt_ref[0, j].astype(F32)
        z = zt_ref[0, j].astype(F32)
        y = y * (z * (1.0 / (1.0 + jnp.exp(-z))))
        y = y * lax.rsqrt(jnp.mean(y * y, axis=0, keepdims=True) + EPS) * nws_ref[...]
        ys_ref[j * BLK:(j + 1) * BLK, :] = y.astype(BF16).T

    def scan_chunk(j):
        _ssd_scan(lambda j, rows: xt_ref[0, j, rows, :],
                  lambda j, k: bc_ref[0, j * BLK:(j + 1) * BLK, k * SSD_STATE:(k + 1) * SSD_STATE],
                  lambda j: acs_ref[0, j * BLK:(j + 1) * BLK, :],
                  lambda j, k: tab_ref[0, j, k], state_ref, put_y, finish, reverse=True, chunks=[j])

    def put_win(j, y):
        yw_ref[j * BLK:(j + 1) * BLK, :] = y

    for j in reversed(range(STEP_BLKS)):
        scan_chunk(j)
        _win_block(win_refs, j, i, ns * STEP_BLKS, put_win)
        if j == STEP_BLKS - 1:
            mix = _dot(yn_ref[0], w3_ref[...])
    mix = mix + _dot(yw_ref[...], w2_ref[...]) + _dot(ys_ref[...], w1_ref[...])
    out = _load_tile(tile_refs, lead_ref, i) + _rms(mix, nw_ref[...])
    o_ref[0] = jnp.where(_live_rows(TM, i), out, 0.0)


N_WIN_REFS = 10


def _mixout(src, win_args, xt, bc, acs, tab, zt, y_fwd, y_na, dsk, nws, w1, w2, w3, nw):
    bsz, nc = xt.shape[:2]
    ns = nc // STEP_BLKS
    order = lambda c: ns - 1 - c
    tok = lambda n: pl.BlockSpec((1, TM, n), lambda b, c: (b, order(c), 0))
    feat = pl.BlockSpec((1, STEP_BLKS, SSD_INNER, BLK), lambda b, c: (b, order(c), 0, 0))
    tabs = pl.BlockSpec((1, STEP_BLKS) + tab.shape[2:], lambda b, c: (b, order(c), 0, 0, 0))
    tile_specs, tile_args = _tile_specs(src, TM, order)
    win_specs, win_ops = _win_specs(*win_args, order)
    assert len(win_specs) == N_WIN_REFS
    consts = [dsk, nws, w1, w2, w3, nw]
    return pl.pallas_call(
        functools.partial(_mixout_kernel, ns=ns, ntile=len(tile_specs) - (src[0] == 'x')),
        grid=(bsz, ns),
        in_specs=tile_specs + win_specs + [feat, tok(bc.shape[-1]), tok(128), tabs, feat, feat,
                                           tok(y_na.shape[-1])] + [_resident(c) for c in consts],
        out_specs=tok(D_MODEL),
        out_shape=jax.ShapeDtypeStruct((bsz, nc * BLK, D_MODEL), F32),
        scratch_shapes=[pltpu.VMEM((SSD_GROUPS, SSD_INNER // SSD_GROUPS, SSD_STATE), F32),
                        pltpu.VMEM((SSD_INNER, BLK), F32), pltpu.VMEM((TM, SSD_INNER), BF16),
                        pltpu.VMEM((TM, WIN_Q_HEADS * HEAD_DIM), BF16)],
        compiler_params=_cparams(("parallel", "arbitrary")),
        name="mixout",
    )(*tile_args, *win_ops, xt, bc, acs, tab, zt, y_fwd, y_na, *consts)


def _ffn_kernel(*refs, nt, ntile, padded):
    tile_refs = refs[:ntile]
    hp_ref, hn_ref, nw1_ref, wup_ref, cw_ref, cb_ref, wdn_ref, nw2_ref, o_ref, act_ref = refs[ntile:]
    i = pl.program_id(1)
    rows = act_ref.shape[0]
    nw1 = nw1_ref[...]
    hc = _load_tile(tile_refs, None, i)
    f = jnp.concatenate([_rms(hp_ref[0], nw1), _rms(hc, nw1), _rms(hn_ref[0], nw1)], axis=0).astype(BF16)
    tail = jnp.where(i < nt - 1, 1.0, 0.0)
    nch = D_FF // FF_CHUNK

    def conv(g, c):
        g = jnp.concatenate([g[:HALO + rows], g[HALO + rows:] * tail], axis=0)
        cols = slice(c * FF_CHUNK, (c + 1) * FF_CHUNK)
        out = _shift_rows(g, -1) * cw_ref[0:1, cols] + cb_ref[:, cols]
        out = out + g[HALO:HALO + rows] * cw_ref[1:2, cols]
        return out + _shift_rows(g, 1) * cw_ref[2:3, cols]

    for c in range(nch):
        gate = conv(_dot(f, wup_ref[:, c * FF_CHUNK:(c + 1) * FF_CHUNK]), c)
        up = conv(_dot(f, wup_ref[:, D_FF + c * FF_CHUNK:D_FF + (c + 1) * FF_CHUNK]), nch + c)
        th = jnp.tanh(gate * (GELU_K + (GELU_K * GELU_C) * (gate * gate)))
        act_ref[:, c * FF_CHUNK:(c + 1) * FF_CHUNK] = (gate * (0.5 * th + 0.5) * up).astype(BF16)
    out = hc + _rms(_dot(act_ref[...], wdn_ref[...]), nw2_ref[...])
    o_ref[0] = jnp.where(_live_rows(rows, i), out, 0.0) if padded else out


def _ffn(h, nw1, wup, cw, cb, wdn, nw2, *, last):
    bsz, lp, _ = h.shape
    full = _resident
    if last:
        n_out = lp - LEAD
        tm = _tile_rows(n_out, FFN_OUT_TILE_CAP, BLK)
        h4 = h.reshape(bsz, lp // BLK, BLK, D_MODEL)
        k = tm // BLK
        tile_specs = [pl.BlockSpec((1, 1, BLK, D_MODEL), lambda b, i, j=j: (b, LEAD // BLK + i * k + j, 0, 0))
                      for j in range(k)]
        tile_args = [h4] * k
        off = LEAD // HALO
    else:
        tm, n_out = TM, lp
        tile_specs, tile_args = _tile_specs(('h', h), tm)
        off = 0
    nt = n_out // tm
    hb = tm // HALO
    last_blk = lp // HALO - 1
    prev = pl.BlockSpec((1, HALO, D_MODEL), lambda b, i: (b, jnp.clip(off + i * hb - 1, 0, last_blk), 0))
    nxt = pl.BlockSpec((1, HALO, D_MODEL), lambda b, i: (b, jnp.clip(off + (i + 1) * hb, 0, last_blk), 0))
    return pl.pallas_call(
        functools.partial(_ffn_kernel, nt=nt, ntile=len(tile_specs), padded=not last),
        grid=(bsz, nt),
        in_specs=tile_specs + [prev, nxt, full(nw1), full(wup), full(cw), full(cb), full(wdn), full(nw2)],
        out_specs=pl.BlockSpec((1, tm, D_MODEL), lambda b, i: (b, i, 0)),
        out_shape=jax.ShapeDtypeStruct((bsz, n_out, D_MODEL), F32),
        scratch_shapes=[pltpu.VMEM((tm, D_FF), BF16)],
        compiler_params=_cparams(("parallel", "parallel")),
        name="ffn",
    )(*tile_args, h, h, nw1, wup, cw, cb, wdn, nw2)


def _pad_lanes(x, n):
    return jnp.pad(x, [(0, 0)] * (x.ndim - 1) + [(0, n - x.shape[-1])])


def _rope_tables(lp):
    half = ROPE_DIM // 2
    pos = jnp.maximum(jnp.arange(lp) - META0, 0).astype(F32)
    inv = jnp.power(ROPE_THETA, -jnp.arange(half, dtype=F32) / half)
    ang = pos[:, None] * inv[None, :]
    cos, sin = jnp.cos(ang), jnp.sin(ang)
    zeros, ones = jnp.zeros_like(cos), jnp.ones((lp, HEAD_DIM - ROPE_DIM), F32)
    rest = jnp.zeros((lp, HEAD_DIM - ROPE_DIM), F32)
    c = jnp.concatenate([cos, cos, ones], axis=1)
    s1 = jnp.concatenate([-sin, zeros, rest], axis=1)
    s2 = jnp.concatenate([zeros, sin, rest], axis=1)
    return tuple(jnp.tile(t, (1, 128 // HEAD_DIM)) for t in (c, s1, s2)) + (cos.T, sin.T)


def _layer_params(i, p):
    row = lambda v: v.reshape(1, -1).astype(F32)
    lanes = lambda v: jnp.broadcast_to(v.astype(F32)[:, None], (v.shape[0], BLK))
    sizes = [SSD_INNER, SSD_XBC, 2 * SSD_HEADS, WIN_Q_HEADS * HEAD_DIM, WIN_KV_HEADS * HEAD_DIM,
             WIN_KV_HEADS * HEAD_DIM, NA_HEADS * HEAD_DIM, NA_HEADS * HEAD_DIM, NA_HEADS * HEAD_DIM]
    w_in = p['w_in'][i].astype(BF16)
    ws = jnp.split(w_in, np.cumsum(sizes)[:-1].tolist(), axis=1)
    ws[2] = _pad_lanes(ws[2], 128)
    w_out = p['w_out'][i].astype(BF16)
    return dict(
        norm_mix_pre=row(p['norm_mix_pre'][i]),
        w_in=ws,
        ssd_conv=(p['ssd_conv_w'][i].astype(F32), row(p['ssd_conv_b'][i])),
        ssd_consts=(_pad_lanes(row(p['ssd_dt_bias'][i]), 128), _pad_lanes(row(p['ssd_a_log'][i]), 128)),
        ssd_d=lanes(jnp.repeat(p['ssd_d'][i], HEAD_DIM)),
        ssd_norm_w=lanes(p['ssd_norm_w'][i]),
        win_sink=row(jnp.repeat(p['win_sink'][i], BLK)),
        na_bias=_na_bias_tables(p['na_rpb'][i]),
        na_meta_bias=jnp.broadcast_to(p['na_meta_bias'][i].astype(F32).T.reshape(N_META, NA_HEADS // NA_QUAD, NA_QUAD, 1),
                                      (N_META, NA_HEADS // NA_QUAD, NA_QUAD, GRID_W)).transpose(1, 0, 2, 3).reshape(
            NA_HEADS // NA_QUAD, N_META, NA_QW),
        w_out=(w_out[:SSD_INNER], w_out[SSD_INNER:SSD_INNER + WIN_Q_HEADS * HEAD_DIM],
               w_out[SSD_INNER + WIN_Q_HEADS * HEAD_DIM:]),
        norm_mix_post=row(p['norm_mix_post'][i]),
        norm_ffn_pre=row(p['norm_ffn_pre'][i]),
        ffn_w_up=p['ffn_w_up'][i].astype(BF16),
        ffn_conv_w=p['ffn_conv_w'][i].astype(F32),
        ffn_conv_b=row(p['ffn_conv_b'][i]),
        ffn_w_down=p['ffn_w_down'][i].astype(BF16),
        norm_ffn_post=row(p['norm_ffn_post'][i]),
    )


def _encode(x, meta_tokens, layers):
    bsz, n_tok, _ = x.shape
    assert (n_tok + LEAD) % TM == 0 and n_tok // GRID_W >= NA_KR
    lp = LEAD + n_tok
    lead = jnp.concatenate([jnp.zeros((META0, D_MODEL), F32), meta_tokens.astype(F32)], axis=0)
    src = ('x', x.astype(F32), lead)
    rope = _rope_tables(lp)
    for li, lw in enumerate(layers):
        zt, xt, bc, acs, tab, wq, wk, wv, nq, nk, nv, y_fwd = _inproj(src, lp, lw['norm_mix_pre'], lw['w_in'],
                                                                      lw['ssd_conv'], lw['ssd_consts'], rope)
        y_na = _na_attention(nq, nk, nv, *lw['na_bias'], lw['na_meta_bias'])
        h = _mixout(src, (wq, wk, wv, lw['win_sink']), xt, bc, acs, tab, zt, y_fwd, y_na, lw['ssd_d'],
                    lw['ssd_norm_w'], *lw['w_out'], lw['norm_mix_post'])
        h = _ffn(h, lw['norm_ffn_pre'], lw['ffn_w_up'], lw['ffn_conv_w'], lw['ffn_conv_b'],
                 lw['ffn_w_down'], lw['norm_ffn_post'], last=li == len(layers) - 1)
        src = ('h', h)
    return h


def kernel(x_prompt, x_sample, meta_tokens, norm_mix_pre, norm_mix_post, w_in, ssd_conv_w, ssd_conv_b,
           ssd_dt_bias, ssd_a_log, ssd_d, ssd_norm_w, win_sink, na_rpb, na_meta_bias, w_out, norm_ffn_pre,
           norm_ffn_post, ffn_w_up, ffn_conv_w, ffn_conv_b, ffn_w_down):
    p = dict(norm_mix_pre=norm_mix_pre, norm_mix_post=norm_mix_post, w_in=w_in, ssd_conv_w=ssd_conv_w,
             ssd_conv_b=ssd_conv_b, ssd_dt_bias=ssd_dt_bias, ssd_a_log=ssd_a_log, ssd_d=ssd_d,
             ssd_norm_w=ssd_norm_w, win_sink=win_sink, na_rpb=na_rpb, na_meta_bias=na_meta_bias, w_out=w_out,
             norm_ffn_pre=norm_ffn_pre, norm_ffn_post=norm_ffn_post, ffn_w_up=ffn_w_up, ffn_conv_w=ffn_conv_w,
             ffn_conv_b=ffn_conv_b, ffn_w_down=ffn_w_down)
    layers = [_layer_params(i, p) for i in range(w_in.shape[0])]
    return (_encode(x_prompt, meta_tokens, layers), _encode(x_sample, meta_tokens, layers))
```

```python
import functools

import jax
import jax.numpy as jnp
import numpy as np
from jax import lax
from jax.experimental import pallas as pl
from jax.experimental.pallas import tpu as pltpu

F32 = jnp.float32
BF16 = jnp.bfloat16

D_MODEL = 1024
N_META = 16
GRID_W = 64
HEAD_DIM = 64

SSD_HEADS = 16
SSD_INNER = SSD_HEADS * HEAD_DIM
SSD_GROUPS = 2
SSD_STATE = 128
SSD_XBC = SSD_INNER + 2 * SSD_GROUPS * SSD_STATE
SSD_CONV = 5

WIN_Q_HEADS = 8
WIN_KV_HEADS = 2
WIN_RADIUS = 128
ROPE_THETA = 500000.0
ROPE_DIM = HEAD_DIM // 4

NA_HEADS = 8
NA_KR = 8
NA_KC = 16

D_FF = 2816
EPS = 1e-6
GELU_K = float(np.sqrt(2.0 / np.pi))
GELU_C = 0.044715

BLK = 128
LEAD = BLK
META0 = LEAD - N_META
HALO = 8
TM = 384
FFN_OUT_TILE_CAP = 720
FF_CHUNK = 256
STEP_BLKS = TM // BLK
VMEM_LIMIT = 56 * 1024 * 1024


def _tile_rows(n, cap, unit=HALO):
    return max(t for t in range(unit, cap + 1, unit) if n % t == 0)


def _resident(arr):
    return pl.BlockSpec(arr.shape, lambda *_: (0,) * arr.ndim, pipeline_mode=pl.Buffered(1))


def _cparams(sem):
    return pltpu.CompilerParams(dimension_semantics=sem, vmem_limit_bytes=VMEM_LIMIT)


def _rms(x, w):
    return x * lax.rsqrt(jnp.mean(x * x, axis=-1, keepdims=True) + EPS) * w


def _dot(a, b):
    return jnp.dot(a, b, preferred_element_type=F32)


def _shift_rows(x, d):
    return pltpu.roll(x, (-d) % x.shape[0], 0)[HALO:x.shape[0] - HALO]


def _load_tile(refs, lead_ref, i):
    if len(refs) == 1:
        return refs[0][0]
    blocks = [r[0, 0] for r in refs]
    if lead_ref is not None:
        blocks[0] = jnp.where(i == 0, lead_ref[...], blocks[0])
    return jnp.concatenate(blocks, axis=0)


def _tile_specs(src, tm, order=lambda i: i):
    if src[0] == 'h':
        return [pl.BlockSpec((1, tm, D_MODEL), lambda b, i: (b, order(i), 0))], [src[1]]
    x, lead = src[1], src[2]
    x4 = x.reshape(x.shape[0], x.shape[1] // BLK, BLK, D_MODEL)
    k = tm // BLK
    specs = [pl.BlockSpec((1, 1, BLK, D_MODEL),
                          lambda b, i, j=j: (b, jnp.maximum(order(i) * k + j - LEAD // BLK, 0), 0, 0))
             for j in range(k)]
    return [pl.BlockSpec(lead.shape, lambda b, i: (0, 0))] + specs, [lead] + [x4] * k


def _halo_specs(src, tm):
    arr = src[1]
    off = 0 if src[0] == 'h' else LEAD // HALO
    last = arr.shape[1] // HALO - 1
    hb = tm // HALO
    prev = pl.BlockSpec((1, HALO, D_MODEL), lambda b, i: (b, jnp.clip(i * hb - 1 - off, 0, last), 0))
    nxt = pl.BlockSpec((1, HALO, D_MODEL), lambda b, i: (b, jnp.clip((i + 1) * hb - off, 0, last), 0))
    return [prev, nxt], [arr, arr]


def _live_rows(tile_rows, tile_index):
    row = tile_index * tile_rows + lax.broadcasted_iota(jnp.int32, (tile_rows, 1), 0)
    return row >= META0


def _dot_nt(a, b):
    return lax.dot_general(a, b, (((1,), (1,)), ((), ())), preferred_element_type=F32)


def _rope128(x, c, s1, s2):
    return x * c + pltpu.roll(x, 128 - ROPE_DIM // 2, 1) * s1 + pltpu.roll(x, ROPE_DIM // 2, 1) * s2


def _scan_tables(dt_raw, dt_bias, a_log, live, acs_ref, tab_ref):
    nh = 2 * SSD_HEADS
    dtr = dt_raw + dt_bias
    dt = jnp.maximum(dtr, 0.0) + jnp.log1p(jnp.exp(-jnp.abs(dtr)))
    dt = jnp.where(live, dt, 0.0)
    a = dt * (-jnp.exp(a_log))
    ri = lax.broadcasted_iota(jnp.int32, (BLK, BLK), 0)
    ci = lax.broadcasted_iota(jnp.int32, (BLK, BLK), 1)
    tri = (ci <= ri).astype(BF16)
    fwd = ci < SSD_HEADS
    fwd_rows = ri < SSD_HEADS
    acs_chunks, tab_chunks = [], []
    for j in range(TM // BLK):
        rows = slice(j * BLK, (j + 1) * BLK)
        aj, dtj = a[rows], dt[rows]
        pre = sum(_dot(tri, part) for part in _split3(aj))
        total = pre[BLK - 1:BLK, :]
        acs = jnp.where(fwd, pre, total - pre + aj)
        acs_ref[0, rows, :] = acs
        acs_t = acs.T[:nh]
        chunk_sum = jnp.where(fwd_rows[:nh, 0:1], acs_t[:, BLK - 1:BLK], acs_t[:, 0:1])
        tabs = [acs_t, dtj.T[:nh],
                (dtj * jnp.exp(total - acs)).T[:nh],
                jnp.exp(acs_t), jnp.broadcast_to(jnp.exp(chunk_sum), (nh, BLK))]
        for k, t in enumerate(tabs):
            tab_ref[0, j, k] = t
        acs_chunks.append(acs)
        tab_chunks.append(tabs)
    return acs_chunks, tab_chunks


def _inproj_kernel(*refs, nt, ntile):
    lead_ref = refs[0] if ntile > 1 else None
    tile_refs = refs[ntile > 1:ntile + (ntile > 1)]
    (hp_ref, hn_ref, nw_ref, wz_ref, wx_ref, wdt_ref, wwq_ref, wwk_ref, wwv_ref,
     wnq_ref, wnk_ref, wnv_ref, cw_ref, cb_ref, dtb_ref, alog_ref, rc_ref, rs1_ref, rs2_ref, rct_ref, rst_ref,
     z_ref, xt_ref, bc_ref, acs_ref, tab_ref, wq_ref, wk_ref, wv_ref, nq_ref, nk_ref, nv_ref, yf_ref,
     state_ref) = refs[ntile + (ntile > 1):]
    i = pl.program_id(1)

    @pl.when(i == 0)
    def _():
        state_ref[...] = jnp.zeros_like(state_ref)

    nw = nw_ref[...]
    a32 = _rms(_load_tile(tile_refs, lead_ref, i), nw)
    a = a32.astype(BF16)
    head = jnp.where(i > 0, 1.0, 0.0)
    tail = jnp.where(i < nt - 1, 1.0, 0.0)
    a_ext = jnp.concatenate([_rms(hp_ref[0], nw) * head, a32, _rms(hn_ref[0], nw) * tail], axis=0).astype(BF16)
    xe = _dot(a_ext, wx_ref[...])

    def put_blocks(ref, xt):
        for j in range(TM // BLK):
            ref[0, j] = xt[:, j * BLK:(j + 1) * BLK].astype(BF16)

    put_blocks(z_ref, _dot(a, wz_ref[...]).astype(BF16).T)
    dt_raw = _dot(a, wdt_ref[...])
    pad = SSD_CONV // 2
    acc = _shift_rows(xe, -pad) * cw_ref[0:1, :] + cb_ref[...]
    for j in range(1, SSD_CONV):
        acc = acc + _shift_rows(xe, j - pad) * cw_ref[j:j + 1, :]
    xc = acc * (1.0 / (1.0 + jnp.exp(-acc)))
    xt_all = xc[:, :SSD_INNER].astype(BF16).T
    xt_chunks = [xt_all[:, j * BLK:(j + 1) * BLK] for j in range(TM // BLK)]
    for j in range(TM // BLK):
        xt_ref[0, j] = xt_chunks[j]
    bc = xc[:, SSD_INNER:].astype(BF16)
    bc_ref[0] = bc
    acs_chunks, tab_chunks = _scan_tables(dt_raw, dtb_ref[...], alog_ref[...], _live_rows(TM, i),
                                          acs_ref, tab_ref)

    def put_y(j, rows, y):
        yf_ref[0, j, rows, :] = y

    def scan_chunk(j):
        _ssd_scan(lambda j, rows: xt_chunks[j][rows],
                  lambda j, k: bc[j * BLK:(j + 1) * BLK, k * SSD_STATE:(k + 1) * SSD_STATE],
                  lambda j: acs_chunks[j], lambda j, k: tab_chunks[j][k], state_ref, put_y, None,
                  reverse=False, chunks=[j])

    scan_chunk(0)
    scale = HEAD_DIM ** -0.5
    qt = _dot(a, wwq_ref[...]).T
    cos_t, sin_t = rct_ref[...], rst_ref[...]
    half = ROPE_DIM // 2
    parts = []
    for hh in range(WIN_Q_HEADS):
        x1 = qt[hh * HEAD_DIM:hh * HEAD_DIM + half]
        x2 = qt[hh * HEAD_DIM + half:hh * HEAD_DIM + ROPE_DIM]
        parts += [x1 * cos_t - x2 * sin_t, x2 * cos_t + x1 * sin_t, qt[hh * HEAD_DIM + ROPE_DIM:(hh + 1) * HEAD_DIM]]
    put_blocks(wq_ref, jnp.concatenate(parts, axis=0) * scale)
    wk_ref[0] = _rope128(_dot(a, wwk_ref[...]), rc_ref[...], rs1_ref[...], rs2_ref[...]).astype(BF16)
    put_blocks(wv_ref, _dot(a, wwv_ref[...]).astype(BF16).T)
    scan_chunk(1)
    put_blocks(nq_ref, (_dot(a, wnq_ref[...]) * scale).astype(BF16).T)
    nk_ref[0] = _dot(a, wnk_ref[...]).astype(BF16)
    scan_chunk(2)
    nv_ref[0] = _dot(a, wnv_ref[...]).astype(BF16)


def _inproj(src, lp, nw, ws, conv, scan, rope):
    bsz = src[1].shape[0]
    nt = lp // TM
    nb = lp // BLK
    tile = lambda n: pl.BlockSpec((1, TM, n), lambda b, i: (b, i, 0))
    tblk = lambda n: pl.BlockSpec((1, TM // BLK, n, BLK), lambda b, i: (b, i, 0, 0))
    tile_specs, tile_args = _tile_specs(src, TM)
    halo_specs, halo_args = _halo_specs(src, TM)
    full = lambda arr: pl.BlockSpec(arr.shape, lambda b, i: (0,) * arr.ndim)
    rtab = pl.BlockSpec((TM, 128), lambda b, i: (i, 0))
    rtab_t = pl.BlockSpec((ROPE_DIM // 2, TM), lambda b, i: (0, i))
    tok = lambda n, d: (tile(n), jax.ShapeDtypeStruct((bsz, lp, n), d))
    blk = lambda n: (tblk(n), jax.ShapeDtypeStruct((bsz, nb, n, BLK), BF16))
    kvw = WIN_KV_HEADS * HEAD_DIM
    tabs = (pl.BlockSpec((1, TM // BLK, 5, 2 * SSD_HEADS, BLK), lambda b, i: (b, i, 0, 0, 0)),
            jax.ShapeDtypeStruct((bsz, nb, 5, 2 * SSD_HEADS, BLK), F32))
    outs = [blk(SSD_INNER), blk(SSD_INNER), tok(SSD_XBC - SSD_INNER, BF16), tok(128, F32), tabs, blk(WIN_Q_HEADS * HEAD_DIM), tok(kvw, BF16),
            blk(kvw), blk(NA_HEADS * HEAD_DIM), tok(NA_HEADS * HEAD_DIM, BF16), tok(NA_HEADS * HEAD_DIM, BF16),
            (tblk(SSD_INNER), jax.ShapeDtypeStruct((bsz, nb, SSD_INNER, BLK), F32))]
    return pl.pallas_call(
        functools.partial(_inproj_kernel, nt=nt, ntile=len(tile_specs) - (src[0] == 'x')),
        grid=(bsz, nt),
        in_specs=tile_specs + halo_specs + [full(nw)] + [full(w) for w in ws] + [full(c) for c in conv + scan]
        + [rtab, rtab, rtab, rtab_t, rtab_t],
        out_specs=[o[0] for o in outs],
        out_shape=[o[1] for o in outs],
        scratch_shapes=[pltpu.VMEM((SSD_GROUPS, SSD_INNER // SSD_GROUPS, SSD_STATE), F32)],
        compiler_params=_cparams(("parallel", "arbitrary")),
        name="inproj",
    )(*tile_args, *halo_args, nw, *ws, *conv, *scan, *rope)


def _split3(x):
    hi = x.astype(BF16)
    r1 = x - hi.astype(F32)
    mid = r1.astype(BF16)
    lo = (r1 - mid.astype(F32)).astype(BF16)
    return hi, mid, lo


def _ssd_scan(get_x, get_bc, get_acs, get_tab, state_ref, put_y, after_chunk, *, reverse, chunks=None):
    ri = lax.broadcasted_iota(jnp.int32, (BLK, BLK), 0)
    ci = lax.broadcasted_iota(jnp.int32, (BLK, BLK), 1)
    feeds = (ri >= ci) if reverse else (ri <= ci)
    hoff = SSD_HEADS if reverse else 0
    rep = SSD_HEADS // SSD_GROUPS
    if chunks is None:
        chunks = reversed(range(STEP_BLKS)) if reverse else range(STEP_BLKS)
    for j in chunks:
        acs = get_acs(j)
        acs_t, dt_t, dt_out_t, e_in_t, e_chunk = (get_tab(j, k) for k in range(5))
        for g in range(SSD_GROUPS):
            bm = get_bc(j, g)
            cm = get_bc(j, SSD_GROUPS + g)
            cbt = _dot_nt(bm, cm)
            ht = state_ref[g]
            y_off = _dot_nt(ht.astype(BF16), cm)
            x_out, decay = [], []
            for r in range(rep):
                h = g * rep + r
                hl = hoff + h
                rows = slice(h * HEAD_DIM, (h + 1) * HEAD_DIM)
                xt = get_x(j, rows).astype(F32)
                x_in = (xt * dt_t[hl:hl + 1, :]).astype(BF16)
                x_out.append((xt * dt_out_t[hl:hl + 1, :]).astype(BF16))
                decay.append(jnp.broadcast_to(e_chunk[hl:hl + 1, :], (HEAD_DIM, SSD_STATE)))
                diff = acs_t[hl:hl + 1, :] - acs[:, hl:hl + 1]
                lt = (cbt * jnp.exp(jnp.where(feeds, diff, -jnp.inf))).astype(BF16)
                put_y(j, rows, _dot(x_in, lt) + y_off[r * HEAD_DIM:(r + 1) * HEAD_DIM] * e_in_t[hl:hl + 1, :])
            s_new = _dot(jnp.concatenate(x_out, axis=0), bm)
            state_ref[g] = ht * jnp.concatenate(decay, axis=0) + s_new
        if after_chunk is not None:
            after_chunk(j)


def _softmax_pv_t(scores, values, pad_last, sink_logit):
    mx = functools.reduce(jnp.maximum, [jnp.max(s, axis=0, keepdims=True) for s in scores] + [sink_logit])
    ps = [jnp.exp(s - mx) for s in scores]
    den = functools.reduce(jnp.add, [jnp.sum(p, axis=0, keepdims=True) for p in ps]) + jnp.exp(sink_logit - mx)
    pb = [p.astype(BF16) for p in ps]
    pb[-1] = jnp.concatenate([pad_last, pb[-1]], axis=0)
    return _dot(jnp.concatenate(values, axis=1), jnp.concatenate(pb, axis=0)), den


def _block_diag_rows(pieces):
    zero = jnp.zeros_like(pieces[0][0])
    rows = []
    for i in range(len(pieces)):
        row = []
        for j, ps in enumerate(pieces):
            row += [p if i == j else zero for p in ps]
        rows.append(jnp.concatenate(row, axis=1))
    return jnp.concatenate(rows, axis=0)


def _win_block(win_refs, j, tile, nb, put):
    qt_ref, kp_ref, kc_ref, kn_ref, km_ref, vp_ref, vc_ref, vn_ref, vm_ref, sink_ref = win_refs
    width = WIN_Q_HEADS * BLK
    ki = lax.broadcasted_iota(jnp.int32, (BLK, width), 0)
    qi = lax.broadcasted_iota(jnp.int32, (BLK, width), 1) & (BLK - 1)
    rep = WIN_Q_HEADS // WIN_KV_HEADS
    keys = [kp_ref[0]] + [kc_ref[0, jj * BLK:(jj + 1) * BLK, :] for jj in range(STEP_BLKS)] + [kn_ref[0]]
    vals = [vp_ref[0, 0]] + [vc_ref[0, jj] for jj in range(STEP_BLKS)] + [vn_ref[0, 0]]
    kmeta, vmeta = km_ref[0, META0:, :], vm_ref[0, 0]
    pad_last = jnp.zeros((META0, width), BF16)
    n = tile * STEP_BLKS + j
    ok_prev = jnp.logical_and(ki >= qi, n >= 2)
    ok_cur = n >= 1
    ok_next = jnp.logical_and(ki <= qi, n + 1 <= nb - 1)
    heads = [qt_ref[0, j, h * HEAD_DIM:(h + 1) * HEAD_DIM, :] for h in range(WIN_Q_HEADS)]
    qbd = _block_diag_rows([heads[g * rep:(g + 1) * rep] for g in range(WIN_KV_HEADS)])
    s = _dot(jnp.concatenate(keys[j:j + 3] + [kmeta], axis=0), qbd)
    scores = [jnp.where(ok_prev, s[:BLK], -jnp.inf),
              jnp.where(ok_cur, s[BLK:2 * BLK], -jnp.inf),
              jnp.where(ok_next, s[2 * BLK:3 * BLK], -jnp.inf),
              s[3 * BLK:]]
    o, den = _softmax_pv_t(scores, vals[j:j + 3] + [vmeta], pad_last, sink_ref[...])
    o = o / den
    outs = [o[(h // rep) * HEAD_DIM:(h // rep + 1) * HEAD_DIM, h * BLK:(h + 1) * BLK] for h in range(WIN_Q_HEADS)]
    put(j, jnp.concatenate(outs, axis=0).astype(BF16).T)


def _win_specs(qt, k, vt, sink, order):
    nb = qt.shape[1]
    kvw = WIN_KV_HEADS * HEAD_DIM
    assert kvw == BLK and WIN_RADIUS == BLK
    kedge = lambda f: pl.BlockSpec((1, BLK, kvw), lambda b, i: (b, f(order(i)), 0))
    vedge = lambda f: pl.BlockSpec((1, 1, kvw, BLK), lambda b, i: (b, f(order(i)), 0, 0))
    before = lambda t: jnp.maximum(t * STEP_BLKS - 1, 0)
    after = lambda t: jnp.minimum((t + 1) * STEP_BLKS, nb - 1)
    first = lambda t: 0
    specs = [pl.BlockSpec((1, STEP_BLKS, WIN_Q_HEADS * HEAD_DIM, BLK), lambda b, i: (b, order(i), 0, 0)),
             kedge(before), pl.BlockSpec((1, TM, kvw), lambda b, i: (b, order(i), 0)), kedge(after), kedge(first),
             vedge(before), pl.BlockSpec((1, STEP_BLKS, kvw, BLK), lambda b, i: (b, order(i), 0, 0)), vedge(after),
             vedge(first), pl.BlockSpec(sink.shape, lambda b, i: (0, 0))]
    return specs, [qt, k, k, k, k, vt, vt, vt, vt, sink]


NA_QUAD = 4
NA_QW = NA_QUAD * HEAD_DIM
NA_WIN = NA_KR * GRID_W


def _na_bias_tables(rpb):
    c = np.arange(GRID_W)[None, :]
    kc = np.arange(GRID_W)[:, None]
    cs = np.clip(c - NA_KC // 2, 0, GRID_W - NA_KC)
    ok = (kc >= cs) & (kc < cs + NA_KC)
    pick = jnp.asarray((kc - c + NA_KC - 1)[None] == np.arange(2 * NA_KC - 1)[:, None, None], F32)
    t = jnp.einsum('hed,dkc->hekc', rpb.astype(F32), pick, precision=lax.Precision.HIGHEST)
    t = jnp.where(jnp.asarray(ok)[None, None], t, -jnp.inf)

    def pack(x):
        h, e = x.shape[:2]
        x = x.reshape(h // NA_QUAD, NA_QUAD, e, GRID_W, GRID_W)
        return x.transpose(0, 2, 3, 1, 4).reshape(h // NA_QUAD, e, GRID_W, NA_QW)

    meta = jnp.broadcast_to(t[:, NA_KR - 1:, :, 0:1], (t.shape[0], NA_KR, GRID_W, GRID_W))
    return pack(t), pack(meta)


NA_KV_BLKS = 7


def _na_window_base(tile, nb):
    return jnp.clip(tile * STEP_BLKS - STEP_BLKS, 0, nb - 1 - NA_KV_BLKS)


def _na_blocks(na_refs, kbuf_ref, vbuf_ref, tile, nb, rows, put, blocks):
    qt_ref, kl_ref, vl_ref, bias_ref, mbias_ref, mb_ref = na_refs
    width = NA_HEADS * HEAD_DIM
    first_half = lax.broadcasted_iota(jnp.int32, (width, BLK), 1) < GRID_W
    ri = lax.broadcasted_iota(jnp.int32, (NA_QW, NA_QW), 0)
    ci = lax.broadcasted_iota(jnp.int32, (NA_QW, NA_QW), 1)
    diag = lax.shift_right_logical(ri, 6) == lax.shift_right_logical(ci, 6)
    head_of_lane = lax.shift_right_logical(lax.broadcasted_iota(jnp.int32, (GRID_W, NA_QW), 1), 6)
    pad_meta = jnp.zeros((META0, NA_QW), F32)
    row0 = _na_window_base(tile, nb) * (BLK // GRID_W)

    def scores(j, half, u, meta):
        qf = qt_ref[0, j].astype(F32)
        qsw = pltpu.roll(qf, GRID_W, 1)
        dup = jnp.where(first_half, qf, qsw) if half == 0 else jnp.where(first_half, qsw, qf)
        if meta:
            start, e0 = 0, None
        else:
            r = (tile * STEP_BLKS + j - 1) * (BLK // GRID_W) + half
            rs = jnp.clip(r - NA_KR // 2, 0, rows - NA_KR)
            start = pl.multiple_of((rs - row0) * GRID_W, GRID_W)
            e0 = rs - r + (NA_KR - 1)
        cols = slice(u * NA_QW, (u + 1) * NA_QW)
        base = dup[u * NA_QW:(u + 1) * NA_QW]
        qbd = jnp.where(diag, jnp.concatenate([base, base], axis=1), 0.0).astype(BF16)
        keys = jnp.concatenate([kbuf_ref[pl.ds(start, NA_WIN), cols], kl_ref[0, META0:LEAD, cols]], axis=0)
        return _dot(keys, qbd), (j, half, u, start, e0, cols)

    def finish(s, unit):
        j, half, u, start, e0, cols = unit
        if e0 is None:
            bias = jnp.concatenate([mbias_ref[u, i] for i in range(NA_KR)], axis=0)
        else:
            bias = jnp.concatenate([bias_ref[u, e0 + i] for i in range(NA_KR)], axis=0)
        sw = s[:NA_WIN] + bias
        sm = s[NA_WIN:] + mb_ref[u]
        mx = jnp.maximum(jnp.max(sw, axis=0, keepdims=True), jnp.max(sm, axis=0, keepdims=True))
        pw, pm = jnp.exp(sw - mx), jnp.exp(sm - mx)
        rden = 1.0 / (jnp.sum(pw, axis=0, keepdims=True) + jnp.sum(pm, axis=0, keepdims=True))
        p_t = jnp.concatenate([pw * rden, pad_meta, pm * rden], axis=0).astype(BF16).T
        vals = jnp.concatenate([vbuf_ref[pl.ds(start, NA_WIN), cols], vl_ref[0, :, cols]], axis=0)
        o = _dot(p_t, vals)
        out = o[(NA_QUAD - 1) * GRID_W:]
        for hq in range(NA_QUAD - 2, -1, -1):
            out = jnp.where(head_of_lane == hq, o[hq * GRID_W:(hq + 1) * GRID_W], out)
        put(j, half, cols, out.astype(BF16))

    def run(units):
        pending = scores(*units[0])
        for nxt in units[1:]:
            ahead = scores(*nxt)
            finish(*pending)
            pending = ahead
        finish(*pending)

    quads = range(NA_HEADS // NA_QUAD)
    if 0 in blocks:
        @pl.when(tile == 0)
        def _():
            for u in quads:
                put(0, 0, slice(u * NA_QW, (u + 1) * NA_QW), jnp.zeros((GRID_W, NA_QW), BF16))
            run([(0, 1, u, True) for u in quads])

        @pl.when(tile > 0)
        def _():
            run([(0, half, u, False) for half in range(2) for u in quads])

    rest = [j for j in blocks if j > 0]
    if rest:
        run([(j, half, u, False) for j in rest for half in range(2) for u in quads])


def _na_load_window(kv_refs, kbuf_ref, vbuf_ref):
    for m in range(NA_KV_BLKS):
        kbuf_ref[m * BLK:(m + 1) * BLK, :] = kv_refs[m][0]
        vbuf_ref[m * BLK:(m + 1) * BLK, :] = kv_refs[NA_KV_BLKS + m][0]


def _na_specs(qt, k, v, bias, mbias, mb, order):
    nb, width = qt.shape[1], qt.shape[2]
    rows = (nb * BLK - LEAD) // GRID_W
    assert nb - 1 >= NA_KV_BLKS and rows >= NA_KR and N_META <= GRID_W and LEAD - GRID_W <= META0
    blk = lambda f: pl.BlockSpec((1, BLK, width), lambda b, i: (b, f(order(i)), 0))
    lead = blk(lambda t: 0)
    win = [blk(lambda t, m=m: LEAD // BLK + _na_window_base(t, nb) + m) for m in range(NA_KV_BLKS)]
    specs = [pl.BlockSpec((1, STEP_BLKS, width, BLK), lambda b, i: (b, order(i), 0, 0)), lead, lead,
             _resident(bias), _resident(mbias), _resident(mb)] + win + win
    return specs, [qt, k, v, bias, mbias, mb] + [k] * NA_KV_BLKS + [v] * NA_KV_BLKS


N_NA_REFS = 6


def _mixout_kernel(*refs, ns, ntile):
    lead_ref = refs[0] if ntile > 1 else None
    tile_refs = refs[ntile > 1:ntile + (ntile > 1)]
    rest = refs[ntile + (ntile > 1):]
    win_refs, rest = rest[:N_WIN_REFS], rest[N_WIN_REFS:]
    na_refs, rest = rest[:N_NA_REFS], rest[N_NA_REFS:]
    kv_refs, rest = rest[:2 * NA_KV_BLKS], rest[2 * NA_KV_BLKS:]
    (xt_ref, bc_ref, acs_ref, tab_ref, zt_ref, yf_ref, dsk_ref, nws_ref, w1_ref, w2_ref, w3_ref,
     nw_ref, o_ref, state_ref, ybuf_ref, ys_ref, yw_ref, yn_ref, kbuf_ref, vbuf_ref) = rest
    i = ns - 1 - pl.program_id(1)
    nb = ns * STEP_BLKS

    @pl.when(pl.program_id(1) == 0)
    def _():
        state_ref[...] = jnp.zeros_like(state_ref)

    def put_y(j, rows, y):
        ybuf_ref[rows, :] = y

    def finish(j):
        y = ybuf_ref[...] + yf_ref[0, j] + dsk_ref[...] * xt_ref[0, j].astype(F32)
        z = zt_ref[0, j].astype(F32)
        y = y * (z * (1.0 / (1.0 + jnp.exp(-z))))
        y = y * lax.rsqrt(jnp.mean(y * y, axis=0, keepdims=True) + EPS) * nws_ref[...]
        ys_ref[j * BLK:(j + 1) * BLK, :] = y.astype(BF16).T

    def scan_chunk(j):
        _ssd_scan(lambda j, rows: xt_ref[0, j, rows, :],
                  lambda j, k: bc_ref[0, j * BLK:(j + 1) * BLK, k * SSD_STATE:(k + 1) * SSD_STATE],
                  lambda j: acs_ref[0, j * BLK:(j + 1) * BLK, :],
                  lambda j, k: tab_ref[0, j, k], state_ref, put_y, finish, reverse=True, chunks=[j])

    def put_win(j, y):
        yw_ref[j * BLK:(j + 1) * BLK, :] = y

    def put_na(j, half, cols, y):
        yn_ref[pl.ds(j * BLK + half * GRID_W, GRID_W), cols] = y

    def na_blocks(blocks):
        _na_blocks(na_refs, kbuf_ref, vbuf_ref, i, nb, (nb * BLK - LEAD) // GRID_W, put_na, blocks)

    _na_load_window(kv_refs, kbuf_ref, vbuf_ref)
    na_blocks([0])
    for j in reversed(range(STEP_BLKS)):
        scan_chunk(j)
        _win_block(win_refs, j, i, nb, put_win)
        if j > 0:
            na_blocks([j])
    mix = _dot(yn_ref[...], w3_ref[...]) + _dot(yw_ref[...], w2_ref[...]) + _dot(ys_ref[...], w1_ref[...])
    out = _load_tile(tile_refs, lead_ref, i) + _rms(mix, nw_ref[...])
    o_ref[0] = jnp.where(_live_rows(TM, i), out, 0.0)


N_WIN_REFS = 10


def _mixout(src, win_args, na_args, xt, bc, acs, tab, zt, y_fwd, dsk, nws, w1, w2, w3, nw):
    bsz, nc = xt.shape[:2]
    ns = nc // STEP_BLKS
    order = lambda c: ns - 1 - c
    tok = lambda n: pl.BlockSpec((1, TM, n), lambda b, c: (b, order(c), 0))
    feat = pl.BlockSpec((1, STEP_BLKS, SSD_INNER, BLK), lambda b, c: (b, order(c), 0, 0))
    tabs = pl.BlockSpec((1, STEP_BLKS) + tab.shape[2:], lambda b, c: (b, order(c), 0, 0, 0))
    tile_specs, tile_args = _tile_specs(src, TM, order)
    win_specs, win_ops = _win_specs(*win_args, order)
    na_specs, na_ops = _na_specs(*na_args, order)
    assert len(win_specs) == N_WIN_REFS and len(na_specs) == N_NA_REFS + 2 * NA_KV_BLKS
    na_width = NA_HEADS * HEAD_DIM
    consts = [dsk, nws, w1, w2, w3, nw]
    return pl.pallas_call(
        functools.partial(_mixout_kernel, ns=ns, ntile=len(tile_specs) - (src[0] == 'x')),
        grid=(bsz, ns),
        in_specs=tile_specs + win_specs + na_specs + [feat, tok(bc.shape[-1]), tok(128), tabs, feat, feat]
        + [_resident(c) for c in consts],
        out_specs=tok(D_MODEL),
        out_shape=jax.ShapeDtypeStruct((bsz, nc * BLK, D_MODEL), F32),
        scratch_shapes=[pltpu.VMEM((SSD_GROUPS, SSD_INNER // SSD_GROUPS, SSD_STATE), F32),
                        pltpu.VMEM((SSD_INNER, BLK), F32), pltpu.VMEM((TM, SSD_INNER), BF16),
                        pltpu.VMEM((TM, WIN_Q_HEADS * HEAD_DIM), BF16), pltpu.VMEM((TM, na_width), BF16),
                        pltpu.VMEM((NA_KV_BLKS * BLK, na_width), BF16), pltpu.VMEM((NA_KV_BLKS * BLK, na_width), BF16)],
        compiler_params=_cparams(("parallel", "arbitrary")),
        name="mixout",
    )(*tile_args, *win_ops, *na_ops, xt, bc, acs, tab, zt, y_fwd, *consts)


def _ffn_kernel(*refs, nt, ntile, padded):
    tile_refs = refs[:ntile]
    hp_ref, hn_ref, nw1_ref, wup_ref, cw_ref, cb_ref, wdn_ref, nw2_ref, o_ref, act_ref = refs[ntile:]
    i = pl.program_id(1)
    rows = act_ref.shape[0]
    nw1 = nw1_ref[...]
    hc = _load_tile(tile_refs, None, i)
    f = jnp.concatenate([_rms(hp_ref[0], nw1), _rms(hc, nw1), _rms(hn_ref[0], nw1)], axis=0).astype(BF16)
    tail = jnp.where(i < nt - 1, 1.0, 0.0)
    nch = D_FF // FF_CHUNK

    def conv(g, c):
        g = jnp.concatenate([g[:HALO + rows], g[HALO + rows:] * tail], axis=0)
        cols = slice(c * FF_CHUNK, (c + 1) * FF_CHUNK)
        out = _shift_rows(g, -1) * cw_ref[0:1, cols] + cb_ref[:, cols]
        out = out + g[HALO:HALO + rows] * cw_ref[1:2, cols]
        return out + _shift_rows(g, 1) * cw_ref[2:3, cols]

    for c in range(nch):
        gate = conv(_dot(f, wup_ref[:, c * FF_CHUNK:(c + 1) * FF_CHUNK]), c)
        up = conv(_dot(f, wup_ref[:, D_FF + c * FF_CHUNK:D_FF + (c + 1) * FF_CHUNK]), nch + c)
        th = jnp.tanh(gate * (GELU_K + (GELU_K * GELU_C) * (gate * gate)))
        act_ref[:, c * FF_CHUNK:(c + 1) * FF_CHUNK] = (gate * (0.5 * th + 0.5) * up).astype(BF16)
    out = hc + _rms(_dot(act_ref[...], wdn_ref[...]), nw2_ref[...])
    o_ref[0] = jnp.where(_live_rows(rows, i), out, 0.0) if padded else out


def _ffn(h, nw1, wup, cw, cb, wdn, nw2, *, last):
    bsz, lp, _ = h.shape
    full = _resident
    if last:
        n_out = lp - LEAD
        tm = _tile_rows(n_out, FFN_OUT_TILE_CAP, BLK)
        h4 = h.reshape(bsz, lp // BLK, BLK, D_MODEL)
        k = tm // BLK
        tile_specs = [pl.BlockSpec((1, 1, BLK, D_MODEL), lambda b, i, j=j: (b, LEAD // BLK + i * k + j, 0, 0))
                      for j in range(k)]
        tile_args = [h4] * k
        off = LEAD // HALO
    else:
        tm, n_out = TM, lp
        tile_specs, tile_args = _tile_specs(('h', h), tm)
        off = 0
    nt = n_out // tm
    hb = tm // HALO
    last_blk = lp // HALO - 1
    prev = pl.BlockSpec((1, HALO, D_MODEL), lambda b, i: (b, jnp.clip(off + i * hb - 1, 0, last_blk), 0))
    nxt = pl.BlockSpec((1, HALO, D_MODEL), lambda b, i: (b, jnp.clip(off + (i + 1) * hb, 0, last_blk), 0))
    return pl.pallas_call(
        functools.partial(_ffn_kernel, nt=nt, ntile=len(tile_specs), padded=not last),
        grid=(bsz, nt),
        in_specs=tile_specs + [prev, nxt, full(nw1), full(wup), full(cw), full(cb), full(wdn), full(nw2)],
        out_specs=pl.BlockSpec((1, tm, D_MODEL), lambda b, i: (b, i, 0)),
        out_shape=jax.ShapeDtypeStruct((bsz, n_out, D_MODEL), F32),
        scratch_shapes=[pltpu.VMEM((tm, D_FF), BF16)],
        compiler_params=_cparams(("parallel", "parallel")),
        name="ffn",
    )(*tile_args, h, h, nw1, wup, cw, cb, wdn, nw2)


def _pad_lanes(x, n):
    return jnp.pad(x, [(0, 0)] * (x.ndim - 1) + [(0, n - x.shape[-1])])


def _rope_tables(lp):
    half = ROPE_DIM // 2
    pos = jnp.maximum(jnp.arange(lp) - META0, 0).astype(F32)
    inv = jnp.power(ROPE_THETA, -jnp.arange(half, dtype=F32) / half)
    ang = pos[:, None] * inv[None, :]
    cos, sin = jnp.cos(ang), jnp.sin(ang)
    zeros, ones = jnp.zeros_like(cos), jnp.ones((lp, HEAD_DIM - ROPE_DIM), F32)
    rest = jnp.zeros((lp, HEAD_DIM - ROPE_DIM), F32)
    c = jnp.concatenate([cos, cos, ones], axis=1)
    s1 = jnp.concatenate([-sin, zeros, rest], axis=1)
    s2 = jnp.concatenate([zeros, sin, rest], axis=1)
    return tuple(jnp.tile(t, (1, 128 // HEAD_DIM)) for t in (c, s1, s2)) + (cos.T, sin.T)


def _layer_params(i, p):
    row = lambda v: v.reshape(1, -1).astype(F32)
    lanes = lambda v: jnp.broadcast_to(v.astype(F32)[:, None], (v.shape[0], BLK))
    sizes = [SSD_INNER, SSD_XBC, 2 * SSD_HEADS, WIN_Q_HEADS * HEAD_DIM, WIN_KV_HEADS * HEAD_DIM,
             WIN_KV_HEADS * HEAD_DIM, NA_HEADS * HEAD_DIM, NA_HEADS * HEAD_DIM, NA_HEADS * HEAD_DIM]
    w_in = p['w_in'][i].astype(BF16)
    ws = jnp.split(w_in, np.cumsum(sizes)[:-1].tolist(), axis=1)
    ws[2] = _pad_lanes(ws[2], 128)
    w_out = p['w_out'][i].astype(BF16)
    return dict(
        norm_mix_pre=row(p['norm_mix_pre'][i]),
        w_in=ws,
        ssd_conv=(p['ssd_conv_w'][i].astype(F32), row(p['ssd_conv_b'][i])),
        ssd_consts=(_pad_lanes(row(p['ssd_dt_bias'][i]), 128), _pad_lanes(row(p['ssd_a_log'][i]), 128)),
        ssd_d=lanes(jnp.repeat(p['ssd_d'][i], HEAD_DIM)),
        ssd_norm_w=lanes(p['ssd_norm_w'][i]),
        win_sink=row(jnp.repeat(p['win_sink'][i], BLK)),
        na_bias=_na_bias_tables(p['na_rpb'][i]),
        na_meta_bias=jnp.broadcast_to(p['na_meta_bias'][i].astype(F32).T.reshape(N_META, NA_HEADS // NA_QUAD, NA_QUAD, 1),
                                      (N_META, NA_HEADS // NA_QUAD, NA_QUAD, GRID_W)).transpose(1, 0, 2, 3).reshape(
            NA_HEADS // NA_QUAD, N_META, NA_QW),
        w_out=(w_out[:SSD_INNER], w_out[SSD_INNER:SSD_INNER + WIN_Q_HEADS * HEAD_DIM],
               w_out[SSD_INNER + WIN_Q_HEADS * HEAD_DIM:]),
        norm_mix_post=row(p['norm_mix_post'][i]),
        norm_ffn_pre=row(p['norm_ffn_pre'][i]),
        ffn_w_up=p['ffn_w_up'][i].astype(BF16),
        ffn_conv_w=p['ffn_conv_w'][i].astype(F32),
        ffn_conv_b=row(p['ffn_conv_b'][i]),
        ffn_w_down=p['ffn_w_down'][i].astype(BF16),
        norm_ffn_post=row(p['norm_ffn_post'][i]),
    )


def _encode(x, meta_tokens, layers):
    bsz, n_tok, _ = x.shape
    assert (n_tok + LEAD) % TM == 0 and n_tok // GRID_W >= NA_KR
    lp = LEAD + n_tok
    lead = jnp.concatenate([jnp.zeros((META0, D_MODEL), F32), meta_tokens.astype(F32)], axis=0)
    src = ('x', x.astype(F32), lead)
    rope = _rope_tables(lp)
    for li, lw in enumerate(layers):
        zt, xt, bc, acs, tab, wq, wk, wv, nq, nk, nv, y_fwd = _inproj(src, lp, lw['norm_mix_pre'], lw['w_in'],
                                                                      lw['ssd_conv'], lw['ssd_consts'], rope)
        h = _mixout(src, (wq, wk, wv, lw['win_sink']), (nq, nk, nv, *lw['na_bias'], lw['na_meta_bias']), xt, bc, acs,
                    tab, zt, y_fwd, lw['ssd_d'], lw['ssd_norm_w'], *lw['w_out'], lw['norm_mix_post'])
        h = _ffn(h, lw['norm_ffn_pre'], lw['ffn_w_up'], lw['ffn_conv_w'], lw['ffn_conv_b'],
                 lw['ffn_w_down'], lw['norm_ffn_post'], last=li == len(layers) - 1)
        src = ('h', h)
    return h


def kernel(x_prompt, x_sample, meta_tokens, norm_mix_pre, norm_mix_post, w_in, ssd_conv_w, ssd_conv_b,
           ssd_dt_bias, ssd_a_log, ssd_d, ssd_norm_w, win_sink, na_rpb, na_meta_bias, w_out, norm_ffn_pre,
           norm_ffn_post, ffn_w_up, ffn_conv_w, ffn_conv_b, ffn_w_down):
    p = dict(norm_mix_pre=norm_mix_pre, norm_mix_post=norm_mix_post, w_in=w_in, ssd_conv_w=ssd_conv_w,
             ssd_conv_b=ssd_conv_b, ssd_dt_bias=ssd_dt_bias, ssd_a_log=ssd_a_log, ssd_d=ssd_d,
             ssd_norm_w=ssd_norm_w, win_sink=win_sink, na_rpb=na_rpb, na_meta_bias=na_meta_bias, w_out=w_out,
             norm_ffn_pre=norm_ffn_pre, norm_ffn_post=norm_ffn_post, ffn_w_up=ffn_w_up, ffn_conv_w=ffn_conv_w,
             ffn_conv_b=ffn_conv_b, ffn_w_down=ffn_w_down)
    layers = [_layer_params(i, p) for i in range(w_in.shape[0])]
    return (_encode(x_prompt, meta_tokens, layers), _encode(x_sample, meta_tokens, layers))
```

```python
import functools

import jax
import jax.numpy as jnp
import numpy as np
from jax import lax
from jax.experimental import pallas as pl
from jax.experimental.pallas import tpu as pltpu

F32 = jnp.float32
BF16 = jnp.bfloat16

D_MODEL = 1024
N_META = 16
GRID_W = 64
HEAD_DIM = 64

SSD_HEADS = 16
SSD_INNER = SSD_HEADS * HEAD_DIM
SSD_GROUPS = 2
SSD_STATE = 128
SSD_XBC = SSD_INNER + 2 * SSD_GROUPS * SSD_STATE
SSD_CONV = 5

WIN_Q_HEADS = 8
WIN_KV_HEADS = 2
WIN_RADIUS = 128
ROPE_THETA = 500000.0
ROPE_DIM = HEAD_DIM // 4

NA_HEADS = 8
NA_KR = 8
NA_KC = 16

D_FF = 2816
EPS = 1e-6
GELU_K = float(np.sqrt(2.0 / np.pi))
GELU_C = 0.044715

BLK = 128
LEAD = BLK
META0 = LEAD - N_META
HALO = 8
TM = 384
FFN_OUT_TILE_CAP = 720
FF_CHUNK = 256
STEP_BLKS = TM // BLK
VMEM_LIMIT = 56 * 1024 * 1024


def _tile_rows(n, cap, unit=HALO):
    return max(t for t in range(unit, cap + 1, unit) if n % t == 0)


def _resident(arr):
    return pl.BlockSpec(arr.shape, lambda *_: (0,) * arr.ndim, pipeline_mode=pl.Buffered(1))


def _cparams(sem):
    return pltpu.CompilerParams(dimension_semantics=sem, vmem_limit_bytes=VMEM_LIMIT)


def _rms(x, w):
    return x * lax.rsqrt(jnp.mean(x * x, axis=-1, keepdims=True) + EPS) * w


def _dot(a, b):
    return jnp.dot(a, b, preferred_element_type=F32)


def _shift_rows(x, d):
    return pltpu.roll(x, (-d) % x.shape[0], 0)[HALO:x.shape[0] - HALO]


def _load_tile(refs, lead_ref, i):
    if len(refs) == 1:
        return refs[0][0]
    blocks = [r[0, 0] for r in refs]
    if lead_ref is not None:
        blocks[0] = jnp.where(i == 0, lead_ref[...], blocks[0])
    return jnp.concatenate(blocks, axis=0)


def _tile_specs(src, tm, order=lambda i: i):
    if src[0] == 'h':
        return [pl.BlockSpec((1, tm, D_MODEL), lambda b, i: (b, order(i), 0))], [src[1]]
    x, lead = src[1], src[2]
    x4 = x.reshape(x.shape[0], x.shape[1] // BLK, BLK, D_MODEL)
    k = tm // BLK
    specs = [pl.BlockSpec((1, 1, BLK, D_MODEL),
                          lambda b, i, j=j: (b, jnp.maximum(order(i) * k + j - LEAD // BLK, 0), 0, 0))
             for j in range(k)]
    return [pl.BlockSpec(lead.shape, lambda b, i: (0, 0))] + specs, [lead] + [x4] * k


def _halo_specs(src, tm):
    arr = src[1]
    off = 0 if src[0] == 'h' else LEAD // HALO
    last = arr.shape[1] // HALO - 1
    hb = tm // HALO
    prev = pl.BlockSpec((1, HALO, D_MODEL), lambda b, i: (b, jnp.clip(i * hb - 1 - off, 0, last), 0))
    nxt = pl.BlockSpec((1, HALO, D_MODEL), lambda b, i: (b, jnp.clip((i + 1) * hb - off, 0, last), 0))
    return [prev, nxt], [arr, arr]


def _live_rows(tile_rows, tile_index):
    row = tile_index * tile_rows + lax.broadcasted_iota(jnp.int32, (tile_rows, 1), 0)
    return row >= META0


def _dot_nt(a, b):
    return lax.dot_general(a, b, (((1,), (1,)), ((), ())), preferred_element_type=F32)


def _rope128(x, c, s1, s2):
    return x * c + pltpu.roll(x, 128 - ROPE_DIM // 2, 1) * s1 + pltpu.roll(x, ROPE_DIM // 2, 1) * s2


def _scan_tables(dt_raw, dt_bias, a_log, live, acs_ref, tab_ref):
    nh = 2 * SSD_HEADS
    dtr = dt_raw + dt_bias
    dt = jnp.maximum(dtr, 0.0) + jnp.log1p(jnp.exp(-jnp.abs(dtr)))
    dt = jnp.where(live, dt, 0.0)
    a = dt * (-jnp.exp(a_log))
    ri = lax.broadcasted_iota(jnp.int32, (BLK, BLK), 0)
    ci = lax.broadcasted_iota(jnp.int32, (BLK, BLK), 1)
    tri = (ci <= ri).astype(BF16)
    fwd = ci < SSD_HEADS
    fwd_rows = ri < SSD_HEADS
    acs_chunks, tab_chunks = [], []
    for j in range(TM // BLK):
        rows = slice(j * BLK, (j + 1) * BLK)
        aj, dtj = a[rows], dt[rows]
        pre = sum(_dot(tri, part) for part in _split3(aj))
        total = pre[BLK - 1:BLK, :]
        acs = jnp.where(fwd, pre, total - pre + aj)
        acs_ref[0, rows, :] = acs
        acs_t = acs.T[:nh]
        chunk_sum = jnp.where(fwd_rows[:nh, 0:1], acs_t[:, BLK - 1:BLK], acs_t[:, 0:1])
        tabs = [acs_t, dtj.T[:nh],
                (dtj * jnp.exp(total - acs)).T[:nh],
                jnp.exp(acs_t), jnp.broadcast_to(jnp.exp(chunk_sum), (nh, BLK))]
        for k, t in enumerate(tabs):
            tab_ref[0, j, k] = t
        acs_chunks.append(acs)
        tab_chunks.append(tabs)
    return acs_chunks, tab_chunks


def _inproj_kernel(*refs, nt, ntile):
    lead_ref = refs[0] if ntile > 1 else None
    tile_refs = refs[ntile > 1:ntile + (ntile > 1)]
    (hp_ref, hn_ref, nw_ref, wz_ref, wx_ref, wdt_ref, wwq_ref, wwk_ref, wwv_ref,
     wnq_ref, wnk_ref, wnv_ref, cw_ref, cb_ref, dtb_ref, alog_ref, rc_ref, rs1_ref, rs2_ref, rct_ref, rst_ref,
     z_ref, xt_ref, bc_ref, acs_ref, tab_ref, wq_ref, wk_ref, wv_ref, nq_ref, nkv_ref, yf_ref,
     state_ref) = refs[ntile + (ntile > 1):]
    i = pl.program_id(1)

    @pl.when(i == 0)
    def _():
        state_ref[...] = jnp.zeros_like(state_ref)

    nw = nw_ref[...]
    a32 = _rms(_load_tile(tile_refs, lead_ref, i), nw)
    a = a32.astype(BF16)
    head = jnp.where(i > 0, 1.0, 0.0)
    tail = jnp.where(i < nt - 1, 1.0, 0.0)
    a_ext = jnp.concatenate([_rms(hp_ref[0], nw) * head, a32, _rms(hn_ref[0], nw) * tail], axis=0).astype(BF16)
    xe = _dot(a_ext, wx_ref[...])

    def put_blocks(ref, xt):
        for j in range(TM // BLK):
            ref[0, j] = xt[:, j * BLK:(j + 1) * BLK].astype(BF16)

    put_blocks(z_ref, _dot(a, wz_ref[...]).astype(BF16).T)
    dt_raw = _dot(a, wdt_ref[...])
    pad = SSD_CONV // 2
    acc = _shift_rows(xe, -pad) * cw_ref[0:1, :] + cb_ref[...]
    for j in range(1, SSD_CONV):
        acc = acc + _shift_rows(xe, j - pad) * cw_ref[j:j + 1, :]
    xc = acc * (1.0 / (1.0 + jnp.exp(-acc)))
    xt_all = xc[:, :SSD_INNER].astype(BF16).T
    xt_chunks = [xt_all[:, j * BLK:(j + 1) * BLK] for j in range(TM // BLK)]
    for j in range(TM // BLK):
        xt_ref[0, j] = xt_chunks[j]
    bc = xc[:, SSD_INNER:].astype(BF16)
    bc_ref[0] = bc
    acs_chunks, tab_chunks = _scan_tables(dt_raw, dtb_ref[...], alog_ref[...], _live_rows(TM, i),
                                          acs_ref, tab_ref)

    def put_y(j, rows, y):
        yf_ref[0, j, rows, :] = y

    def scan_chunk(j):
        _ssd_scan(lambda j, rows: xt_chunks[j][rows],
                  lambda j, k: bc[j * BLK:(j + 1) * BLK, k * SSD_STATE:(k + 1) * SSD_STATE],
                  lambda j: acs_chunks[j], lambda j, k: tab_chunks[j][k], state_ref, put_y, None,
                  reverse=False, chunks=[j])

    scan_chunk(0)
    scale = HEAD_DIM ** -0.5
    qt = _dot(a, wwq_ref[...]).T
    cos_t, sin_t = rct_ref[...], rst_ref[...]
    half = ROPE_DIM // 2
    parts = []
    for hh in range(WIN_Q_HEADS):
        x1 = qt[hh * HEAD_DIM:hh * HEAD_DIM + half]
        x2 = qt[hh * HEAD_DIM + half:hh * HEAD_DIM + ROPE_DIM]
        parts += [x1 * cos_t - x2 * sin_t, x2 * cos_t + x1 * sin_t, qt[hh * HEAD_DIM + ROPE_DIM:(hh + 1) * HEAD_DIM]]
    put_blocks(wq_ref, jnp.concatenate(parts, axis=0) * scale)
    wk_ref[0] = _rope128(_dot(a, wwk_ref[...]), rc_ref[...], rs1_ref[...], rs2_ref[...]).astype(BF16)
    put_blocks(wv_ref, _dot(a, wwv_ref[...]).astype(BF16).T)
    scan_chunk(1)
    put_blocks(nq_ref, (_dot(a, wnq_ref[...]) * scale).astype(BF16).T)
    nkv_ref[0, :, :NA_HEADS * HEAD_DIM] = _dot(a, wnk_ref[...]).astype(BF16)
    scan_chunk(2)
    nkv_ref[0, :, NA_HEADS * HEAD_DIM:] = _dot(a, wnv_ref[...]).astype(BF16)


def _inproj(src, lp, nw, ws, conv, scan, rope):
    bsz = src[1].shape[0]
    nt = lp // TM
    nb = lp // BLK
    tile = lambda n: pl.BlockSpec((1, TM, n), lambda b, i: (b, i, 0))
    tblk = lambda n: pl.BlockSpec((1, TM // BLK, n, BLK), lambda b, i: (b, i, 0, 0))
    tile_specs, tile_args = _tile_specs(src, TM)
    halo_specs, halo_args = _halo_specs(src, TM)
    full = lambda arr: pl.BlockSpec(arr.shape, lambda b, i: (0,) * arr.ndim)
    rtab = pl.BlockSpec((TM, 128), lambda b, i: (i, 0))
    rtab_t = pl.BlockSpec((ROPE_DIM // 2, TM), lambda b, i: (0, i))
    tok = lambda n, d: (tile(n), jax.ShapeDtypeStruct((bsz, lp, n), d))
    blk = lambda n: (tblk(n), jax.ShapeDtypeStruct((bsz, nb, n, BLK), BF16))
    kvw = WIN_KV_HEADS * HEAD_DIM
    tabs = (pl.BlockSpec((1, TM // BLK, 5, 2 * SSD_HEADS, BLK), lambda b, i: (b, i, 0, 0, 0)),
            jax.ShapeDtypeStruct((bsz, nb, 5, 2 * SSD_HEADS, BLK), F32))
    outs = [blk(SSD_INNER), blk(SSD_INNER), tok(SSD_XBC - SSD_INNER, BF16), tok(128, F32), tabs, blk(WIN_Q_HEADS * HEAD_DIM), tok(kvw, BF16),
            blk(kvw), blk(NA_HEADS * HEAD_DIM), tok(2 * NA_HEADS * HEAD_DIM, BF16),
            (tblk(SSD_INNER), jax.ShapeDtypeStruct((bsz, nb, SSD_INNER, BLK), F32))]
    return pl.pallas_call(
        functools.partial(_inproj_kernel, nt=nt, ntile=len(tile_specs) - (src[0] == 'x')),
        grid=(bsz, nt),
        in_specs=tile_specs + halo_specs + [full(nw)] + [full(w) for w in ws] + [full(c) for c in conv + scan]
        + [rtab, rtab, rtab, rtab_t, rtab_t],
        out_specs=[o[0] for o in outs],
        out_shape=[o[1] for o in outs],
        scratch_shapes=[pltpu.VMEM((SSD_GROUPS, SSD_INNER // SSD_GROUPS, SSD_STATE), F32)],
        compiler_params=_cparams(("parallel", "arbitrary")),
        name="inproj",
    )(*tile_args, *halo_args, nw, *ws, *conv, *scan, *rope)


def _split3(x):
    hi = x.astype(BF16)
    r1 = x - hi.astype(F32)
    mid = r1.astype(BF16)
    lo = (r1 - mid.astype(F32)).astype(BF16)
    return hi, mid, lo


def _ssd_scan(get_x, get_bc, get_acs, get_tab, state_ref, put_y, after_chunk, *, reverse, chunks=None):
    ri = lax.broadcasted_iota(jnp.int32, (BLK, BLK), 0)
    ci = lax.broadcasted_iota(jnp.int32, (BLK, BLK), 1)
    feeds = (ri >= ci) if reverse else (ri <= ci)
    hoff = SSD_HEADS if reverse else 0
    rep = SSD_HEADS // SSD_GROUPS
    if chunks is None:
        chunks = reversed(range(STEP_BLKS)) if reverse else range(STEP_BLKS)
    for j in chunks:
        acs = get_acs(j)
        acs_t, dt_t, dt_out_t, e_in_t, e_chunk = (get_tab(j, k) for k in range(5))
        for g in range(SSD_GROUPS):
            bm = get_bc(j, g)
            cm = get_bc(j, SSD_GROUPS + g)
            cbt = _dot_nt(bm, cm)
            ht = state_ref[g]
            y_off = _dot_nt(ht.astype(BF16), cm)
            x_out, decay = [], []
            for r in range(rep):
                h = g * rep + r
                hl = hoff + h
                rows = slice(h * HEAD_DIM, (h + 1) * HEAD_DIM)
                xt = get_x(j, rows).astype(F32)
                x_in = (xt * dt_t[hl:hl + 1, :]).astype(BF16)
                x_out.append((xt * dt_out_t[hl:hl + 1, :]).astype(BF16))
                decay.append(jnp.broadcast_to(e_chunk[hl:hl + 1, :], (HEAD_DIM, SSD_STATE)))
                diff = acs_t[hl:hl + 1, :] - acs[:, hl:hl + 1]
                lt = (cbt * jnp.exp(jnp.where(feeds, diff, -jnp.inf))).astype(BF16)
                put_y(j, rows, _dot(x_in, lt) + y_off[r * HEAD_DIM:(r + 1) * HEAD_DIM] * e_in_t[hl:hl + 1, :])
            s_new = _dot(jnp.concatenate(x_out, axis=0), bm)
            state_ref[g] = ht * jnp.concatenate(decay, axis=0) + s_new
        if after_chunk is not None:
            after_chunk(j)


def _softmax_pv_t(scores, values, pad_last, sink_logit):
    mx = functools.reduce(jnp.maximum, [jnp.max(s, axis=0, keepdims=True) for s in scores] + [sink_logit])
    ps = [jnp.exp(s - mx) for s in scores]
    den = functools.reduce(jnp.add, [jnp.sum(p, axis=0, keepdims=True) for p in ps]) + jnp.exp(sink_logit - mx)
    pb = [p.astype(BF16) for p in ps]
    pb[-1] = jnp.concatenate([pad_last, pb[-1]], axis=0)
    return _dot(jnp.concatenate(values, axis=1), jnp.concatenate(pb, axis=0)), den


def _block_diag_rows(pieces):
    zero = jnp.zeros_like(pieces[0][0])
    rows = []
    for i in range(len(pieces)):
        row = []
        for j, ps in enumerate(pieces):
            row += [p if i == j else zero for p in ps]
        rows.append(jnp.concatenate(row, axis=1))
    return jnp.concatenate(rows, axis=0)


def _win_block(win_refs, j, tile, nb, put):
    qt_ref, kp_ref, kc_ref, kn_ref, km_ref, vp_ref, vc_ref, vn_ref, vm_ref, sink_ref = win_refs
    width = WIN_Q_HEADS * BLK
    ki = lax.broadcasted_iota(jnp.int32, (BLK, width), 0)
    qi = lax.broadcasted_iota(jnp.int32, (BLK, width), 1) & (BLK - 1)
    rep = WIN_Q_HEADS // WIN_KV_HEADS
    keys = [kp_ref[0]] + [kc_ref[0, jj * BLK:(jj + 1) * BLK, :] for jj in range(STEP_BLKS)] + [kn_ref[0]]
    vals = [vp_ref[0, 0]] + [vc_ref[0, jj] for jj in range(STEP_BLKS)] + [vn_ref[0, 0]]
    kmeta, vmeta = km_ref[0, META0:, :], vm_ref[0, 0]
    pad_last = jnp.zeros((META0, width), BF16)
    n = tile * STEP_BLKS + j
    ok_prev = jnp.logical_and(ki >= qi, n >= 2)
    ok_cur = n >= 1
    ok_next = jnp.logical_and(ki <= qi, n + 1 <= nb - 1)
    heads = [qt_ref[0, j, h * HEAD_DIM:(h + 1) * HEAD_DIM, :] for h in range(WIN_Q_HEADS)]
    qbd = _block_diag_rows([heads[g * rep:(g + 1) * rep] for g in range(WIN_KV_HEADS)])
    s = _dot(jnp.concatenate(keys[j:j + 3] + [kmeta], axis=0), qbd)
    scores = [jnp.where(ok_prev, s[:BLK], -jnp.inf),
              jnp.where(ok_cur, s[BLK:2 * BLK], -jnp.inf),
              jnp.where(ok_next, s[2 * BLK:3 * BLK], -jnp.inf),
              s[3 * BLK:]]
    o, den = _softmax_pv_t(scores, vals[j:j + 3] + [vmeta], pad_last, sink_ref[...])
    o = o / den
    outs = [o[(h // rep) * HEAD_DIM:(h // rep + 1) * HEAD_DIM, h * BLK:(h + 1) * BLK] for h in range(WIN_Q_HEADS)]
    put(j, jnp.concatenate(outs, axis=0).astype(BF16).T)


def _win_specs(qt, k, vt, sink, order):
    nb = qt.shape[1]
    kvw = WIN_KV_HEADS * HEAD_DIM
    assert kvw == BLK and WIN_RADIUS == BLK
    kedge = lambda f: pl.BlockSpec((1, BLK, kvw), lambda b, i: (b, f(order(i)), 0))
    vedge = lambda f: pl.BlockSpec((1, 1, kvw, BLK), lambda b, i: (b, f(order(i)), 0, 0))
    before = lambda t: jnp.maximum(t * STEP_BLKS - 1, 0)
    after = lambda t: jnp.minimum((t + 1) * STEP_BLKS, nb - 1)
    first = lambda t: 0
    specs = [pl.BlockSpec((1, STEP_BLKS, WIN_Q_HEADS * HEAD_DIM, BLK), lambda b, i: (b, order(i), 0, 0)),
             kedge(before), pl.BlockSpec((1, TM, kvw), lambda b, i: (b, order(i), 0)), kedge(after), kedge(first),
             vedge(before), pl.BlockSpec((1, STEP_BLKS, kvw, BLK), lambda b, i: (b, order(i), 0, 0)), vedge(after),
             vedge(first), pl.BlockSpec(sink.shape, lambda b, i: (0, 0))]
    return specs, [qt, k, k, k, k, vt, vt, vt, vt, sink]


NA_QUAD = 4
NA_QW = NA_QUAD * HEAD_DIM
NA_WIN = NA_KR * GRID_W


def _na_bias_tables(rpb):
    c = np.arange(GRID_W)[None, :]
    kc = np.arange(GRID_W)[:, None]
    cs = np.clip(c - NA_KC // 2, 0, GRID_W - NA_KC)
    ok = (kc >= cs) & (kc < cs + NA_KC)
    pick = jnp.asarray((kc - c + NA_KC - 1)[None] == np.arange(2 * NA_KC - 1)[:, None, None], F32)
    t = jnp.einsum('hed,dkc->hekc', rpb.astype(F32), pick, precision=lax.Precision.HIGHEST)
    t = jnp.where(jnp.asarray(ok)[None, None], t, -jnp.inf)

    def pack(x):
        h, e = x.shape[:2]
        x = x.reshape(h // NA_QUAD, NA_QUAD, e, GRID_W, GRID_W)
        return x.transpose(0, 2, 3, 1, 4).reshape(h // NA_QUAD, e, GRID_W, NA_QW)

    meta = jnp.broadcast_to(t[:, NA_KR - 1:, :, 0:1], (t.shape[0], NA_KR, GRID_W, GRID_W))
    return pack(t), pack(meta)


NA_KV_BLKS = 7


def _na_window_base(tile, nb):
    return jnp.clip(tile * STEP_BLKS - STEP_BLKS, 0, nb - 1 - NA_KV_BLKS)


def _na_blocks(na_refs, kbuf_ref, vbuf_ref, tile, nb, rows, put, blocks):
    qt_ref, lead_ref, bias_ref, mbias_ref, mb_ref = na_refs
    width = NA_HEADS * HEAD_DIM
    first_half = lax.broadcasted_iota(jnp.int32, (width, BLK), 1) < GRID_W
    ri = lax.broadcasted_iota(jnp.int32, (NA_QW, NA_QW), 0)
    ci = lax.broadcasted_iota(jnp.int32, (NA_QW, NA_QW), 1)
    diag = lax.shift_right_logical(ri, 6) == lax.shift_right_logical(ci, 6)
    head_of_lane = lax.shift_right_logical(lax.broadcasted_iota(jnp.int32, (GRID_W, NA_QW), 1), 6)
    pad_meta = jnp.zeros((META0, NA_QW), F32)
    row0 = _na_window_base(tile, nb) * (BLK // GRID_W)

    def scores(j, half, u, meta):
        qf = qt_ref[0, j].astype(F32)
        qsw = pltpu.roll(qf, GRID_W, 1)
        dup = jnp.where(first_half, qf, qsw) if half == 0 else jnp.where(first_half, qsw, qf)
        if meta:
            start, e0 = 0, None
        else:
            r = (tile * STEP_BLKS + j - 1) * (BLK // GRID_W) + half
            rs = jnp.clip(r - NA_KR // 2, 0, rows - NA_KR)
            start = pl.multiple_of((rs - row0) * GRID_W, GRID_W)
            e0 = rs - r + (NA_KR - 1)
        cols = slice(u * NA_QW, (u + 1) * NA_QW)
        base = dup[u * NA_QW:(u + 1) * NA_QW]
        qbd = jnp.where(diag, jnp.concatenate([base, base], axis=1), 0.0).astype(BF16)
        keys = jnp.concatenate([kbuf_ref[pl.ds(start, NA_WIN), cols], lead_ref[0, META0:LEAD, cols]], axis=0)
        return _dot(keys, qbd), (j, half, u, start, e0, cols)

    def finish(s, unit):
        j, half, u, start, e0, cols = unit
        if e0 is None:
            bias = jnp.concatenate([mbias_ref[u, i] for i in range(NA_KR)], axis=0)
        else:
            bias = jnp.concatenate([bias_ref[u, e0 + i] for i in range(NA_KR)], axis=0)
        sw = s[:NA_WIN] + bias
        sm = s[NA_WIN:] + mb_ref[u]
        mx = jnp.maximum(jnp.max(sw, axis=0, keepdims=True), jnp.max(sm, axis=0, keepdims=True))
        pw, pm = jnp.exp(sw - mx), jnp.exp(sm - mx)
        rden = 1.0 / (jnp.sum(pw, axis=0, keepdims=True) + jnp.sum(pm, axis=0, keepdims=True))
        p_t = jnp.concatenate([pw * rden, pad_meta, pm * rden], axis=0).astype(BF16).T
        vals = jnp.concatenate([vbuf_ref[pl.ds(start, NA_WIN), cols],
                                lead_ref[0, :, u * NA_QW + width:(u + 1) * NA_QW + width]], axis=0)
        o = _dot(p_t, vals)
        out = o[(NA_QUAD - 1) * GRID_W:]
        for hq in range(NA_QUAD - 2, -1, -1):
            out = jnp.where(head_of_lane == hq, o[hq * GRID_W:(hq + 1) * GRID_W], out)
        put(j, half, cols, out.astype(BF16))

    def run(units):
        pending = scores(*units[0])
        for nxt in units[1:]:
            ahead = scores(*nxt)
            finish(*pending)
            pending = ahead
        finish(*pending)

    quads = range(NA_HEADS // NA_QUAD)
    if 0 in blocks:
        @pl.when(tile == 0)
        def _():
            for u in quads:
                put(0, 0, slice(u * NA_QW, (u + 1) * NA_QW), jnp.zeros((GRID_W, NA_QW), BF16))
            run([(0, 1, u, True) for u in quads])

        @pl.when(tile > 0)
        def _():
            run([(0, half, u, False) for half in range(2) for u in quads])

    rest = [j for j in blocks if j > 0]
    if rest:
        run([(j, half, u, False) for j in rest for half in range(2) for u in quads])


def _na_load_window(kv_refs, kbuf_ref, vbuf_ref):
    width = kbuf_ref.shape[1]
    for m in range(NA_KV_BLKS):
        kbuf_ref[m * BLK:(m + 1) * BLK, :] = kv_refs[m][0, :, :width]
        vbuf_ref[m * BLK:(m + 1) * BLK, :] = kv_refs[m][0, :, width:]


def _na_specs(qt, kv, bias, mbias, mb, order):
    nb, width = qt.shape[1], kv.shape[2]
    rows = (nb * BLK - LEAD) // GRID_W
    assert nb - 1 >= NA_KV_BLKS and rows >= NA_KR and N_META <= GRID_W and LEAD - GRID_W <= META0
    blk = lambda f: pl.BlockSpec((1, BLK, width), lambda b, i: (b, f(order(i)), 0))
    lead = blk(lambda t: 0)
    win = [blk(lambda t, m=m: LEAD // BLK + _na_window_base(t, nb) + m) for m in range(NA_KV_BLKS)]
    specs = [pl.BlockSpec((1, STEP_BLKS, qt.shape[2], BLK), lambda b, i: (b, order(i), 0, 0)), lead,
             _resident(bias), _resident(mbias), _resident(mb)] + win
    return specs, [qt, kv, bias, mbias, mb] + [kv] * NA_KV_BLKS


N_NA_REFS = 5


def _mixout_kernel(*refs, ns, ntile):
    lead_ref = refs[0] if ntile > 1 else None
    tile_refs = refs[ntile > 1:ntile + (ntile > 1)]
    rest = refs[ntile + (ntile > 1):]
    win_refs, rest = rest[:N_WIN_REFS], rest[N_WIN_REFS:]
    na_refs, rest = rest[:N_NA_REFS], rest[N_NA_REFS:]
    kv_refs, rest = rest[:NA_KV_BLKS], rest[NA_KV_BLKS:]
    (xt_ref, bc_ref, acs_ref, tab_ref, zt_ref, yf_ref, dsk_ref, nws_ref, w1_ref, w2_ref, w3_ref,
     nw_ref, o_ref, state_ref, ybuf_ref, ys_ref, yw_ref, yn_ref, kbuf_ref, vbuf_ref) = rest
    i = ns - 1 - pl.program_id(1)
    nb = ns * STEP_BLKS

    @pl.when(pl.program_id(1) == 0)
    def _():
        state_ref[...] = jnp.zeros_like(state_ref)

    def put_y(j, rows, y):
        ybuf_ref[rows, :] = y

    def finish(j):
        y = ybuf_ref[...] + yf_ref[0, j] + dsk_ref[...] * xt_ref[0, j].astype(F32)
        z = zt_ref[0, j].astype(F32)
        y = y * (z * (1.0 / (1.0 + jnp.exp(-z))))
        y = y * lax.rsqrt(jnp.mean(y * y, axis=0, keepdims=True) + EPS) * nws_ref[...]
        ys_ref[j * BLK:(j + 1) * BLK, :] = y.astype(BF16).T

    def scan_chunk(j):
        _ssd_scan(lambda j, rows: xt_ref[0, j, rows, :],
                  lambda j, k: bc_ref[0, j * BLK:(j + 1) * BLK, k * SSD_STATE:(k + 1) * SSD_STATE],
                  lambda j: acs_ref[0, j * BLK:(j + 1) * BLK, :],
                  lambda j, k: tab_ref[0, j, k], state_ref, put_y, finish, reverse=True, chunks=[j])

    def put_win(j, y):
        yw_ref[j * BLK:(j + 1) * BLK, :] = y

    def put_na(j, half, cols, y):
        yn_ref[pl.ds(j * BLK + half * GRID_W, GRID_W), cols] = y

    def na_blocks(blocks):
        _na_blocks(na_refs, kbuf_ref, vbuf_ref, i, nb, (nb * BLK - LEAD) // GRID_W, put_na, blocks)

    _na_load_window(kv_refs, kbuf_ref, vbuf_ref)
    na_blocks([0])
    for j in reversed(range(STEP_BLKS)):
        scan_chunk(j)
        _win_block(win_refs, j, i, nb, put_win)
        if j > 0:
            na_blocks([j])
    mix = _dot(yn_ref[...], w3_ref[...]) + _dot(yw_ref[...], w2_ref[...]) + _dot(ys_ref[...], w1_ref[...])
    out = _load_tile(tile_refs, lead_ref, i) + _rms(mix, nw_ref[...])
    o_ref[0] = jnp.where(_live_rows(TM, i), out, 0.0)


N_WIN_REFS = 10


def _mixout(src, win_args, na_args, xt, bc, acs, tab, zt, y_fwd, dsk, nws, w1, w2, w3, nw):
    bsz, nc = xt.shape[:2]
    ns = nc // STEP_BLKS
    order = lambda c: ns - 1 - c
    tok = lambda n: pl.BlockSpec((1, TM, n), lambda b, c: (b, order(c), 0))
    feat = pl.BlockSpec((1, STEP_BLKS, SSD_INNER, BLK), lambda b, c: (b, order(c), 0, 0))
    tabs = pl.BlockSpec((1, STEP_BLKS) + tab.shape[2:], lambda b, c: (b, order(c), 0, 0, 0))
    tile_specs, tile_args = _tile_specs(src, TM, order)
    win_specs, win_ops = _win_specs(*win_args, order)
    na_specs, na_ops = _na_specs(*na_args, order)
    assert len(win_specs) == N_WIN_REFS and len(na_specs) == N_NA_REFS + NA_KV_BLKS
    na_width = NA_HEADS * HEAD_DIM
    consts = [dsk, nws, w1, w2, w3, nw]
    return pl.pallas_call(
        functools.partial(_mixout_kernel, ns=ns, ntile=len(tile_specs) - (src[0] == 'x')),
        grid=(bsz, ns),
        in_specs=tile_specs + win_specs + na_specs + [feat, tok(bc.shape[-1]), tok(128), tabs, feat, feat]
        + [_resident(c) for c in consts],
        out_specs=tok(D_MODEL),
        out_shape=jax.ShapeDtypeStruct((bsz, nc * BLK, D_MODEL), F32),
        scratch_shapes=[pltpu.VMEM((SSD_GROUPS, SSD_INNER // SSD_GROUPS, SSD_STATE), F32),
                        pltpu.VMEM((SSD_INNER, BLK), F32), pltpu.VMEM((TM, SSD_INNER), BF16),
                        pltpu.VMEM((TM, WIN_Q_HEADS * HEAD_DIM), BF16), pltpu.VMEM((TM, na_width), BF16),
                        pltpu.VMEM((NA_KV_BLKS * BLK, na_width), BF16), pltpu.VMEM((NA_KV_BLKS * BLK, na_width), BF16)],
        compiler_params=_cparams(("parallel", "arbitrary")),
        name="mixout",
    )(*tile_args, *win_ops, *na_ops, xt, bc, acs, tab, zt, y_fwd, *consts)


def _ffn_kernel(*refs, nt, ntile, padded):
    tile_refs = refs[:ntile]
    hp_ref, hn_ref, nw1_ref, wup_ref, cw_ref, cb_ref, wdn_ref, nw2_ref, o_ref, act_ref = refs[ntile:]
    i = pl.program_id(1)
    rows = act_ref.shape[0]
    nw1 = nw1_ref[...]
    hc = _load_tile(tile_refs, None, i)
    f = jnp.concatenate([_rms(hp_ref[0], nw1), _rms(hc, nw1), _rms(hn_ref[0], nw1)], axis=0).astype(BF16)
    tail = jnp.where(i < nt - 1, 1.0, 0.0)
    nch = D_FF // FF_CHUNK

    def conv(g, c):
        g = jnp.concatenate([g[:HALO + rows], g[HALO + rows:] * tail], axis=0)
        cols = slice(c * FF_CHUNK, (c + 1) * FF_CHUNK)
        out = _shift_rows(g, -1) * cw_ref[0:1, cols] + cb_ref[:, cols]
        out = out + g[HALO:HALO + rows] * cw_ref[1:2, cols]
        return out + _shift_rows(g, 1) * cw_ref[2:3, cols]

    for c in range(nch):
        gate = conv(_dot(f, wup_ref[:, c * FF_CHUNK:(c + 1) * FF_CHUNK]), c)
        up = conv(_dot(f, wup_ref[:, D_FF + c * FF_CHUNK:D_FF + (c + 1) * FF_CHUNK]), nch + c)
        th = jnp.tanh(gate * (GELU_K + (GELU_K * GELU_C) * (gate * gate)))
        act_ref[:, c * FF_CHUNK:(c + 1) * FF_CHUNK] = (gate * (0.5 * th + 0.5) * up).astype(BF16)
    out = hc + _rms(_dot(act_ref[...], wdn_ref[...]), nw2_ref[...])
    o_ref[0] = jnp.where(_live_rows(rows, i), out, 0.0) if padded else out


def _ffn(h, nw1, wup, cw, cb, wdn, nw2, *, last):
    bsz, lp, _ = h.shape
    full = _resident
    if last:
        n_out = lp - LEAD
        tm = _tile_rows(n_out, FFN_OUT_TILE_CAP, BLK)
        h4 = h.reshape(bsz, lp // BLK, BLK, D_MODEL)
        k = tm // BLK
        tile_specs = [pl.BlockSpec((1, 1, BLK, D_MODEL), lambda b, i, j=j: (b, LEAD // BLK + i * k + j, 0, 0))
                      for j in range(k)]
        tile_args = [h4] * k
        off = LEAD // HALO
    else:
        tm, n_out = TM, lp
        tile_specs, tile_args = _tile_specs(('h', h), tm)
        off = 0
    nt = n_out // tm
    hb = tm // HALO
    last_blk = lp // HALO - 1
    prev = pl.BlockSpec((1, HALO, D_MODEL), lambda b, i: (b, jnp.clip(off + i * hb - 1, 0, last_blk), 0))
    nxt = pl.BlockSpec((1, HALO, D_MODEL), lambda b, i: (b, jnp.clip(off + (i + 1) * hb, 0, last_blk), 0))
    return pl.pallas_call(
        functools.partial(_ffn_kernel, nt=nt, ntile=len(tile_specs), padded=not last),
        grid=(bsz, nt),
        in_specs=tile_specs + [prev, nxt, full(nw1), full(wup), full(cw), full(cb), full(wdn), full(nw2)],
        out_specs=pl.BlockSpec((1, tm, D_MODEL), lambda b, i: (b, i, 0)),
        out_shape=jax.ShapeDtypeStruct((bsz, n_out, D_MODEL), F32),
        scratch_shapes=[pltpu.VMEM((tm, D_FF), BF16)],
        compiler_params=_cparams(("parallel", "parallel")),
        name="ffn",
    )(*tile_args, h, h, nw1, wup, cw, cb, wdn, nw2)


def _pad_lanes(x, n):
    return jnp.pad(x, [(0, 0)] * (x.ndim - 1) + [(0, n - x.shape[-1])])


def _rope_tables(lp):
    half = ROPE_DIM // 2
    pos = jnp.maximum(jnp.arange(lp) - META0, 0).astype(F32)
    inv = jnp.power(ROPE_THETA, -jnp.arange(half, dtype=F32) / half)
    ang = pos[:, None] * inv[None, :]
    cos, sin = jnp.cos(ang), jnp.sin(ang)
    zeros, ones = jnp.zeros_like(cos), jnp.ones((lp, HEAD_DIM - ROPE_DIM), F32)
    rest = jnp.zeros((lp, HEAD_DIM - ROPE_DIM), F32)
    c = jnp.concatenate([cos, cos, ones], axis=1)
    s1 = jnp.concatenate([-sin, zeros, rest], axis=1)
    s2 = jnp.concatenate([zeros, sin, rest], axis=1)
    return tuple(jnp.tile(t, (1, 128 // HEAD_DIM)) for t in (c, s1, s2)) + (cos.T, sin.T)


def _layer_params(i, p):
    row = lambda v: v.reshape(1, -1).astype(F32)
    lanes = lambda v: jnp.broadcast_to(v.astype(F32)[:, None], (v.shape[0], BLK))
    sizes = [SSD_INNER, SSD_XBC, 2 * SSD_HEADS, WIN_Q_HEADS * HEAD_DIM, WIN_KV_HEADS * HEAD_DIM,
             WIN_KV_HEADS * HEAD_DIM, NA_HEADS * HEAD_DIM, NA_HEADS * HEAD_DIM, NA_HEADS * HEAD_DIM]
    w_in = p['w_in'][i].astype(BF16)
    ws = jnp.split(w_in, np.cumsum(sizes)[:-1].tolist(), axis=1)
    ws[2] = _pad_lanes(ws[2], 128)
    w_out = p['w_out'][i].astype(BF16)
    return dict(
        norm_mix_pre=row(p['norm_mix_pre'][i]),
        w_in=ws,
        ssd_conv=(p['ssd_conv_w'][i].astype(F32), row(p['ssd_conv_b'][i])),
        ssd_consts=(_pad_lanes(row(p['ssd_dt_bias'][i]), 128), _pad_lanes(row(p['ssd_a_log'][i]), 128)),
        ssd_d=lanes(jnp.repeat(p['ssd_d'][i], HEAD_DIM)),
        ssd_norm_w=lanes(p['ssd_norm_w'][i]),
        win_sink=row(jnp.repeat(p['win_sink'][i], BLK)),
        na_bias=_na_bias_tables(p['na_rpb'][i]),
        na_meta_bias=jnp.broadcast_to(p['na_meta_bias'][i].astype(F32).T.reshape(N_META, NA_HEADS // NA_QUAD, NA_QUAD, 1),
                                      (N_META, NA_HEADS // NA_QUAD, NA_QUAD, GRID_W)).transpose(1, 0, 2, 3).reshape(
            NA_HEADS // NA_QUAD, N_META, NA_QW),
        w_out=(w_out[:SSD_INNER], w_out[SSD_INNER:SSD_INNER + WIN_Q_HEADS * HEAD_DIM],
               w_out[SSD_INNER + WIN_Q_HEADS * HEAD_DIM:]),
        norm_mix_post=row(p['norm_mix_post'][i]),
        norm_ffn_pre=row(p['norm_ffn_pre'][i]),
        ffn_w_up=p['ffn_w_up'][i].astype(BF16),
        ffn_conv_w=p['ffn_conv_w'][i].astype(F32),
        ffn_conv_b=row(p['ffn_conv_b'][i]),
        ffn_w_down=p['ffn_w_down'][i].astype(BF16),
        norm_ffn_post=row(p['norm_ffn_post'][i]),
    )


def _encode(x, meta_tokens, layers):
    bsz, n_tok, _ = x.shape
    assert (n_tok + LEAD) % TM == 0 and n_tok // GRID_W >= NA_KR
    lp = LEAD + n_tok
    lead = jnp.concatenate([jnp.zeros((META0, D_MODEL), F32), meta_tokens.astype(F32)], axis=0)
    src = ('x', x.astype(F32), lead)
    rope = _rope_tables(lp)
    for li, lw in enumerate(layers):
        zt, xt, bc, acs, tab, wq, wk, wv, nq, nkv, y_fwd = _inproj(src, lp, lw['norm_mix_pre'], lw['w_in'],
                                                                      lw['ssd_conv'], lw['ssd_consts'], rope)
        h = _mixout(src, (wq, wk, wv, lw['win_sink']), (nq, nkv, *lw['na_bias'], lw['na_meta_bias']), xt, bc, acs,
                    tab, zt, y_fwd, lw['ssd_d'], lw['ssd_norm_w'], *lw['w_out'], lw['norm_mix_post'])
        h = _ffn(h, lw['norm_ffn_pre'], lw['ffn_w_up'], lw['ffn_conv_w'], lw['ffn_conv_b'],
                 lw['ffn_w_down'], lw['norm_ffn_post'], last=li == len(layers) - 1)
        src = ('h', h)
    return h


def kernel(x_prompt, x_sample, meta_tokens, norm_mix_pre, norm_mix_post, w_in, ssd_conv_w, ssd_conv_b,
           ssd_dt_bias, ssd_a_log, ssd_d, ssd_norm_w, win_sink, na_rpb, na_meta_bias, w_out, norm_ffn_pre,
           norm_ffn_post, ffn_w_up, ffn_conv_w, ffn_conv_b, ffn_w_down):
    p = dict(norm_mix_pre=norm_mix_pre, norm_mix_post=norm_mix_post, w_in=w_in, ssd_conv_w=ssd_conv_w,
             ssd_conv_b=ssd_conv_b, ssd_dt_bias=ssd_dt_bias, ssd_a_log=ssd_a_log, ssd_d=ssd_d,
             ssd_norm_w=ssd_norm_w, win_sink=win_sink, na_rpb=na_rpb, na_meta_bias=na_meta_bias, w_out=w_out,
             norm_ffn_pre=norm_ffn_pre, norm_ffn_post=norm_ffn_post, ffn_w_up=ffn_w_up, ffn_conv_w=ffn_conv_w,
             ffn_conv_b=ffn_conv_b, ffn_w_down=ffn_w_down)
    layers = [_layer_params(i, p) for i in range(w_in.shape[0])]
    return (_encode(x_prompt, meta_tokens, layers), _encode(x_sample, meta_tokens, layers))
```

```python
import functools

import jax
import jax.numpy as jnp
import numpy as np
from jax import lax
from jax.experimental import pallas as pl
from jax.experimental.pallas import tpu as pltpu

F32 = jnp.float32
BF16 = jnp.bfloat16

D_MODEL = 1024
N_META = 16
GRID_W = 64
HEAD_DIM = 64

SSD_HEADS = 16
SSD_INNER = SSD_HEADS * HEAD_DIM
SSD_GROUPS = 2
SSD_STATE = 128
SSD_XBC = SSD_INNER + 2 * SSD_GROUPS * SSD_STATE
SSD_CONV = 5

WIN_Q_HEADS = 8
WIN_KV_HEADS = 2
WIN_RADIUS = 128
ROPE_THETA = 500000.0
ROPE_DIM = HEAD_DIM // 4

NA_HEADS = 8
NA_KR = 8
NA_KC = 16

D_FF = 2816
EPS = 1e-6
GELU_K = float(np.sqrt(2.0 / np.pi))
GELU_C = 0.044715

BLK = 128
LEAD = BLK
META0 = LEAD - N_META
HALO = 8
TM = 384
FFN_OUT_TILE_CAP = 720
FF_CHUNK = 256
STEP_BLKS = TM // BLK
VMEM_LIMIT = 56 * 1024 * 1024


def _tile_rows(n, cap, unit=HALO):
    return max(t for t in range(unit, cap + 1, unit) if n % t == 0)


def _resident(arr):
    return pl.BlockSpec(arr.shape, lambda *_: (0,) * arr.ndim, pipeline_mode=pl.Buffered(1))


def _cparams(sem):
    return pltpu.CompilerParams(dimension_semantics=sem, vmem_limit_bytes=VMEM_LIMIT)


def _rms(x, w):
    return x * lax.rsqrt(jnp.mean(x * x, axis=-1, keepdims=True) + EPS) * w


def _dot(a, b):
    return jnp.dot(a, b, preferred_element_type=F32)


def _shift_rows(x, d):
    return pltpu.roll(x, (-d) % x.shape[0], 0)[HALO:x.shape[0] - HALO]


def _load_tile(refs, lead_ref, i):
    if len(refs) == 1:
        return refs[0][0]
    blocks = [r[0, 0] for r in refs]
    if lead_ref is not None:
        blocks[0] = jnp.where(i == 0, lead_ref[...], blocks[0])
    return jnp.concatenate(blocks, axis=0)


def _tile_specs(src, tm, order=lambda i: i):
    if src[0] == 'h':
        return [pl.BlockSpec((1, tm, D_MODEL), lambda b, i: (b, order(i), 0))], [src[1]]
    x, lead = src[1], src[2]
    x4 = x.reshape(x.shape[0], x.shape[1] // BLK, BLK, D_MODEL)
    k = tm // BLK
    specs = [pl.BlockSpec((1, 1, BLK, D_MODEL),
                          lambda b, i, j=j: (b, jnp.maximum(order(i) * k + j - LEAD // BLK, 0), 0, 0))
             for j in range(k)]
    return [pl.BlockSpec(lead.shape, lambda b, i: (0, 0))] + specs, [lead] + [x4] * k


def _halo_specs(src, tm):
    arr = src[1]
    off = 0 if src[0] == 'h' else LEAD // HALO
    last = arr.shape[1] // HALO - 1
    hb = tm // HALO
    prev = pl.BlockSpec((1, HALO, D_MODEL), lambda b, i: (b, jnp.clip(i * hb - 1 - off, 0, last), 0))
    nxt = pl.BlockSpec((1, HALO, D_MODEL), lambda b, i: (b, jnp.clip((i + 1) * hb - off, 0, last), 0))
    return [prev, nxt], [arr, arr]


def _live_rows(tile_rows, tile_index):
    row = tile_index * tile_rows + lax.broadcasted_iota(jnp.int32, (tile_rows, 1), 0)
    return row >= META0


def _dot_nt(a, b):
    return lax.dot_general(a, b, (((1,), (1,)), ((), ())), preferred_element_type=F32)


def _rope128(x, c, s1, s2):
    return x * c + pltpu.roll(x, 128 - ROPE_DIM // 2, 1) * s1 + pltpu.roll(x, ROPE_DIM // 2, 1) * s2


def _scan_tables(dt_raw, dt_bias, a_log, live, acs_ref, tab_ref):
    nh = 2 * SSD_HEADS
    dtr = dt_raw + dt_bias
    dt = jnp.maximum(dtr, 0.0) + jnp.log1p(jnp.exp(-jnp.abs(dtr)))
    dt = jnp.where(live, dt, 0.0)
    a = dt * (-jnp.exp(a_log))
    ri = lax.broadcasted_iota(jnp.int32, (BLK, BLK), 0)
    ci = lax.broadcasted_iota(jnp.int32, (BLK, BLK), 1)
    tri = (ci <= ri).astype(BF16)
    fwd = ci < SSD_HEADS
    fwd_rows = ri < SSD_HEADS
    acs_chunks, tab_chunks = [], []
    for j in range(TM // BLK):
        rows = slice(j * BLK, (j + 1) * BLK)
        aj, dtj = a[rows], dt[rows]
        pre = sum(_dot(tri, part) for part in _split3(aj))
        total = pre[BLK - 1:BLK, :]
        acs = jnp.where(fwd, pre, total - pre + aj)
        acs_ref[0, rows, :] = acs
        acs_t = acs.T[:nh]
        chunk_sum = jnp.where(fwd_rows[:nh, 0:1], acs_t[:, BLK - 1:BLK], acs_t[:, 0:1])
        tabs = [acs_t, dtj.T[:nh],
                (dtj * jnp.exp(total - acs)).T[:nh],
                jnp.exp(acs_t), jnp.broadcast_to(jnp.exp(chunk_sum), (nh, BLK))]
        for k, t in enumerate(tabs):
            tab_ref[0, j, k] = t
        acs_chunks.append(acs)
        tab_chunks.append(tabs)
    return acs_chunks, tab_chunks


def _inproj_kernel(*refs, nt, ntile):
    lead_ref = refs[0] if ntile > 1 else None
    tile_refs = refs[ntile > 1:ntile + (ntile > 1)]
    (hp_ref, hn_ref, nw_ref, wz_ref, wx_ref, wdt_ref, wwq_ref, wwk_ref, wwv_ref,
     wnq_ref, wnk_ref, wnv_ref, cw_ref, cb_ref, dtb_ref, alog_ref, rc_ref, rs1_ref, rs2_ref, rct_ref, rst_ref,
     z_ref, xt_ref, bc_ref, acs_ref, tab_ref, wq_ref, wk_ref, wv_ref, nq_ref, nkv_ref, yf_ref,
     state_ref) = refs[ntile + (ntile > 1):]
    i = pl.program_id(1)

    @pl.when(i == 0)
    def _():
        state_ref[...] = jnp.zeros_like(state_ref)

    nw = nw_ref[...]
    a32 = _rms(_load_tile(tile_refs, lead_ref, i), nw)
    a = a32.astype(BF16)
    head = jnp.where(i > 0, 1.0, 0.0)
    tail = jnp.where(i < nt - 1, 1.0, 0.0)
    a_ext = jnp.concatenate([_rms(hp_ref[0], nw) * head, a32, _rms(hn_ref[0], nw) * tail], axis=0).astype(BF16)
    xe = _dot(a_ext, wx_ref[...])

    def put_blocks(ref, xt):
        for j in range(TM // BLK):
            ref[0, j] = xt[:, j * BLK:(j + 1) * BLK].astype(BF16)

    put_blocks(z_ref, _dot(a, wz_ref[...]).astype(BF16).T)
    dt_raw = _dot(a, wdt_ref[...])
    pad = SSD_CONV // 2
    acc = _shift_rows(xe, -pad) * cw_ref[0:1, :] + cb_ref[...]
    for j in range(1, SSD_CONV):
        acc = acc + _shift_rows(xe, j - pad) * cw_ref[j:j + 1, :]
    xc = acc * (1.0 / (1.0 + jnp.exp(-acc)))
    xt_all = xc[:, :SSD_INNER].astype(BF16).T
    xt_chunks = [xt_all[:, j * BLK:(j + 1) * BLK] for j in range(TM // BLK)]
    for j in range(TM // BLK):
        xt_ref[0, j] = xt_chunks[j]
    bc = xc[:, SSD_INNER:].astype(BF16)
    bc_ref[0] = bc
    acs_chunks, tab_chunks = _scan_tables(dt_raw, dtb_ref[...], alog_ref[...], _live_rows(TM, i),
                                          acs_ref, tab_ref)

    def put_y(j, rows, y):
        yf_ref[0, j, rows, :] = y

    def scan_chunk(j):
        _ssd_scan(lambda j, rows: xt_chunks[j][rows],
                  lambda j, k: bc[j * BLK:(j + 1) * BLK, k * SSD_STATE:(k + 1) * SSD_STATE],
                  lambda j: acs_chunks[j], lambda j, k: tab_chunks[j][k], state_ref, put_y, None,
                  reverse=False, chunks=[j])

    scale = HEAD_DIM ** -0.5
    qt = _dot(a, wwq_ref[...]).T
    cos_t, sin_t = rct_ref[...], rst_ref[...]
    half = ROPE_DIM // 2
    parts = []
    for hh in range(WIN_Q_HEADS):
        x1 = qt[hh * HEAD_DIM:hh * HEAD_DIM + half]
        x2 = qt[hh * HEAD_DIM + half:hh * HEAD_DIM + ROPE_DIM]
        parts += [x1 * cos_t - x2 * sin_t, x2 * cos_t + x1 * sin_t, qt[hh * HEAD_DIM + ROPE_DIM:(hh + 1) * HEAD_DIM]]
    put_blocks(wq_ref, jnp.concatenate(parts, axis=0) * scale)
    scan_chunk(0)
    wk_ref[0] = _rope128(_dot(a, wwk_ref[...]), rc_ref[...], rs1_ref[...], rs2_ref[...]).astype(BF16)
    put_blocks(wv_ref, _dot(a, wwv_ref[...]).astype(BF16).T)
    scan_chunk(1)
    put_blocks(nq_ref, (_dot(a, wnq_ref[...]) * scale).astype(BF16).T)
    nkv_ref[0, :, :NA_HEADS * HEAD_DIM] = _dot(a, wnk_ref[...]).astype(BF16)
    scan_chunk(2)
    nkv_ref[0, :, NA_HEADS * HEAD_DIM:] = _dot(a, wnv_ref[...]).astype(BF16)


def _inproj(src, lp, nw, ws, conv, scan, rope):
    bsz = src[1].shape[0]
    nt = lp // TM
    nb = lp // BLK
    tile = lambda n: pl.BlockSpec((1, TM, n), lambda b, i: (b, i, 0))
    tblk = lambda n: pl.BlockSpec((1, TM // BLK, n, BLK), lambda b, i: (b, i, 0, 0))
    tile_specs, tile_args = _tile_specs(src, TM)
    halo_specs, halo_args = _halo_specs(src, TM)
    full = lambda arr: pl.BlockSpec(arr.shape, lambda b, i: (0,) * arr.ndim)
    rtab = pl.BlockSpec((TM, 128), lambda b, i: (i, 0))
    rtab_t = pl.BlockSpec((ROPE_DIM // 2, TM), lambda b, i: (0, i))
    tok = lambda n, d: (tile(n), jax.ShapeDtypeStruct((bsz, lp, n), d))
    blk = lambda n: (tblk(n), jax.ShapeDtypeStruct((bsz, nb, n, BLK), BF16))
    kvw = WIN_KV_HEADS * HEAD_DIM
    tabs = (pl.BlockSpec((1, TM // BLK, 5, 2 * SSD_HEADS, BLK), lambda b, i: (b, i, 0, 0, 0)),
            jax.ShapeDtypeStruct((bsz, nb, 5, 2 * SSD_HEADS, BLK), F32))
    outs = [blk(SSD_INNER), blk(SSD_INNER), tok(SSD_XBC - SSD_INNER, BF16), tok(128, F32), tabs, blk(WIN_Q_HEADS * HEAD_DIM), tok(kvw, BF16),
            blk(kvw), blk(NA_HEADS * HEAD_DIM), tok(2 * NA_HEADS * HEAD_DIM, BF16),
            (tblk(SSD_INNER), jax.ShapeDtypeStruct((bsz, nb, SSD_INNER, BLK), F32))]
    return pl.pallas_call(
        functools.partial(_inproj_kernel, nt=nt, ntile=len(tile_specs) - (src[0] == 'x')),
        grid=(bsz, nt),
        in_specs=tile_specs + halo_specs + [full(nw)] + [full(w) for w in ws] + [full(c) for c in conv + scan]
        + [rtab, rtab, rtab, rtab_t, rtab_t],
        out_specs=[o[0] for o in outs],
        out_shape=[o[1] for o in outs],
        scratch_shapes=[pltpu.VMEM((SSD_GROUPS, SSD_INNER // SSD_GROUPS, SSD_STATE), F32)],
        compiler_params=_cparams(("parallel", "arbitrary")),
        name="inproj",
    )(*tile_args, *halo_args, nw, *ws, *conv, *scan, *rope)


def _split3(x):
    hi = x.astype(BF16)
    r1 = x - hi.astype(F32)
    mid = r1.astype(BF16)
    lo = (r1 - mid.astype(F32)).astype(BF16)
    return hi, mid, lo


def _ssd_scan(get_x, get_bc, get_acs, get_tab, state_ref, put_y, after_chunk, *, reverse, chunks=None):
    ri = lax.broadcasted_iota(jnp.int32, (BLK, BLK), 0)
    ci = lax.broadcasted_iota(jnp.int32, (BLK, BLK), 1)
    feeds = (ri >= ci) if reverse else (ri <= ci)
    hoff = SSD_HEADS if reverse else 0
    rep = SSD_HEADS // SSD_GROUPS
    if chunks is None:
        chunks = reversed(range(STEP_BLKS)) if reverse else range(STEP_BLKS)
    for j in chunks:
        acs = get_acs(j)
        acs_t, dt_t, dt_out_t, e_in_t, e_chunk = (get_tab(j, k) for k in range(5))
        for g in range(SSD_GROUPS):
            bm = get_bc(j, g)
            cm = get_bc(j, SSD_GROUPS + g)
            cbt = _dot_nt(bm, cm)
            ht = state_ref[g]
            y_off = _dot_nt(ht.astype(BF16), cm)
            x_out, decay = [], []
            for r in range(rep):
                h = g * rep + r
                hl = hoff + h
                rows = slice(h * HEAD_DIM, (h + 1) * HEAD_DIM)
                xt = get_x(j, rows).astype(F32)
                x_in = (xt * dt_t[hl:hl + 1, :]).astype(BF16)
                x_out.append((xt * dt_out_t[hl:hl + 1, :]).astype(BF16))
                decay.append(jnp.broadcast_to(e_chunk[hl:hl + 1, :], (HEAD_DIM, SSD_STATE)))
                diff = acs_t[hl:hl + 1, :] - acs[:, hl:hl + 1]
                lt = (cbt * jnp.exp(jnp.where(feeds, diff, -jnp.inf))).astype(BF16)
                put_y(j, rows, _dot(x_in, lt) + y_off[r * HEAD_DIM:(r + 1) * HEAD_DIM] * e_in_t[hl:hl + 1, :])
            s_new = _dot(jnp.concatenate(x_out, axis=0), bm)
            state_ref[g] = ht * jnp.concatenate(decay, axis=0) + s_new
        if after_chunk is not None:
            after_chunk(j)


def _softmax_pv_t(scores, values, pad_last, sink_logit):
    mx = functools.reduce(jnp.maximum, [jnp.max(s, axis=0, keepdims=True) for s in scores] + [sink_logit])
    ps = [jnp.exp(s - mx) for s in scores]
    den = functools.reduce(jnp.add, [jnp.sum(p, axis=0, keepdims=True) for p in ps]) + jnp.exp(sink_logit - mx)
    pb = [p.astype(BF16) for p in ps]
    pb[-1] = jnp.concatenate([pad_last, pb[-1]], axis=0)
    return _dot(jnp.concatenate(values, axis=1), jnp.concatenate(pb, axis=0)), den


def _block_diag_rows(pieces):
    zero = jnp.zeros_like(pieces[0][0])
    rows = []
    for i in range(len(pieces)):
        row = []
        for j, ps in enumerate(pieces):
            row += [p if i == j else zero for p in ps]
        rows.append(jnp.concatenate(row, axis=1))
    return jnp.concatenate(rows, axis=0)


def _win_block(win_refs, j, tile, nb, put):
    qt_ref, kp_ref, kc_ref, kn_ref, km_ref, vp_ref, vc_ref, vn_ref, vm_ref, sink_ref = win_refs
    width = WIN_Q_HEADS * BLK
    ki = lax.broadcasted_iota(jnp.int32, (BLK, width), 0)
    qi = lax.broadcasted_iota(jnp.int32, (BLK, width), 1) & (BLK - 1)
    rep = WIN_Q_HEADS // WIN_KV_HEADS
    keys = [kp_ref[0]] + [kc_ref[0, jj * BLK:(jj + 1) * BLK, :] for jj in range(STEP_BLKS)] + [kn_ref[0]]
    vals = [vp_ref[0, 0]] + [vc_ref[0, jj] for jj in range(STEP_BLKS)] + [vn_ref[0, 0]]
    kmeta, vmeta = km_ref[0, META0:, :], vm_ref[0, 0]
    pad_last = jnp.zeros((META0, width), BF16)
    n = tile * STEP_BLKS + j
    ok_prev = jnp.logical_and(ki >= qi, n >= 2)
    ok_cur = n >= 1
    ok_next = jnp.logical_and(ki <= qi, n + 1 <= nb - 1)
    heads = [qt_ref[0, j, h * HEAD_DIM:(h + 1) * HEAD_DIM, :] for h in range(WIN_Q_HEADS)]
    qbd = _block_diag_rows([heads[g * rep:(g + 1) * rep] for g in range(WIN_KV_HEADS)])
    s = _dot(jnp.concatenate(keys[j:j + 3] + [kmeta], axis=0), qbd)
    scores = [jnp.where(ok_prev, s[:BLK], -jnp.inf),
              jnp.where(ok_cur, s[BLK:2 * BLK], -jnp.inf),
              jnp.where(ok_next, s[2 * BLK:3 * BLK], -jnp.inf),
              s[3 * BLK:]]
    o, den = _softmax_pv_t(scores, vals[j:j + 3] + [vmeta], pad_last, sink_ref[...])
    o = o / den
    outs = [o[(h // rep) * HEAD_DIM:(h // rep + 1) * HEAD_DIM, h * BLK:(h + 1) * BLK] for h in range(WIN_Q_HEADS)]
    put(j, jnp.concatenate(outs, axis=0).astype(BF16).T)


def _win_specs(qt, k, vt, sink, order):
    nb = qt.shape[1]
    kvw = WIN_KV_HEADS * HEAD_DIM
    assert kvw == BLK and WIN_RADIUS == BLK
    kedge = lambda f: pl.BlockSpec((1, BLK, kvw), lambda b, i: (b, f(order(i)), 0))
    vedge = lambda f: pl.BlockSpec((1, 1, kvw, BLK), lambda b, i: (b, f(order(i)), 0, 0))
    before = lambda t: jnp.maximum(t * STEP_BLKS - 1, 0)
    after = lambda t: jnp.minimum((t + 1) * STEP_BLKS, nb - 1)
    first = lambda t: 0
    specs = [pl.BlockSpec((1, STEP_BLKS, WIN_Q_HEADS * HEAD_DIM, BLK), lambda b, i: (b, order(i), 0, 0)),
             kedge(before), pl.BlockSpec((1, TM, kvw), lambda b, i: (b, order(i), 0)), kedge(after), kedge(first),
             vedge(before), pl.BlockSpec((1, STEP_BLKS, kvw, BLK), lambda b, i: (b, order(i), 0, 0)), vedge(after),
             vedge(first), pl.BlockSpec(sink.shape, lambda b, i: (0, 0))]
    return specs, [qt, k, k, k, k, vt, vt, vt, vt, sink]


NA_QUAD = 4
NA_QW = NA_QUAD * HEAD_DIM
NA_WIN = NA_KR * GRID_W


def _na_bias_tables(rpb):
    c = np.arange(GRID_W)[None, :]
    kc = np.arange(GRID_W)[:, None]
    cs = np.clip(c - NA_KC // 2, 0, GRID_W - NA_KC)
    ok = (kc >= cs) & (kc < cs + NA_KC)
    pick = jnp.asarray((kc - c + NA_KC - 1)[None] == np.arange(2 * NA_KC - 1)[:, None, None], F32)
    t = jnp.einsum('hed,dkc->hekc', rpb.astype(F32), pick, precision=lax.Precision.HIGHEST)
    t = jnp.where(jnp.asarray(ok)[None, None], t, -jnp.inf)

    def pack(x):
        h, e = x.shape[:2]
        x = x.reshape(h // NA_QUAD, NA_QUAD, e, GRID_W, GRID_W)
        return x.transpose(0, 2, 3, 1, 4).reshape(h // NA_QUAD, e, GRID_W, NA_QW)

    meta = jnp.broadcast_to(t[:, NA_KR - 1:, :, 0:1], (t.shape[0], NA_KR, GRID_W, GRID_W))
    return pack(t), pack(meta)


NA_KV_BLKS = 7


def _na_window_base(tile, nb):
    return jnp.clip(tile * STEP_BLKS - STEP_BLKS, 0, nb - 1 - NA_KV_BLKS)


def _na_blocks(na_refs, kbuf_ref, vbuf_ref, tile, nb, rows, put, blocks):
    qt_ref, lead_ref, bias_ref, mbias_ref, mb_ref = na_refs
    width = NA_HEADS * HEAD_DIM
    first_half = lax.broadcasted_iota(jnp.int32, (width, BLK), 1) < GRID_W
    ri = lax.broadcasted_iota(jnp.int32, (NA_QW, NA_QW), 0)
    ci = lax.broadcasted_iota(jnp.int32, (NA_QW, NA_QW), 1)
    diag = lax.shift_right_logical(ri, 6) == lax.shift_right_logical(ci, 6)
    head_of_lane = lax.shift_right_logical(lax.broadcasted_iota(jnp.int32, (GRID_W, NA_QW), 1), 6)
    pad_meta = jnp.zeros((META0, NA_QW), F32)
    row0 = _na_window_base(tile, nb) * (BLK // GRID_W)

    def scores(j, half, u, meta):
        qf = qt_ref[0, j].astype(F32)
        qsw = pltpu.roll(qf, GRID_W, 1)
        dup = jnp.where(first_half, qf, qsw) if half == 0 else jnp.where(first_half, qsw, qf)
        if meta:
            start, e0 = 0, None
        else:
            r = (tile * STEP_BLKS + j - 1) * (BLK // GRID_W) + half
            rs = jnp.clip(r - NA_KR // 2, 0, rows - NA_KR)
            start = pl.multiple_of((rs - row0) * GRID_W, GRID_W)
            e0 = rs - r + (NA_KR - 1)
        cols = slice(u * NA_QW, (u + 1) * NA_QW)
        base = dup[u * NA_QW:(u + 1) * NA_QW]
        qbd = jnp.where(diag, jnp.concatenate([base, base], axis=1), 0.0).astype(BF16)
        keys = jnp.concatenate([kbuf_ref[pl.ds(start, NA_WIN), cols], lead_ref[0, META0:LEAD, cols]], axis=0)
        return _dot(keys, qbd), (j, half, u, start, e0, cols)

    def finish(s, unit):
        j, half, u, start, e0, cols = unit
        if e0 is None:
            bias = jnp.concatenate([mbias_ref[u, i] for i in range(NA_KR)], axis=0)
        else:
            bias = jnp.concatenate([bias_ref[u, e0 + i] for i in range(NA_KR)], axis=0)
        sw = s[:NA_WIN] + bias
        sm = s[NA_WIN:] + mb_ref[u]
        mx = jnp.maximum(jnp.max(sw, axis=0, keepdims=True), jnp.max(sm, axis=0, keepdims=True))
        pw, pm = jnp.exp(sw - mx), jnp.exp(sm - mx)
        rden = 1.0 / (jnp.sum(pw, axis=0, keepdims=True) + jnp.sum(pm, axis=0, keepdims=True))
        p_t = jnp.concatenate([pw * rden, pad_meta, pm * rden], axis=0).astype(BF16).T
        vals = jnp.concatenate([vbuf_ref[pl.ds(start, NA_WIN), cols],
                                lead_ref[0, :, u * NA_QW + width:(u + 1) * NA_QW + width]], axis=0)
        o = _dot(p_t, vals)
        out = o[(NA_QUAD - 1) * GRID_W:]
        for hq in range(NA_QUAD - 2, -1, -1):
            out = jnp.where(head_of_lane == hq, o[hq * GRID_W:(hq + 1) * GRID_W], out)
        put(j, half, cols, out.astype(BF16))

    def run(units):
        pending = scores(*units[0])
        for nxt in units[1:]:
            ahead = scores(*nxt)
            finish(*pending)
            pending = ahead
        finish(*pending)

    quads = range(NA_HEADS // NA_QUAD)
    if 0 in blocks:
        @pl.when(tile == 0)
        def _():
            for u in quads:
                put(0, 0, slice(u * NA_QW, (u + 1) * NA_QW), jnp.zeros((GRID_W, NA_QW), BF16))
            run([(0, 1, u, True) for u in quads])

        @pl.when(tile > 0)
        def _():
            run([(0, half, u, False) for half in range(2) for u in quads])

    rest = [j for j in blocks if j > 0]
    if rest:
        run([(j, half, u, False) for j in rest for half in range(2) for u in quads])


def _na_load_window(kv_refs, kbuf_ref, vbuf_ref):
    width = kbuf_ref.shape[1]
    for m in range(NA_KV_BLKS):
        kbuf_ref[m * BLK:(m + 1) * BLK, :] = kv_refs[m][0, :, :width]
        vbuf_ref[m * BLK:(m + 1) * BLK, :] = kv_refs[m][0, :, width:]


def _na_specs(qt, kv, bias, mbias, mb, order):
    nb, width = qt.shape[1], kv.shape[2]
    rows = (nb * BLK - LEAD) // GRID_W
    assert nb - 1 >= NA_KV_BLKS and rows >= NA_KR and N_META <= GRID_W and LEAD - GRID_W <= META0
    blk = lambda f: pl.BlockSpec((1, BLK, width), lambda b, i: (b, f(order(i)), 0))
    lead = blk(lambda t: 0)
    win = [blk(lambda t, m=m: LEAD // BLK + _na_window_base(t, nb) + m) for m in range(NA_KV_BLKS)]
    specs = [pl.BlockSpec((1, STEP_BLKS, qt.shape[2], BLK), lambda b, i: (b, order(i), 0, 0)), lead,
             _resident(bias), _resident(mbias), _resident(mb)] + win
    return specs, [qt, kv, bias, mbias, mb] + [kv] * NA_KV_BLKS


N_NA_REFS = 5


def _mixout_kernel(*refs, ns, ntile):
    lead_ref = refs[0] if ntile > 1 else None
    tile_refs = refs[ntile > 1:ntile + (ntile > 1)]
    rest = refs[ntile + (ntile > 1):]
    win_refs, rest = rest[:N_WIN_REFS], rest[N_WIN_REFS:]
    na_refs, rest = rest[:N_NA_REFS], rest[N_NA_REFS:]
    kv_refs, rest = rest[:NA_KV_BLKS], rest[NA_KV_BLKS:]
    (xt_ref, bc_ref, acs_ref, tab_ref, zt_ref, yf_ref, dsk_ref, nws_ref, w1_ref, w2_ref, w3_ref,
     nw_ref, o_ref, state_ref, ybuf_ref, ys_ref, yw_ref, yn_ref, kbuf_ref, vbuf_ref) = rest
    i = ns - 1 - pl.program_id(1)
    nb = ns * STEP_BLKS

    @pl.when(pl.program_id(1) == 0)
    def _():
        state_ref[...] = jnp.zeros_like(state_ref)

    def put_y(j, rows, y):
        ybuf_ref[rows, :] = y

    def finish(j):
        y = ybuf_ref[...] + yf_ref[0, j] + dsk_ref[...] * xt_ref[0, j].astype(F32)
        z = zt_ref[0, j].astype(F32)
        y = y * (z * (1.0 / (1.0 + jnp.exp(-z))))
        y = y * lax.rsqrt(jnp.mean(y * y, axis=0, keepdims=True) + EPS) * nws_ref[...]
        ys_ref[j * BLK:(j + 1) * BLK, :] = y.astype(BF16).T

    def scan_chunk(j):
        _ssd_scan(lambda j, rows: xt_ref[0, j, rows, :],
                  lambda j, k: bc_ref[0, j * BLK:(j + 1) * BLK, k * SSD_STATE:(k + 1) * SSD_STATE],
                  lambda j: acs_ref[0, j * BLK:(j + 1) * BLK, :],
                  lambda j, k: tab_ref[0, j, k], state_ref, put_y, finish, reverse=True, chunks=[j])

    def put_win(j, y):
        yw_ref[j * BLK:(j + 1) * BLK, :] = y

    def put_na(j, half, cols, y):
        yn_ref[pl.ds(j * BLK + half * GRID_W, GRID_W), cols] = y

    def na_blocks(blocks):
        _na_blocks(na_refs, kbuf_ref, vbuf_ref, i, nb, (nb * BLK - LEAD) // GRID_W, put_na, blocks)

    _na_load_window(kv_refs, kbuf_ref, vbuf_ref)
    na_blocks([0])
    for j in reversed(range(STEP_BLKS)):
        scan_chunk(j)
        _win_block(win_refs, j, i, nb, put_win)
        if j > 0:
            na_blocks([j])
    mix = _dot(yn_ref[...], w3_ref[...]) + _dot(yw_ref[...], w2_ref[...]) + _dot(ys_ref[...], w1_ref[...])
    out = _load_tile(tile_refs, lead_ref, i) + _rms(mix, nw_ref[...])
    o_ref[0] = jnp.where(_live_rows(TM, i), out, 0.0)


N_WIN_REFS = 10


def _mixout(src, win_args, na_args, xt, bc, acs, tab, zt, y_fwd, dsk, nws, w1, w2, w3, nw):
    bsz, nc = xt.shape[:2]
    ns = nc // STEP_BLKS
    order = lambda c: ns - 1 - c
    tok = lambda n: pl.BlockSpec((1, TM, n), lambda b, c: (b, order(c), 0))
    feat = pl.BlockSpec((1, STEP_BLKS, SSD_INNER, BLK), lambda b, c: (b, order(c), 0, 0))
    tabs = pl.BlockSpec((1, STEP_BLKS) + tab.shape[2:], lambda b, c: (b, order(c), 0, 0, 0))
    tile_specs, tile_args = _tile_specs(src, TM, order)
    win_specs, win_ops = _win_specs(*win_args, order)
    na_specs, na_ops = _na_specs(*na_args, order)
    assert len(win_specs) == N_WIN_REFS and len(na_specs) == N_NA_REFS + NA_KV_BLKS
    na_width = NA_HEADS * HEAD_DIM
    consts = [dsk, nws, w1, w2, w3, nw]
    return pl.pallas_call(
        functools.partial(_mixout_kernel, ns=ns, ntile=len(tile_specs) - (src[0] == 'x')),
        grid=(bsz, ns),
        in_specs=tile_specs + win_specs + na_specs + [feat, tok(bc.shape[-1]), tok(128), tabs, feat, feat]
        + [_resident(c) for c in consts],
        out_specs=tok(D_MODEL),
        out_shape=jax.ShapeDtypeStruct((bsz, nc * BLK, D_MODEL), F32),
        scratch_shapes=[pltpu.VMEM((SSD_GROUPS, SSD_INNER // SSD_GROUPS, SSD_STATE), F32),
                        pltpu.VMEM((SSD_INNER, BLK), F32), pltpu.VMEM((TM, SSD_INNER), BF16),
                        pltpu.VMEM((TM, WIN_Q_HEADS * HEAD_DIM), BF16), pltpu.VMEM((TM, na_width), BF16),
                        pltpu.VMEM((NA_KV_BLKS * BLK, na_width), BF16), pltpu.VMEM((NA_KV_BLKS * BLK, na_width), BF16)],
        compiler_params=_cparams(("parallel", "arbitrary")),
        name="mixout",
    )(*tile_args, *win_ops, *na_ops, xt, bc, acs, tab, zt, y_fwd, *consts)


def _ffn_kernel(*refs, nt, ntile, padded):
    tile_refs = refs[:ntile]
    hp_ref, hn_ref, nw1_ref, wup_ref, cw_ref, cb_ref, wdn_ref, nw2_ref, o_ref, act_ref = refs[ntile:]
    i = pl.program_id(1)
    rows = act_ref.shape[0]
    nw1 = nw1_ref[...]
    hc = _load_tile(tile_refs, None, i)
    f = jnp.concatenate([_rms(hp_ref[0], nw1), _rms(hc, nw1), _rms(hn_ref[0], nw1)], axis=0).astype(BF16)
    tail = jnp.where(i < nt - 1, 1.0, 0.0)
    nch = D_FF // FF_CHUNK

    def conv(g, c):
        g = jnp.concatenate([g[:HALO + rows], g[HALO + rows:] * tail], axis=0)
        cols = slice(c * FF_CHUNK, (c + 1) * FF_CHUNK)
        out = _shift_rows(g, -1) * cw_ref[0:1, cols] + cb_ref[:, cols]
        out = out + g[HALO:HALO + rows] * cw_ref[1:2, cols]
        return out + _shift_rows(g, 1) * cw_ref[2:3, cols]

    for c in range(nch):
        gate = conv(_dot(f, wup_ref[:, c * FF_CHUNK:(c + 1) * FF_CHUNK]), c)
        up = conv(_dot(f, wup_ref[:, D_FF + c * FF_CHUNK:D_FF + (c + 1) * FF_CHUNK]), nch + c)
        th = jnp.tanh(gate * (GELU_K + (GELU_K * GELU_C) * (gate * gate)))
        act_ref[:, c * FF_CHUNK:(c + 1) * FF_CHUNK] = (gate * (0.5 * th + 0.5) * up).astype(BF16)
    out = hc + _rms(_dot(act_ref[...], wdn_ref[...]), nw2_ref[...])
    o_ref[0] = jnp.where(_live_rows(rows, i), out, 0.0) if padded else out


def _ffn(h, nw1, wup, cw, cb, wdn, nw2, *, last):
    bsz, lp, _ = h.shape
    full = _resident
    if last:
        n_out = lp - LEAD
        tm = _tile_rows(n_out, FFN_OUT_TILE_CAP, BLK)
        h4 = h.reshape(bsz, lp // BLK, BLK, D_MODEL)
        k = tm // BLK
        tile_specs = [pl.BlockSpec((1, 1, BLK, D_MODEL), lambda b, i, j=j: (b, LEAD // BLK + i * k + j, 0, 0))
                      for j in range(k)]
        tile_args = [h4] * k
        off = LEAD // HALO
    else:
        tm, n_out = TM, lp
        tile_specs, tile_args = _tile_specs(('h', h), tm)
        off = 0
    nt = n_out // tm
    hb = tm // HALO
    last_blk = lp // HALO - 1
    prev = pl.BlockSpec((1, HALO, D_MODEL), lambda b, i: (b, jnp.clip(off + i * hb - 1, 0, last_blk), 0))
    nxt = pl.BlockSpec((1, HALO, D_MODEL), lambda b, i: (b, jnp.clip(off + (i + 1) * hb, 0, last_blk), 0))
    return pl.pallas_call(
        functools.partial(_ffn_kernel, nt=nt, ntile=len(tile_specs), padded=not last),
        grid=(bsz, nt),
        in_specs=tile_specs + [prev, nxt, full(nw1), full(wup), full(cw), full(cb), full(wdn), full(nw2)],
        out_specs=pl.BlockSpec((1, tm, D_MODEL), lambda b, i: (b, i, 0)),
        out_shape=jax.ShapeDtypeStruct((bsz, n_out, D_MODEL), F32),
        scratch_shapes=[pltpu.VMEM((tm, D_FF), BF16)],
        compiler_params=_cparams(("parallel", "parallel")),
        name="ffn",
    )(*tile_args, h, h, nw1, wup, cw, cb, wdn, nw2)


def _pad_lanes(x, n):
    return jnp.pad(x, [(0, 0)] * (x.ndim - 1) + [(0, n - x.shape[-1])])


def _rope_tables(lp):
    half = ROPE_DIM // 2
    pos = jnp.maximum(jnp.arange(lp) - META0, 0).astype(F32)
    inv = jnp.power(ROPE_THETA, -jnp.arange(half, dtype=F32) / half)
    ang = pos[:, None] * inv[None, :]
    cos, sin = jnp.cos(ang), jnp.sin(ang)
    zeros, ones = jnp.zeros_like(cos), jnp.ones((lp, HEAD_DIM - ROPE_DIM), F32)
    rest = jnp.zeros((lp, HEAD_DIM - ROPE_DIM), F32)
    c = jnp.concatenate([cos, cos, ones], axis=1)
    s1 = jnp.concatenate([-sin, zeros, rest], axis=1)
    s2 = jnp.concatenate([zeros, sin, rest], axis=1)
    return tuple(jnp.tile(t, (1, 128 // HEAD_DIM)) for t in (c, s1, s2)) + (cos.T, sin.T)


def _layer_params(i, p):
    row = lambda v: v.reshape(1, -1).astype(F32)
    lanes = lambda v: jnp.broadcast_to(v.astype(F32)[:, None], (v.shape[0], BLK))
    sizes = [SSD_INNER, SSD_XBC, 2 * SSD_HEADS, WIN_Q_HEADS * HEAD_DIM, WIN_KV_HEADS * HEAD_DIM,
             WIN_KV_HEADS * HEAD_DIM, NA_HEADS * HEAD_DIM, NA_HEADS * HEAD_DIM, NA_HEADS * HEAD_DIM]
    w_in = p['w_in'][i].astype(BF16)
    ws = jnp.split(w_in, np.cumsum(sizes)[:-1].tolist(), axis=1)
    ws[2] = _pad_lanes(ws[2], 128)
    w_out = p['w_out'][i].astype(BF16)
    return dict(
        norm_mix_pre=row(p['norm_mix_pre'][i]),
        w_in=ws,
        ssd_conv=(p['ssd_conv_w'][i].astype(F32), row(p['ssd_conv_b'][i])),
        ssd_consts=(_pad_lanes(row(p['ssd_dt_bias'][i]), 128), _pad_lanes(row(p['ssd_a_log'][i]), 128)),
        ssd_d=lanes(jnp.repeat(p['ssd_d'][i], HEAD_DIM)),
        ssd_norm_w=lanes(p['ssd_norm_w'][i]),
        win_sink=row(jnp.repeat(p['win_sink'][i], BLK)),
        na_bias=_na_bias_tables(p['na_rpb'][i]),
        na_meta_bias=jnp.broadcast_to(p['na_meta_bias'][i].astype(F32).T.reshape(N_META, NA_HEADS // NA_QUAD, NA_QUAD, 1),
                                      (N_META, NA_HEADS // NA_QUAD, NA_QUAD, GRID_W)).transpose(1, 0, 2, 3).reshape(
            NA_HEADS // NA_QUAD, N_META, NA_QW),
        w_out=(w_out[:SSD_INNER], w_out[SSD_INNER:SSD_INNER + WIN_Q_HEADS * HEAD_DIM],
               w_out[SSD_INNER + WIN_Q_HEADS * HEAD_DIM:]),
        norm_mix_post=row(p['norm_mix_post'][i]),
        norm_ffn_pre=row(p['norm_ffn_pre'][i]),
        ffn_w_up=p['ffn_w_up'][i].astype(BF16),
        ffn_conv_w=p['ffn_conv_w'][i].astype(F32),
        ffn_conv_b=row(p['ffn_conv_b'][i]),
        ffn_w_down=p['ffn_w_down'][i].astype(BF16),
        norm_ffn_post=row(p['norm_ffn_post'][i]),
    )


def _encode(x, meta_tokens, layers):
    bsz, n_tok, _ = x.shape
    assert (n_tok + LEAD) % TM == 0 and n_tok // GRID_W >= NA_KR
    lp = LEAD + n_tok
    lead = jnp.concatenate([jnp.zeros((META0, D_MODEL), F32), meta_tokens.astype(F32)], axis=0)
    src = ('x', x.astype(F32), lead)
    rope = _rope_tables(lp)
    for li, lw in enumerate(layers):
        zt, xt, bc, acs, tab, wq, wk, wv, nq, nkv, y_fwd = _inproj(src, lp, lw['norm_mix_pre'], lw['w_in'],
                                                                      lw['ssd_conv'], lw['ssd_consts'], rope)
        h = _mixout(src, (wq, wk, wv, lw['win_sink']), (nq, nkv, *lw['na_bias'], lw['na_meta_bias']), xt, bc, acs,
                    tab, zt, y_fwd, lw['ssd_d'], lw['ssd_norm_w'], *lw['w_out'], lw['norm_mix_post'])
        h = _ffn(h, lw['norm_ffn_pre'], lw['ffn_w_up'], lw['ffn_conv_w'], lw['ffn_conv_b'],
                 lw['ffn_w_down'], lw['norm_ffn_post'], last=li == len(layers) - 1)
        src = ('h', h)
    return h


def kernel(x_prompt, x_sample, meta_tokens, norm_mix_pre, norm_mix_post, w_in, ssd_conv_w, ssd_conv_b,
           ssd_dt_bias, ssd_a_log, ssd_d, ssd_norm_w, win_sink, na_rpb, na_meta_bias, w_out, norm_ffn_pre,
           norm_ffn_post, ffn_w_up, ffn_conv_w, ffn_conv_b, ffn_w_down):
    p = dict(norm_mix_pre=norm_mix_pre, norm_mix_post=norm_mix_post, w_in=w_in, ssd_conv_w=ssd_conv_w,
             ssd_conv_b=ssd_conv_b, ssd_dt_bias=ssd_dt_bias, ssd_a_log=ssd_a_log, ssd_d=ssd_d,
             ssd_norm_w=ssd_norm_w, win_sink=win_sink, na_rpb=na_rpb, na_meta_bias=na_meta_bias, w_out=w_out,
             norm_ffn_pre=norm_ffn_pre, norm_ffn_post=norm_ffn_post, ffn_w_up=ffn_w_up, ffn_conv_w=ffn_conv_w,
             ffn_conv_b=ffn_conv_b, ffn_w_down=ffn_w_down)
    layers = [_layer_params(i, p) for i in range(w_in.shape[0])]
    return (_encode(x_prompt, meta_tokens, layers), _encode(x_sample, meta_tokens, layers))
```

```python
import functools

import jax
import jax.numpy as jnp
import numpy as np
from jax import lax
from jax.experimental import pallas as pl
from jax.experimental.pallas import tpu as pltpu

F32 = jnp.float32
BF16 = jnp.bfloat16

D_MODEL = 1024
N_META = 16
GRID_W = 64
HEAD_DIM = 64

SSD_HEADS = 16
SSD_INNER = SSD_HEADS * HEAD_DIM
SSD_GROUPS = 2
SSD_STATE = 128
SSD_XBC = SSD_INNER + 2 * SSD_GROUPS * SSD_STATE
SSD_CONV = 5

WIN_Q_HEADS = 8
WIN_KV_HEADS = 2
WIN_RADIUS = 128
ROPE_THETA = 500000.0
ROPE_DIM = HEAD_DIM // 4

NA_HEADS = 8
NA_KR = 8
NA_KC = 16

D_FF = 2816
EPS = 1e-6
GELU_K = float(np.sqrt(2.0 / np.pi))
GELU_C = 0.044715

BLK = 128
LEAD = BLK
META0 = LEAD - N_META
HALO = 8
TM = 384
FFN_OUT_TILE_CAP = 720
FF_CHUNK = 256
STEP_BLKS = TM // BLK
VMEM_LIMIT = 56 * 1024 * 1024


def _tile_rows(n, cap, unit=HALO):
    return max(t for t in range(unit, cap + 1, unit) if n % t == 0)


def _resident(arr):
    return pl.BlockSpec(arr.shape, lambda *_: (0,) * arr.ndim, pipeline_mode=pl.Buffered(1))


def _cparams(sem):
    return pltpu.CompilerParams(dimension_semantics=sem, vmem_limit_bytes=VMEM_LIMIT)


def _rms(x, w):
    return x * lax.rsqrt(jnp.mean(x * x, axis=-1, keepdims=True) + EPS) * w


def _dot(a, b):
    return jnp.dot(a, b, preferred_element_type=F32)


def _shift_rows(x, d):
    return pltpu.roll(x, (-d) % x.shape[0], 0)[HALO:x.shape[0] - HALO]


def _load_tile(refs, lead_ref, i):
    if len(refs) == 1:
        return refs[0][0]
    blocks = [r[0, 0] for r in refs]
    if lead_ref is not None:
        blocks[0] = jnp.where(i == 0, lead_ref[...], blocks[0])
    return jnp.concatenate(blocks, axis=0)


def _tile_specs(src, tm, order=lambda i: i):
    if src[0] == 'h':
        return [pl.BlockSpec((1, tm, D_MODEL), lambda b, i: (b, order(i), 0))], [src[1]]
    x, lead = src[1], src[2]
    x4 = x.reshape(x.shape[0], x.shape[1] // BLK, BLK, D_MODEL)
    k = tm // BLK
    specs = [pl.BlockSpec((1, 1, BLK, D_MODEL),
                          lambda b, i, j=j: (b, jnp.maximum(order(i) * k + j - LEAD // BLK, 0), 0, 0))
             for j in range(k)]
    return [pl.BlockSpec(lead.shape, lambda b, i: (0, 0))] + specs, [lead] + [x4] * k


def _halo_specs(src, tm):
    arr = src[1]
    off = 0 if src[0] == 'h' else LEAD // HALO
    last = arr.shape[1] // HALO - 1
    hb = tm // HALO
    prev = pl.BlockSpec((1, HALO, D_MODEL), lambda b, i: (b, jnp.clip(i * hb - 1 - off, 0, last), 0))
    nxt = pl.BlockSpec((1, HALO, D_MODEL), lambda b, i: (b, jnp.clip((i + 1) * hb - off, 0, last), 0))
    return [prev, nxt], [arr, arr]


def _live_rows(tile_rows, tile_index):
    row = tile_index * tile_rows + lax.broadcasted_iota(jnp.int32, (tile_rows, 1), 0)
    return row >= META0


def _dot_nt(a, b):
    return lax.dot_general(a, b, (((1,), (1,)), ((), ())), preferred_element_type=F32)


def _rope_t(xt, cos_t, sin_t):
    half = ROPE_DIM // 2
    parts = []
    for hh in range(xt.shape[0] // HEAD_DIM):
        x1 = xt[hh * HEAD_DIM:hh * HEAD_DIM + half]
        x2 = xt[hh * HEAD_DIM + half:hh * HEAD_DIM + ROPE_DIM]
        parts += [x1 * cos_t - x2 * sin_t, x2 * cos_t + x1 * sin_t, xt[hh * HEAD_DIM + ROPE_DIM:(hh + 1) * HEAD_DIM]]
    return jnp.concatenate(parts, axis=0)


def _scan_tables(dt_raw, dt_bias, a_log, live, acs_ref, tab_ref):
    nh = 2 * SSD_HEADS
    dtr = dt_raw + dt_bias
    dt = jnp.maximum(dtr, 0.0) + jnp.log1p(jnp.exp(-jnp.abs(dtr)))
    dt = jnp.where(live, dt, 0.0)
    a = dt * (-jnp.exp(a_log))
    ri = lax.broadcasted_iota(jnp.int32, (BLK, BLK), 0)
    ci = lax.broadcasted_iota(jnp.int32, (BLK, BLK), 1)
    tri = (ci <= ri).astype(BF16)
    fwd = ci < SSD_HEADS
    fwd_rows = ri < SSD_HEADS
    acs_chunks, tab_chunks = [], []
    for j in range(TM // BLK):
        rows = slice(j * BLK, (j + 1) * BLK)
        aj, dtj = a[rows], dt[rows]
        pre = sum(_dot(tri, part) for part in _split3(aj))
        total = pre[BLK - 1:BLK, :]
        acs = jnp.where(fwd, pre, total - pre + aj)
        acs_ref[0, rows, :] = acs
        acs_t = acs.T[:nh]
        chunk_sum = jnp.where(fwd_rows[:nh, 0:1], acs_t[:, BLK - 1:BLK], acs_t[:, 0:1])
        tabs = [acs_t, dtj.T[:nh],
                (dtj * jnp.exp(total - acs)).T[:nh],
                jnp.exp(acs_t), jnp.broadcast_to(jnp.exp(chunk_sum), (nh, BLK))]
        for k, t in enumerate(tabs):
            tab_ref[0, j, k] = t
        acs_chunks.append(acs)
        tab_chunks.append(tabs)
    return acs_chunks, tab_chunks


def _inproj_kernel(*refs, nt, ntile):
    lead_ref = refs[0] if ntile > 1 else None
    tile_refs = refs[ntile > 1:ntile + (ntile > 1)]
    (hp_ref, hn_ref, nw_ref, wz_ref, wx_ref, wdt_ref, wwq_ref, wwk_ref, wwv_ref,
     wnq_ref, wnk_ref, wnv_ref, cw_ref, cb_ref, dtb_ref, alog_ref, rct_ref, rst_ref,
     z_ref, xt_ref, bc_ref, acs_ref, tab_ref, wq_ref, wk_ref, wv_ref, nq_ref, nkv_ref, yf_ref,
     state_ref) = refs[ntile + (ntile > 1):]
    i = pl.program_id(1)

    @pl.when(i == 0)
    def _():
        state_ref[...] = jnp.zeros_like(state_ref)

    nw = nw_ref[...]
    a32 = _rms(_load_tile(tile_refs, lead_ref, i), nw)
    a = a32.astype(BF16)
    head = jnp.where(i > 0, 1.0, 0.0)
    tail = jnp.where(i < nt - 1, 1.0, 0.0)
    a_ext = jnp.concatenate([_rms(hp_ref[0], nw) * head, a32, _rms(hn_ref[0], nw) * tail], axis=0).astype(BF16)
    xe = _dot(a_ext, wx_ref[...])

    def put_blocks(ref, xt):
        for j in range(TM // BLK):
            ref[0, j] = xt[:, j * BLK:(j + 1) * BLK].astype(BF16)

    put_blocks(z_ref, _dot(a, wz_ref[...]).astype(BF16).T)
    dt_raw = _dot(a, wdt_ref[...])
    pad = SSD_CONV // 2
    acc = _shift_rows(xe, -pad) * cw_ref[0:1, :] + cb_ref[...]
    for j in range(1, SSD_CONV):
        acc = acc + _shift_rows(xe, j - pad) * cw_ref[j:j + 1, :]
    xc = acc * (1.0 / (1.0 + jnp.exp(-acc)))
    xt_all = xc[:, :SSD_INNER].astype(BF16).T
    xt_chunks = [xt_all[:, j * BLK:(j + 1) * BLK] for j in range(TM // BLK)]
    for j in range(TM // BLK):
        xt_ref[0, j] = xt_chunks[j]
    bc = xc[:, SSD_INNER:].astype(BF16)
    bc_ref[0] = bc
    acs_chunks, tab_chunks = _scan_tables(dt_raw, dtb_ref[...], alog_ref[...], _live_rows(TM, i),
                                          acs_ref, tab_ref)

    def put_y(j, rows, y):
        yf_ref[0, j, rows, :] = y

    def scan_chunk(j):
        _ssd_scan(lambda j, rows: xt_chunks[j][rows],
                  lambda j, k: bc[j * BLK:(j + 1) * BLK, k * SSD_STATE:(k + 1) * SSD_STATE],
                  lambda j: acs_chunks[j], lambda j, k: tab_chunks[j][k], state_ref, put_y, None,
                  reverse=False, chunks=[j])

    scale = HEAD_DIM ** -0.5
    cos_t, sin_t = rct_ref[...], rst_ref[...]
    put_blocks(wq_ref, _rope_t(_dot(a, wwq_ref[...]).T, cos_t, sin_t) * scale)
    scan_chunk(0)
    wk_ref[0] = _rope_t(_dot(a, wwk_ref[...]).T, cos_t, sin_t).astype(BF16).T
    put_blocks(wv_ref, _dot(a, wwv_ref[...]).astype(BF16).T)
    scan_chunk(1)
    put_blocks(nq_ref, (_dot(a, wnq_ref[...]) * scale).astype(BF16).T)
    nkv_ref[0, :, :NA_HEADS * HEAD_DIM] = _dot(a, wnk_ref[...]).astype(BF16)
    scan_chunk(2)
    nkv_ref[0, :, NA_HEADS * HEAD_DIM:] = _dot(a, wnv_ref[...]).astype(BF16)


def _inproj(src, lp, nw, ws, conv, scan, rope):
    bsz = src[1].shape[0]
    nt = lp // TM
    nb = lp // BLK
    tile = lambda n: pl.BlockSpec((1, TM, n), lambda b, i: (b, i, 0))
    tblk = lambda n: pl.BlockSpec((1, TM // BLK, n, BLK), lambda b, i: (b, i, 0, 0))
    tile_specs, tile_args = _tile_specs(src, TM)
    halo_specs, halo_args = _halo_specs(src, TM)
    full = lambda arr: pl.BlockSpec(arr.shape, lambda b, i: (0,) * arr.ndim)
    rtab_t = pl.BlockSpec((ROPE_DIM // 2, TM), lambda b, i: (0, i))
    tok = lambda n, d: (tile(n), jax.ShapeDtypeStruct((bsz, lp, n), d))
    blk = lambda n: (tblk(n), jax.ShapeDtypeStruct((bsz, nb, n, BLK), BF16))
    kvw = WIN_KV_HEADS * HEAD_DIM
    tabs = (pl.BlockSpec((1, TM // BLK, 5, 2 * SSD_HEADS, BLK), lambda b, i: (b, i, 0, 0, 0)),
            jax.ShapeDtypeStruct((bsz, nb, 5, 2 * SSD_HEADS, BLK), F32))
    outs = [blk(SSD_INNER), blk(SSD_INNER), tok(SSD_XBC - SSD_INNER, BF16), tok(128, F32), tabs, blk(WIN_Q_HEADS * HEAD_DIM), tok(kvw, BF16),
            blk(kvw), blk(NA_HEADS * HEAD_DIM), tok(2 * NA_HEADS * HEAD_DIM, BF16),
            (tblk(SSD_INNER), jax.ShapeDtypeStruct((bsz, nb, SSD_INNER, BLK), F32))]
    return pl.pallas_call(
        functools.partial(_inproj_kernel, nt=nt, ntile=len(tile_specs) - (src[0] == 'x')),
        grid=(bsz, nt),
        in_specs=tile_specs + halo_specs + [full(nw)] + [full(w) for w in ws] + [full(c) for c in conv + scan]
        + [rtab_t, rtab_t],
        out_specs=[o[0] for o in outs],
        out_shape=[o[1] for o in outs],
        scratch_shapes=[pltpu.VMEM((SSD_GROUPS, SSD_INNER // SSD_GROUPS, SSD_STATE), F32)],
        compiler_params=_cparams(("parallel", "arbitrary")),
        name="inproj",
    )(*tile_args, *halo_args, nw, *ws, *conv, *scan, *rope)


def _split3(x):
    hi = x.astype(BF16)
    r1 = x - hi.astype(F32)
    mid = r1.astype(BF16)
    lo = (r1 - mid.astype(F32)).astype(BF16)
    return hi, mid, lo


def _ssd_scan(get_x, get_bc, get_acs, get_tab, state_ref, put_y, after_chunk, *, reverse, chunks=None):
    ri = lax.broadcasted_iota(jnp.int32, (BLK, BLK), 0)
    ci = lax.broadcasted_iota(jnp.int32, (BLK, BLK), 1)
    feeds = (ri >= ci) if reverse else (ri <= ci)
    hoff = SSD_HEADS if reverse else 0
    rep = SSD_HEADS // SSD_GROUPS
    if chunks is None:
        chunks = reversed(range(STEP_BLKS)) if reverse else range(STEP_BLKS)
    for j in chunks:
        acs = get_acs(j)
        acs_t, dt_t, dt_out_t, e_in_t, e_chunk = (get_tab(j, k) for k in range(5))
        for g in range(SSD_GROUPS):
            bm = get_bc(j, g)
            cm = get_bc(j, SSD_GROUPS + g)
            cbt = _dot_nt(bm, cm)
            ht = state_ref[g]
            y_off = _dot_nt(ht.astype(BF16), cm)
            x_out, decay = [], []
            for r in range(rep):
                h = g * rep + r
                hl = hoff + h
                rows = slice(h * HEAD_DIM, (h + 1) * HEAD_DIM)
                xt = get_x(j, rows).astype(F32)
                x_in = (xt * dt_t[hl:hl + 1, :]).astype(BF16)
                x_out.append((xt * dt_out_t[hl:hl + 1, :]).astype(BF16))
                decay.append(jnp.broadcast_to(e_chunk[hl:hl + 1, :], (HEAD_DIM, SSD_STATE)))
                diff = acs_t[hl:hl + 1, :] - acs[:, hl:hl + 1]
                lt = (cbt * jnp.exp(jnp.where(feeds, diff, -jnp.inf))).astype(BF16)
                put_y(j, rows, _dot(x_in, lt) + y_off[r * HEAD_DIM:(r + 1) * HEAD_DIM] * e_in_t[hl:hl + 1, :])
            s_new = _dot(jnp.concatenate(x_out, axis=0), bm)
            state_ref[g] = ht * jnp.concatenate(decay, axis=0) + s_new
        if after_chunk is not None:
            after_chunk(j)


def _softmax_pv_t(scores, values, pad_last, sink_logit):
    mx = functools.reduce(jnp.maximum, [jnp.max(s, axis=0, keepdims=True) for s in scores] + [sink_logit])
    ps = [jnp.exp(s - mx) for s in scores]
    den = functools.reduce(jnp.add, [jnp.sum(p, axis=0, keepdims=True) for p in ps]) + jnp.exp(sink_logit - mx)
    pb = [p.astype(BF16) for p in ps]
    pb[-1] = jnp.concatenate([pad_last, pb[-1]], axis=0)
    return _dot(jnp.concatenate(values, axis=1), jnp.concatenate(pb, axis=0)), den


def _block_diag_rows(pieces):
    zero = jnp.zeros_like(pieces[0][0])
    rows = []
    for i in range(len(pieces)):
        row = []
        for j, ps in enumerate(pieces):
            row += [p if i == j else zero for p in ps]
        rows.append(jnp.concatenate(row, axis=1))
    return jnp.concatenate(rows, axis=0)


def _win_block(win_refs, j, tile, nb, put):
    qt_ref, kp_ref, kc_ref, kn_ref, km_ref, vp_ref, vc_ref, vn_ref, vm_ref, sink_ref = win_refs
    width = WIN_Q_HEADS * BLK
    ki = lax.broadcasted_iota(jnp.int32, (BLK, width), 0)
    qi = lax.broadcasted_iota(jnp.int32, (BLK, width), 1) & (BLK - 1)
    rep = WIN_Q_HEADS // WIN_KV_HEADS
    keys = [kp_ref[0]] + [kc_ref[0, jj * BLK:(jj + 1) * BLK, :] for jj in range(STEP_BLKS)] + [kn_ref[0]]
    vals = [vp_ref[0, 0]] + [vc_ref[0, jj] for jj in range(STEP_BLKS)] + [vn_ref[0, 0]]
    kmeta, vmeta = km_ref[0, META0:, :], vm_ref[0, 0]
    pad_last = jnp.zeros((META0, width), BF16)
    n = tile * STEP_BLKS + j
    ok_prev = jnp.logical_and(ki >= qi, n >= 2)
    ok_cur = n >= 1
    ok_next = jnp.logical_and(ki <= qi, n + 1 <= nb - 1)
    heads = [qt_ref[0, j, h * HEAD_DIM:(h + 1) * HEAD_DIM, :] for h in range(WIN_Q_HEADS)]
    qbd = _block_diag_rows([heads[g * rep:(g + 1) * rep] for g in range(WIN_KV_HEADS)])
    s = _dot(jnp.concatenate(keys[j:j + 3] + [kmeta], axis=0), qbd)
    scores = [jnp.where(ok_prev, s[:BLK], -jnp.inf),
              jnp.where(ok_cur, s[BLK:2 * BLK], -jnp.inf),
              jnp.where(ok_next, s[2 * BLK:3 * BLK], -jnp.inf),
              s[3 * BLK:]]
    o, den = _softmax_pv_t(scores, vals[j:j + 3] + [vmeta], pad_last, sink_ref[...])
    o = o / den
    outs = [o[(h // rep) * HEAD_DIM:(h // rep + 1) * HEAD_DIM, h * BLK:(h + 1) * BLK] for h in range(WIN_Q_HEADS)]
    put(j, jnp.concatenate(outs, axis=0).astype(BF16).T)


def _win_specs(qt, k, vt, sink, order):
    nb = qt.shape[1]
    kvw = WIN_KV_HEADS * HEAD_DIM
    assert kvw == BLK and WIN_RADIUS == BLK
    kedge = lambda f: pl.BlockSpec((1, BLK, kvw), lambda b, i: (b, f(order(i)), 0))
    vedge = lambda f: pl.BlockSpec((1, 1, kvw, BLK), lambda b, i: (b, f(order(i)), 0, 0))
    before = lambda t: jnp.maximum(t * STEP_BLKS - 1, 0)
    after = lambda t: jnp.minimum((t + 1) * STEP_BLKS, nb - 1)
    first = lambda t: 0
    specs = [pl.BlockSpec((1, STEP_BLKS, WIN_Q_HEADS * HEAD_DIM, BLK), lambda b, i: (b, order(i), 0, 0)),
             kedge(before), pl.BlockSpec((1, TM, kvw), lambda b, i: (b, order(i), 0)), kedge(after), kedge(first),
             vedge(before), pl.BlockSpec((1, STEP_BLKS, kvw, BLK), lambda b, i: (b, order(i), 0, 0)), vedge(after),
             vedge(first), pl.BlockSpec(sink.shape, lambda b, i: (0, 0))]
    return specs, [qt, k, k, k, k, vt, vt, vt, vt, sink]


NA_QUAD = 4
NA_QW = NA_QUAD * HEAD_DIM
NA_WIN = NA_KR * GRID_W


def _na_bias_tables(rpb):
    c = np.arange(GRID_W)[None, :]
    kc = np.arange(GRID_W)[:, None]
    cs = np.clip(c - NA_KC // 2, 0, GRID_W - NA_KC)
    ok = (kc >= cs) & (kc < cs + NA_KC)
    pick = jnp.asarray((kc - c + NA_KC - 1)[None] == np.arange(2 * NA_KC - 1)[:, None, None], F32)
    t = jnp.einsum('hed,dkc->hekc', rpb.astype(F32), pick, precision=lax.Precision.HIGHEST)
    t = jnp.where(jnp.asarray(ok)[None, None], t, -jnp.inf)

    def pack(x):
        h, e = x.shape[:2]
        x = x.reshape(h // NA_QUAD, NA_QUAD, e, GRID_W, GRID_W)
        return x.transpose(0, 2, 3, 1, 4).reshape(h // NA_QUAD, e, GRID_W, NA_QW)

    meta = jnp.broadcast_to(t[:, NA_KR - 1:, :, 0:1], (t.shape[0], NA_KR, GRID_W, GRID_W))
    return pack(t), pack(meta)


NA_KV_BLKS = 7


def _na_window_base(tile, nb):
    return jnp.clip(tile * STEP_BLKS - STEP_BLKS, 0, nb - 1 - NA_KV_BLKS)


def _na_blocks(na_refs, kbuf_ref, vbuf_ref, tile, nb, rows, put, blocks):
    qt_ref, lead_ref, bias_ref, mbias_ref, mb_ref = na_refs
    width = NA_HEADS * HEAD_DIM
    first_half = lax.broadcasted_iota(jnp.int32, (width, BLK), 1) < GRID_W
    ri = lax.broadcasted_iota(jnp.int32, (NA_QW, NA_QW), 0)
    ci = lax.broadcasted_iota(jnp.int32, (NA_QW, NA_QW), 1)
    diag = lax.shift_right_logical(ri, 6) == lax.shift_right_logical(ci, 6)
    head_of_lane = lax.shift_right_logical(lax.broadcasted_iota(jnp.int32, (GRID_W, NA_QW), 1), 6)
    pad_meta = jnp.zeros((META0, NA_QW), F32)
    row0 = _na_window_base(tile, nb) * (BLK // GRID_W)

    def scores(j, half, u, meta):
        qf = qt_ref[0, j].astype(F32)
        qsw = pltpu.roll(qf, GRID_W, 1)
        dup = jnp.where(first_half, qf, qsw) if half == 0 else jnp.where(first_half, qsw, qf)
        if meta:
            start, e0 = 0, None
        else:
            r = (tile * STEP_BLKS + j - 1) * (BLK // GRID_W) + half
            rs = jnp.clip(r - NA_KR // 2, 0, rows - NA_KR)
            start = pl.multiple_of((rs - row0) * GRID_W, GRID_W)
            e0 = rs - r + (NA_KR - 1)
        cols = slice(u * NA_QW, (u + 1) * NA_QW)
        base = dup[u * NA_QW:(u + 1) * NA_QW]
        qbd = jnp.where(diag, jnp.concatenate([base, base], axis=1), 0.0).astype(BF16)
        keys = jnp.concatenate([kbuf_ref[pl.ds(start, NA_WIN), cols], lead_ref[0, META0:LEAD, cols]], axis=0)
        return _dot(keys, qbd), (j, half, u, start, e0, cols)

    def finish(s, unit):
        j, half, u, start, e0, cols = unit
        if e0 is None:
            bias = jnp.concatenate([mbias_ref[u, i] for i in range(NA_KR)], axis=0)
        else:
            bias = jnp.concatenate([bias_ref[u, e0 + i] for i in range(NA_KR)], axis=0)
        sw = s[:NA_WIN] + bias
        sm = s[NA_WIN:] + mb_ref[u]
        mx = jnp.maximum(jnp.max(sw, axis=0, keepdims=True), jnp.max(sm, axis=0, keepdims=True))
        pw, pm = jnp.exp(sw - mx), jnp.exp(sm - mx)
        rden = 1.0 / (jnp.sum(pw, axis=0, keepdims=True) + jnp.sum(pm, axis=0, keepdims=True))
        p_t = jnp.concatenate([pw * rden, pad_meta, pm * rden], axis=0).astype(BF16).T
        vals = jnp.concatenate([vbuf_ref[pl.ds(start, NA_WIN), cols],
                                lead_ref[0, :, u * NA_QW + width:(u + 1) * NA_QW + width]], axis=0)
        o = _dot(p_t, vals)
        out = o[(NA_QUAD - 1) * GRID_W:]
        for hq in range(NA_QUAD - 2, -1, -1):
            out = jnp.where(head_of_lane == hq, o[hq * GRID_W:(hq + 1) * GRID_W], out)
        put(j, half, cols, out.astype(BF16))

    def run(units):
        pending = scores(*units[0])
        for nxt in units[1:]:
            ahead = scores(*nxt)
            finish(*pending)
            pending = ahead
        finish(*pending)

    quads = range(NA_HEADS // NA_QUAD)
    if 0 in blocks:
        @pl.when(tile == 0)
        def _():
            for u in quads:
                put(0, 0, slice(u * NA_QW, (u + 1) * NA_QW), jnp.zeros((GRID_W, NA_QW), BF16))
            run([(0, 1, u, True) for u in quads])

        @pl.when(tile > 0)
        def _():
            run([(0, half, u, False) for half in range(2) for u in quads])

    rest = [j for j in blocks if j > 0]
    if rest:
        run([(j, half, u, False) for j in rest for half in range(2) for u in quads])


def _na_load_window(kv_refs, kbuf_ref, vbuf_ref):
    width = kbuf_ref.shape[1]
    for m in range(NA_KV_BLKS):
        kbuf_ref[m * BLK:(m + 1) * BLK, :] = kv_refs[m][0, :, :width]
        vbuf_ref[m * BLK:(m + 1) * BLK, :] = kv_refs[m][0, :, width:]


def _na_specs(qt, kv, bias, mbias, mb, order):
    nb, width = qt.shape[1], kv.shape[2]
    rows = (nb * BLK - LEAD) // GRID_W
    assert nb - 1 >= NA_KV_BLKS and rows >= NA_KR and N_META <= GRID_W and LEAD - GRID_W <= META0
    blk = lambda f: pl.BlockSpec((1, BLK, width), lambda b, i: (b, f(order(i)), 0))
    lead = blk(lambda t: 0)
    win = [blk(lambda t, m=m: LEAD // BLK + _na_window_base(t, nb) + m) for m in range(NA_KV_BLKS)]
    specs = [pl.BlockSpec((1, STEP_BLKS, qt.shape[2], BLK), lambda b, i: (b, order(i), 0, 0)), lead,
             _resident(bias), _resident(mbias), _resident(mb)] + win
    return specs, [qt, kv, bias, mbias, mb] + [kv] * NA_KV_BLKS


N_NA_REFS = 5


def _mixout_kernel(*refs, ns, ntile):
    lead_ref = refs[0] if ntile > 1 else None
    tile_refs = refs[ntile > 1:ntile + (ntile > 1)]
    rest = refs[ntile + (ntile > 1):]
    win_refs, rest = rest[:N_WIN_REFS], rest[N_WIN_REFS:]
    na_refs, rest = rest[:N_NA_REFS], rest[N_NA_REFS:]
    kv_refs, rest = rest[:NA_KV_BLKS], rest[NA_KV_BLKS:]
    (xt_ref, bc_ref, acs_ref, tab_ref, zt_ref, yf_ref, dsk_ref, nws_ref, w1_ref, w2_ref, w3_ref,
     nw_ref, o_ref, state_ref, ybuf_ref, ys_ref, yw_ref, yn_ref, kbuf_ref, vbuf_ref) = rest
    i = ns - 1 - pl.program_id(1)
    nb = ns * STEP_BLKS

    @pl.when(pl.program_id(1) == 0)
    def _():
        state_ref[...] = jnp.zeros_like(state_ref)

    def put_y(j, rows, y):
        ybuf_ref[rows, :] = y

    def finish(j):
        y = ybuf_ref[...] + yf_ref[0, j] + dsk_ref[...] * xt_ref[0, j].astype(F32)
        z = zt_ref[0, j].astype(F32)
        y = y * (z * (1.0 / (1.0 + jnp.exp(-z))))
        y = y * lax.rsqrt(jnp.mean(y * y, axis=0, keepdims=True) + EPS) * nws_ref[...]
        ys_ref[j * BLK:(j + 1) * BLK, :] = y.astype(BF16).T

    def scan_chunk(j):
        _ssd_scan(lambda j, rows: xt_ref[0, j, rows, :],
                  lambda j, k: bc_ref[0, j * BLK:(j + 1) * BLK, k * SSD_STATE:(k + 1) * SSD_STATE],
                  lambda j: acs_ref[0, j * BLK:(j + 1) * BLK, :],
                  lambda j, k: tab_ref[0, j, k], state_ref, put_y, finish, reverse=True, chunks=[j])

    def put_win(j, y):
        yw_ref[j * BLK:(j + 1) * BLK, :] = y

    def put_na(j, half, cols, y):
        yn_ref[pl.ds(j * BLK + half * GRID_W, GRID_W), cols] = y

    def na_blocks(blocks):
        _na_blocks(na_refs, kbuf_ref, vbuf_ref, i, nb, (nb * BLK - LEAD) // GRID_W, put_na, blocks)

    _na_load_window(kv_refs, kbuf_ref, vbuf_ref)
    na_blocks([0])
    for j in reversed(range(STEP_BLKS)):
        scan_chunk(j)
        _win_block(win_refs, j, i, nb, put_win)
        if j > 0:
            na_blocks([j])
    mix = _dot(yn_ref[...], w3_ref[...]) + _dot(yw_ref[...], w2_ref[...]) + _dot(ys_ref[...], w1_ref[...])
    out = _load_tile(tile_refs, lead_ref, i) + _rms(mix, nw_ref[...])
    o_ref[0] = jnp.where(_live_rows(TM, i), out, 0.0)


N_WIN_REFS = 10


def _mixout(src, win_args, na_args, xt, bc, acs, tab, zt, y_fwd, dsk, nws, w1, w2, w3, nw):
    bsz, nc = xt.shape[:2]
    ns = nc // STEP_BLKS
    order = lambda c: ns - 1 - c
    tok = lambda n: pl.BlockSpec((1, TM, n), lambda b, c: (b, order(c), 0))
    feat = pl.BlockSpec((1, STEP_BLKS, SSD_INNER, BLK), lambda b, c: (b, order(c), 0, 0))
    tabs = pl.BlockSpec((1, STEP_BLKS) + tab.shape[2:], lambda b, c: (b, order(c), 0, 0, 0))
    tile_specs, tile_args = _tile_specs(src, TM, order)
    win_specs, win_ops = _win_specs(*win_args, order)
    na_specs, na_ops = _na_specs(*na_args, order)
    assert len(win_specs) == N_WIN_REFS and len(na_specs) == N_NA_REFS + NA_KV_BLKS
    na_width = NA_HEADS * HEAD_DIM
    consts = [dsk, nws, w1, w2, w3, nw]
    return pl.pallas_call(
        functools.partial(_mixout_kernel, ns=ns, ntile=len(tile_specs) - (src[0] == 'x')),
        grid=(bsz, ns),
        in_specs=tile_specs + win_specs + na_specs + [feat, tok(bc.shape[-1]), tok(128), tabs, feat, feat]
        + [_resident(c) for c in consts],
        out_specs=tok(D_MODEL),
        out_shape=jax.ShapeDtypeStruct((bsz, nc * BLK, D_MODEL), F32),
        scratch_shapes=[pltpu.VMEM((SSD_GROUPS, SSD_INNER // SSD_GROUPS, SSD_STATE), F32),
                        pltpu.VMEM((SSD_INNER, BLK), F32), pltpu.VMEM((TM, SSD_INNER), BF16),
                        pltpu.VMEM((TM, WIN_Q_HEADS * HEAD_DIM), BF16), pltpu.VMEM((TM, na_width), BF16),
                        pltpu.VMEM((NA_KV_BLKS * BLK, na_width), BF16), pltpu.VMEM((NA_KV_BLKS * BLK, na_width), BF16)],
        compiler_params=_cparams(("parallel", "arbitrary")),
        name="mixout",
    )(*tile_args, *win_ops, *na_ops, xt, bc, acs, tab, zt, y_fwd, *consts)


def _ffn_kernel(*refs, nt, ntile, padded):
    tile_refs = refs[:ntile]
    hp_ref, hn_ref, nw1_ref, wup_ref, cw_ref, cb_ref, wdn_ref, nw2_ref, o_ref, act_ref = refs[ntile:]
    i = pl.program_id(1)
    rows = act_ref.shape[0]
    nw1 = nw1_ref[...]
    hc = _load_tile(tile_refs, None, i)
    f = jnp.concatenate([_rms(hp_ref[0], nw1), _rms(hc, nw1), _rms(hn_ref[0], nw1)], axis=0).astype(BF16)
    tail = jnp.where(i < nt - 1, 1.0, 0.0)
    nch = D_FF // FF_CHUNK

    def conv(g, c):
        g = jnp.concatenate([g[:HALO + rows], g[HALO + rows:] * tail], axis=0)
        cols = slice(c * FF_CHUNK, (c + 1) * FF_CHUNK)
        out = _shift_rows(g, -1) * cw_ref[0:1, cols] + cb_ref[:, cols]
        out = out + g[HALO:HALO + rows] * cw_ref[1:2, cols]
        return out + _shift_rows(g, 1) * cw_ref[2:3, cols]

    for c in range(nch):
        gate = conv(_dot(f, wup_ref[:, c * FF_CHUNK:(c + 1) * FF_CHUNK]), c)
        up = conv(_dot(f, wup_ref[:, D_FF + c * FF_CHUNK:D_FF + (c + 1) * FF_CHUNK]), nch + c)
        th = jnp.tanh(gate * (GELU_K + (GELU_K * GELU_C) * (gate * gate)))
        act_ref[:, c * FF_CHUNK:(c + 1) * FF_CHUNK] = (gate * (0.5 * th + 0.5) * up).astype(BF16)
    out = hc + _rms(_dot(act_ref[...], wdn_ref[...]), nw2_ref[...])
    o_ref[0] = jnp.where(_live_rows(rows, i), out, 0.0) if padded else out


def _ffn(h, nw1, wup, cw, cb, wdn, nw2, *, last):
    bsz, lp, _ = h.shape
    full = _resident
    if last:
        n_out = lp - LEAD
        tm = _tile_rows(n_out, FFN_OUT_TILE_CAP, BLK)
        h4 = h.reshape(bsz, lp // BLK, BLK, D_MODEL)
        k = tm // BLK
        tile_specs = [pl.BlockSpec((1, 1, BLK, D_MODEL), lambda b, i, j=j: (b, LEAD // BLK + i * k + j, 0, 0))
                      for j in range(k)]
        tile_args = [h4] * k
        off = LEAD // HALO
    else:
        tm, n_out = TM, lp
        tile_specs, tile_args = _tile_specs(('h', h), tm)
        off = 0
    nt = n_out // tm
    hb = tm // HALO
    last_blk = lp // HALO - 1
    prev = pl.BlockSpec((1, HALO, D_MODEL), lambda b, i: (b, jnp.clip(off + i * hb - 1, 0, last_blk), 0))
    nxt = pl.BlockSpec((1, HALO, D_MODEL), lambda b, i: (b, jnp.clip(off + (i + 1) * hb, 0, last_blk), 0))
    return pl.pallas_call(
        functools.partial(_ffn_kernel, nt=nt, ntile=len(tile_specs), padded=not last),
        grid=(bsz, nt),
        in_specs=tile_specs + [prev, nxt, full(nw1), full(wup), full(cw), full(cb), full(wdn), full(nw2)],
        out_specs=pl.BlockSpec((1, tm, D_MODEL), lambda b, i: (b, i, 0)),
        out_shape=jax.ShapeDtypeStruct((bsz, n_out, D_MODEL), F32),
        scratch_shapes=[pltpu.VMEM((tm, D_FF), BF16)],
        compiler_params=_cparams(("parallel", "parallel")),
        name="ffn",
    )(*tile_args, h, h, nw1, wup, cw, cb, wdn, nw2)


def _pad_lanes(x, n):
    return jnp.pad(x, [(0, 0)] * (x.ndim - 1) + [(0, n - x.shape[-1])])


def _rope_tables(lp):
    half = ROPE_DIM // 2
    pos = jnp.maximum(jnp.arange(lp) - META0, 0).astype(F32)
    inv = jnp.power(ROPE_THETA, -jnp.arange(half, dtype=F32) / half)
    ang = pos[:, None] * inv[None, :]
    return jnp.cos(ang).T, jnp.sin(ang).T


def _layer_params(i, p):
    row = lambda v: v.reshape(1, -1).astype(F32)
    lanes = lambda v: jnp.broadcast_to(v.astype(F32)[:, None], (v.shape[0], BLK))
    sizes = [SSD_INNER, SSD_XBC, 2 * SSD_HEADS, WIN_Q_HEADS * HEAD_DIM, WIN_KV_HEADS * HEAD_DIM,
             WIN_KV_HEADS * HEAD_DIM, NA_HEADS * HEAD_DIM, NA_HEADS * HEAD_DIM, NA_HEADS * HEAD_DIM]
    w_in = p['w_in'][i].astype(BF16)
    ws = jnp.split(w_in, np.cumsum(sizes)[:-1].tolist(), axis=1)
    ws[2] = _pad_lanes(ws[2], 128)
    w_out = p['w_out'][i].astype(BF16)
    return dict(
        norm_mix_pre=row(p['norm_mix_pre'][i]),
        w_in=ws,
        ssd_conv=(p['ssd_conv_w'][i].astype(F32), row(p['ssd_conv_b'][i])),
        ssd_consts=(_pad_lanes(row(p['ssd_dt_bias'][i]), 128), _pad_lanes(row(p['ssd_a_log'][i]), 128)),
        ssd_d=lanes(jnp.repeat(p['ssd_d'][i], HEAD_DIM)),
        ssd_norm_w=lanes(p['ssd_norm_w'][i]),
        win_sink=row(jnp.repeat(p['win_sink'][i], BLK)),
        na_bias=_na_bias_tables(p['na_rpb'][i]),
        na_meta_bias=jnp.broadcast_to(p['na_meta_bias'][i].astype(F32).T.reshape(N_META, NA_HEADS // NA_QUAD, NA_QUAD, 1),
                                      (N_META, NA_HEADS // NA_QUAD, NA_QUAD, GRID_W)).transpose(1, 0, 2, 3).reshape(
            NA_HEADS // NA_QUAD, N_META, NA_QW),
        w_out=(w_out[:SSD_INNER], w_out[SSD_INNER:SSD_INNER + WIN_Q_HEADS * HEAD_DIM],
               w_out[SSD_INNER + WIN_Q_HEADS * HEAD_DIM:]),
        norm_mix_post=row(p['norm_mix_post'][i]),
        norm_ffn_pre=row(p['norm_ffn_pre'][i]),
        ffn_w_up=p['ffn_w_up'][i].astype(BF16),
        ffn_conv_w=p['ffn_conv_w'][i].astype(F32),
        ffn_conv_b=row(p['ffn_conv_b'][i]),
        ffn_w_down=p['ffn_w_down'][i].astype(BF16),
        norm_ffn_post=row(p['norm_ffn_post'][i]),
    )


def _encode(x, meta_tokens, layers):
    bsz, n_tok, _ = x.shape
    assert (n_tok + LEAD) % TM == 0 and n_tok // GRID_W >= NA_KR
    lp = LEAD + n_tok
    lead = jnp.concatenate([jnp.zeros((META0, D_MODEL), F32), meta_tokens.astype(F32)], axis=0)
    src = ('x', x.astype(F32), lead)
    rope = _rope_tables(lp)
    for li, lw in enumerate(layers):
        zt, xt, bc, acs, tab, wq, wk, wv, nq, nkv, y_fwd = _inproj(src, lp, lw['norm_mix_pre'], lw['w_in'],
                                                                      lw['ssd_conv'], lw['ssd_consts'], rope)
        h = _mixout(src, (wq, wk, wv, lw['win_sink']), (nq, nkv, *lw['na_bias'], lw['na_meta_bias']), xt, bc, acs,
                    tab, zt, y_fwd, lw['ssd_d'], lw['ssd_norm_w'], *lw['w_out'], lw['norm_mix_post'])
        h = _ffn(h, lw['norm_ffn_pre'], lw['ffn_w_up'], lw['ffn_conv_w'], lw['ffn_conv_b'],
                 lw['ffn_w_down'], lw['norm_ffn_post'], last=li == len(layers) - 1)
        src = ('h', h)
    return h


def kernel(x_prompt, x_sample, meta_tokens, norm_mix_pre, norm_mix_post, w_in, ssd_conv_w, ssd_conv_b,
           ssd_dt_bias, ssd_a_log, ssd_d, ssd_norm_w, win_sink, na_rpb, na_meta_bias, w_out, norm_ffn_pre,
           norm_ffn_post, ffn_w_up, ffn_conv_w, ffn_conv_b, ffn_w_down):
    p = dict(norm_mix_pre=norm_mix_pre, norm_mix_post=norm_mix_post, w_in=w_in, ssd_conv_w=ssd_conv_w,
             ssd_conv_b=ssd_conv_b, ssd_dt_bias=ssd_dt_bias, ssd_a_log=ssd_a_log, ssd_d=ssd_d,
             ssd_norm_w=ssd_norm_w, win_sink=win_sink, na_rpb=na_rpb, na_meta_bias=na_meta_bias, w_out=w_out,
             norm_ffn_pre=norm_ffn_pre, norm_ffn_post=norm_ffn_post, ffn_w_up=ffn_w_up, ffn_conv_w=ffn_conv_w,
             ffn_conv_b=ffn_conv_b, ffn_w_down=ffn_w_down)
    layers = [_layer_params(i, p) for i in range(w_in.shape[0])]
    return (_encode(x_prompt, meta_tokens, layers), _encode(x_sample, meta_tokens, layers))
```

```python
import functools

import jax
import jax.numpy as jnp
import numpy as np
from jax import lax
from jax.experimental import pallas as pl
from jax.experimental.pallas import tpu as pltpu

F32 = jnp.float32
BF16 = jnp.bfloat16

D_MODEL = 1024
N_META = 16
GRID_W = 64
HEAD_DIM = 64

SSD_HEADS = 16
SSD_INNER = SSD_HEADS * HEAD_DIM
SSD_GROUPS = 2
SSD_STATE = 128
SSD_XBC = SSD_INNER + 2 * SSD_GROUPS * SSD_STATE
SSD_CONV = 5

WIN_Q_HEADS = 8
WIN_KV_HEADS = 2
WIN_RADIUS = 128
ROPE_THETA = 500000.0
ROPE_DIM = HEAD_DIM // 4

NA_HEADS = 8
NA_KR = 8
NA_KC = 16

D_FF = 2816
EPS = 1e-6
GELU_K = float(np.sqrt(2.0 / np.pi))
GELU_C = 0.044715

BLK = 128
LEAD = BLK
META0 = LEAD - N_META
HALO = 8
TM = 384
FFN_OUT_TILE_CAP = 720
FF_CHUNK = 256
STEP_BLKS = TM // BLK
VMEM_LIMIT = 56 * 1024 * 1024


def _tile_rows(n, cap, unit=HALO):
    return max(t for t in range(unit, cap + 1, unit) if n % t == 0)


def _resident(arr):
    return pl.BlockSpec(arr.shape, lambda *_: (0,) * arr.ndim, pipeline_mode=pl.Buffered(1))


def _cparams(sem):
    return pltpu.CompilerParams(dimension_semantics=sem, vmem_limit_bytes=VMEM_LIMIT)


def _rms(x, w):
    return x * lax.rsqrt(jnp.mean(x * x, axis=-1, keepdims=True) + EPS) * w


def _dot(a, b):
    return jnp.dot(a, b, preferred_element_type=F32)


def _shift_rows(x, d):
    return pltpu.roll(x, (-d) % x.shape[0], 0)[HALO:x.shape[0] - HALO]


def _load_tile(refs, lead_ref, i):
    if len(refs) == 1:
        return refs[0][0]
    blocks = [r[0, 0] for r in refs]
    if lead_ref is not None:
        blocks[0] = jnp.where(i == 0, lead_ref[...], blocks[0])
    return jnp.concatenate(blocks, axis=0)


def _tile_specs(src, tm, order=lambda i: i):
    if src[0] == 'h':
        return [pl.BlockSpec((1, tm, D_MODEL), lambda b, i: (b, order(i), 0))], [src[1]]
    x, lead = src[1], src[2]
    x4 = x.reshape(x.shape[0], x.shape[1] // BLK, BLK, D_MODEL)
    k = tm // BLK
    specs = [pl.BlockSpec((1, 1, BLK, D_MODEL),
                          lambda b, i, j=j: (b, jnp.maximum(order(i) * k + j - LEAD // BLK, 0), 0, 0))
             for j in range(k)]
    return [pl.BlockSpec(lead.shape, lambda b, i: (0, 0))] + specs, [lead] + [x4] * k


def _halo_specs(src, tm):
    arr = src[1]
    off = 0 if src[0] == 'h' else LEAD // HALO
    last = arr.shape[1] // HALO - 1
    hb = tm // HALO
    prev = pl.BlockSpec((1, HALO, D_MODEL), lambda b, i: (b, jnp.clip(i * hb - 1 - off, 0, last), 0))
    nxt = pl.BlockSpec((1, HALO, D_MODEL), lambda b, i: (b, jnp.clip((i + 1) * hb - off, 0, last), 0))
    return [prev, nxt], [arr, arr]


def _live_rows(tile_rows, tile_index):
    row = tile_index * tile_rows + lax.broadcasted_iota(jnp.int32, (tile_rows, 1), 0)
    return row >= META0


def _dot_nt(a, b):
    return lax.dot_general(a, b, (((1,), (1,)), ((), ())), preferred_element_type=F32)


def _rope_t(xt, cos_t, sin_t):
    half = ROPE_DIM // 2
    parts = []
    for hh in range(xt.shape[0] // HEAD_DIM):
        x1 = xt[hh * HEAD_DIM:hh * HEAD_DIM + half]
        x2 = xt[hh * HEAD_DIM + half:hh * HEAD_DIM + ROPE_DIM]
        parts += [x1 * cos_t - x2 * sin_t, x2 * cos_t + x1 * sin_t, xt[hh * HEAD_DIM + ROPE_DIM:(hh + 1) * HEAD_DIM]]
    return jnp.concatenate(parts, axis=0)


def _scan_tables(dt_raw, dt_bias, a_log, live, acs_ref, tab_ref):
    nh = 2 * SSD_HEADS
    dtr = dt_raw + dt_bias
    dt = jnp.maximum(dtr, 0.0) + jnp.log1p(jnp.exp(-jnp.abs(dtr)))
    dt = jnp.where(live, dt, 0.0)
    a = dt * (-jnp.exp(a_log))
    ri = lax.broadcasted_iota(jnp.int32, (BLK, BLK), 0)
    ci = lax.broadcasted_iota(jnp.int32, (BLK, BLK), 1)
    tri = (ci <= ri).astype(BF16)
    fwd = ci < SSD_HEADS
    fwd_rows = ri < SSD_HEADS
    acs_chunks, tab_chunks = [], []
    for j in range(TM // BLK):
        rows = slice(j * BLK, (j + 1) * BLK)
        aj, dtj = a[rows], dt[rows]
        pre = sum(_dot(tri, part) for part in _split3(aj))
        total = pre[BLK - 1:BLK, :]
        acs = jnp.where(fwd, pre, total - pre + aj)
        acs_ref[0, rows, :] = acs
        acs_t = acs.T[:nh]
        chunk_sum = jnp.where(fwd_rows[:nh, 0:1], acs_t[:, BLK - 1:BLK], acs_t[:, 0:1])
        tabs = [acs_t, dtj.T[:nh],
                (dtj * jnp.exp(total - acs)).T[:nh],
                jnp.exp(acs_t), jnp.broadcast_to(jnp.exp(chunk_sum), (nh, BLK))]
        for k, t in enumerate(tabs):
            tab_ref[0, j, k] = t
        acs_chunks.append(acs)
        tab_chunks.append(tabs)
    return acs_chunks, tab_chunks


def _inproj_kernel(*refs, nt, ntile):
    lead_ref = refs[0] if ntile > 1 else None
    tile_refs = refs[ntile > 1:ntile + (ntile > 1)]
    (hp_ref, hn_ref, nw_ref, wz_ref, wx_ref, wsm_ref, wwq_ref,
     wnq_ref, wnk_ref, wnv_ref, cw_ref, cb_ref, dtb_ref, alog_ref, rct_ref, rst_ref,
     z_ref, xt_ref, bc_ref, acs_ref, tab_ref, wq_ref, wk_ref, wv_ref, nq_ref, nkv_ref, yf_ref,
     state_ref) = refs[ntile + (ntile > 1):]
    i = pl.program_id(1)

    @pl.when(i == 0)
    def _():
        state_ref[...] = jnp.zeros_like(state_ref)

    nw = nw_ref[...]
    a32 = _rms(_load_tile(tile_refs, lead_ref, i), nw)
    a = a32.astype(BF16)
    head = jnp.where(i > 0, 1.0, 0.0)
    tail = jnp.where(i < nt - 1, 1.0, 0.0)
    a_ext = jnp.concatenate([_rms(hp_ref[0], nw) * head, a32, _rms(hn_ref[0], nw) * tail], axis=0).astype(BF16)
    xe = _dot(a_ext, wx_ref[...])

    def put_blocks(ref, xt):
        for j in range(TM // BLK):
            ref[0, j] = xt[:, j * BLK:(j + 1) * BLK].astype(BF16)

    put_blocks(z_ref, _dot(a, wz_ref[...]).astype(BF16).T)
    small = _dot(a, wsm_ref[...])
    dt_raw, k_pre, v_pre = small[:, :BLK], small[:, BLK:2 * BLK], small[:, 2 * BLK:]
    pad = SSD_CONV // 2
    acc = _shift_rows(xe, -pad) * cw_ref[0:1, :] + cb_ref[...]
    for j in range(1, SSD_CONV):
        acc = acc + _shift_rows(xe, j - pad) * cw_ref[j:j + 1, :]
    xc = acc * (1.0 / (1.0 + jnp.exp(-acc)))
    xt_all = xc[:, :SSD_INNER].astype(BF16).T
    xt_chunks = [xt_all[:, j * BLK:(j + 1) * BLK] for j in range(TM // BLK)]
    for j in range(TM // BLK):
        xt_ref[0, j] = xt_chunks[j]
    bc = xc[:, SSD_INNER:].astype(BF16)
    bc_ref[0] = bc
    acs_chunks, tab_chunks = _scan_tables(dt_raw, dtb_ref[...], alog_ref[...], _live_rows(TM, i),
                                          acs_ref, tab_ref)

    def put_y(j, rows, y):
        yf_ref[0, j, rows, :] = y

    def scan_chunk(j):
        _ssd_scan(lambda j, rows: xt_chunks[j][rows],
                  lambda j, k: bc[j * BLK:(j + 1) * BLK, k * SSD_STATE:(k + 1) * SSD_STATE],
                  lambda j: acs_chunks[j], lambda j, k: tab_chunks[j][k], state_ref, put_y, None,
                  reverse=False, chunks=[j])

    scale = HEAD_DIM ** -0.5
    cos_t, sin_t = rct_ref[...], rst_ref[...]
    put_blocks(wq_ref, _rope_t(_dot(a, wwq_ref[...]).T, cos_t, sin_t) * scale)
    scan_chunk(0)
    wk_ref[0] = _rope_t(k_pre.T, cos_t, sin_t).astype(BF16).T
    put_blocks(wv_ref, v_pre.astype(BF16).T)
    scan_chunk(1)
    put_blocks(nq_ref, (_dot(a, wnq_ref[...]) * scale).astype(BF16).T)
    nkv_ref[0, :, :NA_HEADS * HEAD_DIM] = _dot(a, wnk_ref[...]).astype(BF16)
    scan_chunk(2)
    nkv_ref[0, :, NA_HEADS * HEAD_DIM:] = _dot(a, wnv_ref[...]).astype(BF16)


def _inproj(src, lp, nw, ws, conv, scan, rope):
    bsz = src[1].shape[0]
    nt = lp // TM
    nb = lp // BLK
    tile = lambda n: pl.BlockSpec((1, TM, n), lambda b, i: (b, i, 0))
    tblk = lambda n: pl.BlockSpec((1, TM // BLK, n, BLK), lambda b, i: (b, i, 0, 0))
    tile_specs, tile_args = _tile_specs(src, TM)
    halo_specs, halo_args = _halo_specs(src, TM)
    full = lambda arr: pl.BlockSpec(arr.shape, lambda b, i: (0,) * arr.ndim)
    rtab_t = pl.BlockSpec((ROPE_DIM // 2, TM), lambda b, i: (0, i))
    tok = lambda n, d: (tile(n), jax.ShapeDtypeStruct((bsz, lp, n), d))
    blk = lambda n: (tblk(n), jax.ShapeDtypeStruct((bsz, nb, n, BLK), BF16))
    kvw = WIN_KV_HEADS * HEAD_DIM
    tabs = (pl.BlockSpec((1, TM // BLK, 5, 2 * SSD_HEADS, BLK), lambda b, i: (b, i, 0, 0, 0)),
            jax.ShapeDtypeStruct((bsz, nb, 5, 2 * SSD_HEADS, BLK), F32))
    outs = [blk(SSD_INNER), blk(SSD_INNER), tok(SSD_XBC - SSD_INNER, BF16), tok(128, F32), tabs, blk(WIN_Q_HEADS * HEAD_DIM), tok(kvw, BF16),
            blk(kvw), blk(NA_HEADS * HEAD_DIM), tok(2 * NA_HEADS * HEAD_DIM, BF16),
            (tblk(SSD_INNER), jax.ShapeDtypeStruct((bsz, nb, SSD_INNER, BLK), F32))]
    return pl.pallas_call(
        functools.partial(_inproj_kernel, nt=nt, ntile=len(tile_specs) - (src[0] == 'x')),
        grid=(bsz, nt),
        in_specs=tile_specs + halo_specs + [full(nw)] + [full(w) for w in ws] + [full(c) for c in conv + scan]
        + [rtab_t, rtab_t],
        out_specs=[o[0] for o in outs],
        out_shape=[o[1] for o in outs],
        scratch_shapes=[pltpu.VMEM((SSD_GROUPS, SSD_INNER // SSD_GROUPS, SSD_STATE), F32)],
        compiler_params=_cparams(("parallel", "arbitrary")),
        name="inproj",
    )(*tile_args, *halo_args, nw, *ws, *conv, *scan, *rope)


def _split3(x):
    hi = x.astype(BF16)
    r1 = x - hi.astype(F32)
    mid = r1.astype(BF16)
    lo = (r1 - mid.astype(F32)).astype(BF16)
    return hi, mid, lo


def _ssd_scan(get_x, get_bc, get_acs, get_tab, state_ref, put_y, after_chunk, *, reverse, chunks=None):
    ri = lax.broadcasted_iota(jnp.int32, (BLK, BLK), 0)
    ci = lax.broadcasted_iota(jnp.int32, (BLK, BLK), 1)
    feeds = (ri >= ci) if reverse else (ri <= ci)
    hoff = SSD_HEADS if reverse else 0
    rep = SSD_HEADS // SSD_GROUPS
    if chunks is None:
        chunks = reversed(range(STEP_BLKS)) if reverse else range(STEP_BLKS)
    for j in chunks:
        acs = get_acs(j)
        acs_t, dt_t, dt_out_t, e_in_t, e_chunk = (get_tab(j, k) for k in range(5))
        for g in range(SSD_GROUPS):
            bm = get_bc(j, g)
            cm = get_bc(j, SSD_GROUPS + g)
            cbt = _dot_nt(bm, cm)
            ht = state_ref[g]
            y_off = _dot_nt(ht.astype(BF16), cm)
            x_out, decay = [], []
            for r in range(rep):
                h = g * rep + r
                hl = hoff + h
                rows = slice(h * HEAD_DIM, (h + 1) * HEAD_DIM)
                xt = get_x(j, rows).astype(F32)
                x_in = (xt * dt_t[hl:hl + 1, :]).astype(BF16)
                x_out.append((xt * dt_out_t[hl:hl + 1, :]).astype(BF16))
                decay.append(jnp.broadcast_to(e_chunk[hl:hl + 1, :], (HEAD_DIM, SSD_STATE)))
                diff = acs_t[hl:hl + 1, :] - acs[:, hl:hl + 1]
                lt = (cbt * jnp.exp(jnp.where(feeds, diff, -jnp.inf))).astype(BF16)
                put_y(j, rows, _dot(x_in, lt) + y_off[r * HEAD_DIM:(r + 1) * HEAD_DIM] * e_in_t[hl:hl + 1, :])
            s_new = _dot(jnp.concatenate(x_out, axis=0), bm)
            state_ref[g] = ht * jnp.concatenate(decay, axis=0) + s_new
        if after_chunk is not None:
            after_chunk(j)


def _softmax_pv_t(scores, values, pad_last, sink_logit):
    mx = functools.reduce(jnp.maximum, [jnp.max(s, axis=0, keepdims=True) for s in scores] + [sink_logit])
    ps = [jnp.exp(s - mx) for s in scores]
    den = functools.reduce(jnp.add, [jnp.sum(p, axis=0, keepdims=True) for p in ps]) + jnp.exp(sink_logit - mx)
    pb = [p.astype(BF16) for p in ps]
    pb[-1] = jnp.concatenate([pad_last, pb[-1]], axis=0)
    return _dot(jnp.concatenate(values, axis=1), jnp.concatenate(pb, axis=0)), den


def _block_diag_rows(pieces):
    zero = jnp.zeros_like(pieces[0][0])
    rows = []
    for i in range(len(pieces)):
        row = []
        for j, ps in enumerate(pieces):
            row += [p if i == j else zero for p in ps]
        rows.append(jnp.concatenate(row, axis=1))
    return jnp.concatenate(rows, axis=0)


def _win_block(win_refs, j, tile, nb, put):
    qt_ref, kp_ref, kc_ref, kn_ref, km_ref, vp_ref, vc_ref, vn_ref, vm_ref, sink_ref = win_refs
    width = WIN_Q_HEADS * BLK
    ki = lax.broadcasted_iota(jnp.int32, (BLK, width), 0)
    qi = lax.broadcasted_iota(jnp.int32, (BLK, width), 1) & (BLK - 1)
    rep = WIN_Q_HEADS // WIN_KV_HEADS
    keys = [kp_ref[0]] + [kc_ref[0, jj * BLK:(jj + 1) * BLK, :] for jj in range(STEP_BLKS)] + [kn_ref[0]]
    vals = [vp_ref[0, 0]] + [vc_ref[0, jj] for jj in range(STEP_BLKS)] + [vn_ref[0, 0]]
    kmeta, vmeta = km_ref[0, META0:, :], vm_ref[0, 0]
    pad_last = jnp.zeros((META0, width), BF16)
    n = tile * STEP_BLKS + j
    ok_prev = jnp.logical_and(ki >= qi, n >= 2)
    ok_cur = n >= 1
    ok_next = jnp.logical_and(ki <= qi, n + 1 <= nb - 1)
    heads = [qt_ref[0, j, h * HEAD_DIM:(h + 1) * HEAD_DIM, :] for h in range(WIN_Q_HEADS)]
    qbd = _block_diag_rows([heads[g * rep:(g + 1) * rep] for g in range(WIN_KV_HEADS)])
    s = _dot(jnp.concatenate(keys[j:j + 3] + [kmeta], axis=0), qbd)
    scores = [jnp.where(ok_prev, s[:BLK], -jnp.inf),
              jnp.where(ok_cur, s[BLK:2 * BLK], -jnp.inf),
              jnp.where(ok_next, s[2 * BLK:3 * BLK], -jnp.inf),
              s[3 * BLK:]]
    o, den = _softmax_pv_t(scores, vals[j:j + 3] + [vmeta], pad_last, sink_ref[...])
    o = o / den
    outs = [o[(h // rep) * HEAD_DIM:(h // rep + 1) * HEAD_DIM, h * BLK:(h + 1) * BLK] for h in range(WIN_Q_HEADS)]
    put(j, jnp.concatenate(outs, axis=0).astype(BF16).T)


def _win_specs(qt, k, vt, sink, order):
    nb = qt.shape[1]
    kvw = WIN_KV_HEADS * HEAD_DIM
    assert kvw == BLK and WIN_RADIUS == BLK
    kedge = lambda f: pl.BlockSpec((1, BLK, kvw), lambda b, i: (b, f(order(i)), 0))
    vedge = lambda f: pl.BlockSpec((1, 1, kvw, BLK), lambda b, i: (b, f(order(i)), 0, 0))
    before = lambda t: jnp.maximum(t * STEP_BLKS - 1, 0)
    after = lambda t: jnp.minimum((t + 1) * STEP_BLKS, nb - 1)
    first = lambda t: 0
    specs = [pl.BlockSpec((1, STEP_BLKS, WIN_Q_HEADS * HEAD_DIM, BLK), lambda b, i: (b, order(i), 0, 0)),
             kedge(before), pl.BlockSpec((1, TM, kvw), lambda b, i: (b, order(i), 0)), kedge(after), kedge(first),
             vedge(before), pl.BlockSpec((1, STEP_BLKS, kvw, BLK), lambda b, i: (b, order(i), 0, 0)), vedge(after),
             vedge(first), pl.BlockSpec(sink.shape, lambda b, i: (0, 0))]
    return specs, [qt, k, k, k, k, vt, vt, vt, vt, sink]


NA_QUAD = 4
NA_QW = NA_QUAD * HEAD_DIM
NA_WIN = NA_KR * GRID_W


def _na_bias_tables(rpb):
    c = np.arange(GRID_W)[None, :]
    kc = np.arange(GRID_W)[:, None]
    cs = np.clip(c - NA_KC // 2, 0, GRID_W - NA_KC)
    ok = (kc >= cs) & (kc < cs + NA_KC)
    pick = jnp.asarray((kc - c + NA_KC - 1)[None] == np.arange(2 * NA_KC - 1)[:, None, None], F32)
    t = jnp.einsum('hed,dkc->hekc', rpb.astype(F32), pick, precision=lax.Precision.HIGHEST)
    t = jnp.where(jnp.asarray(ok)[None, None], t, -jnp.inf)

    def pack(x):
        h, e = x.shape[:2]
        x = x.reshape(h // NA_QUAD, NA_QUAD, e, GRID_W, GRID_W)
        return x.transpose(0, 2, 3, 1, 4).reshape(h // NA_QUAD, e, GRID_W, NA_QW)

    meta = jnp.broadcast_to(t[:, NA_KR - 1:, :, 0:1], (t.shape[0], NA_KR, GRID_W, GRID_W))
    return pack(t), pack(meta)


NA_KV_BLKS = 7


def _na_window_base(tile, nb):
    return jnp.clip(tile * STEP_BLKS - STEP_BLKS, 0, nb - 1 - NA_KV_BLKS)


def _na_blocks(na_refs, kbuf_ref, vbuf_ref, tile, nb, rows, put, blocks):
    qt_ref, lead_ref, bias_ref, mbias_ref, mb_ref = na_refs
    width = NA_HEADS * HEAD_DIM
    first_half = lax.broadcasted_iota(jnp.int32, (width, BLK), 1) < GRID_W
    ri = lax.broadcasted_iota(jnp.int32, (NA_QW, NA_QW), 0)
    ci = lax.broadcasted_iota(jnp.int32, (NA_QW, NA_QW), 1)
    diag = lax.shift_right_logical(ri, 6) == lax.shift_right_logical(ci, 6)
    head_of_lane = lax.shift_right_logical(lax.broadcasted_iota(jnp.int32, (GRID_W, NA_QW), 1), 6)
    pad_meta = jnp.zeros((META0, NA_QW), F32)
    row0 = _na_window_base(tile, nb) * (BLK // GRID_W)

    def scores(j, half, u, meta):
        qf = qt_ref[0, j].astype(F32)
        qsw = pltpu.roll(qf, GRID_W, 1)
        dup = jnp.where(first_half, qf, qsw) if half == 0 else jnp.where(first_half, qsw, qf)
        if meta:
            start, e0 = 0, None
        else:
            r = (tile * STEP_BLKS + j - 1) * (BLK // GRID_W) + half
            rs = jnp.clip(r - NA_KR // 2, 0, rows - NA_KR)
            start = pl.multiple_of((rs - row0) * GRID_W, GRID_W)
            e0 = rs - r + (NA_KR - 1)
        cols = slice(u * NA_QW, (u + 1) * NA_QW)
        base = dup[u * NA_QW:(u + 1) * NA_QW]
        qbd = jnp.where(diag, jnp.concatenate([base, base], axis=1), 0.0).astype(BF16)
        keys = jnp.concatenate([kbuf_ref[pl.ds(start, NA_WIN), cols], lead_ref[0, META0:LEAD, cols]], axis=0)
        return _dot(keys, qbd), (j, half, u, start, e0, cols)

    def finish(s, unit):
        j, half, u, start, e0, cols = unit
        if e0 is None:
            bias = jnp.concatenate([mbias_ref[u, i] for i in range(NA_KR)], axis=0)
        else:
            bias = jnp.concatenate([bias_ref[u, e0 + i] for i in range(NA_KR)], axis=0)
        sw = s[:NA_WIN] + bias
        sm = s[NA_WIN:] + mb_ref[u]
        mx = jnp.maximum(jnp.max(sw, axis=0, keepdims=True), jnp.max(sm, axis=0, keepdims=True))
        pw, pm = jnp.exp(sw - mx), jnp.exp(sm - mx)
        rden = 1.0 / (jnp.sum(pw, axis=0, keepdims=True) + jnp.sum(pm, axis=0, keepdims=True))
        p_t = jnp.concatenate([pw * rden, pad_meta, pm * rden], axis=0).astype(BF16).T
        vals = jnp.concatenate([vbuf_ref[pl.ds(start, NA_WIN), cols],
                                lead_ref[0, :, u * NA_QW + width:(u + 1) * NA_QW + width]], axis=0)
        o = _dot(p_t, vals)
        out = o[(NA_QUAD - 1) * GRID_W:]
        for hq in range(NA_QUAD - 2, -1, -1):
            out = jnp.where(head_of_lane == hq, o[hq * GRID_W:(hq + 1) * GRID_W], out)
        put(j, half, cols, out.astype(BF16))

    def run(units):
        pending = scores(*units[0])
        for nxt in units[1:]:
            ahead = scores(*nxt)
            finish(*pending)
            pending = ahead
        finish(*pending)

    quads = range(NA_HEADS // NA_QUAD)
    if 0 in blocks:
        @pl.when(tile == 0)
        def _():
            for u in quads:
                put(0, 0, slice(u * NA_QW, (u + 1) * NA_QW), jnp.zeros((GRID_W, NA_QW), BF16))
            run([(0, 1, u, True) for u in quads])

        @pl.when(tile > 0)
        def _():
            run([(0, half, u, False) for half in range(2) for u in quads])

    rest = [j for j in blocks if j > 0]
    if rest:
        run([(j, half, u, False) for j in rest for half in range(2) for u in quads])


def _na_load_window(kv_refs, kbuf_ref, vbuf_ref):
    width = kbuf_ref.shape[1]
    for m in range(NA_KV_BLKS):
        kbuf_ref[m * BLK:(m + 1) * BLK, :] = kv_refs[m][0, :, :width]
        vbuf_ref[m * BLK:(m + 1) * BLK, :] = kv_refs[m][0, :, width:]


def _na_specs(qt, kv, bias, mbias, mb, order):
    nb, width = qt.shape[1], kv.shape[2]
    rows = (nb * BLK - LEAD) // GRID_W
    assert nb - 1 >= NA_KV_BLKS and rows >= NA_KR and N_META <= GRID_W and LEAD - GRID_W <= META0
    blk = lambda f: pl.BlockSpec((1, BLK, width), lambda b, i: (b, f(order(i)), 0))
    lead = blk(lambda t: 0)
    win = [blk(lambda t, m=m: LEAD // BLK + _na_window_base(t, nb) + m) for m in range(NA_KV_BLKS)]
    specs = [pl.BlockSpec((1, STEP_BLKS, qt.shape[2], BLK), lambda b, i: (b, order(i), 0, 0)), lead,
             _resident(bias), _resident(mbias), _resident(mb)] + win
    return specs, [qt, kv, bias, mbias, mb] + [kv] * NA_KV_BLKS


N_NA_REFS = 5


def _mixout_kernel(*refs, ns, ntile):
    lead_ref = refs[0] if ntile > 1 else None
    tile_refs = refs[ntile > 1:ntile + (ntile > 1)]
    rest = refs[ntile + (ntile > 1):]
    win_refs, rest = rest[:N_WIN_REFS], rest[N_WIN_REFS:]
    na_refs, rest = rest[:N_NA_REFS], rest[N_NA_REFS:]
    kv_refs, rest = rest[:NA_KV_BLKS], rest[NA_KV_BLKS:]
    (xt_ref, bc_ref, acs_ref, tab_ref, zt_ref, yf_ref, dsk_ref, nws_ref, w1_ref, w2_ref, w3_ref,
     nw_ref, o_ref, state_ref, ybuf_ref, ys_ref, yw_ref, yn_ref, kbuf_ref, vbuf_ref) = rest
    i = ns - 1 - pl.program_id(1)
    nb = ns * STEP_BLKS

    @pl.when(pl.program_id(1) == 0)
    def _():
        state_ref[...] = jnp.zeros_like(state_ref)

    def put_y(j, rows, y):
        ybuf_ref[rows, :] = y

    def finish(j):
        y = ybuf_ref[...] + yf_ref[0, j] + dsk_ref[...] * xt_ref[0, j].astype(F32)
        z = zt_ref[0, j].astype(F32)
        y = y * (z * (1.0 / (1.0 + jnp.exp(-z))))
        y = y * lax.rsqrt(jnp.mean(y * y, axis=0, keepdims=True) + EPS) * nws_ref[...]
        ys_ref[j * BLK:(j + 1) * BLK, :] = y.astype(BF16).T

    def scan_chunk(j):
        _ssd_scan(lambda j, rows: xt_ref[0, j, rows, :],
                  lambda j, k: bc_ref[0, j * BLK:(j + 1) * BLK, k * SSD_STATE:(k + 1) * SSD_STATE],
                  lambda j: acs_ref[0, j * BLK:(j + 1) * BLK, :],
                  lambda j, k: tab_ref[0, j, k], state_ref, put_y, finish, reverse=True, chunks=[j])

    def put_win(j, y):
        yw_ref[j * BLK:(j + 1) * BLK, :] = y

    def put_na(j, half, cols, y):
        yn_ref[pl.ds(j * BLK + half * GRID_W, GRID_W), cols] = y

    def na_blocks(blocks):
        _na_blocks(na_refs, kbuf_ref, vbuf_ref, i, nb, (nb * BLK - LEAD) // GRID_W, put_na, blocks)

    _na_load_window(kv_refs, kbuf_ref, vbuf_ref)
    na_blocks([0])
    for j in reversed(range(STEP_BLKS)):
        scan_chunk(j)
        _win_block(win_refs, j, i, nb, put_win)
        if j > 0:
            na_blocks([j])
    mix = _dot(yn_ref[...], w3_ref[...]) + _dot(yw_ref[...], w2_ref[...]) + _dot(ys_ref[...], w1_ref[...])
    out = _load_tile(tile_refs, lead_ref, i) + _rms(mix, nw_ref[...])
    o_ref[0] = jnp.where(_live_rows(TM, i), out, 0.0)


N_WIN_REFS = 10


def _mixout(src, win_args, na_args, xt, bc, acs, tab, zt, y_fwd, dsk, nws, w1, w2, w3, nw):
    bsz, nc = xt.shape[:2]
    ns = nc // STEP_BLKS
    order = lambda c: ns - 1 - c
    tok = lambda n: pl.BlockSpec((1, TM, n), lambda b, c: (b, order(c), 0))
    feat = pl.BlockSpec((1, STEP_BLKS, SSD_INNER, BLK), lambda b, c: (b, order(c), 0, 0))
    tabs = pl.BlockSpec((1, STEP_BLKS) + tab.shape[2:], lambda b, c: (b, order(c), 0, 0, 0))
    tile_specs, tile_args = _tile_specs(src, TM, order)
    win_specs, win_ops = _win_specs(*win_args, order)
    na_specs, na_ops = _na_specs(*na_args, order)
    assert len(win_specs) == N_WIN_REFS and len(na_specs) == N_NA_REFS + NA_KV_BLKS
    na_width = NA_HEADS * HEAD_DIM
    consts = [dsk, nws, w1, w2, w3, nw]
    return pl.pallas_call(
        functools.partial(_mixout_kernel, ns=ns, ntile=len(tile_specs) - (src[0] == 'x')),
        grid=(bsz, ns),
        in_specs=tile_specs + win_specs + na_specs + [feat, tok(bc.shape[-1]), tok(128), tabs, feat, feat]
        + [_resident(c) for c in consts],
        out_specs=tok(D_MODEL),
        out_shape=jax.ShapeDtypeStruct((bsz, nc * BLK, D_MODEL), F32),
        scratch_shapes=[pltpu.VMEM((SSD_GROUPS, SSD_INNER // SSD_GROUPS, SSD_STATE), F32),
                        pltpu.VMEM((SSD_INNER, BLK), F32), pltpu.VMEM((TM, SSD_INNER), BF16),
                        pltpu.VMEM((TM, WIN_Q_HEADS * HEAD_DIM), BF16), pltpu.VMEM((TM, na_width), BF16),
                        pltpu.VMEM((NA_KV_BLKS * BLK, na_width), BF16), pltpu.VMEM((NA_KV_BLKS * BLK, na_width), BF16)],
        compiler_params=_cparams(("parallel", "arbitrary")),
        name="mixout",
    )(*tile_args, *win_ops, *na_ops, xt, bc, acs, tab, zt, y_fwd, *consts)


def _ffn_kernel(*refs, nt, ntile, padded):
    tile_refs = refs[:ntile]
    hp_ref, hn_ref, nw1_ref, wup_ref, cw_ref, cb_ref, wdn_ref, nw2_ref, o_ref, act_ref = refs[ntile:]
    i = pl.program_id(1)
    rows = act_ref.shape[0]
    nw1 = nw1_ref[...]
    hc = _load_tile(tile_refs, None, i)
    f = jnp.concatenate([_rms(hp_ref[0], nw1), _rms(hc, nw1), _rms(hn_ref[0], nw1)], axis=0).astype(BF16)
    tail = jnp.where(i < nt - 1, 1.0, 0.0)
    nch = D_FF // FF_CHUNK

    def conv(g, c):
        g = jnp.concatenate([g[:HALO + rows], g[HALO + rows:] * tail], axis=0)
        cols = slice(c * FF_CHUNK, (c + 1) * FF_CHUNK)
        out = _shift_rows(g, -1) * cw_ref[0:1, cols] + cb_ref[:, cols]
        out = out + g[HALO:HALO + rows] * cw_ref[1:2, cols]
        return out + _shift_rows(g, 1) * cw_ref[2:3, cols]

    for c in range(nch):
        gate = conv(_dot(f, wup_ref[:, c * FF_CHUNK:(c + 1) * FF_CHUNK]), c)
        up = conv(_dot(f, wup_ref[:, D_FF + c * FF_CHUNK:D_FF + (c + 1) * FF_CHUNK]), nch + c)
        th = jnp.tanh(gate * (GELU_K + (GELU_K * GELU_C) * (gate * gate)))
        act_ref[:, c * FF_CHUNK:(c + 1) * FF_CHUNK] = (gate * (0.5 * th + 0.5) * up).astype(BF16)
    out = hc + _rms(_dot(act_ref[...], wdn_ref[...]), nw2_ref[...])
    o_ref[0] = jnp.where(_live_rows(rows, i), out, 0.0) if padded else out


def _ffn(h, nw1, wup, cw, cb, wdn, nw2, *, last):
    bsz, lp, _ = h.shape
    full = _resident
    if last:
        n_out = lp - LEAD
        tm = _tile_rows(n_out, FFN_OUT_TILE_CAP, BLK)
        h4 = h.reshape(bsz, lp // BLK, BLK, D_MODEL)
        k = tm // BLK
        tile_specs = [pl.BlockSpec((1, 1, BLK, D_MODEL), lambda b, i, j=j: (b, LEAD // BLK + i * k + j, 0, 0))
                      for j in range(k)]
        tile_args = [h4] * k
        off = LEAD // HALO
    else:
        tm, n_out = TM, lp
        tile_specs, tile_args = _tile_specs(('h', h), tm)
        off = 0
    nt = n_out // tm
    hb = tm // HALO
    last_blk = lp // HALO - 1
    prev = pl.BlockSpec((1, HALO, D_MODEL), lambda b, i: (b, jnp.clip(off + i * hb - 1, 0, last_blk), 0))
    nxt = pl.BlockSpec((1, HALO, D_MODEL), lambda b, i: (b, jnp.clip(off + (i + 1) * hb, 0, last_blk), 0))
    return pl.pallas_call(
        functools.partial(_ffn_kernel, nt=nt, ntile=len(tile_specs), padded=not last),
        grid=(bsz, nt),
        in_specs=tile_specs + [prev, nxt, full(nw1), full(wup), full(cw), full(cb), full(wdn), full(nw2)],
        out_specs=pl.BlockSpec((1, tm, D_MODEL), lambda b, i: (b, i, 0)),
        out_shape=jax.ShapeDtypeStruct((bsz, n_out, D_MODEL), F32),
        scratch_shapes=[pltpu.VMEM((tm, D_FF), BF16)],
        compiler_params=_cparams(("parallel", "parallel")),
        name="ffn",
    )(*tile_args, h, h, nw1, wup, cw, cb, wdn, nw2)


def _pad_lanes(x, n):
    return jnp.pad(x, [(0, 0)] * (x.ndim - 1) + [(0, n - x.shape[-1])])


def _rope_tables(lp):
    half = ROPE_DIM // 2
    pos = jnp.maximum(jnp.arange(lp) - META0, 0).astype(F32)
    inv = jnp.power(ROPE_THETA, -jnp.arange(half, dtype=F32) / half)
    ang = pos[:, None] * inv[None, :]
    return jnp.cos(ang).T, jnp.sin(ang).T


def _layer_params(i, p):
    row = lambda v: v.reshape(1, -1).astype(F32)
    lanes = lambda v: jnp.broadcast_to(v.astype(F32)[:, None], (v.shape[0], BLK))
    sizes = [SSD_INNER, SSD_XBC, 2 * SSD_HEADS, WIN_Q_HEADS * HEAD_DIM, WIN_KV_HEADS * HEAD_DIM,
             WIN_KV_HEADS * HEAD_DIM, NA_HEADS * HEAD_DIM, NA_HEADS * HEAD_DIM, NA_HEADS * HEAD_DIM]
    w_in = p['w_in'][i].astype(BF16)
    ws = jnp.split(w_in, np.cumsum(sizes)[:-1].tolist(), axis=1)
    ws = ws[:2] + [jnp.concatenate([_pad_lanes(ws[2], BLK), ws[4], ws[5]], axis=1), ws[3]] + ws[6:]
    w_out = p['w_out'][i].astype(BF16)
    return dict(
        norm_mix_pre=row(p['norm_mix_pre'][i]),
        w_in=ws,
        ssd_conv=(p['ssd_conv_w'][i].astype(F32), row(p['ssd_conv_b'][i])),
        ssd_consts=(_pad_lanes(row(p['ssd_dt_bias'][i]), 128), _pad_lanes(row(p['ssd_a_log'][i]), 128)),
        ssd_d=lanes(jnp.repeat(p['ssd_d'][i], HEAD_DIM)),
        ssd_norm_w=lanes(p['ssd_norm_w'][i]),
        win_sink=row(jnp.repeat(p['win_sink'][i], BLK)),
        na_bias=_na_bias_tables(p['na_rpb'][i]),
        na_meta_bias=jnp.broadcast_to(p['na_meta_bias'][i].astype(F32).T.reshape(N_META, NA_HEADS // NA_QUAD, NA_QUAD, 1),
                                      (N_META, NA_HEADS // NA_QUAD, NA_QUAD, GRID_W)).transpose(1, 0, 2, 3).reshape(
            NA_HEADS // NA_QUAD, N_META, NA_QW),
        w_out=(w_out[:SSD_INNER], w_out[SSD_INNER:SSD_INNER + WIN_Q_HEADS * HEAD_DIM],
               w_out[SSD_INNER + WIN_Q_HEADS * HEAD_DIM:]),
        norm_mix_post=row(p['norm_mix_post'][i]),
        norm_ffn_pre=row(p['norm_ffn_pre'][i]),
        ffn_w_up=p['ffn_w_up'][i].astype(BF16),
        ffn_conv_w=p['ffn_conv_w'][i].astype(F32),
        ffn_conv_b=row(p['ffn_conv_b'][i]),
        ffn_w_down=p['ffn_w_down'][i].astype(BF16),
        norm_ffn_post=row(p['norm_ffn_post'][i]),
    )


def _encode(x, meta_tokens, layers):
    bsz, n_tok, _ = x.shape
    assert (n_tok + LEAD) % TM == 0 and n_tok // GRID_W >= NA_KR
    lp = LEAD + n_tok
    lead = jnp.concatenate([jnp.zeros((META0, D_MODEL), F32), meta_tokens.astype(F32)], axis=0)
    src = ('x', x.astype(F32), lead)
    rope = _rope_tables(lp)
    for li, lw in enumerate(layers):
        zt, xt, bc, acs, tab, wq, wk, wv, nq, nkv, y_fwd = _inproj(src, lp, lw['norm_mix_pre'], lw['w_in'],
                                                                      lw['ssd_conv'], lw['ssd_consts'], rope)
        h = _mixout(src, (wq, wk, wv, lw['win_sink']), (nq, nkv, *lw['na_bias'], lw['na_meta_bias']), xt, bc, acs,
                    tab, zt, y_fwd, lw['ssd_d'], lw['ssd_norm_w'], *lw['w_out'], lw['norm_mix_post'])
        h = _ffn(h, lw['norm_ffn_pre'], lw['ffn_w_up'], lw['ffn_conv_w'], lw['ffn_conv_b'],
                 lw['ffn_w_down'], lw['norm_ffn_post'], last=li == len(layers) - 1)
        src = ('h', h)
    return h


def kernel(x_prompt, x_sample, meta_tokens, norm_mix_pre, norm_mix_post, w_in, ssd_conv_w, ssd_conv_b,
           ssd_dt_bias, ssd_a_log, ssd_d, ssd_norm_w, win_sink, na_rpb, na_meta_bias, w_out, norm_ffn_pre,
           norm_ffn_post, ffn_w_up, ffn_conv_w, ffn_conv_b, ffn_w_down):
    p = dict(norm_mix_pre=norm_mix_pre, norm_mix_post=norm_mix_post, w_in=w_in, ssd_conv_w=ssd_conv_w,
             ssd_conv_b=ssd_conv_b, ssd_dt_bias=ssd_dt_bias, ssd_a_log=ssd_a_log, ssd_d=ssd_d,
             ssd_norm_w=ssd_norm_w, win_sink=win_sink, na_rpb=na_rpb, na_meta_bias=na_meta_bias, w_out=w_out,
             norm_ffn_pre=norm_ffn_pre, norm_ffn_post=norm_ffn_post, ffn_w_up=ffn_w_up, ffn_conv_w=ffn_conv_w,
             ffn_conv_b=ffn_conv_b, ffn_w_down=ffn_w_down)
    layers = [_layer_params(i, p) for i in range(w_in.shape[0])]
    return (_encode(x_prompt, meta_tokens, layers), _encode(x_sample, meta_tokens, layers))
```

```python
import functools

import jax
import jax.numpy as jnp
import numpy as np
from jax import lax
from jax.experimental import pallas as pl
from jax.experimental.pallas import tpu as pltpu

F32 = jnp.float32
BF16 = jnp.bfloat16

D_MODEL = 1024
N_META = 16
GRID_W = 64
HEAD_DIM = 64

SSD_HEADS = 16
SSD_INNER = SSD_HEADS * HEAD_DIM
SSD_GROUPS = 2
SSD_STATE = 128
SSD_XBC = SSD_INNER + 2 * SSD_GROUPS * SSD_STATE
SSD_CONV = 5

WIN_Q_HEADS = 8
WIN_KV_HEADS = 2
WIN_RADIUS = 128
ROPE_THETA = 500000.0
ROPE_DIM = HEAD_DIM // 4

NA_HEADS = 8
NA_KR = 8
NA_KC = 16

D_FF = 2816
EPS = 1e-6
GELU_K = float(np.sqrt(2.0 / np.pi))
GELU_C = 0.044715

BLK = 128
LEAD = BLK
META0 = LEAD - N_META
HALO = 8
TM = 384
FFN_OUT_TILE_CAP = 720
FF_CHUNK = 256
STEP_BLKS = TM // BLK
VMEM_LIMIT = 56 * 1024 * 1024


def _tile_rows(n, cap, unit=HALO):
    return max(t for t in range(unit, cap + 1, unit) if n % t == 0)


def _resident(arr):
    return pl.BlockSpec(arr.shape, lambda *_: (0,) * arr.ndim, pipeline_mode=pl.Buffered(1))


def _cparams(sem):
    return pltpu.CompilerParams(dimension_semantics=sem, vmem_limit_bytes=VMEM_LIMIT)


def _rms(x, w):
    return x * lax.rsqrt(jnp.mean(x * x, axis=-1, keepdims=True) + EPS) * w


def _dot(a, b):
    return jnp.dot(a, b, preferred_element_type=F32)


def _shift_rows(x, d):
    return pltpu.roll(x, (-d) % x.shape[0], 0)[HALO:x.shape[0] - HALO]


def _load_tile(refs, lead_ref, i):
    if len(refs) == 1:
        return refs[0][0]
    blocks = [r[0, 0] for r in refs]
    if lead_ref is not None:
        blocks[0] = jnp.where(i == 0, lead_ref[...], blocks[0])
    return jnp.concatenate(blocks, axis=0)


def _tile_specs(src, tm, order=lambda i: i):
    if src[0] == 'h':
        return [pl.BlockSpec((1, tm, D_MODEL), lambda b, i: (b, order(i), 0))], [src[1]]
    x, lead = src[1], src[2]
    x4 = x.reshape(x.shape[0], x.shape[1] // BLK, BLK, D_MODEL)
    k = tm // BLK
    specs = [pl.BlockSpec((1, 1, BLK, D_MODEL),
                          lambda b, i, j=j: (b, jnp.maximum(order(i) * k + j - LEAD // BLK, 0), 0, 0))
             for j in range(k)]
    return [pl.BlockSpec(lead.shape, lambda b, i: (0, 0))] + specs, [lead] + [x4] * k


def _halo_specs(src, tm):
    arr = src[1]
    off = 0 if src[0] == 'h' else LEAD // HALO
    last = arr.shape[1] // HALO - 1
    hb = tm // HALO
    prev = pl.BlockSpec((1, HALO, D_MODEL), lambda b, i: (b, jnp.clip(i * hb - 1 - off, 0, last), 0))
    nxt = pl.BlockSpec((1, HALO, D_MODEL), lambda b, i: (b, jnp.clip((i + 1) * hb - off, 0, last), 0))
    return [prev, nxt], [arr, arr]


def _live_rows(tile_rows, tile_index):
    row = tile_index * tile_rows + lax.broadcasted_iota(jnp.int32, (tile_rows, 1), 0)
    return row >= META0


def _dot_nt(a, b):
    return lax.dot_general(a, b, (((1,), (1,)), ((), ())), preferred_element_type=F32)


def _rope_t(xt, cos_t, sin_t):
    half = ROPE_DIM // 2
    parts = []
    for hh in range(xt.shape[0] // HEAD_DIM):
        x1 = xt[hh * HEAD_DIM:hh * HEAD_DIM + half]
        x2 = xt[hh * HEAD_DIM + half:hh * HEAD_DIM + ROPE_DIM]
        parts += [x1 * cos_t - x2 * sin_t, x2 * cos_t + x1 * sin_t, xt[hh * HEAD_DIM + ROPE_DIM:(hh + 1) * HEAD_DIM]]
    return jnp.concatenate(parts, axis=0)


def _scan_tables(dt_raw, dt_bias, a_log, live, acs_ref, tab_ref):
    nh = 2 * SSD_HEADS
    dtr = dt_raw + dt_bias
    dt = jnp.maximum(dtr, 0.0) + jnp.log1p(jnp.exp(-jnp.abs(dtr)))
    dt = jnp.where(live, dt, 0.0)
    a = dt * (-jnp.exp(a_log))
    ri = lax.broadcasted_iota(jnp.int32, (BLK, BLK), 0)
    ci = lax.broadcasted_iota(jnp.int32, (BLK, BLK), 1)
    tri = (ci <= ri).astype(BF16)
    fwd = ci < SSD_HEADS
    fwd_rows = ri < SSD_HEADS
    acs_chunks, tab_chunks = [], []
    for j in range(TM // BLK):
        rows = slice(j * BLK, (j + 1) * BLK)
        aj, dtj = a[rows], dt[rows]
        pre = sum(_dot(tri, part) for part in _split3(aj))
        total = pre[BLK - 1:BLK, :]
        acs = jnp.where(fwd, pre, total - pre + aj)
        acs_ref[0, rows, :] = acs
        acs_t = acs.T[:nh]
        chunk_sum = jnp.where(fwd_rows[:nh, 0:1], acs_t[:, BLK - 1:BLK], acs_t[:, 0:1])
        tabs = [acs_t, dtj.T[:nh],
                (dtj * jnp.exp(total - acs)).T[:nh],
                jnp.exp(acs_t), jnp.broadcast_to(jnp.exp(chunk_sum), (nh, BLK))]
        for k, t in enumerate(tabs):
            tab_ref[0, j, k] = t
        acs_chunks.append(acs)
        tab_chunks.append(tabs)
    return acs_chunks, tab_chunks


def _inproj_kernel(*refs, nt, ntile):
    lead_ref = refs[0] if ntile > 1 else None
    tile_refs = refs[ntile > 1:ntile + (ntile > 1)]
    (hp_ref, hn_ref, nw_ref, wz_ref, wx_ref, wsm_ref, wwq_ref,
     wnq_ref, wnk_ref, wnv_ref, cw_ref, cb_ref, dtb_ref, alog_ref, rct_ref, rst_ref,
     z_ref, xt_ref, bc_ref, acs_ref, tab_ref, wq_ref, wk_ref, wv_ref, nq_ref, nkv_ref, yf_ref,
     state_ref) = refs[ntile + (ntile > 1):]
    i = pl.program_id(1)

    @pl.when(i == 0)
    def _():
        state_ref[...] = jnp.zeros_like(state_ref)

    nw = nw_ref[...]
    a32 = _rms(_load_tile(tile_refs, lead_ref, i), nw)
    a = a32.astype(BF16)
    head = jnp.where(i > 0, 1.0, 0.0)
    tail = jnp.where(i < nt - 1, 1.0, 0.0)
    a_ext = jnp.concatenate([_rms(hp_ref[0], nw) * head, a32, _rms(hn_ref[0], nw) * tail], axis=0).astype(BF16)
    xe = _dot(a_ext, wx_ref[...])

    def put_blocks(ref, xt):
        for j in range(TM // BLK):
            ref[0, j] = xt[:, j * BLK:(j + 1) * BLK].astype(BF16)

    put_blocks(z_ref, _dot(a, wz_ref[...]).astype(BF16).T)
    small = _dot(a, wsm_ref[...])
    dt_raw, k_pre, v_pre = small[:, :BLK], small[:, BLK:2 * BLK], small[:, 2 * BLK:]
    pad = SSD_CONV // 2
    acc = _shift_rows(xe, -pad) * cw_ref[0:1, :] + cb_ref[...]
    for j in range(1, SSD_CONV):
        acc = acc + _shift_rows(xe, j - pad) * cw_ref[j:j + 1, :]
    xc = acc * (1.0 / (1.0 + jnp.exp(-acc)))
    xt_all = xc[:, :SSD_INNER].astype(BF16).T
    xt_chunks = [xt_all[:, j * BLK:(j + 1) * BLK] for j in range(TM // BLK)]
    for j in range(TM // BLK):
        xt_ref[0, j] = xt_chunks[j]
    bc = xc[:, SSD_INNER:].astype(BF16)
    bc_ref[0] = bc
    acs_chunks, tab_chunks = _scan_tables(dt_raw, dtb_ref[...], alog_ref[...], _live_rows(TM, i),
                                          acs_ref, tab_ref)

    def put_y(j, rows, y):
        yf_ref[0, j, rows, :] = y

    def scan_chunk(j):
        _ssd_scan(lambda j, rows: xt_chunks[j][rows],
                  lambda j, k: bc[j * BLK:(j + 1) * BLK, k * SSD_STATE:(k + 1) * SSD_STATE],
                  lambda j: acs_chunks[j], lambda j, k: tab_chunks[j][k], state_ref, put_y, None,
                  reverse=False, chunks=[j])

    scale = HEAD_DIM ** -0.5
    cos_t, sin_t = rct_ref[...], rst_ref[...]
    put_blocks(wq_ref, _rope_t(_dot(a, wwq_ref[...]).T, cos_t, sin_t) * scale)
    scan_chunk(0)
    wk_ref[0] = _rope_t(k_pre.T, cos_t, sin_t).astype(BF16).T
    put_blocks(wv_ref, v_pre.astype(BF16).T)
    scan_chunk(1)
    put_blocks(nq_ref, (_dot(a, wnq_ref[...]) * scale).astype(BF16).T)
    nkv_ref[0, :, :NA_HEADS * HEAD_DIM] = _dot(a, wnk_ref[...]).astype(BF16)
    scan_chunk(2)
    nkv_ref[0, :, NA_HEADS * HEAD_DIM:] = _dot(a, wnv_ref[...]).astype(BF16)


def _inproj(src, lp, nw, ws, conv, scan, rope):
    bsz = src[1].shape[0]
    nt = lp // TM
    nb = lp // BLK
    tile = lambda n: pl.BlockSpec((1, TM, n), lambda b, i: (b, i, 0))
    tblk = lambda n: pl.BlockSpec((1, TM // BLK, n, BLK), lambda b, i: (b, i, 0, 0))
    tile_specs, tile_args = _tile_specs(src, TM)
    halo_specs, halo_args = _halo_specs(src, TM)
    full = lambda arr: pl.BlockSpec(arr.shape, lambda b, i: (0,) * arr.ndim)
    rtab_t = pl.BlockSpec((ROPE_DIM // 2, TM), lambda b, i: (0, i))
    tok = lambda n, d: (tile(n), jax.ShapeDtypeStruct((bsz, lp, n), d))
    blk = lambda n: (tblk(n), jax.ShapeDtypeStruct((bsz, nb, n, BLK), BF16))
    kvw = WIN_KV_HEADS * HEAD_DIM
    tabs = (pl.BlockSpec((1, TM // BLK, 5, 2 * SSD_HEADS, BLK), lambda b, i: (b, i, 0, 0, 0)),
            jax.ShapeDtypeStruct((bsz, nb, 5, 2 * SSD_HEADS, BLK), F32))
    outs = [blk(SSD_INNER), blk(SSD_INNER), tok(SSD_XBC - SSD_INNER, BF16), tok(128, F32), tabs, blk(WIN_Q_HEADS * HEAD_DIM), tok(kvw, BF16),
            blk(kvw), blk(NA_HEADS * HEAD_DIM), tok(2 * NA_HEADS * HEAD_DIM, BF16),
            (tblk(SSD_INNER), jax.ShapeDtypeStruct((bsz, nb, SSD_INNER, BLK), F32))]
    return pl.pallas_call(
        functools.partial(_inproj_kernel, nt=nt, ntile=len(tile_specs) - (src[0] == 'x')),
        grid=(bsz, nt),
        in_specs=tile_specs + halo_specs + [full(nw)] + [full(w) for w in ws] + [full(c) for c in conv + scan]
        + [rtab_t, rtab_t],
        out_specs=[o[0] for o in outs],
        out_shape=[o[1] for o in outs],
        scratch_shapes=[pltpu.VMEM((SSD_GROUPS, SSD_INNER // SSD_GROUPS, SSD_STATE), F32)],
        compiler_params=_cparams(("parallel", "arbitrary")),
        name="inproj",
    )(*tile_args, *halo_args, nw, *ws, *conv, *scan, *rope)


def _split3(x):
    hi = x.astype(BF16)
    r1 = x - hi.astype(F32)
    mid = r1.astype(BF16)
    lo = (r1 - mid.astype(F32)).astype(BF16)
    return hi, mid, lo


def _ssd_scan(get_x, get_bc, get_acs, get_tab, state_ref, put_y, after_chunk, *, reverse, chunks=None):
    ri = lax.broadcasted_iota(jnp.int32, (BLK, BLK), 0)
    ci = lax.broadcasted_iota(jnp.int32, (BLK, BLK), 1)
    feeds = (ri >= ci) if reverse else (ri <= ci)
    hoff = SSD_HEADS if reverse else 0
    rep = SSD_HEADS // SSD_GROUPS
    if chunks is None:
        chunks = reversed(range(STEP_BLKS)) if reverse else range(STEP_BLKS)
    for j in chunks:
        acs = get_acs(j)
        acs_t, dt_t, dt_out_t, e_in_t, e_chunk = (get_tab(j, k) for k in range(5))
        for g in range(SSD_GROUPS):
            bm = get_bc(j, g)
            cm = get_bc(j, SSD_GROUPS + g)
            cbt = _dot_nt(bm, cm)
            ht = state_ref[g]
            y_off = _dot_nt(ht.astype(BF16), cm)
            x_out, decay = [], []
            for r in range(rep):
                h = g * rep + r
                hl = hoff + h
                rows = slice(h * HEAD_DIM, (h + 1) * HEAD_DIM)
                xt = get_x(j, rows).astype(F32)
                x_in = (xt * dt_t[hl:hl + 1, :]).astype(BF16)
                x_out.append((xt * dt_out_t[hl:hl + 1, :]).astype(BF16))
                decay.append(jnp.broadcast_to(e_chunk[hl:hl + 1, :], (HEAD_DIM, SSD_STATE)))
                diff = acs_t[hl:hl + 1, :] - acs[:, hl:hl + 1]
                lt = (cbt * jnp.exp(jnp.where(feeds, diff, -jnp.inf))).astype(BF16)
                put_y(j, rows, _dot(x_in, lt) + y_off[r * HEAD_DIM:(r + 1) * HEAD_DIM] * e_in_t[hl:hl + 1, :])
            s_new = _dot(jnp.concatenate(x_out, axis=0), bm)
            state_ref[g] = ht * jnp.concatenate(decay, axis=0) + s_new
        if after_chunk is not None:
            after_chunk(j)


def _softmax_pv_t(scores, values, pad_last, sink_logit):
    mx = functools.reduce(jnp.maximum, [jnp.max(s, axis=0, keepdims=True) for s in scores] + [sink_logit])
    ps = [jnp.exp(s - mx) for s in scores]
    den = functools.reduce(jnp.add, [jnp.sum(p, axis=0, keepdims=True) for p in ps]) + jnp.exp(sink_logit - mx)
    pb = [p.astype(BF16) for p in ps]
    pb[-1] = jnp.concatenate([pad_last, pb[-1]], axis=0)
    return _dot(jnp.concatenate(values, axis=1), jnp.concatenate(pb, axis=0)), den


def _block_diag_rows(pieces):
    zero = jnp.zeros_like(pieces[0][0])
    rows = []
    for i in range(len(pieces)):
        row = []
        for j, ps in enumerate(pieces):
            row += [p if i == j else zero for p in ps]
        rows.append(jnp.concatenate(row, axis=1))
    return jnp.concatenate(rows, axis=0)


def _win_block(win_refs, j, tile, nb, put):
    qt_ref, kp_ref, kc_ref, kn_ref, km_ref, vp_ref, vc_ref, vn_ref, vm_ref, sink_ref = win_refs
    width = WIN_Q_HEADS * BLK
    ki = lax.broadcasted_iota(jnp.int32, (BLK, width), 0)
    qi = lax.broadcasted_iota(jnp.int32, (BLK, width), 1) & (BLK - 1)
    rep = WIN_Q_HEADS // WIN_KV_HEADS
    keys = [kp_ref[0]] + [kc_ref[0, jj * BLK:(jj + 1) * BLK, :] for jj in range(STEP_BLKS)] + [kn_ref[0]]
    vals = [vp_ref[0, 0]] + [vc_ref[0, jj] for jj in range(STEP_BLKS)] + [vn_ref[0, 0]]
    kmeta, vmeta = km_ref[0, META0:, :], vm_ref[0, 0]
    pad_last = jnp.zeros((META0, width), BF16)
    n = tile * STEP_BLKS + j
    ok_prev = jnp.logical_and(ki >= qi, n >= 2)
    ok_cur = n >= 1
    ok_next = jnp.logical_and(ki <= qi, n + 1 <= nb - 1)
    heads = [qt_ref[0, j, h * HEAD_DIM:(h + 1) * HEAD_DIM, :] for h in range(WIN_Q_HEADS)]
    qbd = _block_diag_rows([heads[g * rep:(g + 1) * rep] for g in range(WIN_KV_HEADS)])
    s = _dot(jnp.concatenate(keys[j:j + 3] + [kmeta], axis=0), qbd)
    scores = [jnp.where(ok_prev, s[:BLK], -jnp.inf),
              jnp.where(ok_cur, s[BLK:2 * BLK], -jnp.inf),
              jnp.where(ok_next, s[2 * BLK:3 * BLK], -jnp.inf),
              s[3 * BLK:]]
    o, den = _softmax_pv_t(scores, vals[j:j + 3] + [vmeta], pad_last, sink_ref[...])
    o = o / den
    outs = [o[(h // rep) * HEAD_DIM:(h // rep + 1) * HEAD_DIM, h * BLK:(h + 1) * BLK] for h in range(WIN_Q_HEADS)]
    put(j, jnp.concatenate(outs, axis=0).astype(BF16).T)


def _win_specs(qt, k, vt, sink, order):
    nb = qt.shape[1]
    kvw = WIN_KV_HEADS * HEAD_DIM
    assert kvw == BLK and WIN_RADIUS == BLK
    kedge = lambda f: pl.BlockSpec((1, BLK, kvw), lambda b, i: (b, f(order(i)), 0))
    vedge = lambda f: pl.BlockSpec((1, 1, kvw, BLK), lambda b, i: (b, f(order(i)), 0, 0))
    before = lambda t: jnp.maximum(t * STEP_BLKS - 1, 0)
    after = lambda t: jnp.minimum((t + 1) * STEP_BLKS, nb - 1)
    first = lambda t: 0
    specs = [pl.BlockSpec((1, STEP_BLKS, WIN_Q_HEADS * HEAD_DIM, BLK), lambda b, i: (b, order(i), 0, 0)),
             kedge(before), pl.BlockSpec((1, TM, kvw), lambda b, i: (b, order(i), 0)), kedge(after), kedge(first),
             vedge(before), pl.BlockSpec((1, STEP_BLKS, kvw, BLK), lambda b, i: (b, order(i), 0, 0)), vedge(after),
             vedge(first), pl.BlockSpec(sink.shape, lambda b, i: (0, 0))]
    return specs, [qt, k, k, k, k, vt, vt, vt, vt, sink]


NA_QUAD = 4
NA_QW = NA_QUAD * HEAD_DIM
NA_WIN = NA_KR * GRID_W


def _na_bias_tables(rpb):
    c = np.arange(GRID_W)[None, :]
    kc = np.arange(GRID_W)[:, None]
    cs = np.clip(c - NA_KC // 2, 0, GRID_W - NA_KC)
    ok = (kc >= cs) & (kc < cs + NA_KC)
    pick = jnp.asarray((kc - c + NA_KC - 1)[None] == np.arange(2 * NA_KC - 1)[:, None, None], F32)
    t = jnp.einsum('hed,dkc->hekc', rpb.astype(F32), pick, precision=lax.Precision.HIGHEST)
    t = jnp.where(jnp.asarray(ok)[None, None], t, -jnp.inf)

    def pack(x):
        h, e = x.shape[:2]
        x = x.reshape(h // NA_QUAD, NA_QUAD, e, GRID_W, GRID_W)
        return x.transpose(0, 2, 3, 1, 4).reshape(h // NA_QUAD, e, GRID_W, NA_QW)

    meta = jnp.broadcast_to(t[:, NA_KR - 1:, :, 0:1], (t.shape[0], NA_KR, GRID_W, GRID_W))
    return pack(t), pack(meta)


NA_KV_BLKS = 7


def _na_window_base(tile, nb):
    return jnp.clip(tile * STEP_BLKS - STEP_BLKS, 0, nb - 1 - NA_KV_BLKS)


def _na_blocks(na_refs, kbuf_ref, vbuf_ref, tile, nb, rows, put, blocks):
    qt_ref, lead_ref, bias_ref, mbias_ref, mb_ref = na_refs
    width = NA_HEADS * HEAD_DIM
    first_half = lax.broadcasted_iota(jnp.int32, (width, BLK), 1) < GRID_W
    ri = lax.broadcasted_iota(jnp.int32, (NA_QW, NA_QW), 0)
    ci = lax.broadcasted_iota(jnp.int32, (NA_QW, NA_QW), 1)
    diag = lax.shift_right_logical(ri, 6) == lax.shift_right_logical(ci, 6)
    head_of_lane = lax.shift_right_logical(lax.broadcasted_iota(jnp.int32, (GRID_W, NA_QW), 1), 6)
    pad_meta = jnp.zeros((META0, NA_QW), F32)
    row0 = _na_window_base(tile, nb) * (BLK // GRID_W)

    def scores(j, half, u, meta):
        qf = qt_ref[0, j].astype(F32)
        qsw = pltpu.roll(qf, GRID_W, 1)
        dup = jnp.where(first_half, qf, qsw) if half == 0 else jnp.where(first_half, qsw, qf)
        if meta:
            start, e0 = 0, None
        else:
            r = (tile * STEP_BLKS + j - 1) * (BLK // GRID_W) + half
            rs = jnp.clip(r - NA_KR // 2, 0, rows - NA_KR)
            start = pl.multiple_of((rs - row0) * GRID_W, GRID_W)
            e0 = rs - r + (NA_KR - 1)
        cols = slice(u * NA_QW, (u + 1) * NA_QW)
        base = dup[u * NA_QW:(u + 1) * NA_QW]
        qbd = jnp.where(diag, jnp.concatenate([base, base], axis=1), 0.0).astype(BF16)
        keys = jnp.concatenate([kbuf_ref[pl.ds(start, NA_WIN), cols], lead_ref[0, META0:LEAD, cols]], axis=0)
        return _dot(keys, qbd), (j, half, u, start, e0, cols)

    def finish(s, unit):
        j, half, u, start, e0, cols = unit
        if e0 is None:
            bias = jnp.concatenate([mbias_ref[u, i] for i in range(NA_KR)], axis=0)
        else:
            bias = jnp.concatenate([bias_ref[u, e0 + i] for i in range(NA_KR)], axis=0)
        sw = s[:NA_WIN] + bias
        sm = s[NA_WIN:] + mb_ref[u]
        mx = jnp.maximum(jnp.max(sw, axis=0, keepdims=True), jnp.max(sm, axis=0, keepdims=True))
        pw, pm = jnp.exp(sw - mx), jnp.exp(sm - mx)
        rden = 1.0 / (jnp.sum(pw, axis=0, keepdims=True) + jnp.sum(pm, axis=0, keepdims=True))
        p_t = jnp.concatenate([pw * rden, pad_meta, pm * rden], axis=0).astype(BF16).T
        vals = jnp.concatenate([vbuf_ref[pl.ds(start, NA_WIN), cols],
                                lead_ref[0, :, u * NA_QW + width:(u + 1) * NA_QW + width]], axis=0)
        o = _dot(p_t, vals)
        out = o[(NA_QUAD - 1) * GRID_W:]
        for hq in range(NA_QUAD - 2, -1, -1):
            out = jnp.where(head_of_lane == hq, o[hq * GRID_W:(hq + 1) * GRID_W], out)
        put(j, half, cols, out.astype(BF16))

    def run(units):
        pending = scores(*units[0])
        for nxt in units[1:]:
            ahead = scores(*nxt)
            finish(*pending)
            pending = ahead
        finish(*pending)

    quads = range(NA_HEADS // NA_QUAD)
    if 0 in blocks:
        @pl.when(tile == 0)
        def _():
            for u in quads:
                put(0, 0, slice(u * NA_QW, (u + 1) * NA_QW), jnp.zeros((GRID_W, NA_QW), BF16))
            run([(0, 1, u, True) for u in quads])

        @pl.when(tile > 0)
        def _():
            run([(0, half, u, False) for half in range(2) for u in quads])

    rest = [j for j in blocks if j > 0]
    if rest:
        run([(j, half, u, False) for j in rest for half in range(2) for u in quads])


def _na_load_window(kv_refs, kbuf_ref, vbuf_ref):
    width = kbuf_ref.shape[1]
    for m in range(NA_KV_BLKS):
        kbuf_ref[m * BLK:(m + 1) * BLK, :] = kv_refs[m][0, :, :width]
        vbuf_ref[m * BLK:(m + 1) * BLK, :] = kv_refs[m][0, :, width:]


def _na_specs(qt, kv, bias, mbias, mb, order):
    nb, width = qt.shape[1], kv.shape[2]
    rows = (nb * BLK - LEAD) // GRID_W
    assert nb - 1 >= NA_KV_BLKS and rows >= NA_KR and N_META <= GRID_W and LEAD - GRID_W <= META0
    blk = lambda f: pl.BlockSpec((1, BLK, width), lambda b, i: (b, f(order(i)), 0))
    lead = blk(lambda t: 0)
    win = [blk(lambda t, m=m: LEAD // BLK + _na_window_base(t, nb) + m) for m in range(NA_KV_BLKS)]
    specs = [pl.BlockSpec((1, STEP_BLKS, qt.shape[2], BLK), lambda b, i: (b, order(i), 0, 0)), lead,
             _resident(bias), _resident(mbias), _resident(mb)] + win
    return specs, [qt, kv, bias, mbias, mb] + [kv] * NA_KV_BLKS


N_NA_REFS = 5


def _mixout_kernel(*refs, ns, ntile):
    lead_ref = refs[0] if ntile > 1 else None
    tile_refs = refs[ntile > 1:ntile + (ntile > 1)]
    rest = refs[ntile + (ntile > 1):]
    win_refs, rest = rest[:N_WIN_REFS], rest[N_WIN_REFS:]
    na_refs, rest = rest[:N_NA_REFS], rest[N_NA_REFS:]
    kv_refs, rest = rest[:NA_KV_BLKS], rest[NA_KV_BLKS:]
    (xt_ref, bc_ref, acs_ref, tab_ref, zt_ref, yf_ref, dsk_ref, nws_ref, wout_ref,
     nw_ref, o_ref, state_ref, ybuf_ref, ycat_ref, kbuf_ref, vbuf_ref) = rest
    win0 = SSD_INNER
    na0 = SSD_INNER + WIN_Q_HEADS * HEAD_DIM
    i = ns - 1 - pl.program_id(1)
    nb = ns * STEP_BLKS

    @pl.when(pl.program_id(1) == 0)
    def _():
        state_ref[...] = jnp.zeros_like(state_ref)

    def put_y(j, rows, y):
        ybuf_ref[rows, :] = y

    def finish(j):
        y = ybuf_ref[...] + yf_ref[0, j] + dsk_ref[...] * xt_ref[0, j].astype(F32)
        z = zt_ref[0, j].astype(F32)
        y = y * (z * (1.0 / (1.0 + jnp.exp(-z))))
        y = y * lax.rsqrt(jnp.mean(y * y, axis=0, keepdims=True) + EPS) * nws_ref[...]
        ycat_ref[j * BLK:(j + 1) * BLK, :win0] = y.astype(BF16).T

    def scan_chunk(j):
        _ssd_scan(lambda j, rows: xt_ref[0, j, rows, :],
                  lambda j, k: bc_ref[0, j * BLK:(j + 1) * BLK, k * SSD_STATE:(k + 1) * SSD_STATE],
                  lambda j: acs_ref[0, j * BLK:(j + 1) * BLK, :],
                  lambda j, k: tab_ref[0, j, k], state_ref, put_y, finish, reverse=True, chunks=[j])

    def put_win(j, y):
        ycat_ref[j * BLK:(j + 1) * BLK, win0:na0] = y

    def put_na(j, half, cols, y):
        ycat_ref[pl.ds(j * BLK + half * GRID_W, GRID_W), na0 + cols.start:na0 + cols.stop] = y

    def na_blocks(blocks):
        _na_blocks(na_refs, kbuf_ref, vbuf_ref, i, nb, (nb * BLK - LEAD) // GRID_W, put_na, blocks)

    _na_load_window(kv_refs, kbuf_ref, vbuf_ref)
    na_blocks([0])
    for j in reversed(range(STEP_BLKS)):
        scan_chunk(j)
        _win_block(win_refs, j, i, nb, put_win)
        if j > 0:
            na_blocks([j])
    mix = _dot(ycat_ref[...], wout_ref[...])
    out = _load_tile(tile_refs, lead_ref, i) + _rms(mix, nw_ref[...])
    o_ref[0] = jnp.where(_live_rows(TM, i), out, 0.0)


N_WIN_REFS = 10


def _mixout(src, win_args, na_args, xt, bc, acs, tab, zt, y_fwd, dsk, nws, w_out, nw):
    bsz, nc = xt.shape[:2]
    ns = nc // STEP_BLKS
    order = lambda c: ns - 1 - c
    tok = lambda n: pl.BlockSpec((1, TM, n), lambda b, c: (b, order(c), 0))
    feat = pl.BlockSpec((1, STEP_BLKS, SSD_INNER, BLK), lambda b, c: (b, order(c), 0, 0))
    tabs = pl.BlockSpec((1, STEP_BLKS) + tab.shape[2:], lambda b, c: (b, order(c), 0, 0, 0))
    tile_specs, tile_args = _tile_specs(src, TM, order)
    win_specs, win_ops = _win_specs(*win_args, order)
    na_specs, na_ops = _na_specs(*na_args, order)
    assert len(win_specs) == N_WIN_REFS and len(na_specs) == N_NA_REFS + NA_KV_BLKS
    na_width = NA_HEADS * HEAD_DIM
    consts = [dsk, nws, w_out, nw]
    return pl.pallas_call(
        functools.partial(_mixout_kernel, ns=ns, ntile=len(tile_specs) - (src[0] == 'x')),
        grid=(bsz, ns),
        in_specs=tile_specs + win_specs + na_specs + [feat, tok(bc.shape[-1]), tok(128), tabs, feat, feat]
        + [_resident(c) for c in consts],
        out_specs=tok(D_MODEL),
        out_shape=jax.ShapeDtypeStruct((bsz, nc * BLK, D_MODEL), F32),
        scratch_shapes=[pltpu.VMEM((SSD_GROUPS, SSD_INNER // SSD_GROUPS, SSD_STATE), F32),
                        pltpu.VMEM((SSD_INNER, BLK), F32), pltpu.VMEM((TM, w_out.shape[0]), BF16),
                        pltpu.VMEM((NA_KV_BLKS * BLK, na_width), BF16), pltpu.VMEM((NA_KV_BLKS * BLK, na_width), BF16)],
        compiler_params=_cparams(("parallel", "arbitrary")),
        name="mixout",
    )(*tile_args, *win_ops, *na_ops, xt, bc, acs, tab, zt, y_fwd, *consts)


def _ffn_kernel(*refs, nt, ntile, padded):
    tile_refs = refs[:ntile]
    hp_ref, hn_ref, nw1_ref, wup_ref, cw_ref, cb_ref, wdn_ref, nw2_ref, o_ref, act_ref = refs[ntile:]
    i = pl.program_id(1)
    rows = act_ref.shape[0]
    nw1 = nw1_ref[...]
    hc = _load_tile(tile_refs, None, i)
    f = jnp.concatenate([_rms(hp_ref[0], nw1), _rms(hc, nw1), _rms(hn_ref[0], nw1)], axis=0).astype(BF16)
    tail = jnp.where(i < nt - 1, 1.0, 0.0)
    nch = D_FF // FF_CHUNK

    def conv(g, c):
        g = jnp.concatenate([g[:HALO + rows], g[HALO + rows:] * tail], axis=0)
        cols = slice(c * FF_CHUNK, (c + 1) * FF_CHUNK)
        out = _shift_rows(g, -1) * cw_ref[0:1, cols] + cb_ref[:, cols]
        out = out + g[HALO:HALO + rows] * cw_ref[1:2, cols]
        return out + _shift_rows(g, 1) * cw_ref[2:3, cols]

    for c in range(nch):
        gate = conv(_dot(f, wup_ref[:, c * FF_CHUNK:(c + 1) * FF_CHUNK]), c)
        up = conv(_dot(f, wup_ref[:, D_FF + c * FF_CHUNK:D_FF + (c + 1) * FF_CHUNK]), nch + c)
        th = jnp.tanh(gate * (GELU_K + (GELU_K * GELU_C) * (gate * gate)))
        act_ref[:, c * FF_CHUNK:(c + 1) * FF_CHUNK] = (gate * (0.5 * th + 0.5) * up).astype(BF16)
    out = hc + _rms(_dot(act_ref[...], wdn_ref[...]), nw2_ref[...])
    o_ref[0] = jnp.where(_live_rows(rows, i), out, 0.0) if padded else out


def _ffn(h, nw1, wup, cw, cb, wdn, nw2, *, last):
    bsz, lp, _ = h.shape
    full = _resident
    if last:
        n_out = lp - LEAD
        tm = _tile_rows(n_out, FFN_OUT_TILE_CAP, BLK)
        h4 = h.reshape(bsz, lp // BLK, BLK, D_MODEL)
        k = tm // BLK
        tile_specs = [pl.BlockSpec((1, 1, BLK, D_MODEL), lambda b, i, j=j: (b, LEAD // BLK + i * k + j, 0, 0))
                      for j in range(k)]
        tile_args = [h4] * k
        off = LEAD // HALO
    else:
        tm, n_out = TM, lp
        tile_specs, tile_args = _tile_specs(('h', h), tm)
        off = 0
    nt = n_out // tm
    hb = tm // HALO
    last_blk = lp // HALO - 1
    prev = pl.BlockSpec((1, HALO, D_MODEL), lambda b, i: (b, jnp.clip(off + i * hb - 1, 0, last_blk), 0))
    nxt = pl.BlockSpec((1, HALO, D_MODEL), lambda b, i: (b, jnp.clip(off + (i + 1) * hb, 0, last_blk), 0))
    return pl.pallas_call(
        functools.partial(_ffn_kernel, nt=nt, ntile=len(tile_specs), padded=not last),
        grid=(bsz, nt),
        in_specs=tile_specs + [prev, nxt, full(nw1), full(wup), full(cw), full(cb), full(wdn), full(nw2)],
        out_specs=pl.BlockSpec((1, tm, D_MODEL), lambda b, i: (b, i, 0)),
        out_shape=jax.ShapeDtypeStruct((bsz, n_out, D_MODEL), F32),
        scratch_shapes=[pltpu.VMEM((tm, D_FF), BF16)],
        compiler_params=_cparams(("parallel", "parallel")),
        name="ffn",
    )(*tile_args, h, h, nw1, wup, cw, cb, wdn, nw2)


def _pad_lanes(x, n):
    return jnp.pad(x, [(0, 0)] * (x.ndim - 1) + [(0, n - x.shape[-1])])


def _rope_tables(lp):
    half = ROPE_DIM // 2
    pos = jnp.maximum(jnp.arange(lp) - META0, 0).astype(F32)
    inv = jnp.power(ROPE_THETA, -jnp.arange(half, dtype=F32) / half)
    ang = pos[:, None] * inv[None, :]
    return jnp.cos(ang).T, jnp.sin(ang).T


def _layer_params(i, p):
    row = lambda v: v.reshape(1, -1).astype(F32)
    lanes = lambda v: jnp.broadcast_to(v.astype(F32)[:, None], (v.shape[0], BLK))
    sizes = [SSD_INNER, SSD_XBC, 2 * SSD_HEADS, WIN_Q_HEADS * HEAD_DIM, WIN_KV_HEADS * HEAD_DIM,
             WIN_KV_HEADS * HEAD_DIM, NA_HEADS * HEAD_DIM, NA_HEADS * HEAD_DIM, NA_HEADS * HEAD_DIM]
    w_in = p['w_in'][i].astype(BF16)
    ws = jnp.split(w_in, np.cumsum(sizes)[:-1].tolist(), axis=1)
    ws = ws[:2] + [jnp.concatenate([_pad_lanes(ws[2], BLK), ws[4], ws[5]], axis=1), ws[3]] + ws[6:]
    return dict(
        norm_mix_pre=row(p['norm_mix_pre'][i]),
        w_in=ws,
        ssd_conv=(p['ssd_conv_w'][i].astype(F32), row(p['ssd_conv_b'][i])),
        ssd_consts=(_pad_lanes(row(p['ssd_dt_bias'][i]), 128), _pad_lanes(row(p['ssd_a_log'][i]), 128)),
        ssd_d=lanes(jnp.repeat(p['ssd_d'][i], HEAD_DIM)),
        ssd_norm_w=lanes(p['ssd_norm_w'][i]),
        win_sink=row(jnp.repeat(p['win_sink'][i], BLK)),
        na_bias=_na_bias_tables(p['na_rpb'][i]),
        na_meta_bias=jnp.broadcast_to(p['na_meta_bias'][i].astype(F32).T.reshape(N_META, NA_HEADS // NA_QUAD, NA_QUAD, 1),
                                      (N_META, NA_HEADS // NA_QUAD, NA_QUAD, GRID_W)).transpose(1, 0, 2, 3).reshape(
            NA_HEADS // NA_QUAD, N_META, NA_QW),
        w_out=p['w_out'][i].astype(BF16),
        norm_mix_post=row(p['norm_mix_post'][i]),
        norm_ffn_pre=row(p['norm_ffn_pre'][i]),
        ffn_w_up=p['ffn_w_up'][i].astype(BF16),
        ffn_conv_w=p['ffn_conv_w'][i].astype(F32),
        ffn_conv_b=row(p['ffn_conv_b'][i]),
        ffn_w_down=p['ffn_w_down'][i].astype(BF16),
        norm_ffn_post=row(p['norm_ffn_post'][i]),
    )


def _encode(x, meta_tokens, layers):
    bsz, n_tok, _ = x.shape
    assert (n_tok + LEAD) % TM == 0 and n_tok // GRID_W >= NA_KR
    lp = LEAD + n_tok
    lead = jnp.concatenate([jnp.zeros((META0, D_MODEL), F32), meta_tokens.astype(F32)], axis=0)
    src = ('x', x.astype(F32), lead)
    rope = _rope_tables(lp)
    for li, lw in enumerate(layers):
        zt, xt, bc, acs, tab, wq, wk, wv, nq, nkv, y_fwd = _inproj(src, lp, lw['norm_mix_pre'], lw['w_in'],
                                                                      lw['ssd_conv'], lw['ssd_consts'], rope)
        h = _mixout(src, (wq, wk, wv, lw['win_sink']), (nq, nkv, *lw['na_bias'], lw['na_meta_bias']), xt, bc, acs,
                    tab, zt, y_fwd, lw['ssd_d'], lw['ssd_norm_w'], lw['w_out'], lw['norm_mix_post'])
        h = _ffn(h, lw['norm_ffn_pre'], lw['ffn_w_up'], lw['ffn_conv_w'], lw['ffn_conv_b'],
                 lw['ffn_w_down'], lw['norm_ffn_post'], last=li == len(layers) - 1)
        src = ('h', h)
    return h


def kernel(x_prompt, x_sample, meta_tokens, norm_mix_pre, norm_mix_post, w_in, ssd_conv_w, ssd_conv_b,
           ssd_dt_bias, ssd_a_log, ssd_d, ssd_norm_w, win_sink, na_rpb, na_meta_bias, w_out, norm_ffn_pre,
           norm_ffn_post, ffn_w_up, ffn_conv_w, ffn_conv_b, ffn_w_down):
    p = dict(norm_mix_pre=norm_mix_pre, norm_mix_post=norm_mix_post, w_in=w_in, ssd_conv_w=ssd_conv_w,
             ssd_conv_b=ssd_conv_b, ssd_dt_bias=ssd_dt_bias, ssd_a_log=ssd_a_log, ssd_d=ssd_d,
             ssd_norm_w=ssd_norm_w, win_sink=win_sink, na_rpb=na_rpb, na_meta_bias=na_meta_bias, w_out=w_out,
             norm_ffn_pre=norm_ffn_pre, norm_ffn_post=norm_ffn_post, ffn_w_up=ffn_w_up, ffn_conv_w=ffn_conv_w,
             ffn_conv_b=ffn_conv_b, ffn_w_down=ffn_w_down)
    layers = [_layer_params(i, p) for i in range(w_in.shape[0])]
    return (_encode(x_prompt, meta_tokens, layers), _encode(x_sample, meta_tokens, layers))
```

```python
import functools

import jax
import jax.numpy as jnp
import numpy as np
from jax import lax
from jax.experimental import pallas as pl
from jax.experimental.pallas import tpu as pltpu

F32 = jnp.float32
BF16 = jnp.bfloat16

D_MODEL = 1024
N_META = 16
GRID_W = 64
HEAD_DIM = 64

SSD_HEADS = 16
SSD_INNER = SSD_HEADS * HEAD_DIM
SSD_GROUPS = 2
SSD_STATE = 128
SSD_XBC = SSD_INNER + 2 * SSD_GROUPS * SSD_STATE
SSD_CONV = 5

WIN_Q_HEADS = 8
WIN_KV_HEADS = 2
WIN_RADIUS = 128
ROPE_THETA = 500000.0
ROPE_DIM = HEAD_DIM // 4

NA_HEADS = 8
NA_KR = 8
NA_KC = 16

D_FF = 2816
EPS = 1e-6
GELU_K = float(np.sqrt(2.0 / np.pi))
GELU_C = 0.044715

BLK = 128
LEAD = BLK
META0 = LEAD - N_META
HALO = 8
TM = 384
FFN_OUT_TILE_CAP = 720
FF_CHUNK = 256
STEP_BLKS = TM // BLK
VMEM_LIMIT = 56 * 1024 * 1024


def _tile_rows(n, cap, unit=HALO):
    return max(t for t in range(unit, cap + 1, unit) if n % t == 0)


def _resident(arr):
    return pl.BlockSpec(arr.shape, lambda *_: (0,) * arr.ndim, pipeline_mode=pl.Buffered(1))


def _cparams(sem):
    return pltpu.CompilerParams(dimension_semantics=sem, vmem_limit_bytes=VMEM_LIMIT)


def _rms(x, w):
    return x * lax.rsqrt(jnp.mean(x * x, axis=-1, keepdims=True) + EPS) * w


def _dot(a, b):
    return jnp.dot(a, b, preferred_element_type=F32)


def _shift_rows(x, d):
    return pltpu.roll(x, (-d) % x.shape[0], 0)[HALO:x.shape[0] - HALO]


def _load_tile(refs, lead_ref, i):
    if len(refs) == 1:
        return refs[0][0]
    blocks = [r[0, 0] for r in refs]
    if lead_ref is not None:
        blocks[0] = jnp.where(i == 0, lead_ref[...], blocks[0])
    return jnp.concatenate(blocks, axis=0)


def _tile_specs(src, tm, order=lambda i: i):
    if src[0] == 'h':
        return [pl.BlockSpec((1, tm, D_MODEL), lambda b, i: (b, order(i), 0))], [src[1]]
    x, lead = src[1], src[2]
    x4 = x.reshape(x.shape[0], x.shape[1] // BLK, BLK, D_MODEL)
    k = tm // BLK
    specs = [pl.BlockSpec((1, 1, BLK, D_MODEL),
                          lambda b, i, j=j: (b, jnp.maximum(order(i) * k + j - LEAD // BLK, 0), 0, 0))
             for j in range(k)]
    return [pl.BlockSpec(lead.shape, lambda b, i: (0, 0))] + specs, [lead] + [x4] * k


def _halo_specs(src, tm):
    arr = src[1]
    off = 0 if src[0] == 'h' else LEAD // HALO
    last = arr.shape[1] // HALO - 1
    hb = tm // HALO
    prev = pl.BlockSpec((1, HALO, D_MODEL), lambda b, i: (b, jnp.clip(i * hb - 1 - off, 0, last), 0))
    nxt = pl.BlockSpec((1, HALO, D_MODEL), lambda b, i: (b, jnp.clip((i + 1) * hb - off, 0, last), 0))
    return [prev, nxt], [arr, arr]


def _live_rows(tile_rows, tile_index):
    row = tile_index * tile_rows + lax.broadcasted_iota(jnp.int32, (tile_rows, 1), 0)
    return row >= META0


def _dot_nt(a, b):
    return lax.dot_general(a, b, (((1,), (1,)), ((), ())), preferred_element_type=F32)


def _rope_t(xt, cos_t, sin_t):
    half = ROPE_DIM // 2
    parts = []
    for hh in range(xt.shape[0] // HEAD_DIM):
        x1 = xt[hh * HEAD_DIM:hh * HEAD_DIM + half]
        x2 = xt[hh * HEAD_DIM + half:hh * HEAD_DIM + ROPE_DIM]
        parts += [x1 * cos_t - x2 * sin_t, x2 * cos_t + x1 * sin_t, xt[hh * HEAD_DIM + ROPE_DIM:(hh + 1) * HEAD_DIM]]
    return jnp.concatenate(parts, axis=0)


def _scan_tables(dt_raw, dt_bias, a_log, live, acs_ref, tab_ref):
    nh = 2 * SSD_HEADS
    dtr = dt_raw + dt_bias
    dt = jnp.maximum(dtr, 0.0) + jnp.log1p(jnp.exp(-jnp.abs(dtr)))
    dt = jnp.where(live, dt, 0.0)
    a = dt * (-jnp.exp(a_log))
    ri = lax.broadcasted_iota(jnp.int32, (BLK, BLK), 0)
    ci = lax.broadcasted_iota(jnp.int32, (BLK, BLK), 1)
    tri = (ci <= ri).astype(BF16)
    fwd = ci < SSD_HEADS
    fwd_rows = ri < SSD_HEADS
    acs_chunks, tab_chunks = [], []
    for j in range(TM // BLK):
        rows = slice(j * BLK, (j + 1) * BLK)
        aj, dtj = a[rows], dt[rows]
        pre = sum(_dot(tri, part) for part in _split3(aj))
        total = pre[BLK - 1:BLK, :]
        acs = jnp.where(fwd, pre, total - pre + aj)
        acs_ref[0, rows, :] = acs
        acs_t = acs.T[:nh]
        chunk_sum = jnp.where(fwd_rows[:nh, 0:1], acs_t[:, BLK - 1:BLK], acs_t[:, 0:1])
        tabs = [acs_t, dtj.T[:nh],
                (dtj * jnp.exp(total - acs)).T[:nh],
                jnp.exp(acs_t), jnp.broadcast_to(jnp.exp(chunk_sum), (nh, BLK))]
        for k, t in enumerate(tabs):
            tab_ref[0, j, k] = t
        acs_chunks.append(acs)
        tab_chunks.append(tabs)
    return acs_chunks, tab_chunks


def _inproj_kernel(*refs, nt, ntile):
    lead_ref = refs[0] if ntile > 1 else None
    tile_refs = refs[ntile > 1:ntile + (ntile > 1)]
    (hp_ref, hn_ref, nw_ref, wz_ref, wx_ref, wsm_ref, wwq_ref,
     wnq_ref, wnk_ref, wnv_ref, cw_ref, cb_ref, dtb_ref, alog_ref, rct_ref, rst_ref,
     z_ref, xt_ref, bc_ref, acs_ref, tab_ref, wq_ref, wk_ref, wv_ref, nq_ref, nkv_ref, yf_ref,
     state_ref) = refs[ntile + (ntile > 1):]
    i = pl.program_id(1)

    @pl.when(i == 0)
    def _():
        state_ref[...] = jnp.zeros_like(state_ref)

    nw = nw_ref[...]
    a32 = _rms(_load_tile(tile_refs, lead_ref, i), nw)
    a = a32.astype(BF16)
    head = jnp.where(i > 0, 1.0, 0.0)
    tail = jnp.where(i < nt - 1, 1.0, 0.0)
    a_ext = jnp.concatenate([_rms(hp_ref[0], nw) * head, a32, _rms(hn_ref[0], nw) * tail], axis=0).astype(BF16)
    xe = _dot(a_ext, wx_ref[...])

    def put_blocks(ref, xt):
        for j in range(TM // BLK):
            ref[0, j] = xt[:, j * BLK:(j + 1) * BLK].astype(BF16)

    put_blocks(z_ref, _dot(a, wz_ref[...]).astype(BF16).T)
    small = _dot(a, wsm_ref[...])
    dt_raw, k_pre, v_pre = small[:, :BLK], small[:, BLK:2 * BLK], small[:, 2 * BLK:]
    pad = SSD_CONV // 2
    acc = _shift_rows(xe, -pad) * cw_ref[0:1, :] + cb_ref[...]
    for j in range(1, SSD_CONV):
        acc = acc + _shift_rows(xe, j - pad) * cw_ref[j:j + 1, :]
    xc = acc * (1.0 / (1.0 + jnp.exp(-acc)))
    xt_all = xc[:, :SSD_INNER].astype(BF16).T
    xt_chunks = [xt_all[:, j * BLK:(j + 1) * BLK] for j in range(TM // BLK)]
    for j in range(TM // BLK):
        xt_ref[0, j] = xt_chunks[j]
    bc = xc[:, SSD_INNER:].astype(BF16)
    bc_ref[0] = bc
    acs_chunks, tab_chunks = _scan_tables(dt_raw, dtb_ref[...], alog_ref[...], _live_rows(TM, i),
                                          acs_ref, tab_ref)

    def put_y(j, rows, y):
        yf_ref[0, j, rows, :] = y

    def scan_chunk(j):
        _ssd_scan(lambda j, rows: xt_chunks[j][rows],
                  lambda j, k: bc[j * BLK:(j + 1) * BLK, k * SSD_STATE:(k + 1) * SSD_STATE],
                  lambda j: acs_chunks[j], lambda j, k: tab_chunks[j][k], state_ref, put_y, None,
                  reverse=False, chunks=[j])

    scale = HEAD_DIM ** -0.5
    cos_t, sin_t = rct_ref[...], rst_ref[...]
    put_blocks(wq_ref, _rope_t(_dot(a, wwq_ref[...]).T, cos_t, sin_t) * scale)
    scan_chunk(0)
    wk_ref[0] = _rope_t(k_pre.T, cos_t, sin_t).astype(BF16).T
    put_blocks(wv_ref, v_pre.astype(BF16).T)
    scan_chunk(1)
    put_blocks(nq_ref, (_dot(a, wnq_ref[...]) * scale).astype(BF16).T)
    nkv_ref[0, :, :NA_HEADS * HEAD_DIM] = _dot(a, wnk_ref[...]).astype(BF16)
    scan_chunk(2)
    nkv_ref[0, :, NA_HEADS * HEAD_DIM:] = _dot(a, wnv_ref[...]).astype(BF16)


def _inproj(src, lp, nw, ws, conv, scan, rope):
    bsz = src[1].shape[0]
    nt = lp // TM
    nb = lp // BLK
    tile = lambda n: pl.BlockSpec((1, TM, n), lambda b, i: (b, i, 0))
    tblk = lambda n: pl.BlockSpec((1, TM // BLK, n, BLK), lambda b, i: (b, i, 0, 0))
    tile_specs, tile_args = _tile_specs(src, TM)
    halo_specs, halo_args = _halo_specs(src, TM)
    full = _resident
    rtab_t = pl.BlockSpec((ROPE_DIM // 2, TM), lambda b, i: (0, i))
    tok = lambda n, d: (tile(n), jax.ShapeDtypeStruct((bsz, lp, n), d))
    blk = lambda n: (tblk(n), jax.ShapeDtypeStruct((bsz, nb, n, BLK), BF16))
    kvw = WIN_KV_HEADS * HEAD_DIM
    tabs = (pl.BlockSpec((1, TM // BLK, 5, 2 * SSD_HEADS, BLK), lambda b, i: (b, i, 0, 0, 0)),
            jax.ShapeDtypeStruct((bsz, nb, 5, 2 * SSD_HEADS, BLK), F32))
    outs = [blk(SSD_INNER), blk(SSD_INNER), tok(SSD_XBC - SSD_INNER, BF16), tok(128, F32), tabs, blk(WIN_Q_HEADS * HEAD_DIM), tok(kvw, BF16),
            blk(kvw), blk(NA_HEADS * HEAD_DIM), tok(2 * NA_HEADS * HEAD_DIM, BF16),
            (tblk(SSD_INNER), jax.ShapeDtypeStruct((bsz, nb, SSD_INNER, BLK), F32))]
    return pl.pallas_call(
        functools.partial(_inproj_kernel, nt=nt, ntile=len(tile_specs) - (src[0] == 'x')),
        grid=(bsz, nt),
        in_specs=tile_specs + halo_specs + [full(nw)] + [full(w) for w in ws] + [full(c) for c in conv + scan]
        + [rtab_t, rtab_t],
        out_specs=[o[0] for o in outs],
        out_shape=[o[1] for o in outs],
        scratch_shapes=[pltpu.VMEM((SSD_GROUPS, SSD_INNER // SSD_GROUPS, SSD_STATE), F32)],
        compiler_params=_cparams(("parallel", "arbitrary")),
        name="inproj",
    )(*tile_args, *halo_args, nw, *ws, *conv, *scan, *rope)


def _split3(x):
    hi = x.astype(BF16)
    r1 = x - hi.astype(F32)
    mid = r1.astype(BF16)
    lo = (r1 - mid.astype(F32)).astype(BF16)
    return hi, mid, lo


def _ssd_scan(get_x, get_bc, get_acs, get_tab, state_ref, put_y, after_chunk, *, reverse, chunks=None):
    ri = lax.broadcasted_iota(jnp.int32, (BLK, BLK), 0)
    ci = lax.broadcasted_iota(jnp.int32, (BLK, BLK), 1)
    feeds = (ri >= ci) if reverse else (ri <= ci)
    hoff = SSD_HEADS if reverse else 0
    rep = SSD_HEADS // SSD_GROUPS
    if chunks is None:
        chunks = reversed(range(STEP_BLKS)) if reverse else range(STEP_BLKS)
    for j in chunks:
        acs = get_acs(j)
        acs_t, dt_t, dt_out_t, e_in_t, e_chunk = (get_tab(j, k) for k in range(5))
        for g in range(SSD_GROUPS):
            bm = get_bc(j, g)
            cm = get_bc(j, SSD_GROUPS + g)
            cbt = _dot_nt(bm, cm)
            ht = state_ref[g]
            y_off = _dot_nt(ht.astype(BF16), cm)
            x_out, decay = [], []
            for r in range(rep):
                h = g * rep + r
                hl = hoff + h
                rows = slice(h * HEAD_DIM, (h + 1) * HEAD_DIM)
                xt = get_x(j, rows).astype(F32)
                x_in = (xt * dt_t[hl:hl + 1, :]).astype(BF16)
                x_out.append((xt * dt_out_t[hl:hl + 1, :]).astype(BF16))
                decay.append(jnp.broadcast_to(e_chunk[hl:hl + 1, :], (HEAD_DIM, SSD_STATE)))
                diff = acs_t[hl:hl + 1, :] - acs[:, hl:hl + 1]
                lt = (cbt * jnp.exp(jnp.where(feeds, diff, -jnp.inf))).astype(BF16)
                put_y(j, rows, _dot(x_in, lt) + y_off[r * HEAD_DIM:(r + 1) * HEAD_DIM] * e_in_t[hl:hl + 1, :])
            s_new = _dot(jnp.concatenate(x_out, axis=0), bm)
            state_ref[g] = ht * jnp.concatenate(decay, axis=0) + s_new
        if after_chunk is not None:
            after_chunk(j)


def _softmax_pv_t(scores, values, pad_last, sink_logit):
    mx = functools.reduce(jnp.maximum, [jnp.max(s, axis=0, keepdims=True) for s in scores] + [sink_logit])
    ps = [jnp.exp(s - mx) for s in scores]
    den = functools.reduce(jnp.add, [jnp.sum(p, axis=0, keepdims=True) for p in ps]) + jnp.exp(sink_logit - mx)
    pb = [p.astype(BF16) for p in ps]
    pb[-1] = jnp.concatenate([pad_last, pb[-1]], axis=0)
    return _dot(jnp.concatenate(values, axis=1), jnp.concatenate(pb, axis=0)), den


def _block_diag_rows(pieces):
    zero = jnp.zeros_like(pieces[0][0])
    rows = []
    for i in range(len(pieces)):
        row = []
        for j, ps in enumerate(pieces):
            row += [p if i == j else zero for p in ps]
        rows.append(jnp.concatenate(row, axis=1))
    return jnp.concatenate(rows, axis=0)


def _win_block(win_refs, j, tile, nb, put):
    qt_ref, kp_ref, kc_ref, kn_ref, km_ref, vp_ref, vc_ref, vn_ref, vm_ref, sink_ref = win_refs
    width = WIN_Q_HEADS * BLK
    ki = lax.broadcasted_iota(jnp.int32, (BLK, width), 0)
    qi = lax.broadcasted_iota(jnp.int32, (BLK, width), 1) & (BLK - 1)
    rep = WIN_Q_HEADS // WIN_KV_HEADS
    keys = [kp_ref[0]] + [kc_ref[0, jj * BLK:(jj + 1) * BLK, :] for jj in range(STEP_BLKS)] + [kn_ref[0]]
    vals = [vp_ref[0, 0]] + [vc_ref[0, jj] for jj in range(STEP_BLKS)] + [vn_ref[0, 0]]
    kmeta, vmeta = km_ref[0, META0:, :], vm_ref[0, 0]
    pad_last = jnp.zeros((META0, width), BF16)
    n = tile * STEP_BLKS + j
    ok_prev = jnp.logical_and(ki >= qi, n >= 2)
    ok_cur = n >= 1
    ok_next = jnp.logical_and(ki <= qi, n + 1 <= nb - 1)
    heads = [qt_ref[0, j, h * HEAD_DIM:(h + 1) * HEAD_DIM, :] for h in range(WIN_Q_HEADS)]
    qbd = _block_diag_rows([heads[g * rep:(g + 1) * rep] for g in range(WIN_KV_HEADS)])
    s = _dot(jnp.concatenate(keys[j:j + 3] + [kmeta], axis=0), qbd)
    scores = [jnp.where(ok_prev, s[:BLK], -jnp.inf),
              jnp.where(ok_cur, s[BLK:2 * BLK], -jnp.inf),
              jnp.where(ok_next, s[2 * BLK:3 * BLK], -jnp.inf),
              s[3 * BLK:]]
    o, den = _softmax_pv_t(scores, vals[j:j + 3] + [vmeta], pad_last, sink_ref[...])
    o = o / den
    outs = [o[(h // rep) * HEAD_DIM:(h // rep + 1) * HEAD_DIM, h * BLK:(h + 1) * BLK] for h in range(WIN_Q_HEADS)]
    put(j, jnp.concatenate(outs, axis=0).astype(BF16).T)


def _win_specs(qt, k, vt, sink, order):
    nb = qt.shape[1]
    kvw = WIN_KV_HEADS * HEAD_DIM
    assert kvw == BLK and WIN_RADIUS == BLK
    kedge = lambda f: pl.BlockSpec((1, BLK, kvw), lambda b, i: (b, f(order(i)), 0))
    vedge = lambda f: pl.BlockSpec((1, 1, kvw, BLK), lambda b, i: (b, f(order(i)), 0, 0))
    before = lambda t: jnp.maximum(t * STEP_BLKS - 1, 0)
    after = lambda t: jnp.minimum((t + 1) * STEP_BLKS, nb - 1)
    first = lambda t: 0
    specs = [pl.BlockSpec((1, STEP_BLKS, WIN_Q_HEADS * HEAD_DIM, BLK), lambda b, i: (b, order(i), 0, 0)),
             kedge(before), pl.BlockSpec((1, TM, kvw), lambda b, i: (b, order(i), 0)), kedge(after), kedge(first),
             vedge(before), pl.BlockSpec((1, STEP_BLKS, kvw, BLK), lambda b, i: (b, order(i), 0, 0)), vedge(after),
             vedge(first), pl.BlockSpec(sink.shape, lambda b, i: (0, 0))]
    return specs, [qt, k, k, k, k, vt, vt, vt, vt, sink]


NA_QUAD = 4
NA_QW = NA_QUAD * HEAD_DIM
NA_WIN = NA_KR * GRID_W


def _na_bias_tables(rpb):
    c = np.arange(GRID_W)[None, :]
    kc = np.arange(GRID_W)[:, None]
    cs = np.clip(c - NA_KC // 2, 0, GRID_W - NA_KC)
    ok = (kc >= cs) & (kc < cs + NA_KC)
    pick = jnp.asarray((kc - c + NA_KC - 1)[None] == np.arange(2 * NA_KC - 1)[:, None, None], F32)
    t = jnp.einsum('hed,dkc->hekc', rpb.astype(F32), pick, precision=lax.Precision.HIGHEST)
    t = jnp.where(jnp.asarray(ok)[None, None], t, -jnp.inf)

    def pack(x):
        h, e = x.shape[:2]
        x = x.reshape(h // NA_QUAD, NA_QUAD, e, GRID_W, GRID_W)
        return x.transpose(0, 2, 3, 1, 4).reshape(h // NA_QUAD, e, GRID_W, NA_QW)

    meta = jnp.broadcast_to(t[:, NA_KR - 1:, :, 0:1], (t.shape[0], NA_KR, GRID_W, GRID_W))
    return pack(t), pack(meta)


NA_KV_BLKS = 7


def _na_window_base(tile, nb):
    return jnp.clip(tile * STEP_BLKS - STEP_BLKS, 0, nb - 1 - NA_KV_BLKS)


def _na_blocks(na_refs, kbuf_ref, vbuf_ref, tile, nb, rows, put, blocks):
    qt_ref, lead_ref, bias_ref, mbias_ref, mb_ref = na_refs
    width = NA_HEADS * HEAD_DIM
    first_half = lax.broadcasted_iota(jnp.int32, (width, BLK), 1) < GRID_W
    ri = lax.broadcasted_iota(jnp.int32, (NA_QW, NA_QW), 0)
    ci = lax.broadcasted_iota(jnp.int32, (NA_QW, NA_QW), 1)
    diag = lax.shift_right_logical(ri, 6) == lax.shift_right_logical(ci, 6)
    head_of_lane = lax.shift_right_logical(lax.broadcasted_iota(jnp.int32, (GRID_W, NA_QW), 1), 6)
    pad_meta = jnp.zeros((META0, NA_QW), F32)
    row0 = _na_window_base(tile, nb) * (BLK // GRID_W)

    def scores(j, half, u, meta):
        qf = qt_ref[0, j].astype(F32)
        qsw = pltpu.roll(qf, GRID_W, 1)
        dup = jnp.where(first_half, qf, qsw) if half == 0 else jnp.where(first_half, qsw, qf)
        if meta:
            start, e0 = 0, None
        else:
            r = (tile * STEP_BLKS + j - 1) * (BLK // GRID_W) + half
            rs = jnp.clip(r - NA_KR // 2, 0, rows - NA_KR)
            start = pl.multiple_of((rs - row0) * GRID_W, GRID_W)
            e0 = rs - r + (NA_KR - 1)
        cols = slice(u * NA_QW, (u + 1) * NA_QW)
        base = dup[u * NA_QW:(u + 1) * NA_QW]
        qbd = jnp.where(diag, jnp.concatenate([base, base], axis=1), 0.0).astype(BF16)
        keys = jnp.concatenate([kbuf_ref[pl.ds(start, NA_WIN), cols], lead_ref[0, META0:LEAD, cols]], axis=0)
        return _dot(keys, qbd), (j, half, u, start, e0, cols)

    def finish(s, unit):
        j, half, u, start, e0, cols = unit
        if e0 is None:
            bias = jnp.concatenate([mbias_ref[u, i] for i in range(NA_KR)], axis=0)
        else:
            bias = jnp.concatenate([bias_ref[u, e0 + i] for i in range(NA_KR)], axis=0)
        sw = s[:NA_WIN] + bias
        sm = s[NA_WIN:] + mb_ref[u]
        mx = jnp.maximum(jnp.max(sw, axis=0, keepdims=True), jnp.max(sm, axis=0, keepdims=True))
        pw, pm = jnp.exp(sw - mx), jnp.exp(sm - mx)
        rden = 1.0 / (jnp.sum(pw, axis=0, keepdims=True) + jnp.sum(pm, axis=0, keepdims=True))
        p_t = jnp.concatenate([pw * rden, pad_meta, pm * rden], axis=0).astype(BF16).T
        vals = jnp.concatenate([vbuf_ref[pl.ds(start, NA_WIN), cols],
                                lead_ref[0, :, u * NA_QW + width:(u + 1) * NA_QW + width]], axis=0)
        o = _dot(p_t, vals)
        out = o[(NA_QUAD - 1) * GRID_W:]
        for hq in range(NA_QUAD - 2, -1, -1):
            out = jnp.where(head_of_lane == hq, o[hq * GRID_W:(hq + 1) * GRID_W], out)
        put(j, half, cols, out.astype(BF16))

    def run(units):
        pending = scores(*units[0])
        for nxt in units[1:]:
            ahead = scores(*nxt)
            finish(*pending)
            pending = ahead
        finish(*pending)

    quads = range(NA_HEADS // NA_QUAD)
    if 0 in blocks:
        @pl.when(tile == 0)
        def _():
            for u in quads:
                put(0, 0, slice(u * NA_QW, (u + 1) * NA_QW), jnp.zeros((GRID_W, NA_QW), BF16))
            run([(0, 1, u, True) for u in quads])

        @pl.when(tile > 0)
        def _():
            run([(0, half, u, False) for half in range(2) for u in quads])

    rest = [j for j in blocks if j > 0]
    if rest:
        run([(j, half, u, False) for j in rest for half in range(2) for u in quads])


def _na_load_window(kv_refs, kbuf_ref, vbuf_ref):
    width = kbuf_ref.shape[1]
    for m in range(NA_KV_BLKS):
        kbuf_ref[m * BLK:(m + 1) * BLK, :] = kv_refs[m][0, :, :width]
        vbuf_ref[m * BLK:(m + 1) * BLK, :] = kv_refs[m][0, :, width:]


def _na_specs(qt, kv, bias, mbias, mb, order):
    nb, width = qt.shape[1], kv.shape[2]
    rows = (nb * BLK - LEAD) // GRID_W
    assert nb - 1 >= NA_KV_BLKS and rows >= NA_KR and N_META <= GRID_W and LEAD - GRID_W <= META0
    blk = lambda f: pl.BlockSpec((1, BLK, width), lambda b, i: (b, f(order(i)), 0))
    lead = blk(lambda t: 0)
    win = [blk(lambda t, m=m: LEAD // BLK + _na_window_base(t, nb) + m) for m in range(NA_KV_BLKS)]
    specs = [pl.BlockSpec((1, STEP_BLKS, qt.shape[2], BLK), lambda b, i: (b, order(i), 0, 0)), lead,
             _resident(bias), _resident(mbias), _resident(mb)] + win
    return specs, [qt, kv, bias, mbias, mb] + [kv] * NA_KV_BLKS


N_NA_REFS = 5


def _mixout_kernel(*refs, ns, ntile):
    lead_ref = refs[0] if ntile > 1 else None
    tile_refs = refs[ntile > 1:ntile + (ntile > 1)]
    rest = refs[ntile + (ntile > 1):]
    win_refs, rest = rest[:N_WIN_REFS], rest[N_WIN_REFS:]
    na_refs, rest = rest[:N_NA_REFS], rest[N_NA_REFS:]
    kv_refs, rest = rest[:NA_KV_BLKS], rest[NA_KV_BLKS:]
    (xt_ref, bc_ref, acs_ref, tab_ref, zt_ref, yf_ref, dsk_ref, nws_ref, wout_ref,
     nw_ref, o_ref, state_ref, ybuf_ref, ycat_ref, kbuf_ref, vbuf_ref) = rest
    win0 = SSD_INNER
    na0 = SSD_INNER + WIN_Q_HEADS * HEAD_DIM
    i = ns - 1 - pl.program_id(1)
    nb = ns * STEP_BLKS

    @pl.when(pl.program_id(1) == 0)
    def _():
        state_ref[...] = jnp.zeros_like(state_ref)

    def put_y(j, rows, y):
        ybuf_ref[rows, :] = y

    def finish(j):
        y = ybuf_ref[...] + yf_ref[0, j] + dsk_ref[...] * xt_ref[0, j].astype(F32)
        z = zt_ref[0, j].astype(F32)
        y = y * (z * (1.0 / (1.0 + jnp.exp(-z))))
        y = y * lax.rsqrt(jnp.mean(y * y, axis=0, keepdims=True) + EPS) * nws_ref[...]
        ycat_ref[j * BLK:(j + 1) * BLK, :win0] = y.astype(BF16).T

    def scan_chunk(j):
        _ssd_scan(lambda j, rows: xt_ref[0, j, rows, :],
                  lambda j, k: bc_ref[0, j * BLK:(j + 1) * BLK, k * SSD_STATE:(k + 1) * SSD_STATE],
                  lambda j: acs_ref[0, j * BLK:(j + 1) * BLK, :],
                  lambda j, k: tab_ref[0, j, k], state_ref, put_y, finish, reverse=True, chunks=[j])

    def put_win(j, y):
        ycat_ref[j * BLK:(j + 1) * BLK, win0:na0] = y

    def put_na(j, half, cols, y):
        ycat_ref[pl.ds(j * BLK + half * GRID_W, GRID_W), na0 + cols.start:na0 + cols.stop] = y

    def na_blocks(blocks):
        _na_blocks(na_refs, kbuf_ref, vbuf_ref, i, nb, (nb * BLK - LEAD) // GRID_W, put_na, blocks)

    _na_load_window(kv_refs, kbuf_ref, vbuf_ref)
    na_blocks([0])
    for j in reversed(range(STEP_BLKS)):
        scan_chunk(j)
        _win_block(win_refs, j, i, nb, put_win)
        if j > 0:
            na_blocks([j])
    mix = _dot(ycat_ref[...], wout_ref[...])
    out = _load_tile(tile_refs, lead_ref, i) + _rms(mix, nw_ref[...])
    o_ref[0] = jnp.where(_live_rows(TM, i), out, 0.0)


N_WIN_REFS = 10


def _mixout(src, win_args, na_args, xt, bc, acs, tab, zt, y_fwd, dsk, nws, w_out, nw):
    bsz, nc = xt.shape[:2]
    ns = nc // STEP_BLKS
    order = lambda c: ns - 1 - c
    tok = lambda n: pl.BlockSpec((1, TM, n), lambda b, c: (b, order(c), 0))
    feat = pl.BlockSpec((1, STEP_BLKS, SSD_INNER, BLK), lambda b, c: (b, order(c), 0, 0))
    tabs = pl.BlockSpec((1, STEP_BLKS) + tab.shape[2:], lambda b, c: (b, order(c), 0, 0, 0))
    tile_specs, tile_args = _tile_specs(src, TM, order)
    win_specs, win_ops = _win_specs(*win_args, order)
    na_specs, na_ops = _na_specs(*na_args, order)
    assert len(win_specs) == N_WIN_REFS and len(na_specs) == N_NA_REFS + NA_KV_BLKS
    na_width = NA_HEADS * HEAD_DIM
    consts = [dsk, nws, w_out, nw]
    return pl.pallas_call(
        functools.partial(_mixout_kernel, ns=ns, ntile=len(tile_specs) - (src[0] == 'x')),
        grid=(bsz, ns),
        in_specs=tile_specs + win_specs + na_specs + [feat, tok(bc.shape[-1]), tok(128), tabs, feat, feat]
        + [_resident(c) for c in consts],
        out_specs=tok(D_MODEL),
        out_shape=jax.ShapeDtypeStruct((bsz, nc * BLK, D_MODEL), F32),
        scratch_shapes=[pltpu.VMEM((SSD_GROUPS, SSD_INNER // SSD_GROUPS, SSD_STATE), F32),
                        pltpu.VMEM((SSD_INNER, BLK), F32), pltpu.VMEM((TM, w_out.shape[0]), BF16),
                        pltpu.VMEM((NA_KV_BLKS * BLK, na_width), BF16), pltpu.VMEM((NA_KV_BLKS * BLK, na_width), BF16)],
        compiler_params=_cparams(("parallel", "arbitrary")),
        name="mixout",
    )(*tile_args, *win_ops, *na_ops, xt, bc, acs, tab, zt, y_fwd, *consts)


def _ffn_kernel(*refs, nt, ntile, padded):
    tile_refs = refs[:ntile]
    hp_ref, hn_ref, nw1_ref, wup_ref, cw_ref, cb_ref, wdn_ref, nw2_ref, o_ref, act_ref = refs[ntile:]
    i = pl.program_id(1)
    rows = act_ref.shape[0]
    nw1 = nw1_ref[...]
    hc = _load_tile(tile_refs, None, i)
    f = jnp.concatenate([_rms(hp_ref[0], nw1), _rms(hc, nw1), _rms(hn_ref[0], nw1)], axis=0).astype(BF16)
    tail = jnp.where(i < nt - 1, 1.0, 0.0)
    nch = D_FF // FF_CHUNK

    def conv(g, c):
        g = jnp.concatenate([g[:HALO + rows], g[HALO + rows:] * tail], axis=0)
        cols = slice(c * FF_CHUNK, (c + 1) * FF_CHUNK)
        out = _shift_rows(g, -1) * cw_ref[0:1, cols] + cb_ref[:, cols]
        out = out + g[HALO:HALO + rows] * cw_ref[1:2, cols]
        return out + _shift_rows(g, 1) * cw_ref[2:3, cols]

    for c in range(nch):
        gate = conv(_dot(f, wup_ref[:, c * FF_CHUNK:(c + 1) * FF_CHUNK]), c)
        up = conv(_dot(f, wup_ref[:, D_FF + c * FF_CHUNK:D_FF + (c + 1) * FF_CHUNK]), nch + c)
        th = jnp.tanh(gate * (GELU_K + (GELU_K * GELU_C) * (gate * gate)))
        act_ref[:, c * FF_CHUNK:(c + 1) * FF_CHUNK] = (gate * (0.5 * th + 0.5) * up).astype(BF16)
    out = hc + _rms(_dot(act_ref[...], wdn_ref[...]), nw2_ref[...])
    o_ref[0] = jnp.where(_live_rows(rows, i), out, 0.0) if padded else out


def _ffn(h, nw1, wup, cw, cb, wdn, nw2, *, last):
    bsz, lp, _ = h.shape
    full = _resident
    if last:
        n_out = lp - LEAD
        tm = _tile_rows(n_out, FFN_OUT_TILE_CAP, BLK)
        h4 = h.reshape(bsz, lp // BLK, BLK, D_MODEL)
        k = tm // BLK
        tile_specs = [pl.BlockSpec((1, 1, BLK, D_MODEL), lambda b, i, j=j: (b, LEAD // BLK + i * k + j, 0, 0))
                      for j in range(k)]
        tile_args = [h4] * k
        off = LEAD // HALO
    else:
        tm, n_out = TM, lp
        tile_specs, tile_args = _tile_specs(('h', h), tm)
        off = 0
    nt = n_out // tm
    hb = tm // HALO
    last_blk = lp // HALO - 1
    prev = pl.BlockSpec((1, HALO, D_MODEL), lambda b, i: (b, jnp.clip(off + i * hb - 1, 0, last_blk), 0))
    nxt = pl.BlockSpec((1, HALO, D_MODEL), lambda b, i: (b, jnp.clip(off + (i + 1) * hb, 0, last_blk), 0))
    return pl.pallas_call(
        functools.partial(_ffn_kernel, nt=nt, ntile=len(tile_specs), padded=not last),
        grid=(bsz, nt),
        in_specs=tile_specs + [prev, nxt, full(nw1), full(wup), full(cw), full(cb), full(wdn), full(nw2)],
        out_specs=pl.BlockSpec((1, tm, D_MODEL), lambda b, i: (b, i, 0)),
        out_shape=jax.ShapeDtypeStruct((bsz, n_out, D_MODEL), F32),
        scratch_shapes=[pltpu.VMEM((tm, D_FF), BF16)],
        compiler_params=_cparams(("parallel", "parallel")),
        name="ffn",
    )(*tile_args, h, h, nw1, wup, cw, cb, wdn, nw2)


def _pad_lanes(x, n):
    return jnp.pad(x, [(0, 0)] * (x.ndim - 1) + [(0, n - x.shape[-1])])


def _rope_tables(lp):
    half = ROPE_DIM // 2
    pos = jnp.maximum(jnp.arange(lp) - META0, 0).astype(F32)
    inv = jnp.power(ROPE_THETA, -jnp.arange(half, dtype=F32) / half)
    ang = pos[:, None] * inv[None, :]
    return jnp.cos(ang).T, jnp.sin(ang).T


def _layer_params(i, p):
    row = lambda v: v.reshape(1, -1).astype(F32)
    lanes = lambda v: jnp.broadcast_to(v.astype(F32)[:, None], (v.shape[0], BLK))
    sizes = [SSD_INNER, SSD_XBC, 2 * SSD_HEADS, WIN_Q_HEADS * HEAD_DIM, WIN_KV_HEADS * HEAD_DIM,
             WIN_KV_HEADS * HEAD_DIM, NA_HEADS * HEAD_DIM, NA_HEADS * HEAD_DIM, NA_HEADS * HEAD_DIM]
    w_in = p['w_in'][i].astype(BF16)
    ws = jnp.split(w_in, np.cumsum(sizes)[:-1].tolist(), axis=1)
    ws = ws[:2] + [jnp.concatenate([_pad_lanes(ws[2], BLK), ws[4], ws[5]], axis=1), ws[3]] + ws[6:]
    return dict(
        norm_mix_pre=row(p['norm_mix_pre'][i]),
        w_in=ws,
        ssd_conv=(p['ssd_conv_w'][i].astype(F32), row(p['ssd_conv_b'][i])),
        ssd_consts=(_pad_lanes(row(p['ssd_dt_bias'][i]), 128), _pad_lanes(row(p['ssd_a_log'][i]), 128)),
        ssd_d=lanes(jnp.repeat(p['ssd_d'][i], HEAD_DIM)),
        ssd_norm_w=lanes(p['ssd_norm_w'][i]),
        win_sink=row(jnp.repeat(p['win_sink'][i], BLK)),
        na_bias=_na_bias_tables(p['na_rpb'][i]),
        na_meta_bias=jnp.broadcast_to(p['na_meta_bias'][i].astype(F32).T.reshape(N_META, NA_HEADS // NA_QUAD, NA_QUAD, 1),
                                      (N_META, NA_HEADS // NA_QUAD, NA_QUAD, GRID_W)).transpose(1, 0, 2, 3).reshape(
            NA_HEADS // NA_QUAD, N_META, NA_QW),
        w_out=p['w_out'][i].astype(BF16),
        norm_mix_post=row(p['norm_mix_post'][i]),
        norm_ffn_pre=row(p['norm_ffn_pre'][i]),
        ffn_w_up=p['ffn_w_up'][i].astype(BF16),
        ffn_conv_w=p['ffn_conv_w'][i].astype(F32),
        ffn_conv_b=row(p['ffn_conv_b'][i]),
        ffn_w_down=p['ffn_w_down'][i].astype(BF16),
        norm_ffn_post=row(p['norm_ffn_post'][i]),
    )


def _encode(x, meta_tokens, layers):
    bsz, n_tok, _ = x.shape
    assert (n_tok + LEAD) % TM == 0 and n_tok // GRID_W >= NA_KR
    lp = LEAD + n_tok
    lead = jnp.concatenate([jnp.zeros((META0, D_MODEL), F32), meta_tokens.astype(F32)], axis=0)
    src = ('x', x.astype(F32), lead)
    rope = _rope_tables(lp)
    for li, lw in enumerate(layers):
        zt, xt, bc, acs, tab, wq, wk, wv, nq, nkv, y_fwd = _inproj(src, lp, lw['norm_mix_pre'], lw['w_in'],
                                                                      lw['ssd_conv'], lw['ssd_consts'], rope)
        h = _mixout(src, (wq, wk, wv, lw['win_sink']), (nq, nkv, *lw['na_bias'], lw['na_meta_bias']), xt, bc, acs,
                    tab, zt, y_fwd, lw['ssd_d'], lw['ssd_norm_w'], lw['w_out'], lw['norm_mix_post'])
        h = _ffn(h, lw['norm_ffn_pre'], lw['ffn_w_up'], lw['ffn_conv_w'], lw['ffn_conv_b'],
                 lw['ffn_w_down'], lw['norm_ffn_post'], last=li == len(layers) - 1)
        src = ('h', h)
    return h


def kernel(x_prompt, x_sample, meta_tokens, norm_mix_pre, norm_mix_post, w_in, ssd_conv_w, ssd_conv_b,
           ssd_dt_bias, ssd_a_log, ssd_d, ssd_norm_w, win_sink, na_rpb, na_meta_bias, w_out, norm_ffn_pre,
           norm_ffn_post, ffn_w_up, ffn_conv_w, ffn_conv_b, ffn_w_down):
    p = dict(norm_mix_pre=norm_mix_pre, norm_mix_post=norm_mix_post, w_in=w_in, ssd_conv_w=ssd_conv_w,
             ssd_conv_b=ssd_conv_b, ssd_dt_bias=ssd_dt_bias, ssd_a_log=ssd_a_log, ssd_d=ssd_d,
             ssd_norm_w=ssd_norm_w, win_sink=win_sink, na_rpb=na_rpb, na_meta_bias=na_meta_bias, w_out=w_out,
             norm_ffn_pre=norm_ffn_pre, norm_ffn_post=norm_ffn_post, ffn_w_up=ffn_w_up, ffn_conv_w=ffn_conv_w,
             ffn_conv_b=ffn_conv_b, ffn_w_down=ffn_w_down)
    layers = [_layer_params(i, p) for i in range(w_in.shape[0])]
    return (_encode(x_prompt, meta_tokens, layers), _encode(x_sample, meta_tokens, layers))
```
